```python
import math
import jax
import jax.numpy as jnp
from jax import lax
import numpy as np

D_MODEL = 1024
BATCH = 8
SEQ = 4096
DEPTH = 1

N_MEM = 256
D_MIX = D_MODEL
D_HYENA = D_MIX // 2
HYENA_GROUPS = 8
HYENA_ORDER = 2
D_MLSTM = D_MIX - D_HYENA
MLSTM_HEADS = 4
MLSTM_HEAD_DIM = D_MLSTM // MLSTM_HEADS
MLSTM_CHUNK = 64
FILTER_BANDS = 16
FILTER_EMB = 1 + 2 * FILTER_BANDS
FILTER_HIDDEN = 64
FILTER_DIRS = 2
FILTER_CH = HYENA_ORDER * FILTER_DIRS * D_HYENA
FILTER_OUT_SCALE = 0.05
DECAY_TARGET = 1e-2
SHORT_DECAY_PCT = 0.3
LONG_DECAY_PCT = 1.5
XATTN_HEADS = 4
XATTN_HEAD_DIM = D_MODEL // XATTN_HEADS
D_FF = 4 * D_MODEL
EPS = 1e-6
HY_COLS = (HYENA_ORDER + 1) * D_HYENA
ML_QK_COLS = 2 * D_MLSTM
ML_GATE_COLS = 4 * MLSTM_HEADS
D_IN_PROJ = HY_COLS + ML_QK_COLS + 2 * D_MLSTM + ML_GATE_COLS

kernel_name = 'hybrid_hyena_mlstm_encoder_layer'


def _rms_norm(x, g):
    xf = x.astype(jnp.float32)
    y = xf * lax.rsqrt(jnp.mean(xf * xf, axis=-1, keepdims=True) + EPS)
    return (y * g.astype(jnp.float32)).astype(x.dtype)


def _group_rms_norm(x, g, n_groups):
    lead = x.shape[:-1]
    c = x.shape[-1]
    xg = x.astype(jnp.float32).reshape(*lead, n_groups, c // n_groups)
    xg = xg * lax.rsqrt(jnp.mean(xg * xg, axis=-1, keepdims=True) + EPS)
    return (xg.reshape(*lead, c) * g.astype(jnp.float32)).astype(x.dtype)


def _short_conv_centred(u, w, b):
    L = u.shape[1]
    up = jnp.pad(u, ((0, 0), (1, 1), (0, 0)))
    return up[:, :L] * w[0] + up[:, 1:L + 1] * w[1] + up[:, 2:] * w[2] + b


def _hyena_filter_spectrum(L, w1, b1, fr1, w2, b2, fr2, w3):
    f32 = jnp.float32
    t = jnp.linspace(0.0, 1.0, L, dtype=f32)[:, None]
    ang = (2.0 * math.pi / L) * jnp.arange(L, dtype=f32)[:, None]
    bands = jnp.linspace(1e-4, FILTER_BANDS - 1, FILTER_BANDS, dtype=f32)[None, :]
    z = jnp.concatenate([t, jnp.cos(bands * ang), -jnp.sin(bands * ang)], axis=-1)
    hid = jnp.sin(fr1.astype(f32) * (z @ w1.astype(f32) + b1.astype(f32)))
    hid = jnp.sin(fr2.astype(f32) * (hid @ w2.astype(f32) + b2.astype(f32)))
    filt = (hid @ w3.astype(f32)).reshape(L, HYENA_ORDER, FILTER_DIRS, D_HYENA)
    max_decay = math.log(DECAY_TARGET) / SHORT_DECAY_PCT
    min_decay = math.log(DECAY_TARGET) / LONG_DECAY_PCT
    deltas = jnp.linspace(min_decay, max_decay, D_HYENA, dtype=f32)
    filt = filt * jnp.exp(-t * jnp.abs(deltas))[:, None, None, :]
    fwd = filt[:, :, 0]
    bwd = jnp.flip(filt[1:, :, 1], axis=0)
    two_sided = jnp.concatenate([fwd, jnp.zeros((1, HYENA_ORDER, D_HYENA), f32), bwd], axis=0)
    return jnp.fft.rfft(two_sided, axis=0)


def _hyena(hy, spec, skip):
    L = hy.shape[1]
    u = hy.astype(jnp.float32)
    z, g1, g2 = u[..., :D_HYENA], u[..., D_HYENA:2 * D_HYENA], u[..., 2 * D_HYENA:]
    for o, gate in enumerate((g1, g2)):
        conv = jnp.fft.irfft(jnp.fft.rfft(z, n=2 * L, axis=1) * spec[:, o], n=2 * L, axis=1)[:, :L]
        z = gate * (conv + z * skip[o].astype(jnp.float32))
    return z


def _mlstm_chunkwise(q, k, v, i_pre, log_f):
    B, H, L, Dh = q.shape
    nc = L // MLSTM_CHUNK

    def chunks(a):
        a = a.reshape(B, H, nc, MLSTM_CHUNK, *a.shape[3:])
        return jnp.moveaxis(a, 2, 0)

    lower = jnp.tril(jnp.ones((MLSTM_CHUNK, MLSTM_CHUNK), dtype=bool))

    def step(carry, inp):
        C, n, m = carry
        qc, kc, vc, ic, fc = inp
        F = jnp.cumsum(fc, axis=-1)
        logw = F[..., :, None] - F[..., None, :] + ic[..., None, :]
        logw = jnp.where(lower, logw, -jnp.inf)
        log_inter = F + m[..., None]
        m_t = jnp.maximum(log_inter, jnp.max(logw, axis=-1))
        w = jnp.exp(logw - m_t[..., None])
        inter = jnp.exp(log_inter - m_t)
        s = jnp.einsum('bhtd,bhsd->bhts', qc, kc) * w
        num = jnp.einsum('bhts,bhse->bhte', s, vc) + inter[..., None] * jnp.einsum('bhtd,bhde->bhte', qc, C)
        den = jnp.sum(s, axis=-1) + inter * jnp.einsum('bhtd,bhd->bht', qc, n)
        h = num / jnp.maximum(jnp.abs(den), jnp.exp(-m_t))[..., None]
        F_end = F[..., -1]
        log_end = F_end[..., None] - F + ic
        m_new = jnp.maximum(F_end + m, jnp.max(log_end, axis=-1))
        we = jnp.exp(log_end - m_new[..., None])
        carry_decay = jnp.exp(F_end + m - m_new)
        C_new = carry_decay[..., None, None] * C + jnp.einsum('bhs,bhsd,bhse->bhde', we, kc, vc)
        n_new = carry_decay[..., None] * n + jnp.einsum('bhs,bhsd->bhd', we, kc)
        return (C_new, n_new, m_new), h

    init = (jnp.zeros((B, H, Dh, Dh), jnp.float32),
            jnp.zeros((B, H, Dh), jnp.float32),
            jnp.zeros((B, H), jnp.float32))
    _, hs = lax.scan(step, init, tuple(chunks(a) for a in (q, k, v, i_pre, log_f)))
    return jnp.moveaxis(hs, 0, 2).reshape(B, H, L, Dh)


def _mlstm_bidirectional(q, k, v, o_pre, gates, gate_b):
    B, L, _ = q.shape
    f32 = jnp.float32

    def heads(a):
        return a.astype(f32).reshape(B, L, MLSTM_HEADS, MLSTM_HEAD_DIM).transpose(0, 2, 1, 3)

    qh = heads(q)
    kh = heads(k) * (MLSTM_HEAD_DIM ** -0.5)
    vh = heads(v)
    g = (gates.astype(f32) + gate_b.astype(f32)).reshape(B, L, 4, MLSTM_HEADS).transpose(2, 0, 3, 1)
    i_f, f_f, i_b, f_b = g[0], g[1], g[2], g[3]
    h_f = _mlstm_chunkwise(qh, kh, vh, i_f, jax.nn.log_sigmoid(f_f))

    def flip(a):
        return jnp.flip(a, axis=2)

    h_b = flip(_mlstm_chunkwise(flip(qh), flip(kh), flip(vh), flip(i_b), flip(jax.nn.log_sigmoid(f_b))))
    h_sum = (h_f + h_b).transpose(0, 2, 1, 3).reshape(B, L, D_MLSTM)
    return jax.nn.sigmoid(o_pre.astype(f32)) * h_sum


def _cross_attention(xn, memn, wq, wk, wv, wo):
    B, L, _ = xn.shape
    M = memn.shape[1]
    q = (xn @ wq).reshape(B, L, XATTN_HEADS, XATTN_HEAD_DIM)
    k = (memn @ wk).reshape(B, M, XATTN_HEADS, XATTN_HEAD_DIM)
    v = (memn @ wv).reshape(B, M, XATTN_HEADS, XATTN_HEAD_DIM)
    s = jnp.einsum('blhd,bmhd->bhlm', q, k).astype(jnp.float32) * (XATTN_HEAD_DIM ** -0.5)
    p = jax.nn.softmax(s, axis=-1).astype(v.dtype)
    o = jnp.einsum('bhlm,bmhd->blhd', p, v).reshape(B, L, D_MODEL)
    return o @ wo


def setup_inputs(seed: int = 0) -> dict:
    key = jax.random.key(seed)
    keys = iter(jax.random.split(key, 32))

    def nrm(shape, scale):
        return jax.random.normal(next(keys), shape, jnp.float32) * scale

    def gain(shape):
        return 1.0 + nrm(shape, 0.02)

    D = D_MODEL
    x = nrm((BATCH, SEQ, D), 1.0)
    mem = nrm((BATCH, N_MEM, D), 1.0)
    f_bias = jnp.linspace(3.0, 6.0, MLSTM_HEADS, dtype=jnp.float32)
    zero_b = jnp.zeros_like(f_bias)
    gate_noise = nrm((DEPTH, 4, MLSTM_HEADS), 0.1)
    ml_gate_b = (gate_noise + jnp.stack([zero_b, f_bias, zero_b, f_bias])).reshape(DEPTH, ML_GATE_COLS)
    return {
        'x': x,
        'mem': mem,
        'norm_mix_g': gain((DEPTH, D)),
        'w_in': nrm((DEPTH, D, D_IN_PROJ), D ** -0.5),
        'hy_conv_w': nrm((DEPTH, 3, HY_COLS), 3 ** -0.5),
        'hy_conv_b': nrm((DEPTH, HY_COLS), 0.02),
        'hy_filt_w1': nrm((DEPTH, FILTER_EMB, FILTER_HIDDEN), FILTER_EMB ** -0.5),
        'hy_filt_b1': nrm((DEPTH, FILTER_HIDDEN), 0.1),
        'hy_filt_freq1': gain((DEPTH, FILTER_HIDDEN)),
        'hy_filt_w2': nrm((DEPTH, FILTER_HIDDEN, FILTER_HIDDEN), FILTER_HIDDEN ** -0.5),
        'hy_filt_b2': nrm((DEPTH, FILTER_HIDDEN), 0.1),
        'hy_filt_freq2': gain((DEPTH, FILTER_HIDDEN)),
        'hy_filt_w3': nrm((DEPTH, FILTER_HIDDEN, FILTER_CH), FILTER_OUT_SCALE * FILTER_HIDDEN ** -0.5),
        'hy_skip': nrm((DEPTH, HYENA_ORDER, D_HYENA), 1.0),
        'hy_norm_g': gain((DEPTH, D_HYENA)),
        'ml_conv_w': nrm((DEPTH, 3, ML_QK_COLS), 3 ** -0.5),
        'ml_conv_b': nrm((DEPTH, ML_QK_COLS), 0.02),
        'ml_gate_b': ml_gate_b,
        'ml_norm_g': gain((DEPTH, D_MLSTM)),
        'w_out': nrm((DEPTH, D_MIX, D), D_MIX ** -0.5),
        'norm_x_g': gain((DEPTH, D)),
        'norm_mem_g': gain((DEPTH, D)),
        'xa_wq': nrm((DEPTH, D, D), D ** -0.5),
        'xa_wk': nrm((DEPTH, D, D), D ** -0.5),
        'xa_wv': nrm((DEPTH, D, D), D ** -0.5),
        'xa_wo': nrm((DEPTH, D, D), D ** -0.5),
        'norm_ff_g': gain((DEPTH, D)),
        'ff_w1': nrm((DEPTH, D, D_FF), D ** -0.5),
        'ff_w2': nrm((DEPTH, D_FF, D), D_FF ** -0.5),
        'final_norm_g': gain((D,)),
    }


def reference(x, mem, norm_mix_g, w_in, hy_conv_w, hy_conv_b, hy_filt_w1, hy_filt_b1, hy_filt_freq1,
              hy_filt_w2, hy_filt_b2, hy_filt_freq2, hy_filt_w3, hy_skip, hy_norm_g, ml_conv_w, ml_conv_b,
              ml_gate_b, ml_norm_g, w_out, norm_x_g, norm_mem_g, xa_wq, xa_wk, xa_wv, xa_wo, norm_ff_g,
              ff_w1, ff_w2, final_norm_g):
    L = x.shape[1]
    off = HY_COLS + ML_QK_COLS
    h = x
    for l in range(DEPTH):
        u = _rms_norm(h, norm_mix_g[l])
        proj = u @ w_in[l]
        hy = _short_conv_centred(proj[..., :HY_COLS], hy_conv_w[l], hy_conv_b[l])
        spec = _hyena_filter_spectrum(L, hy_filt_w1[l], hy_filt_b1[l], hy_filt_freq1[l],
                                      hy_filt_w2[l], hy_filt_b2[l], hy_filt_freq2[l], hy_filt_w3[l])
        y_hy = _group_rms_norm(_hyena(hy, spec, hy_skip[l]), hy_norm_g[l], HYENA_GROUPS)
        qk = jax.nn.silu(_short_conv_centred(proj[..., HY_COLS:off], ml_conv_w[l], ml_conv_b[l]))
        y_ml = _mlstm_bidirectional(qk[..., :D_MLSTM], qk[..., D_MLSTM:],
                                    proj[..., off:off + D_MLSTM],
                                    proj[..., off + D_MLSTM:off + 2 * D_MLSTM],
                                    proj[..., off + 2 * D_MLSTM:], ml_gate_b[l])
        y_ml = _group_rms_norm(y_ml, ml_norm_g[l], MLSTM_HEADS)
        mixed = jnp.concatenate([y_hy.astype(h.dtype), y_ml.astype(h.dtype)], axis=-1)
        h = h + mixed @ w_out[l]
        h = h + _cross_attention(_rms_norm(h, norm_x_g[l]), _rms_norm(mem, norm_mem_g[l]),
                                 xa_wq[l], xa_wk[l], xa_wv[l], xa_wo[l])
        h = h + jnp.square(jax.nn.relu(_rms_norm(h, norm_ff_g[l]) @ ff_w1[l])) @ ff_w2[l]
    return _rms_norm(h, final_norm_g)
```

```python
import functools
import math

import numpy as np
import jax
import jax.numpy as jnp
from jax import lax
from jax.experimental import pallas as pl
from jax.experimental.pallas import tpu as pltpu

F32 = jnp.float32
BF16 = jnp.bfloat16

EPS = 1e-6
HYENA_GROUPS = 8
MLSTM_HEADS = 4
XATTN_HEADS = 4
FILTER_BANDS = 16
DECAY_TARGET = 1e-2
SHORT_DECAY_PCT = 0.3
LONG_DECAY_PCT = 1.5

LANES = 128
VMEM_LIMIT = 56 * 1024 * 1024


def _params(*sem):
    return pltpu.CompilerParams(dimension_semantics=sem, vmem_limit_bytes=VMEM_LIMIT)


def _bdot(a, b):
    return jnp.dot(a.astype(BF16), b.astype(BF16), preferred_element_type=F32)


def _split2(x):
    hi = x.astype(BF16)
    lo = (x - hi.astype(F32)).astype(BF16)
    return hi, lo


def _split3(x):
    hi = x.astype(BF16)
    r = x - hi.astype(F32)
    mid = r.astype(BF16)
    lo = (r - mid.astype(F32)).astype(BF16)
    return hi, mid, lo


def _norm_matmul_kernel(x_ref, g_ref, w_ref, o_ref, xn_ref):
    @pl.when(pl.program_id(1) == 0)
    def _():
        x = x_ref[...]
        ms = jnp.mean(x * x, axis=-1, keepdims=True)
        xn_ref[...] = (x * lax.rsqrt(ms + EPS) * g_ref[...]).astype(BF16)

    o_ref[...] = jnp.dot(xn_ref[...], w_ref[...], preferred_element_type=F32)


def _norm_matmul(x, g, w, tm, tn):
    M, K = x.shape
    N = w.shape[1]
    return pl.pallas_call(
        _norm_matmul_kernel,
        grid=(M // tm, N // tn),
        in_specs=[pl.BlockSpec((tm, K), lambda i, j: (i, 0)),
                  pl.BlockSpec((1, K), lambda i, j: (0, 0)),
                  pl.BlockSpec((K, tn), lambda i, j: (0, j))],
        out_specs=pl.BlockSpec((tm, tn), lambda i, j: (i, j)),
        out_shape=jax.ShapeDtypeStruct((M, N), F32),
        scratch_shapes=[pltpu.VMEM((tm, K), BF16)],
        compiler_params=_params("parallel", "arbitrary"),
        name="norm_matmul",
    )(x, g.reshape(1, K), w)


def _short_conv_kernel(p_ref, w_ref, b_ref, o_ref, *, act, rows):
    L = p_ref.shape[1]
    w0, w1, w2, b = w_ref[0:1, :], w_ref[1:2, :], w_ref[2:3, :], b_ref[...]
    nblk = L // rows

    def body(r, carry):
        r0 = pl.multiple_of(r * rows, rows)
        u = p_ref[0, pl.ds(r0, rows), :]
        row = lax.broadcasted_iota(jnp.int32, u.shape, 0)
        before = p_ref[0, pl.ds(pl.multiple_of(jnp.maximum(r0 - 8, 0), 8), 8), :][7:8, :]
        after = p_ref[0, pl.ds(pl.multiple_of(jnp.minimum(r0 + rows, L - 8), 8), 8), :][0:1, :]
        before = jnp.where(r == 0, 0.0, before)
        after = jnp.where(r == nblk - 1, 0.0, after)
        prev = jnp.where(row == 0, before, pltpu.roll(u, 1, 0))
        nxt = jnp.where(row == rows - 1, after, pltpu.roll(u, rows - 1, 0))
        y = prev * w0 + u * w1 + nxt * w2 + b
        if act:
            y = y * jax.nn.sigmoid(y)
        o_ref[0, pl.ds(r0, rows), :] = y
        return carry

    lax.fori_loop(0, nblk, body, 0)


def _short_conv(proj, w, b, col0, ncols, act, tc=512, rows=256):
    B, L, _ = proj.shape
    cb0 = col0 // tc
    return pl.pallas_call(
        functools.partial(_short_conv_kernel, act=act, rows=rows),
        grid=(B, ncols // tc),
        in_specs=[pl.BlockSpec((1, L, tc), lambda bi, j: (bi, 0, cb0 + j)),
                  pl.BlockSpec((3, tc), lambda bi, j: (0, j)),
                  pl.BlockSpec((1, tc), lambda bi, j: (0, j))],
        out_specs=pl.BlockSpec((1, L, tc), lambda bi, j: (bi, 0, j)),
        out_shape=jax.ShapeDtypeStruct((B, L, ncols), F32),
        compiler_params=_params("parallel", "parallel"),
        name="short_conv",
    )(proj, w, b.reshape(1, ncols))


def _filter_kernel(feat_ref, w1_ref, b1_ref, f1_ref, w2_ref, b2_ref, f2_ref, w3_ref, dec_ref, dir_ref,
                   o_ref, *, L):
    tl = o_ref.shape[0]
    hp = lax.Precision.HIGHEST
    pos = (lax.broadcasted_iota(jnp.int32, (tl, LANES), 0) + pl.program_id(0) * tl).astype(F32)
    lane = lax.broadcasted_iota(jnp.int32, (tl, LANES), 1)
    t = pos * (1.0 / (L - 1))
    arg = feat_ref[...] * (pos * (2.0 * math.pi / L))
    z = jnp.where(lane < FILTER_BANDS, jnp.cos(arg),
                  jnp.where(lane < 2 * FILTER_BANDS, -jnp.sin(arg),
                            jnp.where(lane == 2 * FILTER_BANDS, t, 0.0)))
    hid = jnp.sin(f1_ref[...] * (jnp.dot(z, w1_ref[...], precision=hp, preferred_element_type=F32) + b1_ref[...]))
    hid = jnp.sin(f2_ref[...] * (jnp.dot(hid, w2_ref[...], precision=hp, preferred_element_type=F32) + b2_ref[...]))
    filt = jnp.dot(hid, w3_ref[...], precision=hp, preferred_element_type=F32)
    filt = filt * jnp.exp(-t[:, 0:1] * dec_ref[...])
    filt = jnp.where(pos[:, 0:1] == 0.0, filt * dir_ref[...], filt)
    o_ref[...] = filt


def _hyena_filters(L, w1, b1, fr1, w2, b2, fr2, w3, d_hyena, tl=512):
    n_emb, n_hid = w1.shape
    n_out = w3.shape[1]
    bands = jnp.linspace(1e-4, FILTER_BANDS - 1, FILTER_BANDS, dtype=F32)
    feat = jnp.zeros((1, LANES), F32).at[0, :FILTER_BANDS].set(bands).at[0, FILTER_BANDS:2 * FILTER_BANDS].set(bands)
    w1p = jnp.zeros((LANES, n_hid), F32).at[:n_emb - 1].set(w1[1:]).at[n_emb - 1].set(w1[0])
    max_decay = math.log(DECAY_TARGET) / SHORT_DECAY_PCT
    min_decay = math.log(DECAY_TARGET) / LONG_DECAY_PCT
    deltas = jnp.abs(jnp.linspace(min_decay, max_decay, d_hyena, dtype=F32))
    reps = n_out // d_hyena
    dec = jnp.tile(deltas, reps).reshape(1, n_out)
    dirmask = jnp.tile(jnp.concatenate([jnp.ones((d_hyena,), F32), jnp.zeros((d_hyena,), F32)]), reps // 2)
    full = lambda shape: pl.BlockSpec(shape, lambda i: (0,) * len(shape))
    return pl.pallas_call(
        functools.partial(_filter_kernel, L=L),
        grid=(L // tl,),
        in_specs=[full((1, LANES)), full((LANES, n_hid)), full((1, n_hid)), full((1, n_hid)),
                  full((n_hid, n_hid)), full((1, n_hid)), full((1, n_hid)), full((n_hid, n_out)),
                  full((1, n_out)), full((1, n_out))],
        out_specs=pl.BlockSpec((tl, n_out), lambda i: (i, 0)),
        out_shape=jax.ShapeDtypeStruct((L, n_out), F32),
        compiler_params=_params("parallel"),
        name="hyena_filters",
    )(feat, w1p, b1.reshape(1, -1), fr1.reshape(1, -1), w2, b2.reshape(1, -1), fr2.reshape(1, -1), w3,
      dec, dirmask.reshape(1, n_out))


MID_ROWS = 512


def _fft_tables(L):
    N = 2 * L
    N2 = LANES
    N1 = N // N2
    H1 = N1 // 2
    k1 = np.arange(N1)[:, None]
    n1 = np.arange(H1)[None, :]
    a1 = 2.0 * np.pi * ((k1 * n1) % N1) / N1
    c1, s1 = np.cos(a1), np.sin(a1)
    m1 = np.concatenate([c1, -s1], axis=0)
    minv = np.concatenate([c1.T, -s1.T], axis=1)
    n2 = np.arange(N2)[:, None]
    k2 = np.arange(N2)[None, :]
    a2 = 2.0 * np.pi * ((n2 * k2) % N2) / N2
    cg, sg = np.cos(a2), np.sin(a2)
    g2 = np.block([[cg, -sg], [sg, cg]])
    g2i = np.block([[cg, sg], [-sg, cg]])
    at = 2.0 * np.pi * (np.arange(N1)[:, None] * np.arange(N2)[None, :]) / N
    tw = np.concatenate([np.cos(at), -np.sin(at)], axis=1)
    tw = np.tile(tw, (MID_ROWS // N1, 1))
    as_bf = lambda a: jnp.asarray(a.astype(np.float32)).astype(BF16)
    return dict(m1=as_bf(m1), minv=as_bf(minv), g2=as_bf(g2), g2i=as_bf(g2i),
                tw=jnp.asarray(tw.astype(np.float32)), N1=N1, H1=H1)


def _fft_stage_a(get_pair, npairs, m1, tw_ref, ab_ref, n1):
    twr = tw_ref[0:n1, 0:LANES]
    twi = tw_ref[0:n1, LANES:2 * LANES]

    def body(j, carry):
        res = jnp.dot(m1, get_pair(j).astype(BF16), preferred_element_type=F32)
        base = pl.multiple_of(j * 2 * n1, 2 * n1)
        for c2 in range(2):
            ar = res[0:n1, LANES * c2:LANES * (c2 + 1)]
            ai = res[n1:2 * n1, LANES * c2:LANES * (c2 + 1)]
            rows = pl.ds(base + c2 * n1, n1)
            ab_ref[rows, 0:LANES] = ar * twr - ai * twi
            ab_ref[rows, LANES:2 * LANES] = ar * twi + ai * twr
        return carry

    lax.fori_loop(0, npairs, body, 0)


def _cmul(a, b):
    ar, ai = a[:, 0:LANES], a[:, LANES:2 * LANES]
    br, bi = b[:, 0:LANES], b[:, LANES:2 * LANES]
    return jnp.concatenate([ar * br - ai * bi, ar * bi + ai * br], axis=1)


def _cmul_conj(a, b):
    ar, ai = a[:, 0:LANES], a[:, LANES:2 * LANES]
    br, bi = b[:, 0:LANES], b[:, LANES:2 * LANES]
    return jnp.concatenate([ar * br + ai * bi, ai * br - ar * bi], axis=1)


def _fft_mid(ab_ref, get_h, g2, g2i, tw_ref, nrows):
    tw = tw_ref[...]

    def body(r, carry):
        rows = pl.ds(pl.multiple_of(r * MID_ROWS, MID_ROWS), MID_ROWS)
        spec = _bdot(ab_ref[rows, :], g2)
        prod = _cmul(spec, get_h(rows))
        back = _bdot(prod, g2i)
        ab_ref[rows, :] = _cmul_conj(back, tw)
        return carry

    lax.fori_loop(0, nrows // MID_ROWS, body, 0)


def _fft_stage_a_inv(put_pair, npairs, minv, ab_ref, n1):
    def body(j, carry):
        blk = ab_ref[pl.ds(pl.multiple_of(j * 2 * n1, 2 * n1), 2 * n1), :]
        re = jnp.concatenate([blk[0:n1, 0:LANES], blk[n1:2 * n1, 0:LANES]], axis=1)
        im = jnp.concatenate([blk[0:n1, LANES:2 * LANES], blk[n1:2 * n1, LANES:2 * LANES]], axis=1)
        rhs = jnp.concatenate([re, im], axis=0).astype(BF16)
        put_pair(j, jnp.dot(minv, rhs, preferred_element_type=F32))
        return carry

    lax.fori_loop(0, npairs, body, 0)


def _filter_spectrum_kernel(hf_ref, hb_ref, m1_ref, g2_ref, tw_ref, o_ref, af_ref, ab_ref, *, n1, inv_n):
    cp = hf_ref.shape[0]
    m1, g2 = m1_ref[...], g2_ref[...]
    _fft_stage_a(lambda j: hf_ref[j], cp, m1, tw_ref, af_ref, n1)
    _fft_stage_a(lambda j: hb_ref[j], cp, m1, tw_ref, ab_ref, n1)

    def body(r, carry):
        rows = pl.ds(pl.multiple_of(r * MID_ROWS, MID_ROWS), MID_ROWS)
        sf = _bdot(af_ref[rows, :], g2)
        sb = _bdot(ab_ref[rows, :], g2)
        o_ref[rows, :] = jnp.concatenate([sf[:, 0:LANES] + sb[:, 0:LANES],
                                          sf[:, LANES:] - sb[:, LANES:]], axis=1) * inv_n
        return carry

    lax.fori_loop(0, (cp * 2 * n1) // MID_ROWS, body, 0)


def _to_pairs(a):
    *lead, L, C = a.shape
    a = a.reshape(*lead, L // LANES, LANES, C // 2, 2)
    nl = len(lead)
    a = jnp.transpose(a, (*range(nl), nl + 2, nl, nl + 3, nl + 1))
    return a.reshape(*lead, C // 2, L // LANES, 2 * LANES)


def _from_pairs(a):
    *lead, CP, H1, _ = a.shape
    a = a.reshape(*lead, CP, H1, 2, LANES)
    nl = len(lead)
    a = jnp.transpose(a, (*range(nl), nl + 1, nl + 3, nl, nl + 2))
    return a.reshape(*lead, H1 * LANES, CP * 2)


def _filter_spectrum(filt, L, d_hyena, tabs, cb=64):
    n1, h1 = tabs["N1"], tabs["H1"]
    order = filt.shape[1] // (2 * d_hyena)
    f4 = filt.reshape(L, order, 2, d_hyena)
    hf = _to_pairs(f4[:, :, 0].reshape(L, order * d_hyena))
    hb = _to_pairs(f4[:, :, 1].reshape(L, order * d_hyena))
    nch = order * d_hyena
    cp = cb // 2
    pair_spec = pl.BlockSpec((cp, h1, 2 * LANES), lambda i: (i, 0, 0))
    full = lambda a: pl.BlockSpec(a.shape, lambda i: (0,) * a.ndim)
    return pl.pallas_call(
        functools.partial(_filter_spectrum_kernel, n1=n1, inv_n=1.0 / (2 * L)),
        grid=(nch // cb,),
        in_specs=[pair_spec, pair_spec, full(tabs["m1"]), full(tabs["g2"]), full(tabs["tw"])],
        out_specs=pl.BlockSpec((cb * n1, 2 * LANES), lambda i: (i, 0)),
        out_shape=jax.ShapeDtypeStruct((nch * n1, 2 * LANES), F32),
        scratch_shapes=[pltpu.VMEM((cb * n1, 2 * LANES), F32), pltpu.VMEM((cb * n1, 2 * LANES), F32)],
        compiler_params=_params("parallel"),
        name="filter_spectrum",
    )(hf, hb, tabs["m1"], tabs["g2"], tabs["tw"])


def _hyena_kernel(v_ref, x1_ref, x2_ref, h_ref, d_ref, m1_ref, minv_ref, g2_ref, g2i_ref, tw_ref,
                  o_ref, ab_ref, z_ref, *, n1):
    cp = v_ref.shape[1]
    m1, minv, g2, g2i = m1_ref[...], minv_ref[...], g2_ref[...], g2i_ref[...]
    nrows = cp * 2 * n1

    _fft_stage_a(lambda j: v_ref[0, j], cp, m1, tw_ref, ab_ref, n1)
    _fft_mid(ab_ref, lambda rows: h_ref[0, rows, :], g2, g2i, tw_ref, nrows)

    def put1(j, y):
        z_ref[j] = x1_ref[0, j] * (y + d_ref[0, j] * v_ref[0, j])

    _fft_stage_a_inv(put1, cp, minv, ab_ref, n1)

    _fft_stage_a(lambda j: z_ref[j], cp, m1, tw_ref, ab_ref, n1)
    _fft_mid(ab_ref, lambda rows: h_ref[1, rows, :], g2, g2i, tw_ref, nrows)

    def put2(j, y):
        o_ref[0, j] = x2_ref[0, j] * (y + d_ref[1, j] * z_ref[j])

    _fft_stage_a_inv(put2, cp, minv, ab_ref, n1)


def _hyena(hy, spec, skip, tabs, cb=64):
    B, L, C3 = hy.shape
    D = C3 // 3
    n1, h1 = tabs["N1"], tabs["H1"]
    v, x1, x2 = (_to_pairs(hy[..., i * D:(i + 1) * D]) for i in range(3))
    spec = spec.reshape(2, D * n1, 2 * LANES)
    d = jnp.repeat(skip.astype(F32).reshape(2, D // 2, 1, 2, 1), LANES, axis=-1).reshape(2, D // 2, 1, 2 * LANES)
    cp = cb // 2
    pair_spec = pl.BlockSpec((1, cp, h1, 2 * LANES), lambda c, b: (b, c, 0, 0))
    full = lambda a: pl.BlockSpec(a.shape, lambda c, b: (0,) * a.ndim)
    out = pl.pallas_call(
        functools.partial(_hyena_kernel, n1=n1),
        grid=(D // cb, B),
        in_specs=[pair_spec, pair_spec, pair_spec,
                  pl.BlockSpec((2, cb * n1, 2 * LANES), lambda c, b: (0, c, 0)),
                  pl.BlockSpec((2, cp, 1, 2 * LANES), lambda c, b: (0, c, 0, 0)),
                  full(tabs["m1"]), full(tabs["minv"]), full(tabs["g2"]), full(tabs["g2i"]), full(tabs["tw"])],
        out_specs=pair_spec,
        out_shape=jax.ShapeDtypeStruct((B, D // 2, h1, 2 * LANES), F32),
        scratch_shapes=[pltpu.VMEM((cb * n1, 2 * LANES), F32), pltpu.VMEM((cp, h1, 2 * LANES), F32)],
        compiler_params=_params("parallel", "arbitrary"),
        name="hyena_fftconv",
    )(v, x1, x2, spec, d, tabs["m1"], tabs["minv"], tabs["g2"], tabs["g2i"], tabs["tw"])
    return _from_pairs(out)


ML_CHUNK = 256


def _log_sigmoid(x):
    return jnp.minimum(x, 0.0) - jnp.log1p(jnp.exp(-jnp.abs(x)))


def _mlstm_direction(q, k, v, i_row, f_row, c_ref, m_ref, reverse, scale):
    c, dh = q.shape
    r = lax.broadcasted_iota(jnp.int32, (c, c), 0)
    s = lax.broadcasted_iota(jnp.int32, (c, c), 1)
    mask = (s >= r) if reverse else (s <= r)
    mask_b = mask.astype(BF16)
    logf = _log_sigmoid(f_row)
    dn_nt = (((1,), (1,)), ((), ()))
    dn_nn = (((1,), (0,)), ((), ()))
    f_col_rep = sum(lax.dot_general(mask_b, part, dn_nt, preferred_element_type=F32)
                    for part in _split3(jnp.broadcast_to(logf, (LANES, c))))
    mask_t = ((r >= s) if reverse else (r <= s)).astype(BF16)
    f_row_cum = sum(lax.dot_general(part, mask_t, dn_nn, preferred_element_type=F32)
                    for part in _split3(jnp.broadcast_to(logf, (16, c))))[0:1, :]
    g_row = i_row - f_row_cum
    m_prev = m_ref[0:1, 0:1]
    e = jnp.where(mask, g_row, -jnp.inf)
    a = jnp.maximum(jnp.max(e, axis=1, keepdims=True), m_prev)
    w = jnp.exp(e - a)
    sc = lax.dot_general(q.astype(BF16), k.astype(BF16), (((1,), (1,)), ((), ())), preferred_element_type=F32)
    p = sc * (w * scale)
    vaug = jnp.concatenate([v, jnp.ones_like(v)], axis=1)
    inter = jnp.exp(m_prev - a)
    tot = _bdot(p, vaug) + inter * _bdot(q, c_ref[...])
    num, den = tot[:, 0:dh], tot[:, dh:2 * dh]
    m_t = f_col_rep + a
    h = num / jnp.maximum(jnp.abs(den), jnp.exp(-m_t))
    a_end = jnp.maximum(jnp.max(g_row, axis=1, keepdims=True), m_prev)
    f_end = jnp.sum(logf, axis=1, keepdims=True)
    we = jnp.exp(g_row - a_end) * scale
    kw = k.T * we
    c_ref[...] = jnp.exp(m_prev - a_end) * c_ref[...] + _bdot(kw, vaug)
    m_ref[...] = jnp.broadcast_to(f_end + a_end, m_ref.shape)
    return h


def _mlstm_kernel(qf_ref, kf_ref, vf_ref, gf_ref, qb_ref, kb_ref, vb_ref, gb_ref, bias_ref,
                  hf_ref, hb_ref, cf_ref, mf_ref, cb_ref, mb_ref, *, scale):
    @pl.when(pl.program_id(2) == 0)
    def _():
        cf_ref[...] = jnp.zeros_like(cf_ref)
        cb_ref[...] = jnp.zeros_like(cb_ref)
        mf_ref[...] = jnp.zeros_like(mf_ref)
        mb_ref[...] = jnp.zeros_like(mb_ref)

    gf = gf_ref[0, 0] + bias_ref[0]
    gb = gb_ref[0, 0] + bias_ref[0]
    hf_ref[0] = _mlstm_direction(qf_ref[0], kf_ref[0], vf_ref[0], gf[0:1], gf[1:2], cf_ref, mf_ref, False, scale)
    hb_ref[0] = _mlstm_direction(qb_ref[0], kb_ref[0], vb_ref[0], gb[2:3], gb[3:4], cb_ref, mb_ref, True, scale)


def _mlstm(qk, proj, v_col0, gates, gate_b, d_ml):
    B, L, _ = qk.shape
    H = MLSTM_HEADS
    dh = d_ml // H
    c = min(ML_CHUNK, L)
    nc = L // c
    g = gates.reshape(B, L, 4, H).transpose(0, 3, 2, 1)
    bias = jnp.broadcast_to(gate_b.astype(F32).reshape(4, H).T[:, :, None], (H, 4, c))
    vb0 = v_col0 // dh
    fwd = lambda bi, h, ci: ci
    bwd = lambda bi, h, ci: nc - 1 - ci
    specs = []
    for pos in (fwd, bwd):
        specs += [pl.BlockSpec((1, c, dh), lambda bi, h, ci, pos=pos: (bi, pos(bi, h, ci), h)),
                  pl.BlockSpec((1, c, dh), lambda bi, h, ci, pos=pos: (bi, pos(bi, h, ci), H + h)),
                  pl.BlockSpec((1, c, dh), lambda bi, h, ci, pos=pos: (bi, pos(bi, h, ci), vb0 + h)),
                  pl.BlockSpec((1, 1, 4, c), lambda bi, h, ci, pos=pos: (bi, h, 0, pos(bi, h, ci)))]
    specs.append(pl.BlockSpec((1, 4, c), lambda bi, h, ci: (h, 0, 0)))
    out_shape = jax.ShapeDtypeStruct((B, L, d_ml), F32)
    return pl.pallas_call(
        functools.partial(_mlstm_kernel, scale=dh ** -0.5),
        grid=(B, H, nc),
        in_specs=specs,
        out_specs=[pl.BlockSpec((1, c, dh), lambda bi, h, ci: (bi, ci, h)),
                   pl.BlockSpec((1, c, dh), lambda bi, h, ci: (bi, nc - 1 - ci, h))],
        out_shape=[out_shape, out_shape],
        scratch_shapes=[pltpu.VMEM((dh, 2 * dh), F32), pltpu.VMEM((1, LANES), F32),
                        pltpu.VMEM((dh, 2 * dh), F32), pltpu.VMEM((1, LANES), F32)],
        compiler_params=_params("parallel", "parallel", "arbitrary"),
        name="mlstm_scan",
    )(qk, qk, proj, g, qk, qk, proj, g, bias)


def _group_rms(y, gain, bd, group):
    hi, lo = _split2(y * y)
    ss = jnp.dot(hi, bd, preferred_element_type=F32) + jnp.dot(lo, bd, preferred_element_type=F32)
    return y * lax.rsqrt(ss * (1.0 / group) + EPS) * gain


def _mix_out_kernel(z_ref, hf_ref, hb_ref, o_ref, x_ref, ghy_ref, gml_ref, why_ref, wml_ref, bdh_ref, bdm_ref,
                    out_ref, *, hy_group, ml_group):
    y_hy = _group_rms(z_ref[...], ghy_ref[...], bdh_ref[...], hy_group)
    y_ml = _group_rms(jax.nn.sigmoid(o_ref[...]) * (hf_ref[...] + hb_ref[...]), gml_ref[...], bdm_ref[...], ml_group)
    out_ref[...] = x_ref[...] + _bdot(y_hy, why_ref[...]) + _bdot(y_ml, wml_ref[...])


def _block_diag_ones(n, group):
    idx = np.arange(n) // group
    return jnp.asarray((idx[:, None] == idx[None, :]).astype(np.float32)).astype(BF16)


def _mix_out(z, hf, hb, proj, o_col0, x, g_hy, g_ml, w_out, tm=512):
    M, d_hy = z.shape
    d_ml = hf.shape[1]
    D = x.shape[1]
    ob = o_col0 // d_ml
    row = lambda n: pl.BlockSpec((tm, n), lambda i: (i, 0))
    full = lambda shape: pl.BlockSpec(shape, lambda i: (0, 0))
    hy_group, ml_group = d_hy // HYENA_GROUPS, d_ml // MLSTM_HEADS
    return pl.pallas_call(
        functools.partial(_mix_out_kernel, hy_group=hy_group, ml_group=ml_group),
        grid=(M // tm,),
        in_specs=[row(d_hy), row(d_ml), row(d_ml), pl.BlockSpec((tm, d_ml), lambda i: (i, ob)), row(D),
                  full((1, d_hy)), full((1, d_ml)), full((d_hy, D)), full((d_ml, D)),
                  full((d_hy, d_hy)), full((d_ml, d_ml))],
        out_specs=row(D),
        out_shape=jax.ShapeDtypeStruct((M, D), F32),
        compiler_params=_params("parallel"),
        name="mix_out",
    )(z, hf, hb, proj, x, g_hy.reshape(1, -1), g_ml.reshape(1, -1),
      w_out[:d_hy].astype(BF16), w_out[d_hy:].astype(BF16),
      _block_diag_ones(d_hy, hy_group), _block_diag_ones(d_ml, ml_group))


def _xattn_kernel(h_ref, g_ref, wq_ref, kv_ref, wo_ref, out_ref, *, heads, scale):
    x = h_ref[0]
    D = x.shape[1]
    dh = D // heads
    ms = jnp.mean(x * x, axis=-1, keepdims=True)
    xn = x * lax.rsqrt(ms + EPS) * g_ref[...]
    q = _bdot(xn, wq_ref[...])
    outs = []
    for hd in range(heads):
        qh = q[:, hd * dh:(hd + 1) * dh]
        kh = kv_ref[0, :, hd * dh:(hd + 1) * dh]
        vh = kv_ref[0, :, D + hd * dh:D + (hd + 1) * dh]
        s = lax.dot_general(qh.astype(BF16), kh.astype(BF16), (((1,), (1,)), ((), ())),
                            preferred_element_type=F32) * scale
        e = jnp.exp(s - jnp.max(s, axis=-1, keepdims=True))
        p = e / jnp.sum(e, axis=-1, keepdims=True)
        outs.append(_bdot(p, vh))
    o = jnp.concatenate(outs, axis=1)
    out_ref[0] = x + _bdot(o, wo_ref[...])


def _xattn(h, g, wq, kv, wo, tm=512):
    B, L, D = h.shape
    nm = kv.shape[1]
    full = lambda shape: pl.BlockSpec(shape, lambda bi, i: (0,) * len(shape))
    return pl.pallas_call(
        functools.partial(_xattn_kernel, heads=XATTN_HEADS, scale=(D // XATTN_HEADS) ** -0.5),
        grid=(B, L // tm),
        in_specs=[pl.BlockSpec((1, tm, D), lambda bi, i: (bi, i, 0)), full((1, D)), full((D, D)),
                  pl.BlockSpec((1, nm, 2 * D), lambda bi, i: (bi, 0, 0)), full((D, D))],
        out_specs=pl.BlockSpec((1, tm, D), lambda bi, i: (bi, i, 0)),
        out_shape=jax.ShapeDtypeStruct((B, L, D), F32),
        compiler_params=_params("parallel", "parallel"),
        name="xattn",
    )(h, g.reshape(1, D), wq.astype(BF16), kv, wo.astype(BF16))


def _mlp_kernel(h_ref, g_ref, w1_ref, w2_ref, gf_ref, out_ref, xn_ref, acc_ref, *, final_norm):
    j = pl.program_id(1)

    @pl.when(j == 0)
    def _():
        x = h_ref[...]
        ms = jnp.mean(x * x, axis=-1, keepdims=True)
        xn_ref[...] = (x * lax.rsqrt(ms + EPS) * g_ref[...]).astype(BF16)
        acc_ref[...] = x

    a = jnp.maximum(jnp.dot(xn_ref[...], w1_ref[...], preferred_element_type=F32), 0.0)
    acc_ref[...] += _bdot(a * a, w2_ref[...])

    @pl.when(j == pl.num_programs(1) - 1)
    def _():
        y = acc_ref[...]
        if final_norm:
            ms = jnp.mean(y * y, axis=-1, keepdims=True)
            y = y * lax.rsqrt(ms + EPS) * gf_ref[...]
        out_ref[...] = y


def _mlp(h, g, w1, w2, gf, final_norm, tm=1024, tf=512):
    M, D = h.shape
    dff = w1.shape[1]
    return pl.pallas_call(
        functools.partial(_mlp_kernel, final_norm=final_norm),
        grid=(M // tm, dff // tf),
        in_specs=[pl.BlockSpec((tm, D), lambda i, j: (i, 0)), pl.BlockSpec((1, D), lambda i, j: (0, 0)),
                  pl.BlockSpec((D, tf), lambda i, j: (0, j)), pl.BlockSpec((tf, D), lambda i, j: (j, 0)),
                  pl.BlockSpec((1, D), lambda i, j: (0, 0))],
        out_specs=pl.BlockSpec((tm, D), lambda i, j: (i, 0)),
        out_shape=jax.ShapeDtypeStruct((M, D), F32),
        scratch_shapes=[pltpu.VMEM((tm, D), BF16), pltpu.VMEM((tm, D), F32)],
        compiler_params=_params("parallel", "arbitrary"),
        name="mlp",
    )(h, g.reshape(1, D), w1.astype(BF16), w2.astype(BF16), gf.reshape(1, D))


def _pad_cols(w, n):
    return jnp.pad(w, ((0, 0), (0, n - w.shape[1])))


def _layer(h, mem, p, l):
    B, L, D = h.shape
    M = B * L
    d_hy = p["hy_norm_g"].shape[1]
    d_ml = p["ml_norm_g"].shape[1]
    hy_cols = 3 * d_hy
    qk_cols = 2 * d_ml
    v_col0 = hy_cols + qk_cols
    o_col0 = v_col0 + d_ml
    g_col0 = o_col0 + d_ml
    n_gate = 4 * MLSTM_HEADS

    tn = 768
    n_in = -(-(g_col0 + n_gate) // tn) * tn
    proj = _norm_matmul(h.reshape(M, D), p["norm_mix_g"][l], _pad_cols(p["w_in"][l], n_in).astype(BF16),
                        tm=min(512, M), tn=tn).reshape(B, L, n_in)

    hy = _short_conv(proj, p["hy_conv_w"][l], p["hy_conv_b"][l], 0, hy_cols, act=False)
    tabs = _fft_tables(L)
    filt = _hyena_filters(L, p["hy_filt_w1"][l], p["hy_filt_b1"][l], p["hy_filt_freq1"][l], p["hy_filt_w2"][l],
                          p["hy_filt_b2"][l], p["hy_filt_freq2"][l], p["hy_filt_w3"][l], d_hy,
                          tl=min(512, L))
    spec = _filter_spectrum(filt, L, d_hy, tabs)
    z_hy = _hyena(hy, spec, p["hy_skip"][l], tabs)

    qk = _short_conv(proj, p["ml_conv_w"][l], p["ml_conv_b"][l], hy_cols, qk_cols, act=True)
    gates = proj[..., g_col0:g_col0 + n_gate]
    h_f, h_b = _mlstm(qk, proj, v_col0, gates, p["ml_gate_b"][l], d_ml)

    h1 = _mix_out(z_hy.reshape(M, d_hy), h_f.reshape(M, d_ml), h_b.reshape(M, d_ml), proj.reshape(M, n_in),
                  o_col0, h.reshape(M, D), p["hy_norm_g"][l], p["ml_norm_g"][l], p["w_out"][l],
                  tm=min(512, M))

    nm = mem.shape[1]
    wkv = jnp.concatenate([p["xa_wk"][l], p["xa_wv"][l]], axis=1).astype(BF16)
    kv = _norm_matmul(mem.reshape(B * nm, D), p["norm_mem_g"][l], wkv, tm=min(512, B * nm), tn=512)
    h2 = _xattn(h1.reshape(B, L, D), p["norm_x_g"][l], p["xa_wq"][l], kv.reshape(B, nm, 2 * D), p["xa_wo"][l],
                tm=min(512, L))
    return h2


def kernel(x, mem, norm_mix_g, w_in, hy_conv_w, hy_conv_b, hy_filt_w1, hy_filt_b1, hy_filt_freq1, hy_filt_w2,
           hy_filt_b2, hy_filt_freq2, hy_filt_w3, hy_skip, hy_norm_g, ml_conv_w, ml_conv_b, ml_gate_b, ml_norm_g,
           w_out, norm_x_g, norm_mem_g, xa_wq, xa_wk, xa_wv, xa_wo, norm_ff_g, ff_w1, ff_w2, final_norm_g):
    p = dict(norm_mix_g=norm_mix_g, w_in=w_in, hy_conv_w=hy_conv_w, hy_conv_b=hy_conv_b, hy_filt_w1=hy_filt_w1,
             hy_filt_b1=hy_filt_b1, hy_filt_freq1=hy_filt_freq1, hy_filt_w2=hy_filt_w2, hy_filt_b2=hy_filt_b2,
             hy_filt_freq2=hy_filt_freq2, hy_filt_w3=hy_filt_w3, hy_skip=hy_skip, hy_norm_g=hy_norm_g,
             ml_conv_w=ml_conv_w, ml_conv_b=ml_conv_b, ml_gate_b=ml_gate_b, ml_norm_g=ml_norm_g, w_out=w_out,
             norm_x_g=norm_x_g, norm_mem_g=norm_mem_g, xa_wq=xa_wq, xa_wk=xa_wk, xa_wv=xa_wv, xa_wo=xa_wo)
    B, L, D = x.shape
    depth = w_in.shape[0]
    h = x
    for l in range(depth):
        h = _layer(h, mem, p, l)
        h = _mlp(h.reshape(B * L, D), norm_ff_g[l], ff_w1[l], ff_w2[l], final_norm_g, final_norm=l == depth - 1,
                 tm=min(1024, B * L)).reshape(B, L, D)
    return h
```

```python
import functools
import math

import numpy as np
import jax
import jax.numpy as jnp
from jax import lax
from jax.experimental import pallas as pl
from jax.experimental.pallas import tpu as pltpu

F32 = jnp.float32
BF16 = jnp.bfloat16

EPS = 1e-6
HYENA_GROUPS = 8
MLSTM_HEADS = 4
XATTN_HEADS = 4
FILTER_BANDS = 16
DECAY_TARGET = 1e-2
SHORT_DECAY_PCT = 0.3
LONG_DECAY_PCT = 1.5

LANES = 128
VMEM_LIMIT = 56 * 1024 * 1024


def _params(*sem):
    return pltpu.CompilerParams(dimension_semantics=sem, vmem_limit_bytes=VMEM_LIMIT)


def _bdot(a, b):
    return jnp.dot(a.astype(BF16), b.astype(BF16), preferred_element_type=F32)


def _split2(x):
    hi = x.astype(BF16)
    lo = (x - hi.astype(F32)).astype(BF16)
    return hi, lo


def _split3(x):
    hi = x.astype(BF16)
    r = x - hi.astype(F32)
    mid = r.astype(BF16)
    lo = (r - mid.astype(F32)).astype(BF16)
    return hi, mid, lo


def _norm_matmul_kernel(x_ref, g_ref, w_ref, o_ref, xn_ref):
    @pl.when(pl.program_id(1) == 0)
    def _():
        x = x_ref[...]
        ms = jnp.mean(x * x, axis=-1, keepdims=True)
        xn_ref[...] = (x * lax.rsqrt(ms + EPS) * g_ref[...]).astype(BF16)

    o_ref[...] = jnp.dot(xn_ref[...], w_ref[...], preferred_element_type=F32)


def _norm_matmul(x, g, w, tm, tn):
    M, K = x.shape
    N = w.shape[1]
    return pl.pallas_call(
        _norm_matmul_kernel,
        grid=(M // tm, N // tn),
        in_specs=[pl.BlockSpec((tm, K), lambda i, j: (i, 0)),
                  pl.BlockSpec((1, K), lambda i, j: (0, 0)),
                  pl.BlockSpec((K, tn), lambda i, j: (0, j))],
        out_specs=pl.BlockSpec((tm, tn), lambda i, j: (i, j)),
        out_shape=jax.ShapeDtypeStruct((M, N), F32),
        scratch_shapes=[pltpu.VMEM((tm, K), BF16)],
        compiler_params=_params("parallel", "arbitrary"),
        name="norm_matmul",
    )(x, g.reshape(1, K), w)


def _short_conv_kernel(p_ref, w_ref, b_ref, o_ref, *, act, rows):
    L = p_ref.shape[1]
    w0, w1, w2, b = w_ref[0:1, :], w_ref[1:2, :], w_ref[2:3, :], b_ref[...]
    nblk = L // rows

    def body(r, carry):
        r0 = pl.multiple_of(r * rows, rows)
        u = p_ref[0, pl.ds(r0, rows), :]
        row = lax.broadcasted_iota(jnp.int32, u.shape, 0)
        before = p_ref[0, pl.ds(pl.multiple_of(jnp.maximum(r0 - 8, 0), 8), 8), :][7:8, :]
        after = p_ref[0, pl.ds(pl.multiple_of(jnp.minimum(r0 + rows, L - 8), 8), 8), :][0:1, :]
        before = jnp.where(r == 0, 0.0, before)
        after = jnp.where(r == nblk - 1, 0.0, after)
        prev = jnp.where(row == 0, before, pltpu.roll(u, 1, 0))
        nxt = jnp.where(row == rows - 1, after, pltpu.roll(u, rows - 1, 0))
        y = prev * w0 + u * w1 + nxt * w2 + b
        if act:
            y = y * jax.nn.sigmoid(y)
        o_ref[0, pl.ds(r0, rows), :] = y
        return carry

    lax.fori_loop(0, nblk, body, 0)


def _short_conv(proj, w, b, col0, ncols, act, tc=512, rows=256):
    B, L, _ = proj.shape
    cb0 = col0 // tc
    return pl.pallas_call(
        functools.partial(_short_conv_kernel, act=act, rows=rows),
        grid=(B, ncols // tc),
        in_specs=[pl.BlockSpec((1, L, tc), lambda bi, j: (bi, 0, cb0 + j)),
                  pl.BlockSpec((3, tc), lambda bi, j: (0, j)),
                  pl.BlockSpec((1, tc), lambda bi, j: (0, j))],
        out_specs=pl.BlockSpec((1, L, tc), lambda bi, j: (bi, 0, j)),
        out_shape=jax.ShapeDtypeStruct((B, L, ncols), F32),
        compiler_params=_params("parallel", "parallel"),
        name="short_conv",
    )(proj, w, b.reshape(1, ncols))


def _filter_kernel(feat_ref, w1_ref, b1_ref, f1_ref, w2_ref, b2_ref, f2_ref, w3_ref, dec_ref, dir_ref,
                   o_ref, *, L):
    tl = o_ref.shape[0]
    hp = lax.Precision.HIGHEST
    pos = (lax.broadcasted_iota(jnp.int32, (tl, LANES), 0) + pl.program_id(0) * tl).astype(F32)
    lane = lax.broadcasted_iota(jnp.int32, (tl, LANES), 1)
    t = pos * (1.0 / (L - 1))
    arg = feat_ref[...] * (pos * (2.0 * math.pi / L))
    z = jnp.where(lane < FILTER_BANDS, jnp.cos(arg),
                  jnp.where(lane < 2 * FILTER_BANDS, -jnp.sin(arg),
                            jnp.where(lane == 2 * FILTER_BANDS, t, 0.0)))
    hid = jnp.sin(f1_ref[...] * (jnp.dot(z, w1_ref[...], precision=hp, preferred_element_type=F32) + b1_ref[...]))
    hid = jnp.sin(f2_ref[...] * (jnp.dot(hid, w2_ref[...], precision=hp, preferred_element_type=F32) + b2_ref[...]))
    filt = jnp.dot(hid, w3_ref[...], precision=hp, preferred_element_type=F32)
    filt = filt * jnp.exp(-t[:, 0:1] * dec_ref[...])
    filt = jnp.where(pos[:, 0:1] == 0.0, filt * dir_ref[...], filt)
    o_ref[...] = filt


def _hyena_filters(L, w1, b1, fr1, w2, b2, fr2, w3, d_hyena, tl=512):
    n_emb, n_hid = w1.shape
    n_out = w3.shape[1]
    bands = jnp.linspace(1e-4, FILTER_BANDS - 1, FILTER_BANDS, dtype=F32)
    feat = jnp.zeros((1, LANES), F32).at[0, :FILTER_BANDS].set(bands).at[0, FILTER_BANDS:2 * FILTER_BANDS].set(bands)
    w1p = jnp.zeros((LANES, n_hid), F32).at[:n_emb - 1].set(w1[1:]).at[n_emb - 1].set(w1[0])
    max_decay = math.log(DECAY_TARGET) / SHORT_DECAY_PCT
    min_decay = math.log(DECAY_TARGET) / LONG_DECAY_PCT
    deltas = jnp.abs(jnp.linspace(min_decay, max_decay, d_hyena, dtype=F32))
    reps = n_out // d_hyena
    dec = jnp.tile(deltas, reps).reshape(1, n_out)
    dirmask = jnp.tile(jnp.concatenate([jnp.ones((d_hyena,), F32), jnp.zeros((d_hyena,), F32)]), reps // 2)
    full = lambda shape: pl.BlockSpec(shape, lambda i: (0,) * len(shape))
    return pl.pallas_call(
        functools.partial(_filter_kernel, L=L),
        grid=(L // tl,),
        in_specs=[full((1, LANES)), full((LANES, n_hid)), full((1, n_hid)), full((1, n_hid)),
                  full((n_hid, n_hid)), full((1, n_hid)), full((1, n_hid)), full((n_hid, n_out)),
                  full((1, n_out)), full((1, n_out))],
        out_specs=pl.BlockSpec((tl, n_out), lambda i: (i, 0)),
        out_shape=jax.ShapeDtypeStruct((L, n_out), F32),
        compiler_params=_params("parallel"),
        name="hyena_filters",
    )(feat, w1p, b1.reshape(1, -1), fr1.reshape(1, -1), w2, b2.reshape(1, -1), fr2.reshape(1, -1), w3,
      dec, dirmask.reshape(1, n_out))


MID_ROWS = 4096


def _fft_tables(L):
    N = 2 * L
    N2 = LANES
    N1 = N // N2
    H1 = N1 // 2
    k1 = np.arange(N1)[:, None]
    n1 = np.arange(H1)[None, :]
    a1 = 2.0 * np.pi * ((k1 * n1) % N1) / N1
    c1, s1 = np.cos(a1), np.sin(a1)
    m1 = np.concatenate([c1, -s1], axis=0)
    minv = np.concatenate([c1.T, -s1.T], axis=1)
    n2 = np.arange(N2)[:, None]
    k2 = np.arange(N2)[None, :]
    a2 = 2.0 * np.pi * ((n2 * k2) % N2) / N2
    cg, sg = np.cos(a2), np.sin(a2)
    g2 = np.block([[cg, -sg], [sg, cg]])
    g2i = np.block([[cg, sg], [-sg, cg]])
    at = 2.0 * np.pi * (np.arange(N1)[:, None] * np.arange(N2)[None, :]) / N
    tw = np.concatenate([np.cos(at), -np.sin(at)], axis=1)
    as_bf = lambda a: jnp.asarray(a.astype(np.float32)).astype(BF16)
    return dict(m1=as_bf(m1), minv=as_bf(minv), g2=as_bf(g2), g2i=as_bf(g2i),
                tw=jnp.asarray(tw.astype(np.float32)), N1=N1, H1=H1)


S_PITCH = LANES + 8
GROUP = 8
STAGE_UNROLL = 2


SLAB_UNROLL = 4


def _slab_transpose_in(get_slab, s_ref, h1):
    def body(j, carry):
        for u in range(SLAB_UNROLL):
            i = j * SLAB_UNROLL + u
            s_ref[pl.ds(pl.multiple_of(i * S_PITCH, 8), LANES), :] = get_slab(i).T
        return carry

    lax.fori_loop(0, h1 // SLAB_UNROLL, body, 0)


def _slab_transpose_out(s_ref, put_slab, h1):
    def body(j, carry):
        for u in range(SLAB_UNROLL):
            i = j * SLAB_UNROLL + u
            put_slab(i, s_ref[pl.ds(pl.multiple_of(i * S_PITCH, 8), LANES), :].T)
        return carry

    lax.fori_loop(0, h1 // SLAB_UNROLL, body, 0)


def _slab_rows(i):
    return pl.ds(pl.multiple_of(i * LANES, LANES), LANES)


def _fft_stage_a(s_ref, nch, m1, tw_ref, ab_ref, n1):
    h1 = n1 // 2
    twr = tw_ref[0:n1, 0:LANES]
    twi = tw_ref[0:n1, LANES:2 * LANES]

    def body(g, carry):
        for u in range(STAGE_UNROLL):
            c0 = (g * STAGE_UNROLL + u) * GROUP
            rhs = jnp.concatenate([s_ref[pl.ds(c0 + i, h1, stride=S_PITCH), :] for i in range(GROUP)], axis=1)
            res = jnp.dot(m1, rhs.astype(BF16), preferred_element_type=F32)
            for i in range(GROUP):
                ar = res[0:n1, LANES * i:LANES * (i + 1)]
                ai = res[n1:2 * n1, LANES * i:LANES * (i + 1)]
                rows = pl.ds(pl.multiple_of((c0 + i) * n1, n1), n1)
                ab_ref[rows, 0:LANES] = ar * twr - ai * twi
                ab_ref[rows, LANES:2 * LANES] = ar * twi + ai * twr
        return carry

    lax.fori_loop(0, nch // (GROUP * STAGE_UNROLL), body, 0)


def _cmul(a, b):
    ar, ai = a[:, 0:LANES], a[:, LANES:2 * LANES]
    br, bi = b[:, 0:LANES], b[:, LANES:2 * LANES]
    return jnp.concatenate([ar * br - ai * bi, ar * bi + ai * br], axis=1)


def _fft_mid(ab_ref, get_h, g2, g2i, nrows):
    chunk = min(MID_ROWS, nrows)

    def body(r, carry):
        rows = pl.ds(pl.multiple_of(r * chunk, chunk), chunk)
        spec = _bdot(ab_ref[rows, :], g2)
        ab_ref[rows, :] = _bdot(_cmul(spec, get_h(rows)), g2i)
        return carry

    lax.fori_loop(0, nrows // chunk, body, 0)


def _fft_stage_a_inv(ab_ref, nch, minv, tw_ref, s_ref, n1):
    h1 = n1 // 2
    twr = tw_ref[0:n1, 0:LANES]
    twi = tw_ref[0:n1, LANES:2 * LANES]

    def body(g, carry):
        for u in range(STAGE_UNROLL):
            c0 = (g * STAGE_UNROLL + u) * GROUP
            blk = ab_ref[pl.ds(pl.multiple_of(c0 * n1, GROUP * n1), GROUP * n1), :]
            re, im = [], []
            for i in range(GROUP):
                br = blk[i * n1:(i + 1) * n1, 0:LANES]
                bi = blk[i * n1:(i + 1) * n1, LANES:2 * LANES]
                re.append(br * twr + bi * twi)
                im.append(bi * twr - br * twi)
            rhs = jnp.concatenate([jnp.concatenate(re, axis=1), jnp.concatenate(im, axis=1)], axis=0)
            y = jnp.dot(minv, rhs.astype(BF16), preferred_element_type=F32)
            for i in range(GROUP):
                s_ref[pl.ds(c0 + i, h1, stride=S_PITCH), :] = y[:, LANES * i:LANES * (i + 1)]
        return carry

    lax.fori_loop(0, nch // (GROUP * STAGE_UNROLL), body, 0)


def _filter_spectrum_kernel(hf_ref, hb_ref, m1_ref, g2_ref, tw_ref, o_ref, s_ref, af_ref, ab_ref, *, n1, inv_n):
    nch = hf_ref.shape[1]
    h1 = n1 // 2
    m1, g2 = m1_ref[...], g2_ref[...]
    _slab_transpose_in(lambda i: hf_ref[_slab_rows(i), :], s_ref, h1)
    _fft_stage_a(s_ref, nch, m1, tw_ref, af_ref, n1)
    _slab_transpose_in(lambda i: hb_ref[_slab_rows(i), :], s_ref, h1)
    _fft_stage_a(s_ref, nch, m1, tw_ref, ab_ref, n1)

    chunk = min(MID_ROWS, nch * n1)

    def body(r, carry):
        rows = pl.ds(pl.multiple_of(r * chunk, chunk), chunk)
        sf = _bdot(af_ref[rows, :], g2)
        sb = _bdot(ab_ref[rows, :], g2)
        o_ref[rows, :] = jnp.concatenate([sf[:, 0:LANES] + sb[:, 0:LANES],
                                          sf[:, LANES:] - sb[:, LANES:]], axis=1) * inv_n
        return carry

    lax.fori_loop(0, (nch * n1) // chunk, body, 0)


def _filter_spectrum(filt, L, d_hyena, tabs):
    n1, h1 = tabs["N1"], tabs["H1"]
    order = filt.shape[1] // (2 * d_hyena)
    cb = LANES
    nblk = d_hyena // cb
    full = lambda a: pl.BlockSpec(a.shape, lambda o, c: (0,) * a.ndim)
    return pl.pallas_call(
        functools.partial(_filter_spectrum_kernel, n1=n1, inv_n=1.0 / (2 * L)),
        grid=(order, nblk),
        in_specs=[pl.BlockSpec((L, cb), lambda o, c: (0, o * 2 * nblk + c)),
                  pl.BlockSpec((L, cb), lambda o, c: (0, o * 2 * nblk + nblk + c)),
                  full(tabs["m1"]), full(tabs["g2"]), full(tabs["tw"])],
        out_specs=pl.BlockSpec((cb * n1, 2 * LANES), lambda o, c: (o * nblk + c, 0)),
        out_shape=jax.ShapeDtypeStruct((order * d_hyena * n1, 2 * LANES), F32),
        scratch_shapes=[pltpu.VMEM((h1 * S_PITCH, LANES), F32),
                        pltpu.VMEM((cb * n1, 2 * LANES), F32), pltpu.VMEM((cb * n1, 2 * LANES), F32)],
        compiler_params=_params("parallel", "parallel"),
        name="filter_spectrum",
    )(filt, filt, tabs["m1"], tabs["g2"], tabs["tw"])


def _hyena_kernel(v_ref, x1_ref, x2_ref, h_ref, d_ref, m1_ref, minv_ref, g2_ref, g2i_ref, tw_ref,
                  o_ref, s_ref, ab_ref, z_ref, *, n1):
    nch = v_ref.shape[2]
    h1 = n1 // 2
    m1, minv, g2, g2i = m1_ref[...], minv_ref[...], g2_ref[...], g2i_ref[...]

    def long_conv(get_slab, order, put_slab):
        _slab_transpose_in(get_slab, s_ref, h1)
        _fft_stage_a(s_ref, nch, m1, tw_ref, ab_ref, n1)
        _fft_mid(ab_ref, lambda rows: h_ref[order, rows, :], g2, g2i, nch * n1)
        _fft_stage_a_inv(ab_ref, nch, minv, tw_ref, s_ref, n1)
        _slab_transpose_out(s_ref, put_slab, h1)

    def put1(i, conv):
        rows = _slab_rows(i)
        z_ref[rows, :] = x1_ref[0, rows, :] * (conv + d_ref[0:1, :] * v_ref[0, rows, :])

    long_conv(lambda i: v_ref[0, _slab_rows(i), :], 0, put1)

    def put2(i, conv):
        rows = _slab_rows(i)
        o_ref[0, rows, :] = x2_ref[0, rows, :] * (conv + d_ref[1:2, :] * z_ref[rows, :])

    long_conv(lambda i: z_ref[_slab_rows(i), :], 1, put2)


def _hyena(hy, spec, skip, tabs):
    B, L, C3 = hy.shape
    D = C3 // 3
    n1, h1 = tabs["N1"], tabs["H1"]
    cb = LANES
    nblk = D // cb
    spec = spec.reshape(2, D * n1, 2 * LANES)
    col = lambda part: pl.BlockSpec((1, L, cb), lambda c, b: (b, 0, part * nblk + c))
    full = lambda a: pl.BlockSpec(a.shape, lambda c, b: (0,) * a.ndim)
    return pl.pallas_call(
        functools.partial(_hyena_kernel, n1=n1),
        grid=(nblk, B),
        in_specs=[col(0), col(1), col(2),
                  pl.BlockSpec((2, cb * n1, 2 * LANES), lambda c, b: (0, c, 0), pipeline_mode=pl.Buffered(1)),
                  pl.BlockSpec((2, cb), lambda c, b: (0, c)),
                  full(tabs["m1"]), full(tabs["minv"]), full(tabs["g2"]), full(tabs["g2i"]), full(tabs["tw"])],
        out_specs=pl.BlockSpec((1, L, cb), lambda c, b: (b, 0, c)),
        out_shape=jax.ShapeDtypeStruct((B, L, D), F32),
        scratch_shapes=[pltpu.VMEM((h1 * S_PITCH, LANES), F32), pltpu.VMEM((cb * n1, 2 * LANES), F32),
                        pltpu.VMEM((L, cb), F32)],
        compiler_params=_params("parallel", "arbitrary"),
        name="hyena_fftconv",
    )(hy, hy, hy, spec, skip.astype(F32), tabs["m1"], tabs["minv"], tabs["g2"], tabs["g2i"], tabs["tw"])


ML_CHUNK = 256


def _log_sigmoid(x):
    return jnp.minimum(x, 0.0) - jnp.log1p(jnp.exp(-jnp.abs(x)))


def _mlstm_direction(q, k, v, i_row, f_row, c_ref, m_ref, reverse, scale):
    c, dh = q.shape
    r = lax.broadcasted_iota(jnp.int32, (c, c), 0)
    s = lax.broadcasted_iota(jnp.int32, (c, c), 1)
    mask = (s >= r) if reverse else (s <= r)
    mask_b = mask.astype(BF16)
    logf = _log_sigmoid(f_row)
    dn_nt = (((1,), (1,)), ((), ()))
    dn_nn = (((1,), (0,)), ((), ()))
    f_col_rep = sum(lax.dot_general(mask_b, part, dn_nt, preferred_element_type=F32)
                    for part in _split3(jnp.broadcast_to(logf, (LANES, c))))
    mask_t = ((r >= s) if reverse else (r <= s)).astype(BF16)
    f_row_cum = sum(lax.dot_general(part, mask_t, dn_nn, preferred_element_type=F32)
                    for part in _split3(jnp.broadcast_to(logf, (16, c))))[0:1, :]
    g_row = i_row - f_row_cum
    m_prev = m_ref[0:1, 0:1]
    e = jnp.where(mask, g_row, -jnp.inf)
    a = jnp.maximum(jnp.max(e, axis=1, keepdims=True), m_prev)
    w = jnp.exp(e - a)
    sc = lax.dot_general(q.astype(BF16), k.astype(BF16), (((1,), (1,)), ((), ())), preferred_element_type=F32)
    p = sc * (w * scale)
    vaug = jnp.concatenate([v, jnp.ones_like(v)], axis=1)
    inter = jnp.exp(m_prev - a)
    tot = _bdot(p, vaug) + inter * _bdot(q, c_ref[...])
    num, den = tot[:, 0:dh], tot[:, dh:2 * dh]
    m_t = f_col_rep + a
    h = num / jnp.maximum(jnp.abs(den), jnp.exp(-m_t))
    a_end = jnp.maximum(jnp.max(g_row, axis=1, keepdims=True), m_prev)
    f_end = jnp.sum(logf, axis=1, keepdims=True)
    we = jnp.exp(g_row - a_end) * scale
    kw = k.T * we
    c_ref[...] = jnp.exp(m_prev - a_end) * c_ref[...] + _bdot(kw, vaug)
    m_ref[...] = jnp.broadcast_to(f_end + a_end, m_ref.shape)
    return h


def _mlstm_kernel(qf_ref, kf_ref, vf_ref, gf_ref, qb_ref, kb_ref, vb_ref, gb_ref, bias_ref,
                  hf_ref, hb_ref, cf_ref, mf_ref, cb_ref, mb_ref, *, scale):
    @pl.when(pl.program_id(2) == 0)
    def _():
        cf_ref[...] = jnp.zeros_like(cf_ref)
        cb_ref[...] = jnp.zeros_like(cb_ref)
        mf_ref[...] = jnp.zeros_like(mf_ref)
        mb_ref[...] = jnp.zeros_like(mb_ref)

    gf = gf_ref[0, 0] + bias_ref[0]
    gb = gb_ref[0, 0] + bias_ref[0]
    hf_ref[0] = _mlstm_direction(qf_ref[0], kf_ref[0], vf_ref[0], gf[0:1], gf[1:2], cf_ref, mf_ref, False, scale)
    hb_ref[0] = _mlstm_direction(qb_ref[0], kb_ref[0], vb_ref[0], gb[2:3], gb[3:4], cb_ref, mb_ref, True, scale)


def _mlstm(qk, proj, v_col0, gates, gate_b, d_ml):
    B, L, _ = qk.shape
    H = MLSTM_HEADS
    dh = d_ml // H
    c = min(ML_CHUNK, L)
    nc = L // c
    g = gates.reshape(B, L, 4, H).transpose(0, 3, 2, 1)
    bias = jnp.broadcast_to(gate_b.astype(F32).reshape(4, H).T[:, :, None], (H, 4, c))
    vb0 = v_col0 // dh
    fwd = lambda bi, h, ci: ci
    bwd = lambda bi, h, ci: nc - 1 - ci
    specs = []
    for pos in (fwd, bwd):
        specs += [pl.BlockSpec((1, c, dh), lambda bi, h, ci, pos=pos: (bi, pos(bi, h, ci), h)),
                  pl.BlockSpec((1, c, dh), lambda bi, h, ci, pos=pos: (bi, pos(bi, h, ci), H + h)),
                  pl.BlockSpec((1, c, dh), lambda bi, h, ci, pos=pos: (bi, pos(bi, h, ci), vb0 + h)),
                  pl.BlockSpec((1, 1, 4, c), lambda bi, h, ci, pos=pos: (bi, h, 0, pos(bi, h, ci)))]
    specs.append(pl.BlockSpec((1, 4, c), lambda bi, h, ci: (h, 0, 0)))
    out_shape = jax.ShapeDtypeStruct((B, L, d_ml), F32)
    return pl.pallas_call(
        functools.partial(_mlstm_kernel, scale=dh ** -0.5),
        grid=(B, H, nc),
        in_specs=specs,
        out_specs=[pl.BlockSpec((1, c, dh), lambda bi, h, ci: (bi, ci, h)),
                   pl.BlockSpec((1, c, dh), lambda bi, h, ci: (bi, nc - 1 - ci, h))],
        out_shape=[out_shape, out_shape],
        scratch_shapes=[pltpu.VMEM((dh, 2 * dh), F32), pltpu.VMEM((1, LANES), F32),
                        pltpu.VMEM((dh, 2 * dh), F32), pltpu.VMEM((1, LANES), F32)],
        compiler_params=_params("parallel", "parallel", "arbitrary"),
        name="mlstm_scan",
    )(qk, qk, proj, g, qk, qk, proj, g, bias)


def _group_rms(y, gain, bd, group):
    hi, lo = _split2(y * y)
    ss = jnp.dot(hi, bd, preferred_element_type=F32) + jnp.dot(lo, bd, preferred_element_type=F32)
    return y * lax.rsqrt(ss * (1.0 / group) + EPS) * gain


def _mix_out_kernel(z_ref, hf_ref, hb_ref, o_ref, x_ref, ghy_ref, gml_ref, why_ref, wml_ref, bdh_ref, bdm_ref,
                    out_ref, *, hy_group, ml_group):
    y_hy = _group_rms(z_ref[...], ghy_ref[...], bdh_ref[...], hy_group)
    y_ml = _group_rms(jax.nn.sigmoid(o_ref[...]) * (hf_ref[...] + hb_ref[...]), gml_ref[...], bdm_ref[...], ml_group)
    out_ref[...] = x_ref[...] + _bdot(y_hy, why_ref[...]) + _bdot(y_ml, wml_ref[...])


def _block_diag_ones(n, group):
    idx = np.arange(n) // group
    return jnp.asarray((idx[:, None] == idx[None, :]).astype(np.float32)).astype(BF16)


def _mix_out(z, hf, hb, proj, o_col0, x, g_hy, g_ml, w_out, tm=512):
    M, d_hy = z.shape
    d_ml = hf.shape[1]
    D = x.shape[1]
    ob = o_col0 // d_ml
    row = lambda n: pl.BlockSpec((tm, n), lambda i: (i, 0))
    full = lambda shape: pl.BlockSpec(shape, lambda i: (0, 0))
    hy_group, ml_group = d_hy // HYENA_GROUPS, d_ml // MLSTM_HEADS
    return pl.pallas_call(
        functools.partial(_mix_out_kernel, hy_group=hy_group, ml_group=ml_group),
        grid=(M // tm,),
        in_specs=[row(d_hy), row(d_ml), row(d_ml), pl.BlockSpec((tm, d_ml), lambda i: (i, ob)), row(D),
                  full((1, d_hy)), full((1, d_ml)), full((d_hy, D)), full((d_ml, D)),
                  full((d_hy, d_hy)), full((d_ml, d_ml))],
        out_specs=row(D),
        out_shape=jax.ShapeDtypeStruct((M, D), F32),
        compiler_params=_params("parallel"),
        name="mix_out",
    )(z, hf, hb, proj, x, g_hy.reshape(1, -1), g_ml.reshape(1, -1),
      w_out[:d_hy].astype(BF16), w_out[d_hy:].astype(BF16),
      _block_diag_ones(d_hy, hy_group), _block_diag_ones(d_ml, ml_group))


def _xattn_kernel(h_ref, g_ref, wq_ref, kv_ref, wo_ref, out_ref, *, heads, scale):
    x = h_ref[0]
    D = x.shape[1]
    dh = D // heads
    ms = jnp.mean(x * x, axis=-1, keepdims=True)
    xn = x * lax.rsqrt(ms + EPS) * g_ref[...]
    q = _bdot(xn, wq_ref[...])
    outs = []
    for hd in range(heads):
        qh = q[:, hd * dh:(hd + 1) * dh]
        kh = kv_ref[0, :, hd * dh:(hd + 1) * dh]
        vh = kv_ref[0, :, D + hd * dh:D + (hd + 1) * dh]
        s = lax.dot_general(qh.astype(BF16), kh.astype(BF16), (((1,), (1,)), ((), ())),
                            preferred_element_type=F32) * scale
        e = jnp.exp(s - jnp.max(s, axis=-1, keepdims=True))
        p = e / jnp.sum(e, axis=-1, keepdims=True)
        outs.append(_bdot(p, vh))
    o = jnp.concatenate(outs, axis=1)
    out_ref[0] = x + _bdot(o, wo_ref[...])


def _xattn(h, g, wq, kv, wo, tm=512):
    B, L, D = h.shape
    nm = kv.shape[1]
    full = lambda shape: pl.BlockSpec(shape, lambda bi, i: (0,) * len(shape))
    return pl.pallas_call(
        functools.partial(_xattn_kernel, heads=XATTN_HEADS, scale=(D // XATTN_HEADS) ** -0.5),
        grid=(B, L // tm),
        in_specs=[pl.BlockSpec((1, tm, D), lambda bi, i: (bi, i, 0)), full((1, D)), full((D, D)),
                  pl.BlockSpec((1, nm, 2 * D), lambda bi, i: (bi, 0, 0)), full((D, D))],
        out_specs=pl.BlockSpec((1, tm, D), lambda bi, i: (bi, i, 0)),
        out_shape=jax.ShapeDtypeStruct((B, L, D), F32),
        compiler_params=_params("parallel", "parallel"),
        name="xattn",
    )(h, g.reshape(1, D), wq.astype(BF16), kv, wo.astype(BF16))


def _mlp_kernel(h_ref, g_ref, w1_ref, w2_ref, gf_ref, out_ref, xn_ref, acc_ref, *, final_norm):
    j = pl.program_id(1)

    @pl.when(j == 0)
    def _():
        x = h_ref[...]
        ms = jnp.mean(x * x, axis=-1, keepdims=True)
        xn_ref[...] = (x * lax.rsqrt(ms + EPS) * g_ref[...]).astype(BF16)
        acc_ref[...] = x

    a = jnp.maximum(jnp.dot(xn_ref[...], w1_ref[...], preferred_element_type=F32), 0.0)
    acc_ref[...] += _bdot(a * a, w2_ref[...])

    @pl.when(j == pl.num_programs(1) - 1)
    def _():
        y = acc_ref[...]
        if final_norm:
            ms = jnp.mean(y * y, axis=-1, keepdims=True)
            y = y * lax.rsqrt(ms + EPS) * gf_ref[...]
        out_ref[...] = y


def _mlp(h, g, w1, w2, gf, final_norm, tm=1024, tf=512):
    M, D = h.shape
    dff = w1.shape[1]
    return pl.pallas_call(
        functools.partial(_mlp_kernel, final_norm=final_norm),
        grid=(M // tm, dff // tf),
        in_specs=[pl.BlockSpec((tm, D), lambda i, j: (i, 0)), pl.BlockSpec((1, D), lambda i, j: (0, 0)),
                  pl.BlockSpec((D, tf), lambda i, j: (0, j)), pl.BlockSpec((tf, D), lambda i, j: (j, 0)),
                  pl.BlockSpec((1, D), lambda i, j: (0, 0))],
        out_specs=pl.BlockSpec((tm, D), lambda i, j: (i, 0)),
        out_shape=jax.ShapeDtypeStruct((M, D), F32),
        scratch_shapes=[pltpu.VMEM((tm, D), BF16), pltpu.VMEM((tm, D), F32)],
        compiler_params=_params("parallel", "arbitrary"),
        name="mlp",
    )(h, g.reshape(1, D), w1.astype(BF16), w2.astype(BF16), gf.reshape(1, D))


def _pad_cols(w, n):
    return jnp.pad(w, ((0, 0), (0, n - w.shape[1])))


def _layer(h, mem, p, l):
    B, L, D = h.shape
    M = B * L
    d_hy = p["hy_norm_g"].shape[1]
    d_ml = p["ml_norm_g"].shape[1]
    hy_cols = 3 * d_hy
    qk_cols = 2 * d_ml
    v_col0 = hy_cols + qk_cols
    o_col0 = v_col0 + d_ml
    g_col0 = o_col0 + d_ml
    n_gate = 4 * MLSTM_HEADS

    tn = 768
    n_in = -(-(g_col0 + n_gate) // tn) * tn
    proj = _norm_matmul(h.reshape(M, D), p["norm_mix_g"][l], _pad_cols(p["w_in"][l], n_in).astype(BF16),
                        tm=min(512, M), tn=tn).reshape(B, L, n_in)

    hy = _short_conv(proj, p["hy_conv_w"][l], p["hy_conv_b"][l], 0, hy_cols, act=False)
    tabs = _fft_tables(L)
    filt = _hyena_filters(L, p["hy_filt_w1"][l], p["hy_filt_b1"][l], p["hy_filt_freq1"][l], p["hy_filt_w2"][l],
                          p["hy_filt_b2"][l], p["hy_filt_freq2"][l], p["hy_filt_w3"][l], d_hy,
                          tl=min(512, L))
    spec = _filter_spectrum(filt, L, d_hy, tabs)
    z_hy = _hyena(hy, spec, p["hy_skip"][l], tabs)

    qk = _short_conv(proj, p["ml_conv_w"][l], p["ml_conv_b"][l], hy_cols, qk_cols, act=True)
    gates = proj[..., g_col0:g_col0 + n_gate]
    h_f, h_b = _mlstm(qk, proj, v_col0, gates, p["ml_gate_b"][l], d_ml)

    h1 = _mix_out(z_hy.reshape(M, d_hy), h_f.reshape(M, d_ml), h_b.reshape(M, d_ml), proj.reshape(M, n_in),
                  o_col0, h.reshape(M, D), p["hy_norm_g"][l], p["ml_norm_g"][l], p["w_out"][l],
                  tm=min(512, M))

    nm = mem.shape[1]
    wkv = jnp.concatenate([p["xa_wk"][l], p["xa_wv"][l]], axis=1).astype(BF16)
    kv = _norm_matmul(mem.reshape(B * nm, D), p["norm_mem_g"][l], wkv, tm=min(512, B * nm), tn=512)
    h2 = _xattn(h1.reshape(B, L, D), p["norm_x_g"][l], p["xa_wq"][l], kv.reshape(B, nm, 2 * D), p["xa_wo"][l],
                tm=min(512, L))
    return h2


def kernel(x, mem, norm_mix_g, w_in, hy_conv_w, hy_conv_b, hy_filt_w1, hy_filt_b1, hy_filt_freq1, hy_filt_w2,
           hy_filt_b2, hy_filt_freq2, hy_filt_w3, hy_skip, hy_norm_g, ml_conv_w, ml_conv_b, ml_gate_b, ml_norm_g,
           w_out, norm_x_g, norm_mem_g, xa_wq, xa_wk, xa_wv, xa_wo, norm_ff_g, ff_w1, ff_w2, final_norm_g):
    p = dict(norm_mix_g=norm_mix_g, w_in=w_in, hy_conv_w=hy_conv_w, hy_conv_b=hy_conv_b, hy_filt_w1=hy_filt_w1,
             hy_filt_b1=hy_filt_b1, hy_filt_freq1=hy_filt_freq1, hy_filt_w2=hy_filt_w2, hy_filt_b2=hy_filt_b2,
             hy_filt_freq2=hy_filt_freq2, hy_filt_w3=hy_filt_w3, hy_skip=hy_skip, hy_norm_g=hy_norm_g,
             ml_conv_w=ml_conv_w, ml_conv_b=ml_conv_b, ml_gate_b=ml_gate_b, ml_norm_g=ml_norm_g, w_out=w_out,
             norm_x_g=norm_x_g, norm_mem_g=norm_mem_g, xa_wq=xa_wq, xa_wk=xa_wk, xa_wv=xa_wv, xa_wo=xa_wo)
    B, L, D = x.shape
    depth = w_in.shape[0]
    h = x
    for l in range(depth):
        h = _layer(h, mem, p, l)
        h = _mlp(h.reshape(B * L, D), norm_ff_g[l], ff_w1[l], ff_w2[l], final_norm_g, final_norm=l == depth - 1,
                 tm=min(1024, B * L)).reshape(B, L, D)
    return h
```

```python
import functools
import math

import numpy as np
import jax
import jax.numpy as jnp
from jax import lax
from jax.experimental import pallas as pl
from jax.experimental.pallas import tpu as pltpu

F32 = jnp.float32
BF16 = jnp.bfloat16

EPS = 1e-6
HYENA_GROUPS = 8
MLSTM_HEADS = 4
XATTN_HEADS = 4
FILTER_BANDS = 16
DECAY_TARGET = 1e-2
SHORT_DECAY_PCT = 0.3
LONG_DECAY_PCT = 1.5

LANES = 128
VMEM_LIMIT = 56 * 1024 * 1024


def _params(*sem):
    return pltpu.CompilerParams(dimension_semantics=sem, vmem_limit_bytes=VMEM_LIMIT)


def _bdot(a, b):
    return jnp.dot(a.astype(BF16), b.astype(BF16), preferred_element_type=F32)


def _split2(x):
    hi = x.astype(BF16)
    lo = (x - hi.astype(F32)).astype(BF16)
    return hi, lo


def _split3(x):
    hi = x.astype(BF16)
    r = x - hi.astype(F32)
    mid = r.astype(BF16)
    lo = (r - mid.astype(F32)).astype(BF16)
    return hi, mid, lo


def _norm_matmul_kernel(x_ref, g_ref, w_ref, o_ref, xn_ref):
    @pl.when(pl.program_id(1) == 0)
    def _():
        x = x_ref[...]
        ms = jnp.mean(x * x, axis=-1, keepdims=True)
        xn_ref[...] = (x * lax.rsqrt(ms + EPS) * g_ref[...]).astype(BF16)

    o_ref[...] = jnp.dot(xn_ref[...], w_ref[...], preferred_element_type=F32)


def _norm_matmul(x, g, w, tm, tn):
    M, K = x.shape
    N = w.shape[1]
    return pl.pallas_call(
        _norm_matmul_kernel,
        grid=(M // tm, N // tn),
        in_specs=[pl.BlockSpec((tm, K), lambda i, j: (i, 0)),
                  pl.BlockSpec((1, K), lambda i, j: (0, 0)),
                  pl.BlockSpec((K, tn), lambda i, j: (0, j))],
        out_specs=pl.BlockSpec((tm, tn), lambda i, j: (i, j)),
        out_shape=jax.ShapeDtypeStruct((M, N), F32),
        scratch_shapes=[pltpu.VMEM((tm, K), BF16)],
        compiler_params=_params("parallel", "arbitrary"),
        name="norm_matmul",
    )(x, g.reshape(1, K), w)


def _conv3_rows(p_ref, r, rows, w_ref, b_ref):
    L = p_ref.shape[1]
    r0 = pl.multiple_of(r * rows, rows)
    u = p_ref[0, pl.ds(r0, rows), :]
    row = lax.broadcasted_iota(jnp.int32, u.shape, 0)
    before = p_ref[0, pl.ds(pl.multiple_of(jnp.maximum(r0 - 8, 0), 8), 8), :][7:8, :]
    after = p_ref[0, pl.ds(pl.multiple_of(jnp.minimum(r0 + rows, L - 8), 8), 8), :][0:1, :]
    before = jnp.where(r == 0, 0.0, before)
    after = jnp.where(r == L // rows - 1, 0.0, after)
    prev = jnp.where(row == 0, before, pltpu.roll(u, 1, 0))
    nxt = jnp.where(row == rows - 1, after, pltpu.roll(u, rows - 1, 0))
    return prev * w_ref[0:1, :] + u * w_ref[1:2, :] + nxt * w_ref[2:3, :] + b_ref[...]


def _short_conv_kernel(p_ref, w_ref, b_ref, o_ref, *, act, rows):
    def body(r, carry):
        y = _conv3_rows(p_ref, r, rows, w_ref, b_ref)
        if act:
            y = y * jax.nn.sigmoid(y)
        o_ref[0, pl.ds(pl.multiple_of(r * rows, rows), rows), :] = y.astype(o_ref.dtype)
        return carry

    lax.fori_loop(0, p_ref.shape[1] // rows, body, 0)


def _short_conv(proj, w, b, col0, ncols, act, out_dtype, tc=512, rows=256):
    B, L, _ = proj.shape
    cb0 = col0 // tc
    return pl.pallas_call(
        functools.partial(_short_conv_kernel, act=act, rows=rows),
        grid=(B, ncols // tc),
        in_specs=[pl.BlockSpec((1, L, tc), lambda bi, j: (bi, 0, cb0 + j)),
                  pl.BlockSpec((3, tc), lambda bi, j: (0, j)),
                  pl.BlockSpec((1, tc), lambda bi, j: (0, j))],
        out_specs=pl.BlockSpec((1, L, tc), lambda bi, j: (bi, 0, j)),
        out_shape=jax.ShapeDtypeStruct((B, L, ncols), out_dtype),
        compiler_params=_params("parallel", "parallel"),
        name="short_conv",
    )(proj, w, b.reshape(1, ncols))


def _filter_kernel(feat_ref, w1_ref, b1_ref, f1_ref, w2_ref, b2_ref, f2_ref, w3_ref, dec_ref, dir_ref,
                   o_ref, *, L):
    tl = o_ref.shape[0]
    hp = lax.Precision.HIGHEST
    pos = (lax.broadcasted_iota(jnp.int32, (tl, LANES), 0) + pl.program_id(0) * tl).astype(F32)
    lane = lax.broadcasted_iota(jnp.int32, (tl, LANES), 1)
    t = pos * (1.0 / (L - 1))
    arg = feat_ref[...] * (pos * (2.0 * math.pi / L))
    z = jnp.where(lane < FILTER_BANDS, jnp.cos(arg),
                  jnp.where(lane < 2 * FILTER_BANDS, -jnp.sin(arg),
                            jnp.where(lane == 2 * FILTER_BANDS, t, 0.0)))
    hid = jnp.sin(f1_ref[...] * (jnp.dot(z, w1_ref[...], precision=hp, preferred_element_type=F32) + b1_ref[...]))
    hid = jnp.sin(f2_ref[...] * (jnp.dot(hid, w2_ref[...], precision=hp, preferred_element_type=F32) + b2_ref[...]))
    filt = jnp.dot(hid, w3_ref[...], precision=hp, preferred_element_type=F32)
    filt = filt * jnp.exp(-t[:, 0:1] * dec_ref[...])
    filt = jnp.where(pos[:, 0:1] == 0.0, filt * dir_ref[...], filt)
    o_ref[...] = filt


def _hyena_filters(L, w1, b1, fr1, w2, b2, fr2, w3, d_hyena, tl=512):
    n_emb, n_hid = w1.shape
    n_out = w3.shape[1]
    bands = jnp.linspace(1e-4, FILTER_BANDS - 1, FILTER_BANDS, dtype=F32)
    feat = jnp.zeros((1, LANES), F32).at[0, :FILTER_BANDS].set(bands).at[0, FILTER_BANDS:2 * FILTER_BANDS].set(bands)
    w1p = jnp.zeros((LANES, n_hid), F32).at[:n_emb - 1].set(w1[1:]).at[n_emb - 1].set(w1[0])
    max_decay = math.log(DECAY_TARGET) / SHORT_DECAY_PCT
    min_decay = math.log(DECAY_TARGET) / LONG_DECAY_PCT
    deltas = jnp.abs(jnp.linspace(min_decay, max_decay, d_hyena, dtype=F32))
    reps = n_out // d_hyena
    dec = jnp.tile(deltas, reps).reshape(1, n_out)
    dirmask = jnp.tile(jnp.concatenate([jnp.ones((d_hyena,), F32), jnp.zeros((d_hyena,), F32)]), reps // 2)
    full = lambda shape: pl.BlockSpec(shape, lambda i: (0,) * len(shape))
    return pl.pallas_call(
        functools.partial(_filter_kernel, L=L),
        grid=(L // tl,),
        in_specs=[full((1, LANES)), full((LANES, n_hid)), full((1, n_hid)), full((1, n_hid)),
                  full((n_hid, n_hid)), full((1, n_hid)), full((1, n_hid)), full((n_hid, n_out)),
                  full((1, n_out)), full((1, n_out))],
        out_specs=pl.BlockSpec((tl, n_out), lambda i: (i, 0)),
        out_shape=jax.ShapeDtypeStruct((L, n_out), F32),
        compiler_params=_params("parallel"),
        name="hyena_filters",
    )(feat, w1p, b1.reshape(1, -1), fr1.reshape(1, -1), w2, b2.reshape(1, -1), fr2.reshape(1, -1), w3,
      dec, dirmask.reshape(1, n_out))


MID_ROWS = 4096


def _fft_tables(L):
    N = 2 * L
    N2 = LANES
    N1 = N // N2
    H1 = N1 // 2
    k1 = np.arange(N1)[:, None]
    n1 = np.arange(H1)[None, :]
    a1 = 2.0 * np.pi * ((k1 * n1) % N1) / N1
    c1, s1 = np.cos(a1), np.sin(a1)
    m1 = np.concatenate([c1, -s1], axis=0)
    minv = np.concatenate([c1.T, -s1.T], axis=1)
    n2 = np.arange(N2)[:, None]
    k2 = np.arange(N2)[None, :]
    a2 = 2.0 * np.pi * ((n2 * k2) % N2) / N2
    cg, sg = np.cos(a2), np.sin(a2)
    g2 = np.block([[cg, -sg], [sg, cg]])
    g2i = np.block([[cg, sg], [-sg, cg]])
    at = 2.0 * np.pi * (np.arange(N1)[:, None] * np.arange(N2)[None, :]) / N
    tw = np.concatenate([np.cos(at), -np.sin(at)], axis=1)
    as_bf = lambda a: jnp.asarray(a.astype(np.float32)).astype(BF16)
    return dict(m1=as_bf(m1), minv=as_bf(minv), g2=as_bf(g2), g2i=as_bf(g2i),
                tw=jnp.asarray(tw.astype(np.float32)), N1=N1, H1=H1)


S_PITCH = LANES + 8
GROUP = 8
STAGE_UNROLL = 2


SLAB_UNROLL = 4


def _slab_transpose_in(get_slab, s_ref, h1):
    def body(j, carry):
        for u in range(SLAB_UNROLL):
            i = j * SLAB_UNROLL + u
            s_ref[pl.ds(pl.multiple_of(i * S_PITCH, 8), LANES), :] = get_slab(i).T
        return carry

    lax.fori_loop(0, h1 // SLAB_UNROLL, body, 0)


def _slab_transpose_out(s_ref, put_slab, h1):
    def body(j, carry):
        for u in range(SLAB_UNROLL):
            i = j * SLAB_UNROLL + u
            put_slab(i, s_ref[pl.ds(pl.multiple_of(i * S_PITCH, 8), LANES), :].T)
        return carry

    lax.fori_loop(0, h1 // SLAB_UNROLL, body, 0)


def _slab_rows(i):
    return pl.ds(pl.multiple_of(i * LANES, LANES), LANES)


def _fft_stage_a(s_ref, nch, m1, tw_ref, ab_ref, n1):
    h1 = n1 // 2
    twr = tw_ref[0:n1, 0:LANES]
    twi = tw_ref[0:n1, LANES:2 * LANES]

    def body(g, carry):
        for u in range(STAGE_UNROLL):
            c0 = (g * STAGE_UNROLL + u) * GROUP
            rhs = jnp.concatenate([s_ref[pl.ds(c0 + i, h1, stride=S_PITCH), :] for i in range(GROUP)], axis=1)
            res = jnp.dot(m1, rhs.astype(BF16), preferred_element_type=F32)
            for i in range(GROUP):
                ar = res[0:n1, LANES * i:LANES * (i + 1)]
                ai = res[n1:2 * n1, LANES * i:LANES * (i + 1)]
                rows = pl.ds(pl.multiple_of((c0 + i) * n1, n1), n1)
                ab_ref[rows, 0:LANES] = ar * twr - ai * twi
                ab_ref[rows, LANES:2 * LANES] = ar * twi + ai * twr
        return carry

    lax.fori_loop(0, nch // (GROUP * STAGE_UNROLL), body, 0)


def _cmul(a, b):
    ar, ai = a[:, 0:LANES], a[:, LANES:2 * LANES]
    br, bi = b[:, 0:LANES], b[:, LANES:2 * LANES]
    return jnp.concatenate([ar * br - ai * bi, ar * bi + ai * br], axis=1)


def _fft_mid(ab_ref, get_h, g2, g2i, nrows):
    chunk = min(MID_ROWS, nrows)

    def body(r, carry):
        rows = pl.ds(pl.multiple_of(r * chunk, chunk), chunk)
        spec = _bdot(ab_ref[rows, :], g2)
        ab_ref[rows, :] = _bdot(_cmul(spec, get_h(rows)), g2i)
        return carry

    lax.fori_loop(0, nrows // chunk, body, 0)


def _fft_stage_a_inv(ab_ref, nch, minv, tw_ref, s_ref, n1):
    h1 = n1 // 2
    twr = tw_ref[0:n1, 0:LANES]
    twi = tw_ref[0:n1, LANES:2 * LANES]

    def body(g, carry):
        for u in range(STAGE_UNROLL):
            c0 = (g * STAGE_UNROLL + u) * GROUP
            blk = ab_ref[pl.ds(pl.multiple_of(c0 * n1, GROUP * n1), GROUP * n1), :]
            re, im = [], []
            for i in range(GROUP):
                br = blk[i * n1:(i + 1) * n1, 0:LANES]
                bi = blk[i * n1:(i + 1) * n1, LANES:2 * LANES]
                re.append(br * twr + bi * twi)
                im.append(bi * twr - br * twi)
            rhs = jnp.concatenate([jnp.concatenate(re, axis=1), jnp.concatenate(im, axis=1)], axis=0)
            y = jnp.dot(minv, rhs.astype(BF16), preferred_element_type=F32)
            for i in range(GROUP):
                s_ref[pl.ds(c0 + i, h1, stride=S_PITCH), :] = y[:, LANES * i:LANES * (i + 1)]
        return carry

    lax.fori_loop(0, nch // (GROUP * STAGE_UNROLL), body, 0)


def _filter_spectrum_kernel(hf_ref, hb_ref, m1_ref, g2_ref, tw_ref, o_ref, s_ref, af_ref, ab_ref, *, n1, inv_n):
    nch = hf_ref.shape[1]
    h1 = n1 // 2
    m1, g2 = m1_ref[...], g2_ref[...]
    _slab_transpose_in(lambda i: hf_ref[_slab_rows(i), :], s_ref, h1)
    _fft_stage_a(s_ref, nch, m1, tw_ref, af_ref, n1)
    _slab_transpose_in(lambda i: hb_ref[_slab_rows(i), :], s_ref, h1)
    _fft_stage_a(s_ref, nch, m1, tw_ref, ab_ref, n1)

    chunk = min(MID_ROWS, nch * n1)

    def body(r, carry):
        rows = pl.ds(pl.multiple_of(r * chunk, chunk), chunk)
        sf = _bdot(af_ref[rows, :], g2)
        sb = _bdot(ab_ref[rows, :], g2)
        o_ref[rows, :] = jnp.concatenate([sf[:, 0:LANES] + sb[:, 0:LANES],
                                          sf[:, LANES:] - sb[:, LANES:]], axis=1) * inv_n
        return carry

    lax.fori_loop(0, (nch * n1) // chunk, body, 0)


def _filter_spectrum(filt, L, d_hyena, tabs):
    n1, h1 = tabs["N1"], tabs["H1"]
    order = filt.shape[1] // (2 * d_hyena)
    cb = LANES
    nblk = d_hyena // cb
    full = lambda a: pl.BlockSpec(a.shape, lambda o, c: (0,) * a.ndim)
    return pl.pallas_call(
        functools.partial(_filter_spectrum_kernel, n1=n1, inv_n=1.0 / (2 * L)),
        grid=(order, nblk),
        in_specs=[pl.BlockSpec((L, cb), lambda o, c: (0, o * 2 * nblk + c)),
                  pl.BlockSpec((L, cb), lambda o, c: (0, o * 2 * nblk + nblk + c)),
                  full(tabs["m1"]), full(tabs["g2"]), full(tabs["tw"])],
        out_specs=pl.BlockSpec((cb * n1, 2 * LANES), lambda o, c: (o * nblk + c, 0)),
        out_shape=jax.ShapeDtypeStruct((order * d_hyena * n1, 2 * LANES), F32),
        scratch_shapes=[pltpu.VMEM((h1 * S_PITCH, LANES), F32),
                        pltpu.VMEM((cb * n1, 2 * LANES), F32), pltpu.VMEM((cb * n1, 2 * LANES), F32)],
        compiler_params=_params("parallel", "parallel"),
        name="filter_spectrum",
    )(filt, filt, tabs["m1"], tabs["g2"], tabs["tw"])


def _hyena_kernel(pv_ref, px1_ref, px2_ref, wv_ref, wx1_ref, wx2_ref, bv_ref, bx1_ref, bx2_ref, h_ref, d_ref,
                  m1_ref, minv_ref, g2_ref, g2i_ref, tw_ref, o_ref, s_ref, ab_ref, v_ref, z_ref, *, n1):
    nch = pv_ref.shape[2]
    h1 = n1 // 2
    m1, minv, g2, g2i = m1_ref[...], minv_ref[...], g2_ref[...], g2i_ref[...]

    def long_conv(get_slab, order, put_slab):
        _slab_transpose_in(get_slab, s_ref, h1)
        _fft_stage_a(s_ref, nch, m1, tw_ref, ab_ref, n1)
        _fft_mid(ab_ref, lambda rows: h_ref[order, rows, :], g2, g2i, nch * n1)
        _fft_stage_a_inv(ab_ref, nch, minv, tw_ref, s_ref, n1)
        _slab_transpose_out(s_ref, put_slab, h1)

    def get_v(i):
        v = _conv3_rows(pv_ref, i, LANES, wv_ref, bv_ref)
        v_ref[_slab_rows(i), :] = v
        return v

    def put1(i, conv):
        rows = _slab_rows(i)
        x1 = _conv3_rows(px1_ref, i, LANES, wx1_ref, bx1_ref)
        z_ref[rows, :] = x1 * (conv + d_ref[0:1, :] * v_ref[rows, :])

    long_conv(get_v, 0, put1)

    def put2(i, conv):
        rows = _slab_rows(i)
        x2 = _conv3_rows(px2_ref, i, LANES, wx2_ref, bx2_ref)
        o_ref[0, rows, :] = x2 * (conv + d_ref[1:2, :] * z_ref[rows, :])

    long_conv(lambda i: z_ref[_slab_rows(i), :], 1, put2)


def _hyena(proj, conv_w, conv_b, spec, skip, tabs):
    B, L, _ = proj.shape
    D = skip.shape[1]
    n1, h1 = tabs["N1"], tabs["H1"]
    cb = LANES
    nblk = D // cb
    spec = spec.reshape(2, D * n1, 2 * LANES)
    col = lambda part: pl.BlockSpec((1, L, cb), lambda c, b: (b, 0, part * nblk + c))
    wcol = lambda part, rows: pl.BlockSpec((rows, cb), lambda c, b: (0, part * nblk + c))
    full = lambda a: pl.BlockSpec(a.shape, lambda c, b: (0,) * a.ndim)
    conv_b = conv_b.reshape(1, 3 * D)
    return pl.pallas_call(
        functools.partial(_hyena_kernel, n1=n1),
        grid=(nblk, B),
        in_specs=[col(0), col(1), col(2), wcol(0, 3), wcol(1, 3), wcol(2, 3), wcol(0, 1), wcol(1, 1), wcol(2, 1),
                  pl.BlockSpec((2, cb * n1, 2 * LANES), lambda c, b: (0, c, 0), pipeline_mode=pl.Buffered(1)),
                  pl.BlockSpec((2, cb), lambda c, b: (0, c)),
                  full(tabs["m1"]), full(tabs["minv"]), full(tabs["g2"]), full(tabs["g2i"]), full(tabs["tw"])],
        out_specs=pl.BlockSpec((1, L, cb), lambda c, b: (b, 0, c)),
        out_shape=jax.ShapeDtypeStruct((B, L, D), F32),
        scratch_shapes=[pltpu.VMEM((h1 * S_PITCH, LANES), F32), pltpu.VMEM((cb * n1, 2 * LANES), F32),
                        pltpu.VMEM((L, cb), F32), pltpu.VMEM((L, cb), F32)],
        compiler_params=_params("parallel", "arbitrary"),
        name="hyena_fftconv",
    )(proj, proj, proj, conv_w, conv_w, conv_w, conv_b, conv_b, conv_b, spec, skip.astype(F32),
      tabs["m1"], tabs["minv"], tabs["g2"], tabs["g2i"], tabs["tw"])


ML_CHUNK = 256


def _log_sigmoid(x):
    return jnp.minimum(x, 0.0) - jnp.log1p(jnp.exp(-jnp.abs(x)))


def _mlstm_direction(q, k, v, i_row, cum_row, logf_row, c_ref, m_ref, mask, scale):
    c, dh = q.shape
    g_row = i_row - cum_row
    f_col_rep = jnp.broadcast_to(cum_row, (LANES, c)).T
    m_prev = m_ref[0:1, 0:1]
    e = jnp.where(mask, g_row, -jnp.inf)
    a = jnp.maximum(jnp.max(e, axis=1, keepdims=True), m_prev)
    w = jnp.exp(e - a)
    sc = lax.dot_general(q.astype(BF16), k.astype(BF16), (((1,), (1,)), ((), ())), preferred_element_type=F32)
    p = sc * (w * scale)
    vaug = jnp.concatenate([v, jnp.ones_like(v)], axis=1)
    inter = jnp.exp(m_prev - a)
    tot = _bdot(p, vaug) + inter * _bdot(q, c_ref[...])
    num, den = tot[:, 0:dh], tot[:, dh:2 * dh]
    m_t = f_col_rep + a
    h = num / jnp.maximum(jnp.abs(den), jnp.exp(-m_t))
    a_end = jnp.maximum(jnp.max(g_row, axis=1, keepdims=True), m_prev)
    f_end = jnp.sum(logf_row, axis=1, keepdims=True)
    we = jnp.exp(g_row - a_end) * scale
    kw = k.astype(F32).T * we
    c_ref[...] = jnp.exp(m_prev - a_end) * c_ref[...] + _bdot(kw, vaug)
    m_ref[...] = jnp.broadcast_to(f_end + a_end, m_ref.shape)
    return h


def _mlstm_kernel(qf_ref, kf_ref, vf_ref, gf_ref, qb_ref, kb_ref, vb_ref, gb_ref, bias_ref,
                  hf_ref, hb_ref, c_ref, m_ref, *, heads, scale):
    @pl.when(pl.program_id(1) == 0)
    def _():
        c_ref[...] = jnp.zeros_like(c_ref)
        m_ref[...] = jnp.zeros_like(m_ref)

    c = qf_ref.shape[1]
    dh = qf_ref.shape[2] // heads
    r = lax.broadcasted_iota(jnp.int32, (c, c), 0)
    s = lax.broadcasted_iota(jnp.int32, (c, c), 1)
    mask_f = s <= r
    mask_b = s >= r

    def gate_rows(g_ref, mask_t):
        g = g_ref[0] + bias_ref[...]
        logf = _log_sigmoid(g)
        tri = mask_t.astype(BF16)
        cum = sum(jnp.dot(part, tri, preferred_element_type=F32) for part in _split3(logf))
        return g, logf, cum

    g_f, logf_f, cum_f = gate_rows(gf_ref, mask_b)
    g_b, logf_b, cum_b = gate_rows(gb_ref, mask_f)
    for h in range(heads):
        sl = slice(h * dh, (h + 1) * dh)
        fi, ff, bi, bf = h, heads + h, 2 * heads + h, 3 * heads + h
        hf_ref[0, :, sl] = _mlstm_direction(
            qf_ref[0, :, sl], kf_ref[0, :, sl], vf_ref[0, :, sl], g_f[fi:fi + 1], cum_f[ff:ff + 1],
            logf_f[ff:ff + 1], c_ref.at[h], m_ref.at[h], mask_f, scale)
        hb_ref[0, :, sl] = _mlstm_direction(
            qb_ref[0, :, sl], kb_ref[0, :, sl], vb_ref[0, :, sl], g_b[bi:bi + 1], cum_b[bf:bf + 1],
            logf_b[bf:bf + 1], c_ref.at[heads + h], m_ref.at[heads + h], mask_b, scale)


def _mlstm(qk, proj, v_col0, gates, gate_b, d_ml):
    B, L, _ = qk.shape
    H = MLSTM_HEADS
    dh = d_ml // H
    c = min(ML_CHUNK, L)
    nc = L // c
    g = gates.transpose(0, 2, 1)
    bias = jnp.broadcast_to(gate_b.astype(F32)[:, None], (4 * H, c))
    vb0 = v_col0 // d_ml
    specs = []
    for pos in (lambda ci: ci, lambda ci: nc - 1 - ci):
        specs += [pl.BlockSpec((1, c, d_ml), lambda bi, ci, pos=pos: (bi, pos(ci), 0)),
                  pl.BlockSpec((1, c, d_ml), lambda bi, ci, pos=pos: (bi, pos(ci), 1)),
                  pl.BlockSpec((1, c, d_ml), lambda bi, ci, pos=pos: (bi, pos(ci), vb0)),
                  pl.BlockSpec((1, 4 * H, c), lambda bi, ci, pos=pos: (bi, 0, pos(ci)))]
    specs.append(pl.BlockSpec((4 * H, c), lambda bi, ci: (0, 0)))
    out_shape = jax.ShapeDtypeStruct((B, L, d_ml), F32)
    return pl.pallas_call(
        functools.partial(_mlstm_kernel, heads=H, scale=dh ** -0.5),
        grid=(B, nc),
        in_specs=specs,
        out_specs=[pl.BlockSpec((1, c, d_ml), lambda bi, ci: (bi, ci, 0)),
                   pl.BlockSpec((1, c, d_ml), lambda bi, ci: (bi, nc - 1 - ci, 0))],
        out_shape=[out_shape, out_shape],
        scratch_shapes=[pltpu.VMEM((2 * H, dh, 2 * dh), F32), pltpu.VMEM((2 * H, 1, LANES), F32)],
        compiler_params=_params("parallel", "arbitrary"),
        name="mlstm_scan",
    )(qk, qk, proj, g, qk, qk, proj, g, bias)


def _group_rms(y, gain, bd, group):
    hi, lo = _split2(y * y)
    ss = jnp.dot(hi, bd, preferred_element_type=F32) + jnp.dot(lo, bd, preferred_element_type=F32)
    return y * lax.rsqrt(ss * (1.0 / group) + EPS) * gain


def _mix_out_kernel(z_ref, hf_ref, hb_ref, o_ref, x_ref, ghy_ref, gml_ref, why_ref, wml_ref, bdh_ref, bdm_ref,
                    out_ref, *, hy_group, ml_group):
    y_hy = _group_rms(z_ref[...], ghy_ref[...], bdh_ref[...], hy_group)
    y_ml = _group_rms(jax.nn.sigmoid(o_ref[...]) * (hf_ref[...] + hb_ref[...]), gml_ref[...], bdm_ref[...], ml_group)
    out_ref[...] = x_ref[...] + _bdot(y_hy, why_ref[...]) + _bdot(y_ml, wml_ref[...])


def _block_diag_ones(n, group):
    idx = np.arange(n) // group
    return jnp.asarray((idx[:, None] == idx[None, :]).astype(np.float32)).astype(BF16)


def _mix_out(z, hf, hb, proj, o_col0, x, g_hy, g_ml, w_out, tm=512):
    M, d_hy = z.shape
    d_ml = hf.shape[1]
    D = x.shape[1]
    ob = o_col0 // d_ml
    row = lambda n: pl.BlockSpec((tm, n), lambda i: (i, 0))
    full = lambda shape: pl.BlockSpec(shape, lambda i: (0, 0))
    hy_group, ml_group = d_hy // HYENA_GROUPS, d_ml // MLSTM_HEADS
    return pl.pallas_call(
        functools.partial(_mix_out_kernel, hy_group=hy_group, ml_group=ml_group),
        grid=(M // tm,),
        in_specs=[row(d_hy), row(d_ml), row(d_ml), pl.BlockSpec((tm, d_ml), lambda i: (i, ob)), row(D),
                  full((1, d_hy)), full((1, d_ml)), full((d_hy, D)), full((d_ml, D)),
                  full((d_hy, d_hy)), full((d_ml, d_ml))],
        out_specs=row(D),
        out_shape=jax.ShapeDtypeStruct((M, D), F32),
        compiler_params=_params("parallel"),
        name="mix_out",
    )(z, hf, hb, proj, x, g_hy.reshape(1, -1), g_ml.reshape(1, -1),
      w_out[:d_hy].astype(BF16), w_out[d_hy:].astype(BF16),
      _block_diag_ones(d_hy, hy_group), _block_diag_ones(d_ml, ml_group))


def _xattn_kernel(h_ref, g_ref, wq_ref, kv_ref, wo_ref, out_ref, *, heads, scale):
    x = h_ref[0]
    D = x.shape[1]
    dh = D // heads
    ms = jnp.mean(x * x, axis=-1, keepdims=True)
    xn = x * lax.rsqrt(ms + EPS) * g_ref[...]
    q = _bdot(xn, wq_ref[...])
    outs = []
    for hd in range(heads):
        qh = q[:, hd * dh:(hd + 1) * dh]
        kh = kv_ref[0, :, hd * dh:(hd + 1) * dh]
        vh = kv_ref[0, :, D + hd * dh:D + (hd + 1) * dh]
        s = lax.dot_general(qh.astype(BF16), kh.astype(BF16), (((1,), (1,)), ((), ())),
                            preferred_element_type=F32) * scale
        e = jnp.exp(s - jnp.max(s, axis=-1, keepdims=True))
        p = e / jnp.sum(e, axis=-1, keepdims=True)
        outs.append(_bdot(p, vh))
    o = jnp.concatenate(outs, axis=1)
    out_ref[0] = x + _bdot(o, wo_ref[...])


def _xattn(h, g, wq, kv, wo, tm=512):
    B, L, D = h.shape
    nm = kv.shape[1]
    full = lambda shape: pl.BlockSpec(shape, lambda bi, i: (0,) * len(shape))
    return pl.pallas_call(
        functools.partial(_xattn_kernel, heads=XATTN_HEADS, scale=(D // XATTN_HEADS) ** -0.5),
        grid=(B, L // tm),
        in_specs=[pl.BlockSpec((1, tm, D), lambda bi, i: (bi, i, 0)), full((1, D)), full((D, D)),
                  pl.BlockSpec((1, nm, 2 * D), lambda bi, i: (bi, 0, 0)), full((D, D))],
        out_specs=pl.BlockSpec((1, tm, D), lambda bi, i: (bi, i, 0)),
        out_shape=jax.ShapeDtypeStruct((B, L, D), F32),
        compiler_params=_params("parallel", "parallel"),
        name="xattn",
    )(h, g.reshape(1, D), wq.astype(BF16), kv, wo.astype(BF16))


def _mlp_kernel(h_ref, g_ref, w1_ref, w2_ref, gf_ref, out_ref, xn_ref, acc_ref, *, final_norm):
    j = pl.program_id(1)

    @pl.when(j == 0)
    def _():
        x = h_ref[...]
        ms = jnp.mean(x * x, axis=-1, keepdims=True)
        xn_ref[...] = (x * lax.rsqrt(ms + EPS) * g_ref[...]).astype(BF16)
        acc_ref[...] = x

    a = jnp.maximum(jnp.dot(xn_ref[...], w1_ref[...], preferred_element_type=F32), 0.0)
    acc_ref[...] += _bdot(a * a, w2_ref[...])

    @pl.when(j == pl.num_programs(1) - 1)
    def _():
        y = acc_ref[...]
        if final_norm:
            ms = jnp.mean(y * y, axis=-1, keepdims=True)
            y = y * lax.rsqrt(ms + EPS) * gf_ref[...]
        out_ref[...] = y


def _mlp(h, g, w1, w2, gf, final_norm, tm=1024, tf=512):
    M, D = h.shape
    dff = w1.shape[1]
    return pl.pallas_call(
        functools.partial(_mlp_kernel, final_norm=final_norm),
        grid=(M // tm, dff // tf),
        in_specs=[pl.BlockSpec((tm, D), lambda i, j: (i, 0)), pl.BlockSpec((1, D), lambda i, j: (0, 0)),
                  pl.BlockSpec((D, tf), lambda i, j: (0, j)), pl.BlockSpec((tf, D), lambda i, j: (j, 0)),
                  pl.BlockSpec((1, D), lambda i, j: (0, 0))],
        out_specs=pl.BlockSpec((tm, D), lambda i, j: (i, 0)),
        out_shape=jax.ShapeDtypeStruct((M, D), F32),
        scratch_shapes=[pltpu.VMEM((tm, D), BF16), pltpu.VMEM((tm, D), F32)],
        compiler_params=_params("parallel", "arbitrary"),
        name="mlp",
    )(h, g.reshape(1, D), w1.astype(BF16), w2.astype(BF16), gf.reshape(1, D))


def _pad_cols(w, n):
    return jnp.pad(w, ((0, 0), (0, n - w.shape[1])))


def _layer(h, mem, p, l):
    B, L, D = h.shape
    M = B * L
    d_hy = p["hy_norm_g"].shape[1]
    d_ml = p["ml_norm_g"].shape[1]
    hy_cols = 3 * d_hy
    qk_cols = 2 * d_ml
    v_col0 = hy_cols + qk_cols
    o_col0 = v_col0 + d_ml
    g_col0 = o_col0 + d_ml
    n_gate = 4 * MLSTM_HEADS

    tn = 768
    n_in = -(-(g_col0 + n_gate) // tn) * tn
    proj = _norm_matmul(h.reshape(M, D), p["norm_mix_g"][l], _pad_cols(p["w_in"][l], n_in).astype(BF16),
                        tm=min(1024, M), tn=tn).reshape(B, L, n_in)

    tabs = _fft_tables(L)
    filt = _hyena_filters(L, p["hy_filt_w1"][l], p["hy_filt_b1"][l], p["hy_filt_freq1"][l], p["hy_filt_w2"][l],
                          p["hy_filt_b2"][l], p["hy_filt_freq2"][l], p["hy_filt_w3"][l], d_hy,
                          tl=min(512, L))
    spec = _filter_spectrum(filt, L, d_hy, tabs)
    z_hy = _hyena(proj, p["hy_conv_w"][l], p["hy_conv_b"][l], spec, p["hy_skip"][l], tabs)

    qk = _short_conv(proj, p["ml_conv_w"][l], p["ml_conv_b"][l], hy_cols, qk_cols, act=True, out_dtype=BF16)
    gates = proj[..., g_col0:g_col0 + n_gate]
    h_f, h_b = _mlstm(qk, proj, v_col0, gates, p["ml_gate_b"][l], d_ml)

    h1 = _mix_out(z_hy.reshape(M, d_hy), h_f.reshape(M, d_ml), h_b.reshape(M, d_ml), proj.reshape(M, n_in),
                  o_col0, h.reshape(M, D), p["hy_norm_g"][l], p["ml_norm_g"][l], p["w_out"][l],
                  tm=min(512, M))

    nm = mem.shape[1]
    wkv = jnp.concatenate([p["xa_wk"][l], p["xa_wv"][l]], axis=1).astype(BF16)
    kv = _norm_matmul(mem.reshape(B * nm, D), p["norm_mem_g"][l], wkv, tm=min(512, B * nm), tn=512)
    h2 = _xattn(h1.reshape(B, L, D), p["norm_x_g"][l], p["xa_wq"][l], kv.reshape(B, nm, 2 * D), p["xa_wo"][l],
                tm=min(512, L))
    return h2


def kernel(x, mem, norm_mix_g, w_in, hy_conv_w, hy_conv_b, hy_filt_w1, hy_filt_b1, hy_filt_freq1, hy_filt_w2,
           hy_filt_b2, hy_filt_freq2, hy_filt_w3, hy_skip, hy_norm_g, ml_conv_w, ml_conv_b, ml_gate_b, ml_norm_g,
           w_out, norm_x_g, norm_mem_g, xa_wq, xa_wk, xa_wv, xa_wo, norm_ff_g, ff_w1, ff_w2, final_norm_g):
    p = dict(norm_mix_g=norm_mix_g, w_in=w_in, hy_conv_w=hy_conv_w, hy_conv_b=hy_conv_b, hy_filt_w1=hy_filt_w1,
             hy_filt_b1=hy_filt_b1, hy_filt_freq1=hy_filt_freq1, hy_filt_w2=hy_filt_w2, hy_filt_b2=hy_filt_b2,
             hy_filt_freq2=hy_filt_freq2, hy_filt_w3=hy_filt_w3, hy_skip=hy_skip, hy_norm_g=hy_norm_g,
             ml_conv_w=ml_conv_w, ml_conv_b=ml_conv_b, ml_gate_b=ml_gate_b, ml_norm_g=ml_norm_g, w_out=w_out,
             norm_x_g=norm_x_g, norm_mem_g=norm_mem_g, xa_wq=xa_wq, xa_wk=xa_wk, xa_wv=xa_wv, xa_wo=xa_wo)
    B, L, D = x.shape
    depth = w_in.shape[0]
    h = x
    for l in range(depth):
        h = _layer(h, mem, p, l)
        h = _mlp(h.reshape(B * L, D), norm_ff_g[l], ff_w1[l], ff_w2[l], final_norm_g, final_norm=l == depth - 1,
                 tm=min(1024, B * L)).reshape(B, L, D)
    return h
```

```python
import functools
import math

import numpy as np
import jax
import jax.numpy as jnp
from jax import lax
from jax.experimental import pallas as pl
from jax.experimental.pallas import tpu as pltpu

F32 = jnp.float32
BF16 = jnp.bfloat16

EPS = 1e-6
HYENA_GROUPS = 8
MLSTM_HEADS = 4
XATTN_HEADS = 4
FILTER_BANDS = 16
DECAY_TARGET = 1e-2
SHORT_DECAY_PCT = 0.3
LONG_DECAY_PCT = 1.5

LANES = 128
VMEM_LIMIT = 56 * 1024 * 1024


def _params(*sem):
    return pltpu.CompilerParams(dimension_semantics=sem, vmem_limit_bytes=VMEM_LIMIT)


def _bdot(a, b):
    return jnp.dot(a.astype(BF16), b.astype(BF16), preferred_element_type=F32)


def _split2(x):
    hi = x.astype(BF16)
    lo = (x - hi.astype(F32)).astype(BF16)
    return hi, lo


def _split3(x):
    hi = x.astype(BF16)
    r = x - hi.astype(F32)
    mid = r.astype(BF16)
    lo = (r - mid.astype(F32)).astype(BF16)
    return hi, mid, lo


def _norm_matmul_kernel(x_ref, g_ref, w_ref, o_ref, xn_ref):
    @pl.when(pl.program_id(1) == 0)
    def _():
        x = x_ref[...]
        ms = jnp.mean(x * x, axis=-1, keepdims=True)
        xn_ref[...] = (x * lax.rsqrt(ms + EPS) * g_ref[...]).astype(BF16)

    o_ref[...] = jnp.dot(xn_ref[...], w_ref[...], preferred_element_type=F32).astype(o_ref.dtype)


def _norm_matmul(x, g, w, tm, tn, out_dtype=F32):
    M, K = x.shape
    N = w.shape[1]
    return pl.pallas_call(
        _norm_matmul_kernel,
        grid=(M // tm, N // tn),
        in_specs=[pl.BlockSpec((tm, K), lambda i, j: (i, 0)),
                  pl.BlockSpec((1, K), lambda i, j: (0, 0)),
                  pl.BlockSpec((K, tn), lambda i, j: (0, j))],
        out_specs=pl.BlockSpec((tm, tn), lambda i, j: (i, j)),
        out_shape=jax.ShapeDtypeStruct((M, N), out_dtype),
        scratch_shapes=[pltpu.VMEM((tm, K), BF16)],
        compiler_params=_params("parallel", "arbitrary"),
        name="norm_matmul",
    )(x, g.reshape(1, K), w)


def _in_proj_kernel(x_ref, g_ref, w_ref, wg_ref, o_ref, og_ref, xn_ref):
    @pl.when(pl.program_id(1) == 0)
    def _():
        x = x_ref[...]
        ms = jnp.mean(x * x, axis=-1, keepdims=True)
        xn = (x * lax.rsqrt(ms + EPS) * g_ref[...]).astype(BF16)
        xn_ref[...] = xn
        og_ref[...] = jnp.dot(xn, wg_ref[...], preferred_element_type=F32)

    o_ref[...] = jnp.dot(xn_ref[...], w_ref[...], preferred_element_type=F32).astype(o_ref.dtype)


def _in_proj(x, g, w, w_gate, tm, tn):
    M, K = x.shape
    N = w.shape[1]
    ng = w_gate.shape[1]
    return pl.pallas_call(
        _in_proj_kernel,
        grid=(M // tm, N // tn),
        in_specs=[pl.BlockSpec((tm, K), lambda i, j: (i, 0)),
                  pl.BlockSpec((1, K), lambda i, j: (0, 0)),
                  pl.BlockSpec((K, tn), lambda i, j: (0, j)),
                  pl.BlockSpec((K, ng), lambda i, j: (0, 0))],
        out_specs=[pl.BlockSpec((tm, tn), lambda i, j: (i, j)), pl.BlockSpec((tm, ng), lambda i, j: (i, 0))],
        out_shape=[jax.ShapeDtypeStruct((M, N), BF16), jax.ShapeDtypeStruct((M, ng), F32)],
        scratch_shapes=[pltpu.VMEM((tm, K), BF16)],
        compiler_params=_params("parallel", "arbitrary"),
        name="in_proj",
    )(x, g.reshape(1, K), w, w_gate)


def _conv3_rows(p_ref, r, rows, w_ref, b_ref):
    L = p_ref.shape[1]
    r0 = pl.multiple_of(r * rows, rows)
    u = p_ref[0, pl.ds(r0, rows), :].astype(F32)
    row = lax.broadcasted_iota(jnp.int32, u.shape, 0)
    before = p_ref[0, pl.ds(pl.multiple_of(jnp.maximum(r0 - 16, 0), 16), 16), :].astype(F32)[15:16, :]
    after = p_ref[0, pl.ds(pl.multiple_of(jnp.minimum(r0 + rows, L - 16), 16), 16), :].astype(F32)[0:1, :]
    before = jnp.where(r == 0, 0.0, before)
    after = jnp.where(r == L // rows - 1, 0.0, after)
    prev = jnp.where(row == 0, before, pltpu.roll(u, 1, 0))
    nxt = jnp.where(row == rows - 1, after, pltpu.roll(u, rows - 1, 0))
    return prev * w_ref[0:1, :] + u * w_ref[1:2, :] + nxt * w_ref[2:3, :] + b_ref[...]


def _short_conv_kernel(p_ref, w_ref, b_ref, o_ref, *, act, rows):
    def body(r, carry):
        y = _conv3_rows(p_ref, r, rows, w_ref, b_ref)
        if act:
            y = y * jax.nn.sigmoid(y)
        o_ref[0, pl.ds(pl.multiple_of(r * rows, rows), rows), :] = y.astype(o_ref.dtype)
        return carry

    lax.fori_loop(0, p_ref.shape[1] // rows, body, 0)


def _short_conv(proj, w, b, col0, ncols, act, out_dtype, tc=512, rows=256):
    B, L, _ = proj.shape
    cb0 = col0 // tc
    return pl.pallas_call(
        functools.partial(_short_conv_kernel, act=act, rows=rows),
        grid=(B, ncols // tc),
        in_specs=[pl.BlockSpec((1, L, tc), lambda bi, j: (bi, 0, cb0 + j)),
                  pl.BlockSpec((3, tc), lambda bi, j: (0, j)),
                  pl.BlockSpec((1, tc), lambda bi, j: (0, j))],
        out_specs=pl.BlockSpec((1, L, tc), lambda bi, j: (bi, 0, j)),
        out_shape=jax.ShapeDtypeStruct((B, L, ncols), out_dtype),
        compiler_params=_params("parallel", "parallel"),
        name="short_conv",
    )(proj, w, b.reshape(1, ncols))


def _filter_kernel(feat_ref, w1_ref, b1_ref, f1_ref, w2_ref, b2_ref, f2_ref, w3_ref, dec_ref, dir_ref,
                   o_ref, *, L):
    tl = o_ref.shape[0]
    hp = lax.Precision.HIGHEST
    pos = (lax.broadcasted_iota(jnp.int32, (tl, LANES), 0) + pl.program_id(0) * tl).astype(F32)
    lane = lax.broadcasted_iota(jnp.int32, (tl, LANES), 1)
    t = pos * (1.0 / (L - 1))
    arg = feat_ref[...] * (pos * (2.0 * math.pi / L))
    z = jnp.where(lane < FILTER_BANDS, jnp.cos(arg),
                  jnp.where(lane < 2 * FILTER_BANDS, -jnp.sin(arg),
                            jnp.where(lane == 2 * FILTER_BANDS, t, 0.0)))
    hid = jnp.sin(f1_ref[...] * (jnp.dot(z, w1_ref[...], precision=hp, preferred_element_type=F32) + b1_ref[...]))
    hid = jnp.sin(f2_ref[...] * (jnp.dot(hid, w2_ref[...], precision=hp, preferred_element_type=F32) + b2_ref[...]))
    filt = jnp.dot(hid, w3_ref[...], precision=hp, preferred_element_type=F32)
    filt = filt * jnp.exp(-t[:, 0:1] * dec_ref[...])
    filt = jnp.where(pos[:, 0:1] == 0.0, filt * dir_ref[...], filt)
    o_ref[...] = filt


def _hyena_filters(L, w1, b1, fr1, w2, b2, fr2, w3, d_hyena, tl=512):
    n_emb, n_hid = w1.shape
    n_out = w3.shape[1]
    bands = jnp.linspace(1e-4, FILTER_BANDS - 1, FILTER_BANDS, dtype=F32)
    feat = jnp.zeros((1, LANES), F32).at[0, :FILTER_BANDS].set(bands).at[0, FILTER_BANDS:2 * FILTER_BANDS].set(bands)
    w1p = jnp.zeros((LANES, n_hid), F32).at[:n_emb - 1].set(w1[1:]).at[n_emb - 1].set(w1[0])
    max_decay = math.log(DECAY_TARGET) / SHORT_DECAY_PCT
    min_decay = math.log(DECAY_TARGET) / LONG_DECAY_PCT
    deltas = jnp.abs(jnp.linspace(min_decay, max_decay, d_hyena, dtype=F32))
    reps = n_out // d_hyena
    dec = jnp.tile(deltas, reps).reshape(1, n_out)
    dirmask = jnp.tile(jnp.concatenate([jnp.ones((d_hyena,), F32), jnp.zeros((d_hyena,), F32)]), reps // 2)
    full = lambda shape: pl.BlockSpec(shape, lambda i: (0,) * len(shape))
    return pl.pallas_call(
        functools.partial(_filter_kernel, L=L),
        grid=(L // tl,),
        in_specs=[full((1, LANES)), full((LANES, n_hid)), full((1, n_hid)), full((1, n_hid)),
                  full((n_hid, n_hid)), full((1, n_hid)), full((1, n_hid)), full((n_hid, n_out)),
                  full((1, n_out)), full((1, n_out))],
        out_specs=pl.BlockSpec((tl, n_out), lambda i: (i, 0)),
        out_shape=jax.ShapeDtypeStruct((L, n_out), F32),
        compiler_params=_params("parallel"),
        name="hyena_filters",
    )(feat, w1p, b1.reshape(1, -1), fr1.reshape(1, -1), w2, b2.reshape(1, -1), fr2.reshape(1, -1), w3,
      dec, dirmask.reshape(1, n_out))


MID_ROWS = 4096


def _fft_tables(L):
    N = 2 * L
    N2 = LANES
    N1 = N // N2
    H1 = N1 // 2
    k1 = np.arange(N1)[:, None]
    n1 = np.arange(H1)[None, :]
    a1 = 2.0 * np.pi * ((k1 * n1) % N1) / N1
    c1, s1 = np.cos(a1), np.sin(a1)
    m1 = np.concatenate([c1, -s1], axis=0)
    minv = np.concatenate([c1.T, -s1.T], axis=1)
    n2 = np.arange(N2)[:, None]
    k2 = np.arange(N2)[None, :]
    a2 = 2.0 * np.pi * ((n2 * k2) % N2) / N2
    cg, sg = np.cos(a2), np.sin(a2)
    g2 = np.block([[cg, -sg], [sg, cg]])
    g2i = np.block([[cg, sg], [-sg, cg]])
    at = 2.0 * np.pi * (np.arange(N1)[:, None] * np.arange(N2)[None, :]) / N
    tw = np.concatenate([np.cos(at), -np.sin(at)], axis=1)
    as_bf = lambda a: jnp.asarray(a.astype(np.float32)).astype(BF16)
    return dict(m1=as_bf(m1), minv=as_bf(minv), g2=as_bf(g2), g2i=as_bf(g2i),
                tw=jnp.asarray(tw.astype(np.float32)), N1=N1, H1=H1)


S_PITCH = LANES + 8
GROUP = 8
STAGE_UNROLL = 2


SLAB_UNROLL = 4


def _slab_transpose_in(get_slab, s_ref, h1):
    def body(j, carry):
        for u in range(SLAB_UNROLL):
            i = j * SLAB_UNROLL + u
            s_ref[pl.ds(pl.multiple_of(i * S_PITCH, 8), LANES), :] = get_slab(i).T
        return carry

    lax.fori_loop(0, h1 // SLAB_UNROLL, body, 0)


def _slab_transpose_out(s_ref, put_slab, h1):
    def body(j, carry):
        for u in range(SLAB_UNROLL):
            i = j * SLAB_UNROLL + u
            put_slab(i, s_ref[pl.ds(pl.multiple_of(i * S_PITCH, 8), LANES), :].T)
        return carry

    lax.fori_loop(0, h1 // SLAB_UNROLL, body, 0)


def _slab_rows(i):
    return pl.ds(pl.multiple_of(i * LANES, LANES), LANES)


def _fft_stage_a(s_ref, nch, m1, tw_ref, ab_ref, n1):
    h1 = n1 // 2
    twr = tw_ref[0:n1, 0:LANES]
    twi = tw_ref[0:n1, LANES:2 * LANES]

    def body(g, carry):
        for u in range(STAGE_UNROLL):
            c0 = (g * STAGE_UNROLL + u) * GROUP
            rhs = jnp.concatenate([s_ref[pl.ds(c0 + i, h1, stride=S_PITCH), :] for i in range(GROUP)], axis=1)
            res = jnp.dot(m1, rhs.astype(BF16), preferred_element_type=F32)
            for i in range(GROUP):
                ar = res[0:n1, LANES * i:LANES * (i + 1)]
                ai = res[n1:2 * n1, LANES * i:LANES * (i + 1)]
                rows = pl.ds(pl.multiple_of((c0 + i) * n1, n1), n1)
                ab_ref[rows, 0:LANES] = ar * twr - ai * twi
                ab_ref[rows, LANES:2 * LANES] = ar * twi + ai * twr
        return carry

    lax.fori_loop(0, nch // (GROUP * STAGE_UNROLL), body, 0)


def _cmul(a, b):
    ar, ai = a[:, 0:LANES], a[:, LANES:2 * LANES]
    br, bi = b[:, 0:LANES], b[:, LANES:2 * LANES]
    return jnp.concatenate([ar * br - ai * bi, ar * bi + ai * br], axis=1)


def _fft_mid(ab_ref, get_h, g2, g2i, nrows):
    chunk = min(MID_ROWS, nrows)

    def body(r, carry):
        rows = pl.ds(pl.multiple_of(r * chunk, chunk), chunk)
        spec = _bdot(ab_ref[rows, :], g2)
        ab_ref[rows, :] = _bdot(_cmul(spec, get_h(rows)), g2i)
        return carry

    lax.fori_loop(0, nrows // chunk, body, 0)


def _fft_stage_a_inv(ab_ref, nch, minv, tw_ref, s_ref, n1):
    h1 = n1 // 2
    twr = tw_ref[0:n1, 0:LANES]
    twi = tw_ref[0:n1, LANES:2 * LANES]

    def body(g, carry):
        for u in range(STAGE_UNROLL):
            c0 = (g * STAGE_UNROLL + u) * GROUP
            blk = ab_ref[pl.ds(pl.multiple_of(c0 * n1, GROUP * n1), GROUP * n1), :]
            re, im = [], []
            for i in range(GROUP):
                br = blk[i * n1:(i + 1) * n1, 0:LANES]
                bi = blk[i * n1:(i + 1) * n1, LANES:2 * LANES]
                re.append(br * twr + bi * twi)
                im.append(bi * twr - br * twi)
            rhs = jnp.concatenate([jnp.concatenate(re, axis=1), jnp.concatenate(im, axis=1)], axis=0)
            y = jnp.dot(minv, rhs.astype(BF16), preferred_element_type=F32)
            for i in range(GROUP):
                s_ref[pl.ds(c0 + i, h1, stride=S_PITCH), :] = y[:, LANES * i:LANES * (i + 1)]
        return carry

    lax.fori_loop(0, nch // (GROUP * STAGE_UNROLL), body, 0)


def _filter_spectrum_kernel(hf_ref, hb_ref, m1_ref, g2_ref, tw_ref, o_ref, s_ref, af_ref, ab_ref, *, n1, inv_n):
    nch = hf_ref.shape[1]
    h1 = n1 // 2
    m1, g2 = m1_ref[...], g2_ref[...]
    _slab_transpose_in(lambda i: hf_ref[_slab_rows(i), :], s_ref, h1)
    _fft_stage_a(s_ref, nch, m1, tw_ref, af_ref, n1)
    _slab_transpose_in(lambda i: hb_ref[_slab_rows(i), :], s_ref, h1)
    _fft_stage_a(s_ref, nch, m1, tw_ref, ab_ref, n1)

    chunk = min(MID_ROWS, nch * n1)

    def body(r, carry):
        rows = pl.ds(pl.multiple_of(r * chunk, chunk), chunk)
        sf = _bdot(af_ref[rows, :], g2)
        sb = _bdot(ab_ref[rows, :], g2)
        o_ref[rows, :] = jnp.concatenate([sf[:, 0:LANES] + sb[:, 0:LANES],
                                          sf[:, LANES:] - sb[:, LANES:]], axis=1) * inv_n
        return carry

    lax.fori_loop(0, (nch * n1) // chunk, body, 0)


def _filter_spectrum(filt, L, d_hyena, tabs):
    n1, h1 = tabs["N1"], tabs["H1"]
    order = filt.shape[1] // (2 * d_hyena)
    cb = LANES
    nblk = d_hyena // cb
    full = lambda a: pl.BlockSpec(a.shape, lambda o, c: (0,) * a.ndim)
    return pl.pallas_call(
        functools.partial(_filter_spectrum_kernel, n1=n1, inv_n=1.0 / (2 * L)),
        grid=(order, nblk),
        in_specs=[pl.BlockSpec((L, cb), lambda o, c: (0, o * 2 * nblk + c)),
                  pl.BlockSpec((L, cb), lambda o, c: (0, o * 2 * nblk + nblk + c)),
                  full(tabs["m1"]), full(tabs["g2"]), full(tabs["tw"])],
        out_specs=pl.BlockSpec((cb * n1, 2 * LANES), lambda o, c: (o * nblk + c, 0)),
        out_shape=jax.ShapeDtypeStruct((order * d_hyena * n1, 2 * LANES), F32),
        scratch_shapes=[pltpu.VMEM((h1 * S_PITCH, LANES), F32),
                        pltpu.VMEM((cb * n1, 2 * LANES), F32), pltpu.VMEM((cb * n1, 2 * LANES), F32)],
        compiler_params=_params("parallel", "parallel"),
        name="filter_spectrum",
    )(filt, filt, tabs["m1"], tabs["g2"], tabs["tw"])


def _hyena_kernel(pv_ref, px1_ref, px2_ref, wv_ref, wx1_ref, wx2_ref, bv_ref, bx1_ref, bx2_ref, h_ref, d_ref,
                  m1_ref, minv_ref, g2_ref, g2i_ref, tw_ref, o_ref, s_ref, ab_ref, v_ref, z_ref, *, n1):
    nch = pv_ref.shape[2]
    h1 = n1 // 2
    m1, minv, g2, g2i = m1_ref[...], minv_ref[...], g2_ref[...], g2i_ref[...]

    def long_conv(get_slab, order, put_slab):
        _slab_transpose_in(get_slab, s_ref, h1)
        _fft_stage_a(s_ref, nch, m1, tw_ref, ab_ref, n1)
        _fft_mid(ab_ref, lambda rows: h_ref[order, rows, :], g2, g2i, nch * n1)
        _fft_stage_a_inv(ab_ref, nch, minv, tw_ref, s_ref, n1)
        _slab_transpose_out(s_ref, put_slab, h1)

    def get_v(i):
        v = _conv3_rows(pv_ref, i, LANES, wv_ref, bv_ref)
        v_ref[_slab_rows(i), :] = v
        return v

    def put1(i, conv):
        rows = _slab_rows(i)
        x1 = _conv3_rows(px1_ref, i, LANES, wx1_ref, bx1_ref)
        z_ref[rows, :] = x1 * (conv + d_ref[0:1, :] * v_ref[rows, :])

    long_conv(get_v, 0, put1)

    def put2(i, conv):
        rows = _slab_rows(i)
        x2 = _conv3_rows(px2_ref, i, LANES, wx2_ref, bx2_ref)
        o_ref[0, rows, :] = (x2 * (conv + d_ref[1:2, :] * z_ref[rows, :])).astype(o_ref.dtype)

    long_conv(lambda i: z_ref[_slab_rows(i), :], 1, put2)


def _hyena(proj, conv_w, conv_b, spec, skip, tabs):
    B, L, _ = proj.shape
    D = skip.shape[1]
    n1, h1 = tabs["N1"], tabs["H1"]
    cb = LANES
    nblk = D // cb
    spec = spec.reshape(2, D * n1, 2 * LANES)
    col = lambda part: pl.BlockSpec((1, L, cb), lambda c, b: (b, 0, part * nblk + c))
    wcol = lambda part, rows: pl.BlockSpec((rows, cb), lambda c, b: (0, part * nblk + c))
    full = lambda a: pl.BlockSpec(a.shape, lambda c, b: (0,) * a.ndim)
    conv_b = conv_b.reshape(1, 3 * D)
    return pl.pallas_call(
        functools.partial(_hyena_kernel, n1=n1),
        grid=(nblk, B),
        in_specs=[col(0), col(1), col(2), wcol(0, 3), wcol(1, 3), wcol(2, 3), wcol(0, 1), wcol(1, 1), wcol(2, 1),
                  pl.BlockSpec((2, cb * n1, 2 * LANES), lambda c, b: (0, c, 0), pipeline_mode=pl.Buffered(1)),
                  pl.BlockSpec((2, cb), lambda c, b: (0, c)),
                  full(tabs["m1"]), full(tabs["minv"]), full(tabs["g2"]), full(tabs["g2i"]), full(tabs["tw"])],
        out_specs=pl.BlockSpec((1, L, cb), lambda c, b: (b, 0, c)),
        out_shape=jax.ShapeDtypeStruct((B, L, D), BF16),
        scratch_shapes=[pltpu.VMEM((h1 * S_PITCH, LANES), F32), pltpu.VMEM((cb * n1, 2 * LANES), F32),
                        pltpu.VMEM((L, cb), F32), pltpu.VMEM((L, cb), F32)],
        compiler_params=_params("parallel", "arbitrary"),
        name="hyena_fftconv",
    )(proj, proj, proj, conv_w, conv_w, conv_w, conv_b, conv_b, conv_b, spec, skip.astype(F32),
      tabs["m1"], tabs["minv"], tabs["g2"], tabs["g2i"], tabs["tw"])


ML_CHUNK = 256


def _log_sigmoid(x):
    return jnp.minimum(x, 0.0) - jnp.log1p(jnp.exp(-jnp.abs(x)))


def _mlstm_direction(q, k, v, i_row, cum_row, logf_row, c_ref, m_ref, mask, scale):
    c, dh = q.shape
    g_row = i_row - cum_row
    f_col_rep = jnp.broadcast_to(cum_row, (LANES, c)).T
    m_prev = m_ref[0:1, 0:1]
    e = jnp.where(mask, g_row, -jnp.inf)
    a = jnp.maximum(jnp.max(e, axis=1, keepdims=True), m_prev)
    w = jnp.exp(e - a)
    sc = lax.dot_general(q.astype(BF16), k.astype(BF16), (((1,), (1,)), ((), ())), preferred_element_type=F32)
    p = sc * (w * scale)
    vaug = jnp.concatenate([v, jnp.ones_like(v)], axis=1)
    inter = jnp.exp(m_prev - a)
    tot = _bdot(p, vaug) + inter * _bdot(q, c_ref[...])
    num, den = tot[:, 0:dh], tot[:, dh:2 * dh]
    m_t = f_col_rep + a
    h = num / jnp.maximum(jnp.abs(den), jnp.exp(-m_t))
    a_end = jnp.maximum(jnp.max(g_row, axis=1, keepdims=True), m_prev)
    f_end = jnp.sum(logf_row, axis=1, keepdims=True)
    we = jnp.exp(g_row - a_end) * scale
    kw = k.astype(F32).T * we
    c_ref[...] = jnp.exp(m_prev - a_end) * c_ref[...] + _bdot(kw, vaug)
    m_ref[...] = jnp.broadcast_to(f_end + a_end, m_ref.shape)
    return h


def _mlstm_kernel(qf_ref, kf_ref, vf_ref, gf_ref, qb_ref, kb_ref, vb_ref, gb_ref, bias_ref,
                  hf_ref, hb_ref, c_ref, m_ref, *, heads, scale):
    @pl.when(pl.program_id(1) == 0)
    def _():
        c_ref[...] = jnp.zeros_like(c_ref)
        m_ref[...] = jnp.zeros_like(m_ref)

    c = qf_ref.shape[1]
    dh = qf_ref.shape[2] // heads
    r = lax.broadcasted_iota(jnp.int32, (c, c), 0)
    s = lax.broadcasted_iota(jnp.int32, (c, c), 1)
    mask_f = s <= r
    mask_b = s >= r

    def gate_rows(g_ref, mask_t):
        g = g_ref[0] + bias_ref[...]
        logf = _log_sigmoid(g)
        tri = mask_t.astype(BF16)
        cum = sum(jnp.dot(part, tri, preferred_element_type=F32) for part in _split3(logf))
        return g, logf, cum

    g_f, logf_f, cum_f = gate_rows(gf_ref, mask_b)
    g_b, logf_b, cum_b = gate_rows(gb_ref, mask_f)
    for h in range(heads):
        sl = slice(h * dh, (h + 1) * dh)
        fi, ff, bi, bf = h, heads + h, 2 * heads + h, 3 * heads + h
        hf_ref[0, :, sl] = _mlstm_direction(
            qf_ref[0, :, sl], kf_ref[0, :, sl], vf_ref[0, :, sl], g_f[fi:fi + 1], cum_f[ff:ff + 1],
            logf_f[ff:ff + 1], c_ref.at[h], m_ref.at[h], mask_f, scale).astype(hf_ref.dtype)
        hb_ref[0, :, sl] = _mlstm_direction(
            qb_ref[0, :, sl], kb_ref[0, :, sl], vb_ref[0, :, sl], g_b[bi:bi + 1], cum_b[bf:bf + 1],
            logf_b[bf:bf + 1], c_ref.at[heads + h], m_ref.at[heads + h], mask_b, scale).astype(hb_ref.dtype)


def _mlstm(qk, proj, v_col0, gates, gate_b, d_ml):
    B, L, _ = qk.shape
    H = MLSTM_HEADS
    dh = d_ml // H
    c = min(ML_CHUNK, L)
    nc = L // c
    g = gates.transpose(0, 2, 1)
    bias = jnp.broadcast_to(gate_b.astype(F32)[:, None], (4 * H, c))
    vb0 = v_col0 // d_ml
    specs = []
    for pos in (lambda ci: ci, lambda ci: nc - 1 - ci):
        specs += [pl.BlockSpec((1, c, d_ml), lambda bi, ci, pos=pos: (bi, pos(ci), 0)),
                  pl.BlockSpec((1, c, d_ml), lambda bi, ci, pos=pos: (bi, pos(ci), 1)),
                  pl.BlockSpec((1, c, d_ml), lambda bi, ci, pos=pos: (bi, pos(ci), vb0)),
                  pl.BlockSpec((1, 4 * H, c), lambda bi, ci, pos=pos: (bi, 0, pos(ci)))]
    specs.append(pl.BlockSpec((4 * H, c), lambda bi, ci: (0, 0)))
    out_shape = jax.ShapeDtypeStruct((B, L, d_ml), BF16)
    return pl.pallas_call(
        functools.partial(_mlstm_kernel, heads=H, scale=dh ** -0.5),
        grid=(B, nc),
        in_specs=specs,
        out_specs=[pl.BlockSpec((1, c, d_ml), lambda bi, ci: (bi, ci, 0)),
                   pl.BlockSpec((1, c, d_ml), lambda bi, ci: (bi, nc - 1 - ci, 0))],
        out_shape=[out_shape, out_shape],
        scratch_shapes=[pltpu.VMEM((2 * H, dh, 2 * dh), F32), pltpu.VMEM((2 * H, 1, LANES), F32)],
        compiler_params=_params("parallel", "arbitrary"),
        name="mlstm_scan",
    )(qk, qk, proj, g, qk, qk, proj, g, bias)


def _group_rms(y, gain, bd, group):
    hi, lo = _split2(y * y)
    ss = jnp.dot(hi, bd, preferred_element_type=F32) + jnp.dot(lo, bd, preferred_element_type=F32)
    return y * lax.rsqrt(ss * (1.0 / group) + EPS) * gain


def _mix_out_kernel(z_ref, hf_ref, hb_ref, o_ref, x_ref, ghy_ref, gml_ref, why_ref, wml_ref, bdh_ref, bdm_ref,
                    out_ref, *, hy_group, ml_group):
    y_hy = _group_rms(z_ref[...].astype(F32), ghy_ref[...], bdh_ref[...], hy_group)
    h_sum = hf_ref[...].astype(F32) + hb_ref[...].astype(F32)
    y_ml = _group_rms(jax.nn.sigmoid(o_ref[...].astype(F32)) * h_sum, gml_ref[...], bdm_ref[...], ml_group)
    out_ref[...] = x_ref[...] + _bdot(y_hy, why_ref[...]) + _bdot(y_ml, wml_ref[...])


def _block_diag_ones(n, group):
    idx = np.arange(n) // group
    return jnp.asarray((idx[:, None] == idx[None, :]).astype(np.float32)).astype(BF16)


def _mix_out(z, hf, hb, proj, o_col0, x, g_hy, g_ml, w_out, tm=512):
    M, d_hy = z.shape
    d_ml = hf.shape[1]
    D = x.shape[1]
    ob = o_col0 // d_ml
    row = lambda n: pl.BlockSpec((tm, n), lambda i: (i, 0))
    full = lambda shape: pl.BlockSpec(shape, lambda i: (0, 0))
    hy_group, ml_group = d_hy // HYENA_GROUPS, d_ml // MLSTM_HEADS
    return pl.pallas_call(
        functools.partial(_mix_out_kernel, hy_group=hy_group, ml_group=ml_group),
        grid=(M // tm,),
        in_specs=[row(d_hy), row(d_ml), row(d_ml), pl.BlockSpec((tm, d_ml), lambda i: (i, ob)), row(D),
                  full((1, d_hy)), full((1, d_ml)), full((d_hy, D)), full((d_ml, D)),
                  full((d_hy, d_hy)), full((d_ml, d_ml))],
        out_specs=row(D),
        out_shape=jax.ShapeDtypeStruct((M, D), F32),
        compiler_params=_params("parallel"),
        name="mix_out",
    )(z, hf, hb, proj, x, g_hy.reshape(1, -1), g_ml.reshape(1, -1),
      w_out[:d_hy].astype(BF16), w_out[d_hy:].astype(BF16),
      _block_diag_ones(d_hy, hy_group), _block_diag_ones(d_ml, ml_group))


def _xattn_kernel(h_ref, g_ref, wq_ref, kv_ref, wo_ref, out_ref, *, heads, scale):
    x = h_ref[0]
    D = x.shape[1]
    dh = D // heads
    ms = jnp.mean(x * x, axis=-1, keepdims=True)
    xn = x * lax.rsqrt(ms + EPS) * g_ref[...]
    q = _bdot(xn, wq_ref[...])
    outs = []
    for hd in range(heads):
        qh = q[:, hd * dh:(hd + 1) * dh]
        kh = kv_ref[0, :, hd * dh:(hd + 1) * dh]
        vh = kv_ref[0, :, D + hd * dh:D + (hd + 1) * dh]
        s = lax.dot_general(qh.astype(BF16), kh.astype(BF16), (((1,), (1,)), ((), ())),
                            preferred_element_type=F32) * scale
        e = jnp.exp(s - jnp.max(s, axis=-1, keepdims=True))
        p = e / jnp.sum(e, axis=-1, keepdims=True)
        outs.append(_bdot(p, vh))
    o = jnp.concatenate(outs, axis=1)
    out_ref[0] = x + _bdot(o, wo_ref[...])


def _xattn(h, g, wq, kv, wo, tm=512):
    B, L, D = h.shape
    nm = kv.shape[1]
    full = lambda shape: pl.BlockSpec(shape, lambda bi, i: (0,) * len(shape))
    return pl.pallas_call(
        functools.partial(_xattn_kernel, heads=XATTN_HEADS, scale=(D // XATTN_HEADS) ** -0.5),
        grid=(B, L // tm),
        in_specs=[pl.BlockSpec((1, tm, D), lambda bi, i: (bi, i, 0)), full((1, D)), full((D, D)),
                  pl.BlockSpec((1, nm, 2 * D), lambda bi, i: (bi, 0, 0)), full((D, D))],
        out_specs=pl.BlockSpec((1, tm, D), lambda bi, i: (bi, i, 0)),
        out_shape=jax.ShapeDtypeStruct((B, L, D), F32),
        compiler_params=_params("parallel", "parallel"),
        name="xattn",
    )(h, g.reshape(1, D), wq.astype(BF16), kv, wo.astype(BF16))


def _mlp_kernel(h_ref, g_ref, w1_ref, w2_ref, gf_ref, out_ref, xn_ref, acc_ref, *, final_norm):
    j = pl.program_id(1)

    @pl.when(j == 0)
    def _():
        x = h_ref[...]
        ms = jnp.mean(x * x, axis=-1, keepdims=True)
        xn_ref[...] = (x * lax.rsqrt(ms + EPS) * g_ref[...]).astype(BF16)
        acc_ref[...] = x

    a = jnp.maximum(jnp.dot(xn_ref[...], w1_ref[...], preferred_element_type=F32), 0.0)
    acc_ref[...] += _bdot(a * a, w2_ref[...])

    @pl.when(j == pl.num_programs(1) - 1)
    def _():
        y = acc_ref[...]
        if final_norm:
            ms = jnp.mean(y * y, axis=-1, keepdims=True)
            y = y * lax.rsqrt(ms + EPS) * gf_ref[...]
        out_ref[...] = y


def _mlp(h, g, w1, w2, gf, final_norm, tm=1024, tf=512):
    M, D = h.shape
    dff = w1.shape[1]
    return pl.pallas_call(
        functools.partial(_mlp_kernel, final_norm=final_norm),
        grid=(M // tm, dff // tf),
        in_specs=[pl.BlockSpec((tm, D), lambda i, j: (i, 0)), pl.BlockSpec((1, D), lambda i, j: (0, 0)),
                  pl.BlockSpec((D, tf), lambda i, j: (0, j)), pl.BlockSpec((tf, D), lambda i, j: (j, 0)),
                  pl.BlockSpec((1, D), lambda i, j: (0, 0))],
        out_specs=pl.BlockSpec((tm, D), lambda i, j: (i, 0)),
        out_shape=jax.ShapeDtypeStruct((M, D), F32),
        scratch_shapes=[pltpu.VMEM((tm, D), BF16), pltpu.VMEM((tm, D), F32)],
        compiler_params=_params("parallel", "arbitrary"),
        name="mlp",
    )(h, g.reshape(1, D), w1.astype(BF16), w2.astype(BF16), gf.reshape(1, D))


def _pad_cols(w, n):
    return jnp.pad(w, ((0, 0), (0, n - w.shape[1])))


def _layer(h, mem, p, l):
    B, L, D = h.shape
    M = B * L
    d_hy = p["hy_norm_g"].shape[1]
    d_ml = p["ml_norm_g"].shape[1]
    hy_cols = 3 * d_hy
    qk_cols = 2 * d_ml
    v_col0 = hy_cols + qk_cols
    o_col0 = v_col0 + d_ml
    g_col0 = o_col0 + d_ml
    n_gate = 4 * MLSTM_HEADS

    w_in = p["w_in"][l]
    n_in = g_col0
    proj, gates = _in_proj(h.reshape(M, D), p["norm_mix_g"][l], w_in[:, :n_in].astype(BF16),
                           _pad_cols(w_in[:, g_col0:g_col0 + n_gate], LANES).astype(BF16),
                           tm=min(1024, M), tn=n_in // 2)
    proj = proj.reshape(B, L, n_in)
    gates = gates[:, :n_gate].reshape(B, L, n_gate)

    tabs = _fft_tables(L)
    filt = _hyena_filters(L, p["hy_filt_w1"][l], p["hy_filt_b1"][l], p["hy_filt_freq1"][l], p["hy_filt_w2"][l],
                          p["hy_filt_b2"][l], p["hy_filt_freq2"][l], p["hy_filt_w3"][l], d_hy,
                          tl=min(512, L))
    spec = _filter_spectrum(filt, L, d_hy, tabs)
    z_hy = _hyena(proj, p["hy_conv_w"][l], p["hy_conv_b"][l], spec, p["hy_skip"][l], tabs)

    qk = _short_conv(proj, p["ml_conv_w"][l], p["ml_conv_b"][l], hy_cols, qk_cols, act=True, out_dtype=BF16)
    h_f, h_b = _mlstm(qk, proj, v_col0, gates, p["ml_gate_b"][l], d_ml)

    h1 = _mix_out(z_hy.reshape(M, d_hy), h_f.reshape(M, d_ml), h_b.reshape(M, d_ml), proj.reshape(M, n_in),
                  o_col0, h.reshape(M, D), p["hy_norm_g"][l], p["ml_norm_g"][l], p["w_out"][l],
                  tm=min(512, M))

    nm = mem.shape[1]
    wkv = jnp.concatenate([p["xa_wk"][l], p["xa_wv"][l]], axis=1).astype(BF16)
    kv = _norm_matmul(mem.reshape(B * nm, D), p["norm_mem_g"][l], wkv, tm=min(512, B * nm), tn=512)
    h2 = _xattn(h1.reshape(B, L, D), p["norm_x_g"][l], p["xa_wq"][l], kv.reshape(B, nm, 2 * D), p["xa_wo"][l],
                tm=min(512, L))
    return h2


def kernel(x, mem, norm_mix_g, w_in, hy_conv_w, hy_conv_b, hy_filt_w1, hy_filt_b1, hy_filt_freq1, hy_filt_w2,
           hy_filt_b2, hy_filt_freq2, hy_filt_w3, hy_skip, hy_norm_g, ml_conv_w, ml_conv_b, ml_gate_b, ml_norm_g,
           w_out, norm_x_g, norm_mem_g, xa_wq, xa_wk, xa_wv, xa_wo, norm_ff_g, ff_w1, ff_w2, final_norm_g):
    p = dict(norm_mix_g=norm_mix_g, w_in=w_in, hy_conv_w=hy_conv_w, hy_conv_b=hy_conv_b, hy_filt_w1=hy_filt_w1,
             hy_filt_b1=hy_filt_b1, hy_filt_freq1=hy_filt_freq1, hy_filt_w2=hy_filt_w2, hy_filt_b2=hy_filt_b2,
             hy_filt_freq2=hy_filt_freq2, hy_filt_w3=hy_filt_w3, hy_skip=hy_skip, hy_norm_g=hy_norm_g,
             ml_conv_w=ml_conv_w, ml_conv_b=ml_conv_b, ml_gate_b=ml_gate_b, ml_norm_g=ml_norm_g, w_out=w_out,
             norm_x_g=norm_x_g, norm_mem_g=norm_mem_g, xa_wq=xa_wq, xa_wk=xa_wk, xa_wv=xa_wv, xa_wo=xa_wo)
    B, L, D = x.shape
    depth = w_in.shape[0]
    h = x
    for l in range(depth):
        h = _layer(h, mem, p, l)
        h = _mlp(h.reshape(B * L, D), norm_ff_g[l], ff_w1[l], ff_w2[l], final_norm_g, final_norm=l == depth - 1,
                 tm=min(1024, B * L)).reshape(B, L, D)
    return h
```

```python
import functools
import math

import numpy as np
import jax
import jax.numpy as jnp
from jax import lax
from jax.experimental import pallas as pl
from jax.experimental.pallas import tpu as pltpu

F32 = jnp.float32
BF16 = jnp.bfloat16

EPS = 1e-6
HYENA_GROUPS = 8
MLSTM_HEADS = 4
XATTN_HEADS = 4
FILTER_BANDS = 16
DECAY_TARGET = 1e-2
SHORT_DECAY_PCT = 0.3
LONG_DECAY_PCT = 1.5

LANES = 128
VMEM_LIMIT = 56 * 1024 * 1024


def _params(*sem):
    return pltpu.CompilerParams(dimension_semantics=sem, vmem_limit_bytes=VMEM_LIMIT)


def _bdot(a, b):
    return jnp.dot(a.astype(BF16), b.astype(BF16), preferred_element_type=F32)


def _split2(x):
    hi = x.astype(BF16)
    lo = (x - hi.astype(F32)).astype(BF16)
    return hi, lo


def _split3(x):
    hi = x.astype(BF16)
    r = x - hi.astype(F32)
    mid = r.astype(BF16)
    lo = (r - mid.astype(F32)).astype(BF16)
    return hi, mid, lo


def _norm_matmul_kernel(x_ref, g_ref, w_ref, o_ref, xn_ref):
    @pl.when(pl.program_id(1) == 0)
    def _():
        x = x_ref[...]
        ms = jnp.mean(x * x, axis=-1, keepdims=True)
        xn_ref[...] = (x * lax.rsqrt(ms + EPS) * g_ref[...]).astype(BF16)

    o_ref[...] = jnp.dot(xn_ref[...], w_ref[...], preferred_element_type=F32).astype(o_ref.dtype)


def _norm_matmul(x, g, w, tm, tn, out_dtype=F32):
    M, K = x.shape
    N = w.shape[1]
    return pl.pallas_call(
        _norm_matmul_kernel,
        grid=(M // tm, N // tn),
        in_specs=[pl.BlockSpec((tm, K), lambda i, j: (i, 0)),
                  pl.BlockSpec((1, K), lambda i, j: (0, 0)),
                  pl.BlockSpec((K, tn), lambda i, j: (0, j))],
        out_specs=pl.BlockSpec((tm, tn), lambda i, j: (i, j)),
        out_shape=jax.ShapeDtypeStruct((M, N), out_dtype),
        scratch_shapes=[pltpu.VMEM((tm, K), BF16)],
        compiler_params=_params("parallel", "arbitrary"),
        name="norm_matmul",
    )(x, g.reshape(1, K), w)


def _in_proj_kernel(x_ref, g_ref, w_ref, wg_ref, o_ref, og_ref, xn_ref):
    @pl.when(pl.program_id(1) == 0)
    def _():
        x = x_ref[...]
        ms = jnp.mean(x * x, axis=-1, keepdims=True)
        xn = (x * lax.rsqrt(ms + EPS) * g_ref[...]).astype(BF16)
        xn_ref[...] = xn
        og_ref[...] = jnp.dot(xn, wg_ref[...], preferred_element_type=F32)

    o_ref[...] = jnp.dot(xn_ref[...], w_ref[...], preferred_element_type=F32).astype(o_ref.dtype)


def _in_proj(x, g, w, w_gate, tm, tn):
    M, K = x.shape
    N = w.shape[1]
    ng = w_gate.shape[1]
    return pl.pallas_call(
        _in_proj_kernel,
        grid=(M // tm, N // tn),
        in_specs=[pl.BlockSpec((tm, K), lambda i, j: (i, 0)),
                  pl.BlockSpec((1, K), lambda i, j: (0, 0)),
                  pl.BlockSpec((K, tn), lambda i, j: (0, j)),
                  pl.BlockSpec((K, ng), lambda i, j: (0, 0))],
        out_specs=[pl.BlockSpec((tm, tn), lambda i, j: (i, j)), pl.BlockSpec((tm, ng), lambda i, j: (i, 0))],
        out_shape=[jax.ShapeDtypeStruct((M, N), BF16), jax.ShapeDtypeStruct((M, ng), F32)],
        scratch_shapes=[pltpu.VMEM((tm, K), BF16)],
        compiler_params=_params("parallel", "arbitrary"),
        name="in_proj",
    )(x, g.reshape(1, K), w, w_gate)


def _conv3_rows(p_ref, r, rows, w_ref, b_ref):
    L = p_ref.shape[1]
    r0 = pl.multiple_of(r * rows, rows)
    u = p_ref[0, pl.ds(r0, rows), :].astype(F32)
    row = lax.broadcasted_iota(jnp.int32, u.shape, 0)
    before = p_ref[0, pl.ds(pl.multiple_of(jnp.maximum(r0 - 16, 0), 16), 16), :].astype(F32)[15:16, :]
    after = p_ref[0, pl.ds(pl.multiple_of(jnp.minimum(r0 + rows, L - 16), 16), 16), :].astype(F32)[0:1, :]
    before = jnp.where(r == 0, 0.0, before)
    after = jnp.where(r == L // rows - 1, 0.0, after)
    prev = jnp.where(row == 0, before, pltpu.roll(u, 1, 0))
    nxt = jnp.where(row == rows - 1, after, pltpu.roll(u, rows - 1, 0))
    return prev * w_ref[0:1, :] + u * w_ref[1:2, :] + nxt * w_ref[2:3, :] + b_ref[...]


def _short_conv_kernel(p_ref, w_ref, b_ref, o_ref, *, act, rows):
    def body(r, carry):
        y = _conv3_rows(p_ref, r, rows, w_ref, b_ref)
        if act:
            y = y * jax.nn.sigmoid(y)
        o_ref[0, pl.ds(pl.multiple_of(r * rows, rows), rows), :] = y.astype(o_ref.dtype)
        return carry

    lax.fori_loop(0, p_ref.shape[1] // rows, body, 0)


def _short_conv(proj, w, b, col0, ncols, act, out_dtype, tc=512, rows=256):
    B, L, _ = proj.shape
    cb0 = col0 // tc
    return pl.pallas_call(
        functools.partial(_short_conv_kernel, act=act, rows=rows),
        grid=(B, ncols // tc),
        in_specs=[pl.BlockSpec((1, L, tc), lambda bi, j: (bi, 0, cb0 + j)),
                  pl.BlockSpec((3, tc), lambda bi, j: (0, j)),
                  pl.BlockSpec((1, tc), lambda bi, j: (0, j))],
        out_specs=pl.BlockSpec((1, L, tc), lambda bi, j: (bi, 0, j)),
        out_shape=jax.ShapeDtypeStruct((B, L, ncols), out_dtype),
        compiler_params=_params("parallel", "parallel"),
        name="short_conv",
    )(proj, w, b.reshape(1, ncols))


def _filter_kernel(feat_ref, w1_ref, b1_ref, f1_ref, w2_ref, b2_ref, f2_ref, w3_ref, dec_ref, dir_ref,
                   o_ref, *, L):
    tl = o_ref.shape[0]
    hp = lax.Precision.HIGHEST
    pos = (lax.broadcasted_iota(jnp.int32, (tl, LANES), 0) + pl.program_id(0) * tl).astype(F32)
    lane = lax.broadcasted_iota(jnp.int32, (tl, LANES), 1)
    t = pos * (1.0 / (L - 1))
    arg = feat_ref[...] * (pos * (2.0 * math.pi / L))
    z = jnp.where(lane < FILTER_BANDS, jnp.cos(arg),
                  jnp.where(lane < 2 * FILTER_BANDS, -jnp.sin(arg),
                            jnp.where(lane == 2 * FILTER_BANDS, t, 0.0)))
    hid = jnp.sin(f1_ref[...] * (jnp.dot(z, w1_ref[...], precision=hp, preferred_element_type=F32) + b1_ref[...]))
    hid = jnp.sin(f2_ref[...] * (jnp.dot(hid, w2_ref[...], precision=hp, preferred_element_type=F32) + b2_ref[...]))
    filt = jnp.dot(hid, w3_ref[...], precision=hp, preferred_element_type=F32)
    filt = filt * jnp.exp(-t[:, 0:1] * dec_ref[...])
    filt = jnp.where(pos[:, 0:1] == 0.0, filt * dir_ref[...], filt)
    o_ref[...] = filt


def _hyena_filters(L, w1, b1, fr1, w2, b2, fr2, w3, d_hyena, tl=512):
    n_emb, n_hid = w1.shape
    n_out = w3.shape[1]
    bands = jnp.linspace(1e-4, FILTER_BANDS - 1, FILTER_BANDS, dtype=F32)
    feat = jnp.zeros((1, LANES), F32).at[0, :FILTER_BANDS].set(bands).at[0, FILTER_BANDS:2 * FILTER_BANDS].set(bands)
    w1p = jnp.zeros((LANES, n_hid), F32).at[:n_emb - 1].set(w1[1:]).at[n_emb - 1].set(w1[0])
    max_decay = math.log(DECAY_TARGET) / SHORT_DECAY_PCT
    min_decay = math.log(DECAY_TARGET) / LONG_DECAY_PCT
    deltas = jnp.abs(jnp.linspace(min_decay, max_decay, d_hyena, dtype=F32))
    reps = n_out // d_hyena
    dec = jnp.tile(deltas, reps).reshape(1, n_out)
    dirmask = jnp.tile(jnp.concatenate([jnp.ones((d_hyena,), F32), jnp.zeros((d_hyena,), F32)]), reps // 2)
    full = lambda shape: pl.BlockSpec(shape, lambda i: (0,) * len(shape))
    return pl.pallas_call(
        functools.partial(_filter_kernel, L=L),
        grid=(L // tl,),
        in_specs=[full((1, LANES)), full((LANES, n_hid)), full((1, n_hid)), full((1, n_hid)),
                  full((n_hid, n_hid)), full((1, n_hid)), full((1, n_hid)), full((n_hid, n_out)),
                  full((1, n_out)), full((1, n_out))],
        out_specs=pl.BlockSpec((tl, n_out), lambda i: (i, 0)),
        out_shape=jax.ShapeDtypeStruct((L, n_out), F32),
        compiler_params=_params("parallel"),
        name="hyena_filters",
    )(feat, w1p, b1.reshape(1, -1), fr1.reshape(1, -1), w2, b2.reshape(1, -1), fr2.reshape(1, -1), w3,
      dec, dirmask.reshape(1, n_out))


MID_ROWS = 4096
SIDE_PAD = 16


def _fft_tables(L):
    N = 2 * L
    N2 = LANES
    N1 = N // N2
    H1 = N1 // 2
    k1 = np.arange(H1)[:, None]
    n1 = np.arange(H1)[None, :]
    a1 = 2.0 * np.pi * ((k1 * n1) % N1) / N1
    c1, s1 = np.cos(a1), np.sin(a1)
    sign = np.where(np.arange(H1) % 2 == 0, 1.0, -1.0)[None, :]
    pad = np.zeros((SIDE_PAD - 1, H1))
    m1 = np.concatenate([c1, -s1, sign, pad], axis=0)
    wgt = np.where(np.arange(H1) == 0, 1.0, 2.0)[:, None]
    minv = np.concatenate([(wgt * c1).T, (-wgt * s1).T, sign.T, pad.T], axis=1)
    n2 = np.arange(N2)[:, None]
    k2 = np.arange(N2)[None, :]
    a2 = 2.0 * np.pi * ((n2 * k2) % N2) / N2
    cg, sg = np.cos(a2), np.sin(a2)
    g2 = np.block([[cg, -sg], [sg, cg]])
    g2i = np.block([[cg, sg], [-sg, cg]])
    at = 2.0 * np.pi * (np.arange(H1 + 8)[:, None] * np.arange(N2)[None, :]) / N
    tw = np.concatenate([np.cos(at), -np.sin(at)], axis=1)
    as_bf = lambda a: jnp.asarray(a.astype(np.float32)).astype(BF16)
    return dict(m1=as_bf(m1), minv=as_bf(minv), g2=as_bf(g2), g2i=as_bf(g2i),
                tw=jnp.asarray(tw.astype(np.float32)), N1=N1, H1=H1)


S_PITCH = LANES + 8
GROUP = 8
STAGE_UNROLL = 2


SLAB_UNROLL = 4


def _slab_transpose_in(get_slab, s_ref, h1):
    def body(j, carry):
        for u in range(SLAB_UNROLL):
            i = j * SLAB_UNROLL + u
            s_ref[pl.ds(pl.multiple_of(i * S_PITCH, 8), LANES), :] = get_slab(i).T
        return carry

    lax.fori_loop(0, h1 // SLAB_UNROLL, body, 0)


def _slab_transpose_out(s_ref, put_slab, h1):
    def body(j, carry):
        for u in range(SLAB_UNROLL):
            i = j * SLAB_UNROLL + u
            put_slab(i, s_ref[pl.ds(pl.multiple_of(i * S_PITCH, 8), LANES), :].T)
        return carry

    lax.fori_loop(0, h1 // SLAB_UNROLL, body, 0)


def _slab_rows(i):
    return pl.ds(pl.multiple_of(i * LANES, LANES), LANES)


def _fft_stage_a(s_ref, nch, m1, tw_ref, ab_ref, side_ref, n1):
    h1 = n1 // 2
    twr = tw_ref[0:h1, 0:LANES]
    twi = tw_ref[0:h1, LANES:2 * LANES]
    tsr = tw_ref[h1:h1 + 1, 0:LANES]
    tsi = tw_ref[h1:h1 + 1, LANES:2 * LANES]

    def body(g, carry):
        for u in range(STAGE_UNROLL):
            c0 = (g * STAGE_UNROLL + u) * GROUP
            rhs = jnp.concatenate([s_ref[pl.ds(c0 + i, h1, stride=S_PITCH), :] for i in range(GROUP)], axis=1)
            res = jnp.dot(m1, rhs.astype(BF16), preferred_element_type=F32)
            for i in range(GROUP):
                ar = res[0:h1, LANES * i:LANES * (i + 1)]
                ai = res[h1:2 * h1, LANES * i:LANES * (i + 1)]
                rows = pl.ds(pl.multiple_of((c0 + i) * h1, h1), h1)
                ab_ref[rows, 0:LANES] = ar * twr - ai * twi
                ab_ref[rows, LANES:2 * LANES] = ar * twi + ai * twr
            a_side = jnp.concatenate([res[2 * h1:2 * h1 + 1, LANES * i:LANES * (i + 1)] for i in range(GROUP)],
                                     axis=0)
            rows = pl.ds(pl.multiple_of(c0, GROUP), GROUP)
            side_ref[rows, 0:LANES] = a_side * tsr
            side_ref[rows, LANES:2 * LANES] = a_side * tsi
        return carry

    lax.fori_loop(0, nch // (GROUP * STAGE_UNROLL), body, 0)


def _cmul(a, b):
    ar, ai = a[:, 0:LANES], a[:, LANES:2 * LANES]
    br, bi = b[:, 0:LANES], b[:, LANES:2 * LANES]
    return jnp.concatenate([ar * br - ai * bi, ar * bi + ai * br], axis=1)


def _fft_mid(ab_ref, side_ref, get_h, get_h_side, g2, g2i, nrows):
    chunk = min(MID_ROWS, nrows)

    def body(r, carry):
        rows = pl.ds(pl.multiple_of(r * chunk, chunk), chunk)
        spec = _bdot(ab_ref[rows, :], g2)
        ab_ref[rows, :] = _bdot(_cmul(spec, get_h(rows)), g2i)
        return carry

    lax.fori_loop(0, nrows // chunk, body, 0)
    side_ref[...] = _bdot(_cmul(_bdot(side_ref[...], g2), get_h_side()), g2i)


def _fft_stage_a_inv(ab_ref, side_ref, nch, minv, tw_ref, s_ref, n1):
    h1 = n1 // 2
    twr = tw_ref[0:h1, 0:LANES]
    twi = tw_ref[0:h1, LANES:2 * LANES]
    tsr = tw_ref[h1:h1 + 1, 0:LANES]
    tsi = tw_ref[h1:h1 + 1, LANES:2 * LANES]
    first_row = lax.broadcasted_iota(jnp.int32, (SIDE_PAD, GROUP * LANES), 0) == 0

    def body(g, carry):
        for u in range(STAGE_UNROLL):
            c0 = (g * STAGE_UNROLL + u) * GROUP
            blk = ab_ref[pl.ds(pl.multiple_of(c0 * h1, GROUP * h1), GROUP * h1), :]
            re, im = [], []
            for i in range(GROUP):
                br = blk[i * h1:(i + 1) * h1, 0:LANES]
                bi = blk[i * h1:(i + 1) * h1, LANES:2 * LANES]
                re.append(br * twr + bi * twi)
                im.append(bi * twr - br * twi)
            side = side_ref[pl.ds(pl.multiple_of(c0, GROUP), GROUP), :]
            side_re = side[:, 0:LANES] * tsr + side[:, LANES:2 * LANES] * tsi
            side_row = jnp.concatenate([side_re[i:i + 1, :] for i in range(GROUP)], axis=1)
            side_blk = jnp.where(first_row, jnp.broadcast_to(side_row, first_row.shape), 0.0)
            rhs = jnp.concatenate([jnp.concatenate(re, axis=1), jnp.concatenate(im, axis=1), side_blk], axis=0)
            y = jnp.dot(minv, rhs.astype(BF16), preferred_element_type=F32)
            for i in range(GROUP):
                s_ref[pl.ds(c0 + i, h1, stride=S_PITCH), :] = y[:, LANES * i:LANES * (i + 1)]
        return carry

    lax.fori_loop(0, nch // (GROUP * STAGE_UNROLL), body, 0)


def _filter_spectrum_kernel(hf_ref, hb_ref, m1_ref, g2_ref, tw_ref, o_ref, oside_ref,
                            s_ref, af_ref, ab_ref, afs_ref, abs_ref, *, n1, inv_n):
    nch = hf_ref.shape[1]
    h1 = n1 // 2
    m1, g2 = m1_ref[...], g2_ref[...]
    _slab_transpose_in(lambda i: hf_ref[_slab_rows(i), :], s_ref, h1)
    _fft_stage_a(s_ref, nch, m1, tw_ref, af_ref, afs_ref, n1)
    _slab_transpose_in(lambda i: hb_ref[_slab_rows(i), :], s_ref, h1)
    _fft_stage_a(s_ref, nch, m1, tw_ref, ab_ref, abs_ref, n1)

    def two_sided(fwd, bwd):
        sf = _bdot(fwd, g2)
        sb = _bdot(bwd, g2)
        return jnp.concatenate([sf[:, 0:LANES] + sb[:, 0:LANES], sf[:, LANES:] - sb[:, LANES:]], axis=1) * inv_n

    chunk = min(MID_ROWS, nch * h1)

    def body(r, carry):
        rows = pl.ds(pl.multiple_of(r * chunk, chunk), chunk)
        o_ref[rows, :] = two_sided(af_ref[rows, :], ab_ref[rows, :])
        return carry

    lax.fori_loop(0, (nch * h1) // chunk, body, 0)
    oside_ref[...] = two_sided(afs_ref[...], abs_ref[...])


def _filter_spectrum(filt, L, d_hyena, tabs):
    h1 = tabs["H1"]
    order = filt.shape[1] // (2 * d_hyena)
    cb = LANES
    nblk = d_hyena // cb
    full = lambda a: pl.BlockSpec(a.shape, lambda o, c: (0,) * a.ndim)
    main = pltpu.VMEM((cb * h1, 2 * LANES), F32)
    side = pltpu.VMEM((cb, 2 * LANES), F32)
    return pl.pallas_call(
        functools.partial(_filter_spectrum_kernel, n1=tabs["N1"], inv_n=1.0 / (2 * L)),
        grid=(order, nblk),
        in_specs=[pl.BlockSpec((L, cb), lambda o, c: (0, o * 2 * nblk + c)),
                  pl.BlockSpec((L, cb), lambda o, c: (0, o * 2 * nblk + nblk + c)),
                  full(tabs["m1"]), full(tabs["g2"]), full(tabs["tw"])],
        out_specs=[pl.BlockSpec((cb * h1, 2 * LANES), lambda o, c: (o * nblk + c, 0)),
                   pl.BlockSpec((cb, 2 * LANES), lambda o, c: (o * nblk + c, 0))],
        out_shape=[jax.ShapeDtypeStruct((order * d_hyena * h1, 2 * LANES), F32),
                   jax.ShapeDtypeStruct((order * d_hyena, 2 * LANES), F32)],
        scratch_shapes=[pltpu.VMEM((h1 * S_PITCH, LANES), F32), main, main, side, side],
        compiler_params=_params("parallel", "parallel"),
        name="filter_spectrum",
    )(filt, filt, tabs["m1"], tabs["g2"], tabs["tw"])


def _hyena_kernel(pv_ref, px1_ref, px2_ref, wv_ref, wx1_ref, wx2_ref, bv_ref, bx1_ref, bx2_ref, h_ref, hside_ref,
                  d_ref, m1_ref, minv_ref, g2_ref, g2i_ref, tw_ref, o_ref, s_ref, ab_ref, side_ref, v_ref, z_ref,
                  *, n1):
    nch = pv_ref.shape[2]
    h1 = n1 // 2
    m1, minv, g2, g2i = m1_ref[...], minv_ref[...], g2_ref[...], g2i_ref[...]

    def long_conv(get_slab, order, put_slab):
        _slab_transpose_in(get_slab, s_ref, h1)
        _fft_stage_a(s_ref, nch, m1, tw_ref, ab_ref, side_ref, n1)
        _fft_mid(ab_ref, side_ref, lambda rows: h_ref[order, rows, :], lambda: hside_ref[order], g2, g2i, nch * h1)
        _fft_stage_a_inv(ab_ref, side_ref, nch, minv, tw_ref, s_ref, n1)
        _slab_transpose_out(s_ref, put_slab, h1)

    def get_v(i):
        v = _conv3_rows(pv_ref, i, LANES, wv_ref, bv_ref)
        v_ref[_slab_rows(i), :] = v
        return v

    def put1(i, conv):
        rows = _slab_rows(i)
        x1 = _conv3_rows(px1_ref, i, LANES, wx1_ref, bx1_ref)
        z_ref[rows, :] = x1 * (conv + d_ref[0:1, :] * v_ref[rows, :])

    long_conv(get_v, 0, put1)

    def put2(i, conv):
        rows = _slab_rows(i)
        x2 = _conv3_rows(px2_ref, i, LANES, wx2_ref, bx2_ref)
        o_ref[0, rows, :] = (x2 * (conv + d_ref[1:2, :] * z_ref[rows, :])).astype(o_ref.dtype)

    long_conv(lambda i: z_ref[_slab_rows(i), :], 1, put2)


def _hyena(proj, conv_w, conv_b, spec, spec_side, skip, tabs):
    B, L, _ = proj.shape
    D = skip.shape[1]
    n1, h1 = tabs["N1"], tabs["H1"]
    cb = LANES
    nblk = D // cb
    spec = spec.reshape(2, D * h1, 2 * LANES)
    spec_side = spec_side.reshape(2, D, 2 * LANES)
    col = lambda part: pl.BlockSpec((1, L, cb), lambda c, b: (b, 0, part * nblk + c))
    wcol = lambda part, rows: pl.BlockSpec((rows, cb), lambda c, b: (0, part * nblk + c))
    full = lambda a: pl.BlockSpec(a.shape, lambda c, b: (0,) * a.ndim)
    conv_b = conv_b.reshape(1, 3 * D)
    return pl.pallas_call(
        functools.partial(_hyena_kernel, n1=n1),
        grid=(nblk, B),
        in_specs=[col(0), col(1), col(2), wcol(0, 3), wcol(1, 3), wcol(2, 3), wcol(0, 1), wcol(1, 1), wcol(2, 1),
                  pl.BlockSpec((2, cb * h1, 2 * LANES), lambda c, b: (0, c, 0), pipeline_mode=pl.Buffered(1)),
                  pl.BlockSpec((2, cb, 2 * LANES), lambda c, b: (0, c, 0)),
                  pl.BlockSpec((2, cb), lambda c, b: (0, c)),
                  full(tabs["m1"]), full(tabs["minv"]), full(tabs["g2"]), full(tabs["g2i"]), full(tabs["tw"])],
        out_specs=pl.BlockSpec((1, L, cb), lambda c, b: (b, 0, c)),
        out_shape=jax.ShapeDtypeStruct((B, L, D), BF16),
        scratch_shapes=[pltpu.VMEM((h1 * S_PITCH, LANES), F32), pltpu.VMEM((cb * h1, 2 * LANES), F32),
                        pltpu.VMEM((cb, 2 * LANES), F32), pltpu.VMEM((L, cb), F32), pltpu.VMEM((L, cb), F32)],
        compiler_params=_params("parallel", "arbitrary"),
        name="hyena_fftconv",
    )(proj, proj, proj, conv_w, conv_w, conv_w, conv_b, conv_b, conv_b, spec, spec_side, skip.astype(F32),
      tabs["m1"], tabs["minv"], tabs["g2"], tabs["g2i"], tabs["tw"])


ML_CHUNK = 256


def _log_sigmoid(x):
    return jnp.minimum(x, 0.0) - jnp.log1p(jnp.exp(-jnp.abs(x)))


def _mlstm_direction(q, k, v, i_row, cum_row, logf_row, c_ref, m_ref, mask, scale):
    c, dh = q.shape
    g_row = i_row - cum_row
    f_col_rep = jnp.broadcast_to(cum_row, (LANES, c)).T
    m_prev = m_ref[0:1, 0:1]
    e = jnp.where(mask, g_row, -jnp.inf)
    a = jnp.maximum(jnp.max(e, axis=1, keepdims=True), m_prev)
    w = jnp.exp(e - a)
    sc = lax.dot_general(q.astype(BF16), k.astype(BF16), (((1,), (1,)), ((), ())), preferred_element_type=F32)
    p = sc * (w * scale)
    vaug = jnp.concatenate([v, jnp.ones_like(v)], axis=1)
    inter = jnp.exp(m_prev - a)
    tot = _bdot(p, vaug) + inter * _bdot(q, c_ref[...])
    num, den = tot[:, 0:dh], tot[:, dh:2 * dh]
    m_t = f_col_rep + a
    h = num / jnp.maximum(jnp.abs(den), jnp.exp(-m_t))
    a_end = jnp.maximum(jnp.max(g_row, axis=1, keepdims=True), m_prev)
    f_end = jnp.sum(logf_row, axis=1, keepdims=True)
    we = jnp.exp(g_row - a_end) * scale
    kw = k.astype(F32).T * we
    c_ref[...] = jnp.exp(m_prev - a_end) * c_ref[...] + _bdot(kw, vaug)
    m_ref[...] = jnp.broadcast_to(f_end + a_end, m_ref.shape)
    return h


def _mlstm_kernel(qf_ref, kf_ref, vf_ref, gf_ref, qb_ref, kb_ref, vb_ref, gb_ref, bias_ref,
                  hf_ref, hb_ref, c_ref, m_ref, *, heads, scale):
    @pl.when(pl.program_id(1) == 0)
    def _():
        c_ref[...] = jnp.zeros_like(c_ref)
        m_ref[...] = jnp.zeros_like(m_ref)

    c = qf_ref.shape[1]
    dh = qf_ref.shape[2] // heads
    r = lax.broadcasted_iota(jnp.int32, (c, c), 0)
    s = lax.broadcasted_iota(jnp.int32, (c, c), 1)
    mask_f = s <= r
    mask_b = s >= r

    def gate_rows(g_ref, mask_t):
        g = g_ref[0] + bias_ref[...]
        logf = _log_sigmoid(g)
        tri = mask_t.astype(BF16)
        cum = sum(jnp.dot(part, tri, preferred_element_type=F32) for part in _split3(logf))
        return g, logf, cum

    g_f, logf_f, cum_f = gate_rows(gf_ref, mask_b)
    g_b, logf_b, cum_b = gate_rows(gb_ref, mask_f)
    for h in range(heads):
        sl = slice(h * dh, (h + 1) * dh)
        fi, ff, bi, bf = h, heads + h, 2 * heads + h, 3 * heads + h
        hf_ref[0, :, sl] = _mlstm_direction(
            qf_ref[0, :, sl], kf_ref[0, :, sl], vf_ref[0, :, sl], g_f[fi:fi + 1], cum_f[ff:ff + 1],
            logf_f[ff:ff + 1], c_ref.at[h], m_ref.at[h], mask_f, scale).astype(hf_ref.dtype)
        hb_ref[0, :, sl] = _mlstm_direction(
            qb_ref[0, :, sl], kb_ref[0, :, sl], vb_ref[0, :, sl], g_b[bi:bi + 1], cum_b[bf:bf + 1],
            logf_b[bf:bf + 1], c_ref.at[heads + h], m_ref.at[heads + h], mask_b, scale).astype(hb_ref.dtype)


def _mlstm(qk, proj, v_col0, gates, gate_b, d_ml):
    B, L, _ = qk.shape
    H = MLSTM_HEADS
    dh = d_ml // H
    c = min(ML_CHUNK, L)
    nc = L // c
    g = gates.transpose(0, 2, 1)
    bias = jnp.broadcast_to(gate_b.astype(F32)[:, None], (4 * H, c))
    vb0 = v_col0 // d_ml
    specs = []
    for pos in (lambda ci: ci, lambda ci: nc - 1 - ci):
        specs += [pl.BlockSpec((1, c, d_ml), lambda bi, ci, pos=pos: (bi, pos(ci), 0)),
                  pl.BlockSpec((1, c, d_ml), lambda bi, ci, pos=pos: (bi, pos(ci), 1)),
                  pl.BlockSpec((1, c, d_ml), lambda bi, ci, pos=pos: (bi, pos(ci), vb0)),
                  pl.BlockSpec((1, 4 * H, c), lambda bi, ci, pos=pos: (bi, 0, pos(ci)))]
    specs.append(pl.BlockSpec((4 * H, c), lambda bi, ci: (0, 0)))
    out_shape = jax.ShapeDtypeStruct((B, L, d_ml), BF16)
    return pl.pallas_call(
        functools.partial(_mlstm_kernel, heads=H, scale=dh ** -0.5),
        grid=(B, nc),
        in_specs=specs,
        out_specs=[pl.BlockSpec((1, c, d_ml), lambda bi, ci: (bi, ci, 0)),
                   pl.BlockSpec((1, c, d_ml), lambda bi, ci: (bi, nc - 1 - ci, 0))],
        out_shape=[out_shape, out_shape],
        scratch_shapes=[pltpu.VMEM((2 * H, dh, 2 * dh), F32), pltpu.VMEM((2 * H, 1, LANES), F32)],
        compiler_params=_params("parallel", "arbitrary"),
        name="mlstm_scan",
    )(qk, qk, proj, g, qk, qk, proj, g, bias)


def _group_rms(y, gain, bd, group):
    ss = jnp.dot((y * y).astype(BF16), bd, preferred_element_type=F32)
    return y * lax.rsqrt(ss * (1.0 / group) + EPS) * gain


def _mix_out_kernel(z_ref, hf_ref, hb_ref, o_ref, x_ref, ghy_ref, gml_ref, why_ref, wml_ref, bdh_ref, bdm_ref,
                    out_ref, *, hy_group, ml_group):
    y_hy = _group_rms(z_ref[...].astype(F32), ghy_ref[...], bdh_ref[...], hy_group)
    h_sum = hf_ref[...].astype(F32) + hb_ref[...].astype(F32)
    y_ml = _group_rms(jax.nn.sigmoid(o_ref[...].astype(F32)) * h_sum, gml_ref[...], bdm_ref[...], ml_group)
    out_ref[...] = x_ref[...] + _bdot(y_hy, why_ref[...]) + _bdot(y_ml, wml_ref[...])


def _block_diag_ones(n, group):
    idx = np.arange(n) // group
    return jnp.asarray((idx[:, None] == idx[None, :]).astype(np.float32)).astype(BF16)


def _mix_out(z, hf, hb, proj, o_col0, x, g_hy, g_ml, w_out, tm=512):
    M, d_hy = z.shape
    d_ml = hf.shape[1]
    D = x.shape[1]
    ob = o_col0 // d_ml
    row = lambda n: pl.BlockSpec((tm, n), lambda i: (i, 0))
    full = lambda shape: pl.BlockSpec(shape, lambda i: (0, 0))
    hy_group, ml_group = d_hy // HYENA_GROUPS, d_ml // MLSTM_HEADS
    return pl.pallas_call(
        functools.partial(_mix_out_kernel, hy_group=hy_group, ml_group=ml_group),
        grid=(M // tm,),
        in_specs=[row(d_hy), row(d_ml), row(d_ml), pl.BlockSpec((tm, d_ml), lambda i: (i, ob)), row(D),
                  full((1, d_hy)), full((1, d_ml)), full((d_hy, D)), full((d_ml, D)),
                  full((d_hy, d_hy)), full((d_ml, d_ml))],
        out_specs=row(D),
        out_shape=jax.ShapeDtypeStruct((M, D), F32),
        compiler_params=_params("parallel"),
        name="mix_out",
    )(z, hf, hb, proj, x, g_hy.reshape(1, -1), g_ml.reshape(1, -1),
      w_out[:d_hy].astype(BF16), w_out[d_hy:].astype(BF16),
      _block_diag_ones(d_hy, hy_group), _block_diag_ones(d_ml, ml_group))


def _xattn_kernel(h_ref, g_ref, wq_ref, kv_ref, wo_ref, out_ref, *, heads, scale):
    x = h_ref[0]
    D = x.shape[1]
    dh = D // heads
    ms = jnp.mean(x * x, axis=-1, keepdims=True)
    xn = x * lax.rsqrt(ms + EPS) * g_ref[...]
    q = _bdot(xn, wq_ref[...])
    outs = []
    for hd in range(heads):
        qh = q[:, hd * dh:(hd + 1) * dh]
        kh = kv_ref[0, :, hd * dh:(hd + 1) * dh]
        vh = kv_ref[0, :, D + hd * dh:D + (hd + 1) * dh]
        s = lax.dot_general(qh.astype(BF16), kh.astype(BF16), (((1,), (1,)), ((), ())),
                            preferred_element_type=F32) * scale
        e = jnp.exp(s - jnp.max(s, axis=-1, keepdims=True))
        p = e / jnp.sum(e, axis=-1, keepdims=True)
        outs.append(_bdot(p, vh))
    o = jnp.concatenate(outs, axis=1)
    out_ref[0] = x + _bdot(o, wo_ref[...])


def _xattn(h, g, wq, kv, wo, tm=512):
    B, L, D = h.shape
    nm = kv.shape[1]
    full = lambda shape: pl.BlockSpec(shape, lambda bi, i: (0,) * len(shape))
    return pl.pallas_call(
        functools.partial(_xattn_kernel, heads=XATTN_HEADS, scale=(D // XATTN_HEADS) ** -0.5),
        grid=(B, L // tm),
        in_specs=[pl.BlockSpec((1, tm, D), lambda bi, i: (bi, i, 0)), full((1, D)), full((D, D)),
                  pl.BlockSpec((1, nm, 2 * D), lambda bi, i: (bi, 0, 0)), full((D, D))],
        out_specs=pl.BlockSpec((1, tm, D), lambda bi, i: (bi, i, 0)),
        out_shape=jax.ShapeDtypeStruct((B, L, D), F32),
        compiler_params=_params("parallel", "parallel"),
        name="xattn",
    )(h, g.reshape(1, D), wq.astype(BF16), kv, wo.astype(BF16))


def _mlp_kernel(h_ref, g_ref, w1_ref, w2_ref, gf_ref, out_ref, xn_ref, acc_ref, *, final_norm):
    j = pl.program_id(1)

    @pl.when(j == 0)
    def _():
        x = h_ref[...]
        ms = jnp.mean(x * x, axis=-1, keepdims=True)
        xn_ref[...] = (x * lax.rsqrt(ms + EPS) * g_ref[...]).astype(BF16)
        acc_ref[...] = x

    a = jnp.maximum(jnp.dot(xn_ref[...], w1_ref[...], preferred_element_type=F32), 0.0)
    acc_ref[...] += _bdot(a * a, w2_ref[...])

    @pl.when(j == pl.num_programs(1) - 1)
    def _():
        y = acc_ref[...]
        if final_norm:
            ms = jnp.mean(y * y, axis=-1, keepdims=True)
            y = y * lax.rsqrt(ms + EPS) * gf_ref[...]
        out_ref[...] = y


def _mlp(h, g, w1, w2, gf, final_norm, tm=1024, tf=512):
    M, D = h.shape
    dff = w1.shape[1]
    return pl.pallas_call(
        functools.partial(_mlp_kernel, final_norm=final_norm),
        grid=(M // tm, dff // tf),
        in_specs=[pl.BlockSpec((tm, D), lambda i, j: (i, 0)), pl.BlockSpec((1, D), lambda i, j: (0, 0)),
                  pl.BlockSpec((D, tf), lambda i, j: (0, j)), pl.BlockSpec((tf, D), lambda i, j: (j, 0)),
                  pl.BlockSpec((1, D), lambda i, j: (0, 0))],
        out_specs=pl.BlockSpec((tm, D), lambda i, j: (i, 0)),
        out_shape=jax.ShapeDtypeStruct((M, D), F32),
        scratch_shapes=[pltpu.VMEM((tm, D), BF16), pltpu.VMEM((tm, D), F32)],
        compiler_params=_params("parallel", "arbitrary"),
        name="mlp",
    )(h, g.reshape(1, D), w1.astype(BF16), w2.astype(BF16), gf.reshape(1, D))


def _pad_cols(w, n):
    return jnp.pad(w, ((0, 0), (0, n - w.shape[1])))


def _layer(h, mem, p, l):
    B, L, D = h.shape
    M = B * L
    d_hy = p["hy_norm_g"].shape[1]
    d_ml = p["ml_norm_g"].shape[1]
    hy_cols = 3 * d_hy
    qk_cols = 2 * d_ml
    v_col0 = hy_cols + qk_cols
    o_col0 = v_col0 + d_ml
    g_col0 = o_col0 + d_ml
    n_gate = 4 * MLSTM_HEADS

    w_in = p["w_in"][l]
    n_in = g_col0
    proj, gates = _in_proj(h.reshape(M, D), p["norm_mix_g"][l], w_in[:, :n_in].astype(BF16),
                           _pad_cols(w_in[:, g_col0:g_col0 + n_gate], LANES).astype(BF16),
                           tm=min(1024, M), tn=n_in // 2)
    proj = proj.reshape(B, L, n_in)
    gates = gates[:, :n_gate].reshape(B, L, n_gate)

    tabs = _fft_tables(L)
    filt = _hyena_filters(L, p["hy_filt_w1"][l], p["hy_filt_b1"][l], p["hy_filt_freq1"][l], p["hy_filt_w2"][l],
                          p["hy_filt_b2"][l], p["hy_filt_freq2"][l], p["hy_filt_w3"][l], d_hy,
                          tl=min(512, L))
    spec, spec_side = _filter_spectrum(filt, L, d_hy, tabs)
    z_hy = _hyena(proj, p["hy_conv_w"][l], p["hy_conv_b"][l], spec, spec_side, p["hy_skip"][l], tabs)

    qk = _short_conv(proj, p["ml_conv_w"][l], p["ml_conv_b"][l], hy_cols, qk_cols, act=True, out_dtype=BF16)
    h_f, h_b = _mlstm(qk, proj, v_col0, gates, p["ml_gate_b"][l], d_ml)

    h1 = _mix_out(z_hy.reshape(M, d_hy), h_f.reshape(M, d_ml), h_b.reshape(M, d_ml), proj.reshape(M, n_in),
                  o_col0, h.reshape(M, D), p["hy_norm_g"][l], p["ml_norm_g"][l], p["w_out"][l],
                  tm=min(512, M))

    nm = mem.shape[1]
    wkv = jnp.concatenate([p["xa_wk"][l], p["xa_wv"][l]], axis=1).astype(BF16)
    kv = _norm_matmul(mem.reshape(B * nm, D), p["norm_mem_g"][l], wkv, tm=min(512, B * nm), tn=512)
    h2 = _xattn(h1.reshape(B, L, D), p["norm_x_g"][l], p["xa_wq"][l], kv.reshape(B, nm, 2 * D), p["xa_wo"][l],
                tm=min(512, L))
    return h2


def kernel(x, mem, norm_mix_g, w_in, hy_conv_w, hy_conv_b, hy_filt_w1, hy_filt_b1, hy_filt_freq1, hy_filt_w2,
           hy_filt_b2, hy_filt_freq2, hy_filt_w3, hy_skip, hy_norm_g, ml_conv_w, ml_conv_b, ml_gate_b, ml_norm_g,
           w_out, norm_x_g, norm_mem_g, xa_wq, xa_wk, xa_wv, xa_wo, norm_ff_g, ff_w1, ff_w2, final_norm_g):
    p = dict(norm_mix_g=norm_mix_g, w_in=w_in, hy_conv_w=hy_conv_w, hy_conv_b=hy_conv_b, hy_filt_w1=hy_filt_w1,
             hy_filt_b1=hy_filt_b1, hy_filt_freq1=hy_filt_freq1, hy_filt_w2=hy_filt_w2, hy_filt_b2=hy_filt_b2,
             hy_filt_freq2=hy_filt_freq2, hy_filt_w3=hy_filt_w3, hy_skip=hy_skip, hy_norm_g=hy_norm_g,
             ml_conv_w=ml_conv_w, ml_conv_b=ml_conv_b, ml_gate_b=ml_gate_b, ml_norm_g=ml_norm_g, w_out=w_out,
             norm_x_g=norm_x_g, norm_mem_g=norm_mem_g, xa_wq=xa_wq, xa_wk=xa_wk, xa_wv=xa_wv, xa_wo=xa_wo)
    B, L, D = x.shape
    depth = w_in.shape[0]
    h = x
    for l in range(depth):
        h = _layer(h, mem, p, l)
        h = _mlp(h.reshape(B * L, D), norm_ff_g[l], ff_w1[l], ff_w2[l], final_norm_g, final_norm=l == depth - 1,
                 tm=min(1024, B * L)).reshape(B, L, D)
    return h
```

```python
import functools
import math

import numpy as np
import jax
import jax.numpy as jnp
from jax import lax
from jax.experimental import pallas as pl
from jax.experimental.pallas import tpu as pltpu

F32 = jnp.float32
BF16 = jnp.bfloat16

EPS = 1e-6
HYENA_GROUPS = 8
MLSTM_HEADS = 4
XATTN_HEADS = 4
FILTER_BANDS = 16
DECAY_TARGET = 1e-2
SHORT_DECAY_PCT = 0.3
LONG_DECAY_PCT = 1.5

LANES = 128
VMEM_LIMIT = 56 * 1024 * 1024


def _params(*sem):
    return pltpu.CompilerParams(dimension_semantics=sem, vmem_limit_bytes=VMEM_LIMIT)


def _bdot(a, b):
    return jnp.dot(a.astype(BF16), b.astype(BF16), preferred_element_type=F32)


def _split2(x):
    hi = x.astype(BF16)
    lo = (x - hi.astype(F32)).astype(BF16)
    return hi, lo


def _split3(x):
    hi = x.astype(BF16)
    r = x - hi.astype(F32)
    mid = r.astype(BF16)
    lo = (r - mid.astype(F32)).astype(BF16)
    return hi, mid, lo


def _norm_matmul_kernel(x_ref, g_ref, w_ref, o_ref, xn_ref):
    @pl.when(pl.program_id(1) == 0)
    def _():
        x = x_ref[...]
        ms = jnp.mean(x * x, axis=-1, keepdims=True)
        xn_ref[...] = (x * lax.rsqrt(ms + EPS) * g_ref[...]).astype(BF16)

    o_ref[...] = jnp.dot(xn_ref[...], w_ref[...], preferred_element_type=F32).astype(o_ref.dtype)


def _norm_matmul(x, g, w, tm, tn, out_dtype=F32):
    M, K = x.shape
    N = w.shape[1]
    return pl.pallas_call(
        _norm_matmul_kernel,
        grid=(M // tm, N // tn),
        in_specs=[pl.BlockSpec((tm, K), lambda i, j: (i, 0)),
                  pl.BlockSpec((1, K), lambda i, j: (0, 0)),
                  pl.BlockSpec((K, tn), lambda i, j: (0, j))],
        out_specs=pl.BlockSpec((tm, tn), lambda i, j: (i, j)),
        out_shape=jax.ShapeDtypeStruct((M, N), out_dtype),
        scratch_shapes=[pltpu.VMEM((tm, K), BF16)],
        compiler_params=_params("parallel", "arbitrary"),
        name="norm_matmul",
    )(x, g.reshape(1, K), w)


IN_CHUNK = 512
HALO = 8


def _rms(x, g):
    return x * lax.rsqrt(jnp.mean(x * x, axis=-1, keepdims=True) + EPS) * g


def _in_proj_kernel(x_ref, xb_ref, xa_ref, g_ref, w_ref, wqk_ref, wg_ref, cw_ref, cb_ref,
                    proj_ref, qk_ref, og_ref, xn_ref, *, tiles_per_seq):
    tm = x_ref.shape[0]
    g = g_ref[...]
    xn_ref[...] = _rms(x_ref[...], g).astype(BF16)
    og_ref[...] = jnp.dot(xn_ref[...], wg_ref[...], preferred_element_type=F32)
    for c in range(w_ref.shape[1] // IN_CHUNK):
        cols = slice(c * IN_CHUNK, (c + 1) * IN_CHUNK)
        proj_ref[:, cols] = jnp.dot(xn_ref[...], w_ref[:, cols], preferred_element_type=F32).astype(proj_ref.dtype)

    halo = _rms(jnp.concatenate([xb_ref[...], xa_ref[...]], axis=0), g).astype(BF16)
    pos = pl.program_id(0) % tiles_per_seq
    row = lax.broadcasted_iota(jnp.int32, (tm, IN_CHUNK), 0)
    for c in range(wqk_ref.shape[1] // IN_CHUNK):
        cols = slice(c * IN_CHUNK, (c + 1) * IN_CHUNK)
        u = jnp.dot(xn_ref[...], wqk_ref[:, cols], preferred_element_type=F32)
        uh = jnp.dot(halo, wqk_ref[:, cols], preferred_element_type=F32)
        before = jnp.where(pos == 0, 0.0, uh[HALO - 1:HALO, :])
        after = jnp.where(pos == tiles_per_seq - 1, 0.0, uh[HALO:HALO + 1, :])
        prev = jnp.where(row == 0, before, pltpu.roll(u, 1, 0))
        nxt = jnp.where(row == tm - 1, after, pltpu.roll(u, tm - 1, 0))
        y = prev * cw_ref[0:1, cols] + u * cw_ref[1:2, cols] + nxt * cw_ref[2:3, cols] + cb_ref[:, cols]
        qk_ref[:, cols] = (y * jax.nn.sigmoid(y)).astype(qk_ref.dtype)


def _in_proj(x, g, w, w_qk, w_gate, conv_w, conv_b, seq_len, tm):
    M, K = x.shape
    N, nqk, ng = w.shape[1], w_qk.shape[1], w_gate.shape[1]
    hb = tm // HALO
    const = lambda a: pl.BlockSpec(a.shape, lambda i: (0,) * a.ndim, pipeline_mode=pl.Buffered(1))
    row = lambda n: pl.BlockSpec((tm, n), lambda i: (i, 0))
    g = g.reshape(1, K)
    conv_b = conv_b.reshape(1, nqk)
    return pl.pallas_call(
        functools.partial(_in_proj_kernel, tiles_per_seq=seq_len // tm),
        grid=(M // tm,),
        in_specs=[row(K),
                  pl.BlockSpec((HALO, K), lambda i: (jnp.maximum(i * hb - 1, 0), 0)),
                  pl.BlockSpec((HALO, K), lambda i: (jnp.minimum((i + 1) * hb, M // HALO - 1), 0)),
                  const(g), const(w), const(w_qk), const(w_gate), const(conv_w), const(conv_b)],
        out_specs=[row(N), row(nqk), row(ng)],
        out_shape=[jax.ShapeDtypeStruct((M, N), BF16), jax.ShapeDtypeStruct((M, nqk), BF16),
                   jax.ShapeDtypeStruct((M, ng), F32)],
        scratch_shapes=[pltpu.VMEM((tm, K), BF16)],
        compiler_params=_params("parallel"),
        name="in_proj",
    )(x, x, x, g, w, w_qk, w_gate, conv_w, conv_b)


def _conv3_rows(p_ref, r, rows, w_ref, b_ref):
    L = p_ref.shape[1]
    r0 = pl.multiple_of(r * rows, rows)
    u = p_ref[0, pl.ds(r0, rows), :].astype(F32)
    row = lax.broadcasted_iota(jnp.int32, u.shape, 0)
    before = p_ref[0, pl.ds(pl.multiple_of(jnp.maximum(r0 - 16, 0), 16), 16), :].astype(F32)[15:16, :]
    after = p_ref[0, pl.ds(pl.multiple_of(jnp.minimum(r0 + rows, L - 16), 16), 16), :].astype(F32)[0:1, :]
    before = jnp.where(r == 0, 0.0, before)
    after = jnp.where(r == L // rows - 1, 0.0, after)
    prev = jnp.where(row == 0, before, pltpu.roll(u, 1, 0))
    nxt = jnp.where(row == rows - 1, after, pltpu.roll(u, rows - 1, 0))
    return prev * w_ref[0:1, :] + u * w_ref[1:2, :] + nxt * w_ref[2:3, :] + b_ref[...]


def _filter_kernel(feat_ref, w1_ref, b1_ref, f1_ref, w2_ref, b2_ref, f2_ref, w3_ref, dec_ref, dir_ref,
                   o_ref, *, L):
    tl = o_ref.shape[0]
    hp = lax.Precision.HIGHEST
    pos = (lax.broadcasted_iota(jnp.int32, (tl, LANES), 0) + pl.program_id(0) * tl).astype(F32)
    lane = lax.broadcasted_iota(jnp.int32, (tl, LANES), 1)
    t = pos * (1.0 / (L - 1))
    arg = feat_ref[...] * (pos * (2.0 * math.pi / L))
    z = jnp.where(lane < FILTER_BANDS, jnp.cos(arg),
                  jnp.where(lane < 2 * FILTER_BANDS, -jnp.sin(arg),
                            jnp.where(lane == 2 * FILTER_BANDS, t, 0.0)))
    hid = jnp.sin(f1_ref[...] * (jnp.dot(z, w1_ref[...], precision=hp, preferred_element_type=F32) + b1_ref[...]))
    hid = jnp.sin(f2_ref[...] * (jnp.dot(hid, w2_ref[...], precision=hp, preferred_element_type=F32) + b2_ref[...]))
    filt = jnp.dot(hid, w3_ref[...], precision=hp, preferred_element_type=F32)
    filt = filt * jnp.exp(-t[:, 0:1] * dec_ref[...])
    filt = jnp.where(pos[:, 0:1] == 0.0, filt * dir_ref[...], filt)
    o_ref[...] = filt


def _hyena_filters(L, w1, b1, fr1, w2, b2, fr2, w3, d_hyena, tl=512):
    n_emb, n_hid = w1.shape
    n_out = w3.shape[1]
    bands = jnp.linspace(1e-4, FILTER_BANDS - 1, FILTER_BANDS, dtype=F32)
    feat = jnp.zeros((1, LANES), F32).at[0, :FILTER_BANDS].set(bands).at[0, FILTER_BANDS:2 * FILTER_BANDS].set(bands)
    w1p = jnp.zeros((LANES, n_hid), F32).at[:n_emb - 1].set(w1[1:]).at[n_emb - 1].set(w1[0])
    max_decay = math.log(DECAY_TARGET) / SHORT_DECAY_PCT
    min_decay = math.log(DECAY_TARGET) / LONG_DECAY_PCT
    deltas = jnp.abs(jnp.linspace(min_decay, max_decay, d_hyena, dtype=F32))
    reps = n_out // d_hyena
    dec = jnp.tile(deltas, reps).reshape(1, n_out)
    dirmask = jnp.tile(jnp.concatenate([jnp.ones((d_hyena,), F32), jnp.zeros((d_hyena,), F32)]), reps // 2)
    full = lambda shape: pl.BlockSpec(shape, lambda i: (0,) * len(shape))
    return pl.pallas_call(
        functools.partial(_filter_kernel, L=L),
        grid=(L // tl,),
        in_specs=[full((1, LANES)), full((LANES, n_hid)), full((1, n_hid)), full((1, n_hid)),
                  full((n_hid, n_hid)), full((1, n_hid)), full((1, n_hid)), full((n_hid, n_out)),
                  full((1, n_out)), full((1, n_out))],
        out_specs=pl.BlockSpec((tl, n_out), lambda i: (i, 0)),
        out_shape=jax.ShapeDtypeStruct((L, n_out), F32),
        compiler_params=_params("parallel"),
        name="hyena_filters",
    )(feat, w1p, b1.reshape(1, -1), fr1.reshape(1, -1), w2, b2.reshape(1, -1), fr2.reshape(1, -1), w3,
      dec, dirmask.reshape(1, n_out))


MID_ROWS = 4096
SIDE_PAD = 16


def _fft_tables(L):
    N = 2 * L
    N2 = LANES
    N1 = N // N2
    H1 = N1 // 2
    k1 = np.arange(H1)[:, None]
    n1 = np.arange(H1)[None, :]
    a1 = 2.0 * np.pi * ((k1 * n1) % N1) / N1
    c1, s1 = np.cos(a1), np.sin(a1)
    sign = np.where(np.arange(H1) % 2 == 0, 1.0, -1.0)[None, :]
    pad = np.zeros((SIDE_PAD - 1, H1))
    m1 = np.concatenate([c1, -s1, sign, pad], axis=0)
    wgt = np.where(np.arange(H1) == 0, 1.0, 2.0)[:, None]
    minv = np.concatenate([(wgt * c1).T, (-wgt * s1).T, sign.T, pad.T], axis=1)
    n2 = np.arange(N2)[:, None]
    k2 = np.arange(N2)[None, :]
    a2 = 2.0 * np.pi * ((n2 * k2) % N2) / N2
    cg, sg = np.cos(a2), np.sin(a2)
    g2 = np.block([[cg, -sg], [sg, cg]])
    g2i = np.block([[cg, sg], [-sg, cg]])
    at = 2.0 * np.pi * (np.arange(H1 + 8)[:, None] * np.arange(N2)[None, :]) / N
    tw = np.concatenate([np.cos(at), -np.sin(at)], axis=1)
    as_bf = lambda a: jnp.asarray(a.astype(np.float32)).astype(BF16)
    return dict(m1=as_bf(m1), minv=as_bf(minv), g2=as_bf(g2), g2i=as_bf(g2i),
                tw=jnp.asarray(tw.astype(np.float32)), N1=N1, H1=H1)


S_PITCH = LANES + 8
GROUP = 8
STAGE_UNROLL = 8


SLAB_UNROLL = 4


def _slab_transpose_in(get_slab, s_ref, h1):
    def body(j, carry):
        for u in range(SLAB_UNROLL):
            i = j * SLAB_UNROLL + u
            s_ref[pl.ds(pl.multiple_of(i * S_PITCH, 8), LANES), :] = get_slab(i).T
        return carry

    lax.fori_loop(0, h1 // SLAB_UNROLL, body, 0)


def _slab_transpose_out(s_ref, put_slab, h1):
    def body(j, carry):
        for u in range(SLAB_UNROLL):
            i = j * SLAB_UNROLL + u
            put_slab(i, s_ref[pl.ds(pl.multiple_of(i * S_PITCH, 8), LANES), :].T)
        return carry

    lax.fori_loop(0, h1 // SLAB_UNROLL, body, 0)


def _slab_rows(i):
    return pl.ds(pl.multiple_of(i * LANES, LANES), LANES)


def _fft_stage_a(s_ref, nch, m1, tw_ref, ab_ref, side_ref, n1):
    h1 = n1 // 2
    twr = tw_ref[0:h1, 0:LANES]
    twi = tw_ref[0:h1, LANES:2 * LANES]
    tsr = tw_ref[h1:h1 + 1, 0:LANES]
    tsi = tw_ref[h1:h1 + 1, LANES:2 * LANES]

    def body(g, carry):
        for u in range(STAGE_UNROLL):
            c0 = (g * STAGE_UNROLL + u) * GROUP
            rhs = jnp.concatenate([s_ref[pl.ds(c0 + i, h1, stride=S_PITCH), :] for i in range(GROUP)], axis=1)
            res = jnp.dot(m1, rhs.astype(BF16), preferred_element_type=F32)
            for i in range(GROUP):
                ar = res[0:h1, LANES * i:LANES * (i + 1)]
                ai = res[h1:2 * h1, LANES * i:LANES * (i + 1)]
                rows = pl.ds(pl.multiple_of((c0 + i) * h1, h1), h1)
                ab_ref[rows, 0:LANES] = ar * twr - ai * twi
                ab_ref[rows, LANES:2 * LANES] = ar * twi + ai * twr
            a_side = jnp.concatenate([res[2 * h1:2 * h1 + 1, LANES * i:LANES * (i + 1)] for i in range(GROUP)],
                                     axis=0)
            rows = pl.ds(pl.multiple_of(c0, GROUP), GROUP)
            side_ref[rows, 0:LANES] = a_side * tsr
            side_ref[rows, LANES:2 * LANES] = a_side * tsi
        return carry

    lax.fori_loop(0, nch // (GROUP * STAGE_UNROLL), body, 0)


def _cmul(a, b):
    ar, ai = a[:, 0:LANES], a[:, LANES:2 * LANES]
    br, bi = b[:, 0:LANES], b[:, LANES:2 * LANES]
    return jnp.concatenate([ar * br - ai * bi, ar * bi + ai * br], axis=1)


def _fft_mid(ab_ref, side_ref, get_h, get_h_side, g2, g2i, nrows):
    chunk = min(MID_ROWS, nrows)

    def body(r, carry):
        rows = pl.ds(pl.multiple_of(r * chunk, chunk), chunk)
        spec = _bdot(ab_ref[rows, :], g2)
        ab_ref[rows, :] = _bdot(_cmul(spec, get_h(rows)), g2i)
        return carry

    lax.fori_loop(0, nrows // chunk, body, 0)
    side_ref[...] = _bdot(_cmul(_bdot(side_ref[...], g2), get_h_side()), g2i)


def _fft_stage_a_inv(ab_ref, side_ref, nch, minv, tw_ref, s_ref, n1):
    h1 = n1 // 2
    twr = tw_ref[0:h1, 0:LANES]
    twi = tw_ref[0:h1, LANES:2 * LANES]
    tsr = tw_ref[h1:h1 + 1, 0:LANES]
    tsi = tw_ref[h1:h1 + 1, LANES:2 * LANES]
    first_row = lax.broadcasted_iota(jnp.int32, (SIDE_PAD, GROUP * LANES), 0) == 0

    def body(g, carry):
        for u in range(STAGE_UNROLL):
            c0 = (g * STAGE_UNROLL + u) * GROUP
            blk = ab_ref[pl.ds(pl.multiple_of(c0 * h1, GROUP * h1), GROUP * h1), :]
            re, im = [], []
            for i in range(GROUP):
                br = blk[i * h1:(i + 1) * h1, 0:LANES]
                bi = blk[i * h1:(i + 1) * h1, LANES:2 * LANES]
                re.append(br * twr + bi * twi)
                im.append(bi * twr - br * twi)
            side = side_ref[pl.ds(pl.multiple_of(c0, GROUP), GROUP), :]
            side_re = side[:, 0:LANES] * tsr + side[:, LANES:2 * LANES] * tsi
            side_row = jnp.concatenate([side_re[i:i + 1, :] for i in range(GROUP)], axis=1)
            side_blk = jnp.where(first_row, jnp.broadcast_to(side_row, first_row.shape), 0.0)
            rhs = jnp.concatenate([jnp.concatenate(re, axis=1), jnp.concatenate(im, axis=1), side_blk], axis=0)
            y = jnp.dot(minv, rhs.astype(BF16), preferred_element_type=F32)
            for i in range(GROUP):
                s_ref[pl.ds(c0 + i, h1, stride=S_PITCH), :] = y[:, LANES * i:LANES * (i + 1)]
        return carry

    lax.fori_loop(0, nch // (GROUP * STAGE_UNROLL), body, 0)


def _filter_spectrum_kernel(hf_ref, hb_ref, m1_ref, g2_ref, tw_ref, o_ref, oside_ref,
                            s_ref, af_ref, ab_ref, afs_ref, abs_ref, *, n1, inv_n):
    nch = hf_ref.shape[1]
    h1 = n1 // 2
    m1, g2 = m1_ref[...], g2_ref[...]
    _slab_transpose_in(lambda i: hf_ref[_slab_rows(i), :], s_ref, h1)
    _fft_stage_a(s_ref, nch, m1, tw_ref, af_ref, afs_ref, n1)
    _slab_transpose_in(lambda i: hb_ref[_slab_rows(i), :], s_ref, h1)
    _fft_stage_a(s_ref, nch, m1, tw_ref, ab_ref, abs_ref, n1)

    def two_sided(fwd, bwd):
        sf = _bdot(fwd, g2)
        sb = _bdot(bwd, g2)
        return jnp.concatenate([sf[:, 0:LANES] + sb[:, 0:LANES], sf[:, LANES:] - sb[:, LANES:]], axis=1) * inv_n

    chunk = min(MID_ROWS, nch * h1)

    def body(r, carry):
        rows = pl.ds(pl.multiple_of(r * chunk, chunk), chunk)
        o_ref[rows, :] = two_sided(af_ref[rows, :], ab_ref[rows, :])
        return carry

    lax.fori_loop(0, (nch * h1) // chunk, body, 0)
    oside_ref[...] = two_sided(afs_ref[...], abs_ref[...])


def _filter_spectrum(filt, L, d_hyena, tabs):
    h1 = tabs["H1"]
    order = filt.shape[1] // (2 * d_hyena)
    cb = LANES
    nblk = d_hyena // cb
    full = lambda a: pl.BlockSpec(a.shape, lambda o, c: (0,) * a.ndim)
    main = pltpu.VMEM((cb * h1, 2 * LANES), F32)
    side = pltpu.VMEM((cb, 2 * LANES), F32)
    return pl.pallas_call(
        functools.partial(_filter_spectrum_kernel, n1=tabs["N1"], inv_n=1.0 / (2 * L)),
        grid=(order, nblk),
        in_specs=[pl.BlockSpec((L, cb), lambda o, c: (0, o * 2 * nblk + c)),
                  pl.BlockSpec((L, cb), lambda o, c: (0, o * 2 * nblk + nblk + c)),
                  full(tabs["m1"]), full(tabs["g2"]), full(tabs["tw"])],
        out_specs=[pl.BlockSpec((cb * h1, 2 * LANES), lambda o, c: (o * nblk + c, 0)),
                   pl.BlockSpec((cb, 2 * LANES), lambda o, c: (o * nblk + c, 0))],
        out_shape=[jax.ShapeDtypeStruct((order * d_hyena * h1, 2 * LANES), F32),
                   jax.ShapeDtypeStruct((order * d_hyena, 2 * LANES), F32)],
        scratch_shapes=[pltpu.VMEM((h1 * S_PITCH, LANES), F32), main, main, side, side],
        compiler_params=_params("parallel", "parallel"),
        name="filter_spectrum",
    )(filt, filt, tabs["m1"], tabs["g2"], tabs["tw"])


def _hyena_kernel(pv_ref, px1_ref, px2_ref, wv_ref, wx1_ref, wx2_ref, bv_ref, bx1_ref, bx2_ref, h_ref, hside_ref,
                  d_ref, m1_ref, minv_ref, g2_ref, g2i_ref, tw_ref, o_ref, s_ref, ab_ref, side_ref, v_ref, z_ref,
                  *, n1):
    nch = pv_ref.shape[2]
    h1 = n1 // 2
    m1, minv, g2, g2i = m1_ref[...], minv_ref[...], g2_ref[...], g2i_ref[...]

    def long_conv(get_slab, order, put_slab):
        _slab_transpose_in(get_slab, s_ref, h1)
        _fft_stage_a(s_ref, nch, m1, tw_ref, ab_ref, side_ref, n1)
        _fft_mid(ab_ref, side_ref, lambda rows: h_ref[order, rows, :], lambda: hside_ref[order], g2, g2i, nch * h1)
        _fft_stage_a_inv(ab_ref, side_ref, nch, minv, tw_ref, s_ref, n1)
        _slab_transpose_out(s_ref, put_slab, h1)

    def get_v(i):
        v = _conv3_rows(pv_ref, i, LANES, wv_ref, bv_ref)
        v_ref[_slab_rows(i), :] = v
        return v

    def put1(i, conv):
        rows = _slab_rows(i)
        x1 = _conv3_rows(px1_ref, i, LANES, wx1_ref, bx1_ref)
        z_ref[rows, :] = x1 * (conv + d_ref[0:1, :] * v_ref[rows, :])

    long_conv(get_v, 0, put1)

    def put2(i, conv):
        rows = _slab_rows(i)
        x2 = _conv3_rows(px2_ref, i, LANES, wx2_ref, bx2_ref)
        o_ref[0, rows, :] = (x2 * (conv + d_ref[1:2, :] * z_ref[rows, :])).astype(o_ref.dtype)

    long_conv(lambda i: z_ref[_slab_rows(i), :], 1, put2)


def _hyena(proj, conv_w, conv_b, spec, spec_side, skip, tabs):
    B, L, _ = proj.shape
    D = skip.shape[1]
    n1, h1 = tabs["N1"], tabs["H1"]
    cb = LANES
    nblk = D // cb
    spec = spec.reshape(2, D * h1, 2 * LANES)
    spec_side = spec_side.reshape(2, D, 2 * LANES)
    col = lambda part: pl.BlockSpec((1, L, cb), lambda c, b: (b, 0, part * nblk + c))
    wcol = lambda part, rows: pl.BlockSpec((rows, cb), lambda c, b: (0, part * nblk + c))
    full = lambda a: pl.BlockSpec(a.shape, lambda c, b: (0,) * a.ndim)
    conv_b = conv_b.reshape(1, 3 * D)
    return pl.pallas_call(
        functools.partial(_hyena_kernel, n1=n1),
        grid=(nblk, B),
        in_specs=[col(0), col(1), col(2), wcol(0, 3), wcol(1, 3), wcol(2, 3), wcol(0, 1), wcol(1, 1), wcol(2, 1),
                  pl.BlockSpec((2, cb * h1, 2 * LANES), lambda c, b: (0, c, 0), pipeline_mode=pl.Buffered(1)),
                  pl.BlockSpec((2, cb, 2 * LANES), lambda c, b: (0, c, 0)),
                  pl.BlockSpec((2, cb), lambda c, b: (0, c)),
                  full(tabs["m1"]), full(tabs["minv"]), full(tabs["g2"]), full(tabs["g2i"]), full(tabs["tw"])],
        out_specs=pl.BlockSpec((1, L, cb), lambda c, b: (b, 0, c)),
        out_shape=jax.ShapeDtypeStruct((B, L, D), BF16),
        scratch_shapes=[pltpu.VMEM((h1 * S_PITCH, LANES), F32), pltpu.VMEM((cb * h1, 2 * LANES), F32),
                        pltpu.VMEM((cb, 2 * LANES), F32), pltpu.VMEM((L, cb), F32), pltpu.VMEM((L, cb), F32)],
        compiler_params=_params("parallel", "arbitrary"),
        name="hyena_fftconv",
    )(proj, proj, proj, conv_w, conv_w, conv_w, conv_b, conv_b, conv_b, spec, spec_side, skip.astype(F32),
      tabs["m1"], tabs["minv"], tabs["g2"], tabs["g2i"], tabs["tw"])


ML_CHUNK = 256


def _log_sigmoid(x):
    return jnp.minimum(x, 0.0) - jnp.log1p(jnp.exp(-jnp.abs(x)))


def _mlstm_direction(q, k, v, i_row, cum_row, logf_row, c_ref, m_ref, mask, scale):
    c, dh = q.shape
    g_row = i_row - cum_row
    f_col_rep = jnp.broadcast_to(cum_row, (LANES, c)).T
    m_prev = m_ref[0:1, 0:1]
    e = jnp.where(mask, g_row, -jnp.inf)
    a = jnp.maximum(jnp.max(e, axis=1, keepdims=True), m_prev)
    w = jnp.exp(e - a)
    sc = lax.dot_general(q.astype(BF16), k.astype(BF16), (((1,), (1,)), ((), ())), preferred_element_type=F32)
    p = sc * (w * scale)
    vaug = jnp.concatenate([v, jnp.ones_like(v)], axis=1)
    inter = jnp.exp(m_prev - a)
    tot = _bdot(p, vaug) + inter * _bdot(q, c_ref[...])
    num, den = tot[:, 0:dh], tot[:, dh:2 * dh]
    m_t = f_col_rep + a
    h = num / jnp.maximum(jnp.abs(den), jnp.exp(-m_t))
    a_end = jnp.maximum(jnp.max(g_row, axis=1, keepdims=True), m_prev)
    f_end = jnp.sum(logf_row, axis=1, keepdims=True)
    we = jnp.exp(g_row - a_end) * scale
    kw = k.astype(F32).T * we
    c_ref[...] = jnp.exp(m_prev - a_end) * c_ref[...] + _bdot(kw, vaug)
    m_ref[...] = jnp.broadcast_to(f_end + a_end, m_ref.shape)
    return h


def _mlstm_kernel(qf_ref, kf_ref, vf_ref, gf_ref, qb_ref, kb_ref, vb_ref, gb_ref, bias_ref,
                  hf_ref, hb_ref, c_ref, m_ref, *, heads, scale):
    @pl.when(pl.program_id(1) == 0)
    def _():
        c_ref[...] = jnp.zeros_like(c_ref)
        m_ref[...] = jnp.zeros_like(m_ref)

    c = qf_ref.shape[1]
    dh = qf_ref.shape[2] // heads
    r = lax.broadcasted_iota(jnp.int32, (c, c), 0)
    s = lax.broadcasted_iota(jnp.int32, (c, c), 1)
    mask_f = s <= r
    mask_b = s >= r

    def gate_rows(g_ref, mask_t):
        g = g_ref[0] + bias_ref[...]
        logf = _log_sigmoid(g)
        tri = mask_t.astype(BF16)
        cum = sum(jnp.dot(part, tri, preferred_element_type=F32) for part in _split3(logf))
        return g, logf, cum

    g_f, logf_f, cum_f = gate_rows(gf_ref, mask_b)
    g_b, logf_b, cum_b = gate_rows(gb_ref, mask_f)
    for h in range(heads):
        sl = slice(h * dh, (h + 1) * dh)
        fi, ff, bi, bf = h, heads + h, 2 * heads + h, 3 * heads + h
        hf_ref[0, :, sl] = _mlstm_direction(
            qf_ref[0, :, sl], kf_ref[0, :, sl], vf_ref[0, :, sl], g_f[fi:fi + 1], cum_f[ff:ff + 1],
            logf_f[ff:ff + 1], c_ref.at[h], m_ref.at[h], mask_f, scale).astype(hf_ref.dtype)
        hb_ref[0, :, sl] = _mlstm_direction(
            qb_ref[0, :, sl], kb_ref[0, :, sl], vb_ref[0, :, sl], g_b[bi:bi + 1], cum_b[bf:bf + 1],
            logf_b[bf:bf + 1], c_ref.at[heads + h], m_ref.at[heads + h], mask_b, scale).astype(hb_ref.dtype)


def _mlstm(qk, proj, v_col0, gates, gate_b, d_ml):
    B, L, _ = qk.shape
    H = MLSTM_HEADS
    dh = d_ml // H
    c = min(ML_CHUNK, L)
    nc = L // c
    g = gates.transpose(0, 2, 1)
    bias = jnp.broadcast_to(gate_b.astype(F32)[:, None], (4 * H, c))
    vb0 = v_col0 // d_ml
    specs = []
    for pos in (lambda ci: ci, lambda ci: nc - 1 - ci):
        specs += [pl.BlockSpec((1, c, d_ml), lambda bi, ci, pos=pos: (bi, pos(ci), 0)),
                  pl.BlockSpec((1, c, d_ml), lambda bi, ci, pos=pos: (bi, pos(ci), 1)),
                  pl.BlockSpec((1, c, d_ml), lambda bi, ci, pos=pos: (bi, pos(ci), vb0)),
                  pl.BlockSpec((1, 4 * H, c), lambda bi, ci, pos=pos: (bi, 0, pos(ci)))]
    specs.append(pl.BlockSpec((4 * H, c), lambda bi, ci: (0, 0)))
    out_shape = jax.ShapeDtypeStruct((B, L, d_ml), BF16)
    return pl.pallas_call(
        functools.partial(_mlstm_kernel, heads=H, scale=dh ** -0.5),
        grid=(B, nc),
        in_specs=specs,
        out_specs=[pl.BlockSpec((1, c, d_ml), lambda bi, ci: (bi, ci, 0)),
                   pl.BlockSpec((1, c, d_ml), lambda bi, ci: (bi, nc - 1 - ci, 0))],
        out_shape=[out_shape, out_shape],
        scratch_shapes=[pltpu.VMEM((2 * H, dh, 2 * dh), F32), pltpu.VMEM((2 * H, 1, LANES), F32)],
        compiler_params=_params("parallel", "arbitrary"),
        name="mlstm_scan",
    )(qk, qk, proj, g, qk, qk, proj, g, bias)


def _group_rms(y, gain, bd, group):
    ss = jnp.dot((y * y).astype(BF16), bd, preferred_element_type=F32)
    return y * lax.rsqrt(ss * (1.0 / group) + EPS) * gain


def _mix_out_kernel(z_ref, hf_ref, hb_ref, o_ref, x_ref, ghy_ref, gml_ref, why_ref, wml_ref, bdh_ref, bdm_ref,
                    out_ref, *, hy_group, ml_group):
    y_hy = _group_rms(z_ref[...].astype(F32), ghy_ref[...], bdh_ref[...], hy_group)
    h_sum = hf_ref[...].astype(F32) + hb_ref[...].astype(F32)
    y_ml = _group_rms(jax.nn.sigmoid(o_ref[...].astype(F32)) * h_sum, gml_ref[...], bdm_ref[...], ml_group)
    out_ref[...] = x_ref[...] + _bdot(y_hy, why_ref[...]) + _bdot(y_ml, wml_ref[...])


def _block_diag_ones(n, group):
    idx = np.arange(n) // group
    return jnp.asarray((idx[:, None] == idx[None, :]).astype(np.float32)).astype(BF16)


def _mix_out(z, hf, hb, proj, o_col0, x, g_hy, g_ml, w_out, tm=512):
    M, d_hy = z.shape
    d_ml = hf.shape[1]
    D = x.shape[1]
    ob = o_col0 // d_ml
    row = lambda n: pl.BlockSpec((tm, n), lambda i: (i, 0))
    full = lambda shape: pl.BlockSpec(shape, lambda i: (0, 0))
    hy_group, ml_group = d_hy // HYENA_GROUPS, d_ml // MLSTM_HEADS
    return pl.pallas_call(
        functools.partial(_mix_out_kernel, hy_group=hy_group, ml_group=ml_group),
        grid=(M // tm,),
        in_specs=[row(d_hy), row(d_ml), row(d_ml), pl.BlockSpec((tm, d_ml), lambda i: (i, ob)), row(D),
                  full((1, d_hy)), full((1, d_ml)), full((d_hy, D)), full((d_ml, D)),
                  full((d_hy, d_hy)), full((d_ml, d_ml))],
        out_specs=row(D),
        out_shape=jax.ShapeDtypeStruct((M, D), F32),
        compiler_params=_params("parallel"),
        name="mix_out",
    )(z, hf, hb, proj, x, g_hy.reshape(1, -1), g_ml.reshape(1, -1),
      w_out[:d_hy].astype(BF16), w_out[d_hy:].astype(BF16),
      _block_diag_ones(d_hy, hy_group), _block_diag_ones(d_ml, ml_group))


def _xattn_kernel(h_ref, g_ref, wq_ref, kv_ref, wo_ref, out_ref, *, heads, scale):
    x = h_ref[0]
    D = x.shape[1]
    dh = D // heads
    ms = jnp.mean(x * x, axis=-1, keepdims=True)
    xn = x * lax.rsqrt(ms + EPS) * g_ref[...]
    q = _bdot(xn, wq_ref[...])
    outs = []
    for hd in range(heads):
        qh = q[:, hd * dh:(hd + 1) * dh]
        kh = kv_ref[0, :, hd * dh:(hd + 1) * dh]
        vh = kv_ref[0, :, D + hd * dh:D + (hd + 1) * dh]
        s = lax.dot_general(qh.astype(BF16), kh.astype(BF16), (((1,), (1,)), ((), ())),
                            preferred_element_type=F32) * scale
        e = jnp.exp(s - jnp.max(s, axis=-1, keepdims=True))
        p = e / jnp.sum(e, axis=-1, keepdims=True)
        outs.append(_bdot(p, vh))
    o = jnp.concatenate(outs, axis=1)
    out_ref[0] = x + _bdot(o, wo_ref[...])


def _xattn(h, g, wq, kv, wo, tm=512):
    B, L, D = h.shape
    nm = kv.shape[1]
    full = lambda shape: pl.BlockSpec(shape, lambda bi, i: (0,) * len(shape))
    return pl.pallas_call(
        functools.partial(_xattn_kernel, heads=XATTN_HEADS, scale=(D // XATTN_HEADS) ** -0.5),
        grid=(B, L // tm),
        in_specs=[pl.BlockSpec((1, tm, D), lambda bi, i: (bi, i, 0)), full((1, D)), full((D, D)),
                  pl.BlockSpec((1, nm, 2 * D), lambda bi, i: (bi, 0, 0)), full((D, D))],
        out_specs=pl.BlockSpec((1, tm, D), lambda bi, i: (bi, i, 0)),
        out_shape=jax.ShapeDtypeStruct((B, L, D), F32),
        compiler_params=_params("parallel", "parallel"),
        name="xattn",
    )(h, g.reshape(1, D), wq.astype(BF16), kv, wo.astype(BF16))


def _mlp_kernel(h_ref, g_ref, w1_ref, w2_ref, gf_ref, out_ref, xn_ref, acc_ref, *, final_norm):
    j = pl.program_id(1)

    @pl.when(j == 0)
    def _():
        x = h_ref[...]
        ms = jnp.mean(x * x, axis=-1, keepdims=True)
        xn_ref[...] = (x * lax.rsqrt(ms + EPS) * g_ref[...]).astype(BF16)
        acc_ref[...] = x

    a = jnp.maximum(jnp.dot(xn_ref[...], w1_ref[...], preferred_element_type=F32), 0.0)
    acc_ref[...] += _bdot(a * a, w2_ref[...])

    @pl.when(j == pl.num_programs(1) - 1)
    def _():
        y = acc_ref[...]
        if final_norm:
            ms = jnp.mean(y * y, axis=-1, keepdims=True)
            y = y * lax.rsqrt(ms + EPS) * gf_ref[...]
        out_ref[...] = y


def _mlp(h, g, w1, w2, gf, final_norm, tm=1024, tf=512):
    M, D = h.shape
    dff = w1.shape[1]
    return pl.pallas_call(
        functools.partial(_mlp_kernel, final_norm=final_norm),
        grid=(M // tm, dff // tf),
        in_specs=[pl.BlockSpec((tm, D), lambda i, j: (i, 0)), pl.BlockSpec((1, D), lambda i, j: (0, 0)),
                  pl.BlockSpec((D, tf), lambda i, j: (0, j)), pl.BlockSpec((tf, D), lambda i, j: (j, 0)),
                  pl.BlockSpec((1, D), lambda i, j: (0, 0))],
        out_specs=pl.BlockSpec((tm, D), lambda i, j: (i, 0)),
        out_shape=jax.ShapeDtypeStruct((M, D), F32),
        scratch_shapes=[pltpu.VMEM((tm, D), BF16), pltpu.VMEM((tm, D), F32)],
        compiler_params=_params("parallel", "arbitrary"),
        name="mlp",
    )(h, g.reshape(1, D), w1.astype(BF16), w2.astype(BF16), gf.reshape(1, D))


def _pad_cols(w, n):
    return jnp.pad(w, ((0, 0), (0, n - w.shape[1])))


def _layer(h, mem, p, l):
    B, L, D = h.shape
    M = B * L
    d_hy = p["hy_norm_g"].shape[1]
    d_ml = p["ml_norm_g"].shape[1]
    hy_cols = 3 * d_hy
    qk_cols = 2 * d_ml
    n_gate = 4 * MLSTM_HEADS
    w_in = p["w_in"][l]
    qk0 = hy_cols
    vo0 = qk0 + qk_cols
    g0 = vo0 + 2 * d_ml
    w_main = jnp.concatenate([w_in[:, :hy_cols], w_in[:, vo0:g0]], axis=1).astype(BF16)
    v_col0 = hy_cols
    o_col0 = hy_cols + d_ml
    n_in = hy_cols + 2 * d_ml
    proj, qk, gates = _in_proj(h.reshape(M, D), p["norm_mix_g"][l], w_main, w_in[:, qk0:vo0].astype(BF16),
                               _pad_cols(w_in[:, g0:g0 + n_gate], LANES).astype(BF16),
                               p["ml_conv_w"][l], p["ml_conv_b"][l], seq_len=L, tm=min(1024, L))
    proj = proj.reshape(B, L, n_in)
    qk = qk.reshape(B, L, qk_cols)
    gates = gates[:, :n_gate].reshape(B, L, n_gate)

    tabs = _fft_tables(L)
    filt = _hyena_filters(L, p["hy_filt_w1"][l], p["hy_filt_b1"][l], p["hy_filt_freq1"][l], p["hy_filt_w2"][l],
                          p["hy_filt_b2"][l], p["hy_filt_freq2"][l], p["hy_filt_w3"][l], d_hy,
                          tl=min(512, L))
    spec, spec_side = _filter_spectrum(filt, L, d_hy, tabs)
    z_hy = _hyena(proj, p["hy_conv_w"][l], p["hy_conv_b"][l], spec, spec_side, p["hy_skip"][l], tabs)

    h_f, h_b = _mlstm(qk, proj, v_col0, gates, p["ml_gate_b"][l], d_ml)

    h1 = _mix_out(z_hy.reshape(M, d_hy), h_f.reshape(M, d_ml), h_b.reshape(M, d_ml), proj.reshape(M, n_in),
                  o_col0, h.reshape(M, D), p["hy_norm_g"][l], p["ml_norm_g"][l], p["w_out"][l],
                  tm=min(512, M))

    nm = mem.shape[1]
    wkv = jnp.concatenate([p["xa_wk"][l], p["xa_wv"][l]], axis=1).astype(BF16)
    kv = _norm_matmul(mem.reshape(B * nm, D), p["norm_mem_g"][l], wkv, tm=min(512, B * nm), tn=512)
    h2 = _xattn(h1.reshape(B, L, D), p["norm_x_g"][l], p["xa_wq"][l], kv.reshape(B, nm, 2 * D), p["xa_wo"][l],
                tm=min(1024, L))
    return h2


def kernel(x, mem, norm_mix_g, w_in, hy_conv_w, hy_conv_b, hy_filt_w1, hy_filt_b1, hy_filt_freq1, hy_filt_w2,
           hy_filt_b2, hy_filt_freq2, hy_filt_w3, hy_skip, hy_norm_g, ml_conv_w, ml_conv_b, ml_gate_b, ml_norm_g,
           w_out, norm_x_g, norm_mem_g, xa_wq, xa_wk, xa_wv, xa_wo, norm_ff_g, ff_w1, ff_w2, final_norm_g):
    p = dict(norm_mix_g=norm_mix_g, w_in=w_in, hy_conv_w=hy_conv_w, hy_conv_b=hy_conv_b, hy_filt_w1=hy_filt_w1,
             hy_filt_b1=hy_filt_b1, hy_filt_freq1=hy_filt_freq1, hy_filt_w2=hy_filt_w2, hy_filt_b2=hy_filt_b2,
             hy_filt_freq2=hy_filt_freq2, hy_filt_w3=hy_filt_w3, hy_skip=hy_skip, hy_norm_g=hy_norm_g,
             ml_conv_w=ml_conv_w, ml_conv_b=ml_conv_b, ml_gate_b=ml_gate_b, ml_norm_g=ml_norm_g, w_out=w_out,
             norm_x_g=norm_x_g, norm_mem_g=norm_mem_g, xa_wq=xa_wq, xa_wk=xa_wk, xa_wv=xa_wv, xa_wo=xa_wo)
    B, L, D = x.shape
    depth = w_in.shape[0]
    h = x
    for l in range(depth):
        h = _layer(h, mem, p, l)
        h = _mlp(h.reshape(B * L, D), norm_ff_g[l], ff_w1[l], ff_w2[l], final_norm_g, final_norm=l == depth - 1,
                 tm=min(1024, B * L)).reshape(B, L, D)
    return h
```

```python
import functools
import math

import numpy as np
import jax
import jax.numpy as jnp
from jax import lax
from jax.experimental import pallas as pl
from jax.experimental.pallas import tpu as pltpu

F32 = jnp.float32
BF16 = jnp.bfloat16

EPS = 1e-6
HYENA_GROUPS = 8
MLSTM_HEADS = 4
XATTN_HEADS = 4
FILTER_BANDS = 16
DECAY_TARGET = 1e-2
SHORT_DECAY_PCT = 0.3
LONG_DECAY_PCT = 1.5

LANES = 128
VMEM_LIMIT = 56 * 1024 * 1024


def _params(*sem):
    return pltpu.CompilerParams(dimension_semantics=sem, vmem_limit_bytes=VMEM_LIMIT)


def _bdot(a, b):
    return jnp.dot(a.astype(BF16), b.astype(BF16), preferred_element_type=F32)


def _split3(x):
    hi = x.astype(BF16)
    r = x - hi.astype(F32)
    mid = r.astype(BF16)
    lo = (r - mid.astype(F32)).astype(BF16)
    return hi, mid, lo


IN_CHUNK = 512
HALO = 8


def _rms(x, g):
    return x * lax.rsqrt(jnp.mean(x * x, axis=-1, keepdims=True) + EPS) * g


def _in_proj_kernel(x_ref, xb_ref, xa_ref, g_ref, w_ref, wqk_ref, wg_ref, cw_ref, cb_ref,
                    proj_ref, qk_ref, og_ref, xn_ref, *, tiles_per_seq):
    tm = x_ref.shape[0]
    g = g_ref[...]
    xn_ref[...] = _rms(x_ref[...], g).astype(BF16)
    og_ref[...] = jnp.dot(xn_ref[...], wg_ref[...], preferred_element_type=F32)
    for c in range(w_ref.shape[1] // IN_CHUNK):
        cols = slice(c * IN_CHUNK, (c + 1) * IN_CHUNK)
        proj_ref[:, cols] = jnp.dot(xn_ref[...], w_ref[:, cols], preferred_element_type=F32).astype(proj_ref.dtype)

    halo = _rms(jnp.concatenate([xb_ref[...], xa_ref[...]], axis=0), g).astype(BF16)
    pos = pl.program_id(0) % tiles_per_seq
    row = lax.broadcasted_iota(jnp.int32, (tm, IN_CHUNK), 0)
    for c in range(wqk_ref.shape[1] // IN_CHUNK):
        cols = slice(c * IN_CHUNK, (c + 1) * IN_CHUNK)
        u = jnp.dot(xn_ref[...], wqk_ref[:, cols], preferred_element_type=F32)
        uh = jnp.dot(halo, wqk_ref[:, cols], preferred_element_type=F32)
        before = jnp.where(pos == 0, 0.0, uh[HALO - 1:HALO, :])
        after = jnp.where(pos == tiles_per_seq - 1, 0.0, uh[HALO:HALO + 1, :])
        prev = jnp.where(row == 0, before, pltpu.roll(u, 1, 0))
        nxt = jnp.where(row == tm - 1, after, pltpu.roll(u, tm - 1, 0))
        y = prev * cw_ref[0:1, cols] + u * cw_ref[1:2, cols] + nxt * cw_ref[2:3, cols] + cb_ref[:, cols]
        qk_ref[:, cols] = (y * jax.nn.sigmoid(y)).astype(qk_ref.dtype)


def _in_proj(x, g, w, w_qk, w_gate, conv_w, conv_b, seq_len, tm):
    M, K = x.shape
    N, nqk, ng = w.shape[1], w_qk.shape[1], w_gate.shape[1]
    hb = tm // HALO
    const = lambda a: pl.BlockSpec(a.shape, lambda i: (0,) * a.ndim, pipeline_mode=pl.Buffered(1))
    row = lambda n: pl.BlockSpec((tm, n), lambda i: (i, 0))
    g = g.reshape(1, K)
    conv_b = conv_b.reshape(1, nqk)
    return pl.pallas_call(
        functools.partial(_in_proj_kernel, tiles_per_seq=seq_len // tm),
        grid=(M // tm,),
        in_specs=[row(K),
                  pl.BlockSpec((HALO, K), lambda i: (jnp.maximum(i * hb - 1, 0), 0)),
                  pl.BlockSpec((HALO, K), lambda i: (jnp.minimum((i + 1) * hb, M // HALO - 1), 0)),
                  const(g), const(w), const(w_qk), const(w_gate), const(conv_w), const(conv_b)],
        out_specs=[row(N), row(nqk), row(ng)],
        out_shape=[jax.ShapeDtypeStruct((M, N), BF16), jax.ShapeDtypeStruct((M, nqk), BF16),
                   jax.ShapeDtypeStruct((M, ng), F32)],
        scratch_shapes=[pltpu.VMEM((tm, K), BF16)],
        compiler_params=_params("parallel"),
        name="in_proj",
    )(x, x, x, g, w, w_qk, w_gate, conv_w, conv_b)


def _conv3_rows(p_ref, r, rows, w_ref, b_ref):
    L = p_ref.shape[1]
    r0 = pl.multiple_of(r * rows, rows)
    u = p_ref[0, pl.ds(r0, rows), :].astype(F32)
    row = lax.broadcasted_iota(jnp.int32, u.shape, 0)
    before = p_ref[0, pl.ds(pl.multiple_of(jnp.maximum(r0 - 16, 0), 16), 16), :].astype(F32)[15:16, :]
    after = p_ref[0, pl.ds(pl.multiple_of(jnp.minimum(r0 + rows, L - 16), 16), 16), :].astype(F32)[0:1, :]
    before = jnp.where(r == 0, 0.0, before)
    after = jnp.where(r == L // rows - 1, 0.0, after)
    prev = jnp.where(row == 0, before, pltpu.roll(u, 1, 0))
    nxt = jnp.where(row == rows - 1, after, pltpu.roll(u, rows - 1, 0))
    return prev * w_ref[0:1, :] + u * w_ref[1:2, :] + nxt * w_ref[2:3, :] + b_ref[...]


def _filter_kernel(feat_ref, w1_ref, b1_ref, f1_ref, w2_ref, b2_ref, f2_ref, w3_ref, dec_ref, dir_ref,
                   o_ref, *, L):
    tl = o_ref.shape[0]
    hp = lax.Precision.HIGHEST
    pos = (lax.broadcasted_iota(jnp.int32, (tl, LANES), 0) + pl.program_id(0) * tl).astype(F32)
    lane = lax.broadcasted_iota(jnp.int32, (tl, LANES), 1)
    t = pos * (1.0 / (L - 1))
    arg = feat_ref[...] * (pos * (2.0 * math.pi / L))
    z = jnp.where(lane < FILTER_BANDS, jnp.cos(arg),
                  jnp.where(lane < 2 * FILTER_BANDS, -jnp.sin(arg),
                            jnp.where(lane == 2 * FILTER_BANDS, t, 0.0)))
    hid = jnp.sin(f1_ref[...] * (jnp.dot(z, w1_ref[...], precision=hp, preferred_element_type=F32) + b1_ref[...]))
    hid = jnp.sin(f2_ref[...] * (jnp.dot(hid, w2_ref[...], precision=hp, preferred_element_type=F32) + b2_ref[...]))
    filt = jnp.dot(hid, w3_ref[...], precision=hp, preferred_element_type=F32)
    filt = filt * jnp.exp(-t[:, 0:1] * dec_ref[...])
    filt = jnp.where(pos[:, 0:1] == 0.0, filt * dir_ref[...], filt)
    o_ref[...] = filt


def _hyena_filters(L, w1, b1, fr1, w2, b2, fr2, w3, d_hyena, tl=512):
    n_emb, n_hid = w1.shape
    n_out = w3.shape[1]
    bands = jnp.linspace(1e-4, FILTER_BANDS - 1, FILTER_BANDS, dtype=F32)
    feat = jnp.zeros((1, LANES), F32).at[0, :FILTER_BANDS].set(bands).at[0, FILTER_BANDS:2 * FILTER_BANDS].set(bands)
    w1p = jnp.zeros((LANES, n_hid), F32).at[:n_emb - 1].set(w1[1:]).at[n_emb - 1].set(w1[0])
    max_decay = math.log(DECAY_TARGET) / SHORT_DECAY_PCT
    min_decay = math.log(DECAY_TARGET) / LONG_DECAY_PCT
    deltas = jnp.abs(jnp.linspace(min_decay, max_decay, d_hyena, dtype=F32))
    reps = n_out // d_hyena
    dec = jnp.tile(deltas, reps).reshape(1, n_out)
    dirmask = jnp.tile(jnp.concatenate([jnp.ones((d_hyena,), F32), jnp.zeros((d_hyena,), F32)]), reps // 2)
    full = lambda shape: pl.BlockSpec(shape, lambda i: (0,) * len(shape))
    return pl.pallas_call(
        functools.partial(_filter_kernel, L=L),
        grid=(L // tl,),
        in_specs=[full((1, LANES)), full((LANES, n_hid)), full((1, n_hid)), full((1, n_hid)),
                  full((n_hid, n_hid)), full((1, n_hid)), full((1, n_hid)), full((n_hid, n_out)),
                  full((1, n_out)), full((1, n_out))],
        out_specs=pl.BlockSpec((tl, n_out), lambda i: (i, 0)),
        out_shape=jax.ShapeDtypeStruct((L, n_out), F32),
        compiler_params=_params("parallel"),
        name="hyena_filters",
    )(feat, w1p, b1.reshape(1, -1), fr1.reshape(1, -1), w2, b2.reshape(1, -1), fr2.reshape(1, -1), w3,
      dec, dirmask.reshape(1, n_out))


MID_ROWS = 4096
SIDE_PAD = 16


def _fft_tables(L):
    N = 2 * L
    N2 = LANES
    N1 = N // N2
    H1 = N1 // 2
    k1 = np.arange(H1)[:, None]
    n1 = np.arange(H1)[None, :]
    a1 = 2.0 * np.pi * ((k1 * n1) % N1) / N1
    c1, s1 = np.cos(a1), np.sin(a1)
    sign = np.where(np.arange(H1) % 2 == 0, 1.0, -1.0)[None, :]
    pad = np.zeros((SIDE_PAD - 1, H1))
    m1 = np.concatenate([c1, -s1, sign, pad], axis=0)
    wgt = np.where(np.arange(H1) == 0, 1.0, 2.0)[:, None]
    minv = np.concatenate([(wgt * c1).T, (-wgt * s1).T, sign.T, pad.T], axis=1)
    n2 = np.arange(N2)[:, None]
    k2 = np.arange(N2)[None, :]
    a2 = 2.0 * np.pi * ((n2 * k2) % N2) / N2
    cg, sg = np.cos(a2), np.sin(a2)
    g2 = np.block([[cg, -sg], [sg, cg]])
    g2i = np.block([[cg, sg], [-sg, cg]])
    at = 2.0 * np.pi * (np.arange(H1 + 8)[:, None] * np.arange(N2)[None, :]) / N
    tw = np.concatenate([np.cos(at), -np.sin(at)], axis=1)
    as_bf = lambda a: jnp.asarray(a.astype(np.float32)).astype(BF16)
    return dict(m1=as_bf(m1), minv=as_bf(minv), g2=as_bf(g2), g2i=as_bf(g2i),
                tw=jnp.asarray(tw.astype(np.float32)), N1=N1, H1=H1)


S_PITCH = LANES + 8
GROUP = 8
STAGE_UNROLL = 8


SLAB_UNROLL = 4


def _slab_transpose_in(get_slab, s_ref, h1):
    def body(j, carry):
        for u in range(SLAB_UNROLL):
            i = j * SLAB_UNROLL + u
            s_ref[pl.ds(pl.multiple_of(i * S_PITCH, 8), LANES), :] = get_slab(i).T
        return carry

    lax.fori_loop(0, h1 // SLAB_UNROLL, body, 0)


def _slab_transpose_out(s_ref, put_slab, h1):
    def body(j, carry):
        for u in range(SLAB_UNROLL):
            i = j * SLAB_UNROLL + u
            put_slab(i, s_ref[pl.ds(pl.multiple_of(i * S_PITCH, 8), LANES), :].T)
        return carry

    lax.fori_loop(0, h1 // SLAB_UNROLL, body, 0)


def _slab_rows(i):
    return pl.ds(pl.multiple_of(i * LANES, LANES), LANES)


def _fft_stage_a(s_ref, nch, m1, tw_ref, ab_ref, side_ref, n1):
    h1 = n1 // 2
    twr = tw_ref[0:h1, 0:LANES]
    twi = tw_ref[0:h1, LANES:2 * LANES]
    tsr = tw_ref[h1:h1 + 1, 0:LANES]
    tsi = tw_ref[h1:h1 + 1, LANES:2 * LANES]

    def body(g, carry):
        for u in range(STAGE_UNROLL):
            c0 = (g * STAGE_UNROLL + u) * GROUP
            rhs = jnp.concatenate([s_ref[pl.ds(c0 + i, h1, stride=S_PITCH), :] for i in range(GROUP)], axis=1)
            res = jnp.dot(m1, rhs.astype(BF16), preferred_element_type=F32)
            for i in range(GROUP):
                ar = res[0:h1, LANES * i:LANES * (i + 1)]
                ai = res[h1:2 * h1, LANES * i:LANES * (i + 1)]
                rows = pl.ds(pl.multiple_of((c0 + i) * h1, h1), h1)
                ab_ref[rows, 0:LANES] = ar * twr - ai * twi
                ab_ref[rows, LANES:2 * LANES] = ar * twi + ai * twr
            a_side = jnp.concatenate([res[2 * h1:2 * h1 + 1, LANES * i:LANES * (i + 1)] for i in range(GROUP)],
                                     axis=0)
            rows = pl.ds(pl.multiple_of(c0, GROUP), GROUP)
            side_ref[rows, 0:LANES] = a_side * tsr
            side_ref[rows, LANES:2 * LANES] = a_side * tsi
        return carry

    lax.fori_loop(0, nch // (GROUP * STAGE_UNROLL), body, 0)


def _cmul(a, b):
    ar, ai = a[:, 0:LANES], a[:, LANES:2 * LANES]
    br, bi = b[:, 0:LANES], b[:, LANES:2 * LANES]
    return jnp.concatenate([ar * br - ai * bi, ar * bi + ai * br], axis=1)


def _fft_mid(ab_ref, side_ref, get_h, get_h_side, g2, g2i, nrows):
    chunk = min(MID_ROWS, nrows)

    def body(r, carry):
        rows = pl.ds(pl.multiple_of(r * chunk, chunk), chunk)
        spec = _bdot(ab_ref[rows, :], g2)
        ab_ref[rows, :] = _bdot(_cmul(spec, get_h(rows)), g2i)
        return carry

    lax.fori_loop(0, nrows // chunk, body, 0)
    side_ref[...] = _bdot(_cmul(_bdot(side_ref[...], g2), get_h_side()), g2i)


def _fft_stage_a_inv(ab_ref, side_ref, nch, minv, tw_ref, s_ref, n1):
    h1 = n1 // 2
    twr = tw_ref[0:h1, 0:LANES]
    twi = tw_ref[0:h1, LANES:2 * LANES]
    tsr = tw_ref[h1:h1 + 1, 0:LANES]
    tsi = tw_ref[h1:h1 + 1, LANES:2 * LANES]
    first_row = lax.broadcasted_iota(jnp.int32, (SIDE_PAD, GROUP * LANES), 0) == 0

    def body(g, carry):
        for u in range(STAGE_UNROLL):
            c0 = (g * STAGE_UNROLL + u) * GROUP
            blk = ab_ref[pl.ds(pl.multiple_of(c0 * h1, GROUP * h1), GROUP * h1), :]
            re, im = [], []
            for i in range(GROUP):
                br = blk[i * h1:(i + 1) * h1, 0:LANES]
                bi = blk[i * h1:(i + 1) * h1, LANES:2 * LANES]
                re.append(br * twr + bi * twi)
                im.append(bi * twr - br * twi)
            side = side_ref[pl.ds(pl.multiple_of(c0, GROUP), GROUP), :]
            side_re = side[:, 0:LANES] * tsr + side[:, LANES:2 * LANES] * tsi
            side_row = jnp.concatenate([side_re[i:i + 1, :] for i in range(GROUP)], axis=1)
            side_blk = jnp.where(first_row, jnp.broadcast_to(side_row, first_row.shape), 0.0)
            rhs = jnp.concatenate([jnp.concatenate(re, axis=1), jnp.concatenate(im, axis=1), side_blk], axis=0)
            y = jnp.dot(minv, rhs.astype(BF16), preferred_element_type=F32)
            for i in range(GROUP):
                s_ref[pl.ds(c0 + i, h1, stride=S_PITCH), :] = y[:, LANES * i:LANES * (i + 1)]
        return carry

    lax.fori_loop(0, nch // (GROUP * STAGE_UNROLL), body, 0)


def _filter_spectrum_kernel(hf_ref, hb_ref, m1_ref, g2_ref, tw_ref, o_ref, oside_ref,
                            s_ref, af_ref, ab_ref, afs_ref, abs_ref, *, n1, inv_n):
    nch = hf_ref.shape[1]
    h1 = n1 // 2
    m1, g2 = m1_ref[...], g2_ref[...]
    _slab_transpose_in(lambda i: hf_ref[_slab_rows(i), :], s_ref, h1)
    _fft_stage_a(s_ref, nch, m1, tw_ref, af_ref, afs_ref, n1)
    _slab_transpose_in(lambda i: hb_ref[_slab_rows(i), :], s_ref, h1)
    _fft_stage_a(s_ref, nch, m1, tw_ref, ab_ref, abs_ref, n1)

    def two_sided(fwd, bwd):
        sf = _bdot(fwd, g2)
        sb = _bdot(bwd, g2)
        return jnp.concatenate([sf[:, 0:LANES] + sb[:, 0:LANES], sf[:, LANES:] - sb[:, LANES:]], axis=1) * inv_n

    chunk = min(MID_ROWS, nch * h1)

    def body(r, carry):
        rows = pl.ds(pl.multiple_of(r * chunk, chunk), chunk)
        o_ref[rows, :] = two_sided(af_ref[rows, :], ab_ref[rows, :])
        return carry

    lax.fori_loop(0, (nch * h1) // chunk, body, 0)
    oside_ref[...] = two_sided(afs_ref[...], abs_ref[...])


def _filter_spectrum(filt, L, d_hyena, tabs):
    h1 = tabs["H1"]
    order = filt.shape[1] // (2 * d_hyena)
    cb = LANES
    nblk = d_hyena // cb
    full = lambda a: pl.BlockSpec(a.shape, lambda o, c: (0,) * a.ndim)
    main = pltpu.VMEM((cb * h1, 2 * LANES), F32)
    side = pltpu.VMEM((cb, 2 * LANES), F32)
    return pl.pallas_call(
        functools.partial(_filter_spectrum_kernel, n1=tabs["N1"], inv_n=1.0 / (2 * L)),
        grid=(order, nblk),
        in_specs=[pl.BlockSpec((L, cb), lambda o, c: (0, o * 2 * nblk + c)),
                  pl.BlockSpec((L, cb), lambda o, c: (0, o * 2 * nblk + nblk + c)),
                  full(tabs["m1"]), full(tabs["g2"]), full(tabs["tw"])],
        out_specs=[pl.BlockSpec((cb * h1, 2 * LANES), lambda o, c: (o * nblk + c, 0)),
                   pl.BlockSpec((cb, 2 * LANES), lambda o, c: (o * nblk + c, 0))],
        out_shape=[jax.ShapeDtypeStruct((order * d_hyena * h1, 2 * LANES), F32),
                   jax.ShapeDtypeStruct((order * d_hyena, 2 * LANES), F32)],
        scratch_shapes=[pltpu.VMEM((h1 * S_PITCH, LANES), F32), main, main, side, side],
        compiler_params=_params("parallel", "parallel"),
        name="filter_spectrum",
    )(filt, filt, tabs["m1"], tabs["g2"], tabs["tw"])


def _hyena_kernel(pv_ref, px1_ref, px2_ref, wv_ref, wx1_ref, wx2_ref, bv_ref, bx1_ref, bx2_ref, h_ref, hside_ref,
                  d_ref, m1_ref, minv_ref, g2_ref, g2i_ref, tw_ref, o_ref, s_ref, ab_ref, side_ref, v_ref, z_ref,
                  *, n1):
    nch = pv_ref.shape[2]
    h1 = n1 // 2
    m1, minv, g2, g2i = m1_ref[...], minv_ref[...], g2_ref[...], g2i_ref[...]

    def long_conv(get_slab, order, put_slab):
        _slab_transpose_in(get_slab, s_ref, h1)
        _fft_stage_a(s_ref, nch, m1, tw_ref, ab_ref, side_ref, n1)
        _fft_mid(ab_ref, side_ref, lambda rows: h_ref[order, rows, :], lambda: hside_ref[order], g2, g2i, nch * h1)
        _fft_stage_a_inv(ab_ref, side_ref, nch, minv, tw_ref, s_ref, n1)
        _slab_transpose_out(s_ref, put_slab, h1)

    def get_v(i):
        v = _conv3_rows(pv_ref, i, LANES, wv_ref, bv_ref)
        v_ref[_slab_rows(i), :] = v
        return v

    def put1(i, conv):
        rows = _slab_rows(i)
        x1 = _conv3_rows(px1_ref, i, LANES, wx1_ref, bx1_ref)
        z_ref[rows, :] = x1 * (conv + d_ref[0:1, :] * v_ref[rows, :])

    long_conv(get_v, 0, put1)

    def put2(i, conv):
        rows = _slab_rows(i)
        x2 = _conv3_rows(px2_ref, i, LANES, wx2_ref, bx2_ref)
        o_ref[0, rows, :] = (x2 * (conv + d_ref[1:2, :] * z_ref[rows, :])).astype(o_ref.dtype)

    long_conv(lambda i: z_ref[_slab_rows(i), :], 1, put2)


def _hyena(proj, conv_w, conv_b, spec, spec_side, skip, tabs):
    B, L, _ = proj.shape
    D = skip.shape[1]
    n1, h1 = tabs["N1"], tabs["H1"]
    cb = LANES
    nblk = D // cb
    spec = spec.reshape(2, D * h1, 2 * LANES)
    spec_side = spec_side.reshape(2, D, 2 * LANES)
    col = lambda part: pl.BlockSpec((1, L, cb), lambda c, b: (b, 0, part * nblk + c))
    wcol = lambda part, rows: pl.BlockSpec((rows, cb), lambda c, b: (0, part * nblk + c))
    full = lambda a: pl.BlockSpec(a.shape, lambda c, b: (0,) * a.ndim)
    conv_b = conv_b.reshape(1, 3 * D)
    return pl.pallas_call(
        functools.partial(_hyena_kernel, n1=n1),
        grid=(nblk, B),
        in_specs=[col(0), col(1), col(2), wcol(0, 3), wcol(1, 3), wcol(2, 3), wcol(0, 1), wcol(1, 1), wcol(2, 1),
                  pl.BlockSpec((2, cb * h1, 2 * LANES), lambda c, b: (0, c, 0), pipeline_mode=pl.Buffered(1)),
                  pl.BlockSpec((2, cb, 2 * LANES), lambda c, b: (0, c, 0)),
                  pl.BlockSpec((2, cb), lambda c, b: (0, c)),
                  full(tabs["m1"]), full(tabs["minv"]), full(tabs["g2"]), full(tabs["g2i"]), full(tabs["tw"])],
        out_specs=pl.BlockSpec((1, L, cb), lambda c, b: (b, 0, c)),
        out_shape=jax.ShapeDtypeStruct((B, L, D), BF16),
        scratch_shapes=[pltpu.VMEM((h1 * S_PITCH, LANES), F32), pltpu.VMEM((cb * h1, 2 * LANES), F32),
                        pltpu.VMEM((cb, 2 * LANES), F32), pltpu.VMEM((L, cb), F32), pltpu.VMEM((L, cb), F32)],
        compiler_params=_params("parallel", "arbitrary"),
        name="hyena_fftconv",
    )(proj, proj, proj, conv_w, conv_w, conv_w, conv_b, conv_b, conv_b, spec, spec_side, skip.astype(F32),
      tabs["m1"], tabs["minv"], tabs["g2"], tabs["g2i"], tabs["tw"])


ML_CHUNK = 256


def _log_sigmoid(x):
    return jnp.minimum(x, 0.0) - jnp.log1p(jnp.exp(-jnp.abs(x)))


def _mlstm_direction(q, k, v, i_row, cum_row, logf_row, c_ref, m_ref, mask, scale):
    c, dh = q.shape
    g_row = i_row - cum_row
    f_col_rep = jnp.broadcast_to(cum_row, (LANES, c)).T
    m_prev = m_ref[0:1, 0:1]
    e = jnp.where(mask, g_row, -jnp.inf)
    a = jnp.maximum(jnp.max(e, axis=1, keepdims=True), m_prev)
    w = jnp.exp(e - a)
    sc = lax.dot_general(q.astype(BF16), k.astype(BF16), (((1,), (1,)), ((), ())), preferred_element_type=F32)
    p = sc * (w * scale)
    vaug = jnp.concatenate([v, jnp.ones_like(v)], axis=1)
    inter = jnp.exp(m_prev - a)
    tot = _bdot(p, vaug) + inter * _bdot(q, c_ref[...])
    num, den = tot[:, 0:dh], tot[:, dh:2 * dh]
    m_t = f_col_rep + a
    h = num / jnp.maximum(jnp.abs(den), jnp.exp(-m_t))
    a_end = jnp.maximum(jnp.max(g_row, axis=1, keepdims=True), m_prev)
    f_end = jnp.sum(logf_row, axis=1, keepdims=True)
    we = jnp.exp(g_row - a_end) * scale
    kw = k.astype(F32).T * we
    c_ref[...] = jnp.exp(m_prev - a_end) * c_ref[...] + _bdot(kw, vaug)
    m_ref[...] = jnp.broadcast_to(f_end + a_end, m_ref.shape)
    return h


def _mlstm_kernel(qf_ref, kf_ref, vf_ref, gf_ref, qb_ref, kb_ref, vb_ref, gb_ref, bias_ref,
                  hf_ref, hb_ref, c_ref, m_ref, *, heads, scale):
    @pl.when(pl.program_id(1) == 0)
    def _():
        c_ref[...] = jnp.zeros_like(c_ref)
        m_ref[...] = jnp.zeros_like(m_ref)

    c = qf_ref.shape[1]
    dh = qf_ref.shape[2] // heads
    r = lax.broadcasted_iota(jnp.int32, (c, c), 0)
    s = lax.broadcasted_iota(jnp.int32, (c, c), 1)
    mask_f = s <= r
    mask_b = s >= r

    def gate_rows(g_ref, mask_t):
        g = g_ref[0] + bias_ref[...]
        logf = _log_sigmoid(g)
        tri = mask_t.astype(BF16)
        cum = sum(jnp.dot(part, tri, preferred_element_type=F32) for part in _split3(logf))
        return g, logf, cum

    g_f, logf_f, cum_f = gate_rows(gf_ref, mask_b)
    g_b, logf_b, cum_b = gate_rows(gb_ref, mask_f)
    for h in range(heads):
        sl = slice(h * dh, (h + 1) * dh)
        fi, ff, bi, bf = h, heads + h, 2 * heads + h, 3 * heads + h
        hf_ref[0, :, sl] = _mlstm_direction(
            qf_ref[0, :, sl], kf_ref[0, :, sl], vf_ref[0, :, sl], g_f[fi:fi + 1], cum_f[ff:ff + 1],
            logf_f[ff:ff + 1], c_ref.at[h], m_ref.at[h], mask_f, scale).astype(hf_ref.dtype)
        hb_ref[0, :, sl] = _mlstm_direction(
            qb_ref[0, :, sl], kb_ref[0, :, sl], vb_ref[0, :, sl], g_b[bi:bi + 1], cum_b[bf:bf + 1],
            logf_b[bf:bf + 1], c_ref.at[heads + h], m_ref.at[heads + h], mask_b, scale).astype(hb_ref.dtype)


def _mlstm(qk, proj, v_col0, gates, gate_b, d_ml):
    B, L, _ = qk.shape
    H = MLSTM_HEADS
    dh = d_ml // H
    c = min(ML_CHUNK, L)
    nc = L // c
    g = gates.transpose(0, 2, 1)
    bias = jnp.broadcast_to(gate_b.astype(F32)[:, None], (4 * H, c))
    vb0 = v_col0 // d_ml
    specs = []
    for pos in (lambda ci: ci, lambda ci: nc - 1 - ci):
        specs += [pl.BlockSpec((1, c, d_ml), lambda bi, ci, pos=pos: (bi, pos(ci), 0)),
                  pl.BlockSpec((1, c, d_ml), lambda bi, ci, pos=pos: (bi, pos(ci), 1)),
                  pl.BlockSpec((1, c, d_ml), lambda bi, ci, pos=pos: (bi, pos(ci), vb0)),
                  pl.BlockSpec((1, 4 * H, c), lambda bi, ci, pos=pos: (bi, 0, pos(ci)))]
    specs.append(pl.BlockSpec((4 * H, c), lambda bi, ci: (0, 0)))
    out_shape = jax.ShapeDtypeStruct((B, L, d_ml), BF16)
    return pl.pallas_call(
        functools.partial(_mlstm_kernel, heads=H, scale=dh ** -0.5),
        grid=(B, nc),
        in_specs=specs,
        out_specs=[pl.BlockSpec((1, c, d_ml), lambda bi, ci: (bi, ci, 0)),
                   pl.BlockSpec((1, c, d_ml), lambda bi, ci: (bi, nc - 1 - ci, 0))],
        out_shape=[out_shape, out_shape],
        scratch_shapes=[pltpu.VMEM((2 * H, dh, 2 * dh), F32), pltpu.VMEM((2 * H, 1, LANES), F32)],
        compiler_params=_params("parallel", "arbitrary"),
        name="mlstm_scan",
    )(qk, qk, proj, g, qk, qk, proj, g, bias)


def _group_rms(y, gain, bd, group):
    ss = jnp.dot((y * y).astype(BF16), bd, preferred_element_type=F32)
    return y * lax.rsqrt(ss * (1.0 / group) + EPS) * gain


def _block_diag_ones(n, group):
    idx = np.arange(n) // group
    return jnp.asarray((idx[:, None] == idx[None, :]).astype(np.float32)).astype(BF16)


def _mix_xattn_kernel(z_ref, hf_ref, hb_ref, o_ref, x_ref, mem_ref, ghy_ref, gml_ref, why_ref, wml_ref, bdh_ref,
                      bdm_ref, gx_ref, gm_ref, wq_ref, wkv_ref, wo_ref, out_ref, kv_ref,
                      *, hy_group, ml_group, heads, scale):
    @pl.when(pl.program_id(1) == 0)
    def _():
        kv_ref[...] = _bdot(_rms(mem_ref[0], gm_ref[...]), wkv_ref[...]).astype(kv_ref.dtype)

    y_hy = _group_rms(z_ref[0].astype(F32), ghy_ref[...], bdh_ref[...], hy_group)
    h_sum = hf_ref[0].astype(F32) + hb_ref[0].astype(F32)
    y_ml = _group_rms(jax.nn.sigmoid(o_ref[0].astype(F32)) * h_sum, gml_ref[...], bdm_ref[...], ml_group)
    h1 = x_ref[0] + _bdot(y_hy, why_ref[...]) + _bdot(y_ml, wml_ref[...])

    D = h1.shape[1]
    dh = D // heads
    q = _bdot(_rms(h1, gx_ref[...]), wq_ref[...])
    outs = []
    for hd in range(heads):
        qh = q[:, hd * dh:(hd + 1) * dh]
        kh = kv_ref[:, hd * dh:(hd + 1) * dh]
        vh = kv_ref[:, D + hd * dh:D + (hd + 1) * dh]
        s = lax.dot_general(qh.astype(BF16), kh, (((1,), (1,)), ((), ())), preferred_element_type=F32) * scale
        e = jnp.exp(s - jnp.max(s, axis=-1, keepdims=True))
        p = e / jnp.sum(e, axis=-1, keepdims=True)
        outs.append(_bdot(p, vh))
    out_ref[0] = h1 + _bdot(jnp.concatenate(outs, axis=1), wo_ref[...])


def _mix_xattn(z, hf, hb, proj, o_col0, x, mem, g_hy, g_ml, w_out, g_x, g_mem, wq, wk, wv, wo, tm=512):
    B, L, d_hy = z.shape
    d_ml = hf.shape[2]
    D = x.shape[2]
    nm = mem.shape[1]
    ob = o_col0 // d_ml
    row = lambda n: pl.BlockSpec((1, tm, n), lambda bi, i: (bi, i, 0))
    const = lambda a: pl.BlockSpec(a.shape, lambda bi, i: (0,) * a.ndim, pipeline_mode=pl.Buffered(1))
    hy_group, ml_group = d_hy // HYENA_GROUPS, d_ml // MLSTM_HEADS
    consts = [g_hy.reshape(1, -1), g_ml.reshape(1, -1), w_out[:d_hy].astype(BF16), w_out[d_hy:].astype(BF16),
              _block_diag_ones(d_hy, hy_group), _block_diag_ones(d_ml, ml_group), g_x.reshape(1, D),
              g_mem.reshape(1, D), wq.astype(BF16), jnp.concatenate([wk, wv], axis=1).astype(BF16), wo.astype(BF16)]
    return pl.pallas_call(
        functools.partial(_mix_xattn_kernel, hy_group=hy_group, ml_group=ml_group, heads=XATTN_HEADS,
                          scale=(D // XATTN_HEADS) ** -0.5),
        grid=(B, L // tm),
        in_specs=[row(d_hy), row(d_ml), row(d_ml), pl.BlockSpec((1, tm, d_ml), lambda bi, i: (bi, i, ob)), row(D),
                  pl.BlockSpec((1, nm, D), lambda bi, i: (bi, 0, 0))] + [const(a) for a in consts],
        out_specs=row(D),
        out_shape=jax.ShapeDtypeStruct((B, L, D), F32),
        scratch_shapes=[pltpu.VMEM((nm, 2 * D), BF16)],
        compiler_params=_params("parallel", "arbitrary"),
        name="mix_xattn",
    )(z, hf, hb, proj, x, mem, *consts)


def _mlp_kernel(h_ref, g_ref, w1_ref, w2_ref, gf_ref, out_ref, xn_ref, acc_ref, *, final_norm):
    j = pl.program_id(1)

    @pl.when(j == 0)
    def _():
        x = h_ref[...]
        ms = jnp.mean(x * x, axis=-1, keepdims=True)
        xn_ref[...] = (x * lax.rsqrt(ms + EPS) * g_ref[...]).astype(BF16)
        acc_ref[...] = x

    a = jnp.maximum(jnp.dot(xn_ref[...], w1_ref[...], preferred_element_type=F32), 0.0)
    acc_ref[...] += _bdot(a * a, w2_ref[...])

    @pl.when(j == pl.num_programs(1) - 1)
    def _():
        y = acc_ref[...]
        if final_norm:
            ms = jnp.mean(y * y, axis=-1, keepdims=True)
            y = y * lax.rsqrt(ms + EPS) * gf_ref[...]
        out_ref[...] = y


def _mlp(h, g, w1, w2, gf, final_norm, tm=1024, tf=512):
    M, D = h.shape
    dff = w1.shape[1]
    return pl.pallas_call(
        functools.partial(_mlp_kernel, final_norm=final_norm),
        grid=(M // tm, dff // tf),
        in_specs=[pl.BlockSpec((tm, D), lambda i, j: (i, 0)), pl.BlockSpec((1, D), lambda i, j: (0, 0)),
                  pl.BlockSpec((D, tf), lambda i, j: (0, j)), pl.BlockSpec((tf, D), lambda i, j: (j, 0)),
                  pl.BlockSpec((1, D), lambda i, j: (0, 0))],
        out_specs=pl.BlockSpec((tm, D), lambda i, j: (i, 0)),
        out_shape=jax.ShapeDtypeStruct((M, D), F32),
        scratch_shapes=[pltpu.VMEM((tm, D), BF16), pltpu.VMEM((tm, D), F32)],
        compiler_params=_params("parallel", "arbitrary"),
        name="mlp",
    )(h, g.reshape(1, D), w1.astype(BF16), w2.astype(BF16), gf.reshape(1, D))


def _pad_cols(w, n):
    return jnp.pad(w, ((0, 0), (0, n - w.shape[1])))


def _layer(h, mem, p, l):
    B, L, D = h.shape
    M = B * L
    d_hy = p["hy_norm_g"].shape[1]
    d_ml = p["ml_norm_g"].shape[1]
    hy_cols = 3 * d_hy
    qk_cols = 2 * d_ml
    n_gate = 4 * MLSTM_HEADS
    w_in = p["w_in"][l]
    qk0 = hy_cols
    vo0 = qk0 + qk_cols
    g0 = vo0 + 2 * d_ml
    w_main = jnp.concatenate([w_in[:, :hy_cols], w_in[:, vo0:g0]], axis=1).astype(BF16)
    v_col0 = hy_cols
    o_col0 = hy_cols + d_ml
    n_in = hy_cols + 2 * d_ml
    proj, qk, gates = _in_proj(h.reshape(M, D), p["norm_mix_g"][l], w_main, w_in[:, qk0:vo0].astype(BF16),
                               _pad_cols(w_in[:, g0:g0 + n_gate], LANES).astype(BF16),
                               p["ml_conv_w"][l], p["ml_conv_b"][l], seq_len=L, tm=min(1024, L))
    proj = proj.reshape(B, L, n_in)
    qk = qk.reshape(B, L, qk_cols)
    gates = gates[:, :n_gate].reshape(B, L, n_gate)

    tabs = _fft_tables(L)
    filt = _hyena_filters(L, p["hy_filt_w1"][l], p["hy_filt_b1"][l], p["hy_filt_freq1"][l], p["hy_filt_w2"][l],
                          p["hy_filt_b2"][l], p["hy_filt_freq2"][l], p["hy_filt_w3"][l], d_hy,
                          tl=min(512, L))
    spec, spec_side = _filter_spectrum(filt, L, d_hy, tabs)
    z_hy = _hyena(proj, p["hy_conv_w"][l], p["hy_conv_b"][l], spec, spec_side, p["hy_skip"][l], tabs)

    h_f, h_b = _mlstm(qk, proj, v_col0, gates, p["ml_gate_b"][l], d_ml)

    return _mix_xattn(z_hy, h_f, h_b, proj, o_col0, h, mem, p["hy_norm_g"][l], p["ml_norm_g"][l], p["w_out"][l],
                      p["norm_x_g"][l], p["norm_mem_g"][l], p["xa_wq"][l], p["xa_wk"][l], p["xa_wv"][l],
                      p["xa_wo"][l], tm=min(512, L))


def kernel(x, mem, norm_mix_g, w_in, hy_conv_w, hy_conv_b, hy_filt_w1, hy_filt_b1, hy_filt_freq1, hy_filt_w2,
           hy_filt_b2, hy_filt_freq2, hy_filt_w3, hy_skip, hy_norm_g, ml_conv_w, ml_conv_b, ml_gate_b, ml_norm_g,
           w_out, norm_x_g, norm_mem_g, xa_wq, xa_wk, xa_wv, xa_wo, norm_ff_g, ff_w1, ff_w2, final_norm_g):
    p = dict(norm_mix_g=norm_mix_g, w_in=w_in, hy_conv_w=hy_conv_w, hy_conv_b=hy_conv_b, hy_filt_w1=hy_filt_w1,
             hy_filt_b1=hy_filt_b1, hy_filt_freq1=hy_filt_freq1, hy_filt_w2=hy_filt_w2, hy_filt_b2=hy_filt_b2,
             hy_filt_freq2=hy_filt_freq2, hy_filt_w3=hy_filt_w3, hy_skip=hy_skip, hy_norm_g=hy_norm_g,
             ml_conv_w=ml_conv_w, ml_conv_b=ml_conv_b, ml_gate_b=ml_gate_b, ml_norm_g=ml_norm_g, w_out=w_out,
             norm_x_g=norm_x_g, norm_mem_g=norm_mem_g, xa_wq=xa_wq, xa_wk=xa_wk, xa_wv=xa_wv, xa_wo=xa_wo)
    B, L, D = x.shape
    depth = w_in.shape[0]
    h = x
    for l in range(depth):
        h = _layer(h, mem, p, l)
        h = _mlp(h.reshape(B * L, D), norm_ff_g[l], ff_w1[l], ff_w2[l], final_norm_g, final_norm=l == depth - 1,
                 tm=min(1024, B * L)).reshape(B, L, D)
    return h
```

```python
import functools
import math

import numpy as np
import jax
import jax.numpy as jnp
from jax import lax
from jax.experimental import pallas as pl
from jax.experimental.pallas import tpu as pltpu

F32 = jnp.float32
BF16 = jnp.bfloat16

EPS = 1e-6
HYENA_GROUPS = 8
MLSTM_HEADS = 4
XATTN_HEADS = 4
FILTER_BANDS = 16
DECAY_TARGET = 1e-2
SHORT_DECAY_PCT = 0.3
LONG_DECAY_PCT = 1.5

LANES = 128
VMEM_LIMIT = 56 * 1024 * 1024


def _params(*sem):
    return pltpu.CompilerParams(dimension_semantics=sem, vmem_limit_bytes=VMEM_LIMIT)


def _bdot(a, b):
    return jnp.dot(a.astype(BF16), b.astype(BF16), preferred_element_type=F32)


def _split3(x):
    hi = x.astype(BF16)
    r = x - hi.astype(F32)
    mid = r.astype(BF16)
    lo = (r - mid.astype(F32)).astype(BF16)
    return hi, mid, lo


IN_CHUNK = 512
HALO = 8


def _rms(x, g):
    return x * lax.rsqrt(jnp.mean(x * x, axis=-1, keepdims=True) + EPS) * g


def _in_proj_kernel(x_ref, xb_ref, xa_ref, g_ref, w_ref, wqk_ref, wg_ref, cw_ref, cb_ref,
                    proj_ref, qk_ref, og_ref, xn_ref, *, tiles_per_seq):
    tm = x_ref.shape[0]
    g = g_ref[...]
    x = x_ref[...]
    xn_ref[...] = (x * g).astype(BF16)
    r = lax.rsqrt(jnp.mean(x * x, axis=-1, keepdims=True) + EPS)
    og_ref[...] = jnp.dot(xn_ref[...], wg_ref[...], preferred_element_type=F32) * r
    for c in range(w_ref.shape[1] // IN_CHUNK):
        cols = slice(c * IN_CHUNK, (c + 1) * IN_CHUNK)
        proj_ref[:, cols] = (jnp.dot(xn_ref[...], w_ref[:, cols], preferred_element_type=F32) * r
                             ).astype(proj_ref.dtype)

    halo = _rms(jnp.concatenate([xb_ref[...], xa_ref[...]], axis=0), g).astype(BF16)
    pos = pl.program_id(0) % tiles_per_seq
    row = lax.broadcasted_iota(jnp.int32, (tm, IN_CHUNK), 0)
    for c in range(wqk_ref.shape[1] // IN_CHUNK):
        cols = slice(c * IN_CHUNK, (c + 1) * IN_CHUNK)
        u = jnp.dot(xn_ref[...], wqk_ref[:, cols], preferred_element_type=F32) * r
        uh = jnp.dot(halo, wqk_ref[:, cols], preferred_element_type=F32)
        before = jnp.where(pos == 0, 0.0, uh[HALO - 1:HALO, :])
        after = jnp.where(pos == tiles_per_seq - 1, 0.0, uh[HALO:HALO + 1, :])
        prev = jnp.where(row == 0, before, pltpu.roll(u, 1, 0))
        nxt = jnp.where(row == tm - 1, after, pltpu.roll(u, tm - 1, 0))
        y = prev * cw_ref[0:1, cols] + u * cw_ref[1:2, cols] + nxt * cw_ref[2:3, cols] + cb_ref[:, cols]
        qk_ref[:, cols] = (y * jax.nn.sigmoid(y)).astype(qk_ref.dtype)


def _in_proj(x, g, w, w_qk, w_gate, conv_w, conv_b, seq_len, tm):
    M, K = x.shape
    N, nqk, ng = w.shape[1], w_qk.shape[1], w_gate.shape[1]
    hb = tm // HALO
    const = lambda a: pl.BlockSpec(a.shape, lambda i: (0,) * a.ndim, pipeline_mode=pl.Buffered(1))
    row = lambda n: pl.BlockSpec((tm, n), lambda i: (i, 0))
    g = g.reshape(1, K)
    conv_b = conv_b.reshape(1, nqk)
    return pl.pallas_call(
        functools.partial(_in_proj_kernel, tiles_per_seq=seq_len // tm),
        grid=(M // tm,),
        in_specs=[row(K),
                  pl.BlockSpec((HALO, K), lambda i: (jnp.maximum(i * hb - 1, 0), 0)),
                  pl.BlockSpec((HALO, K), lambda i: (jnp.minimum((i + 1) * hb, M // HALO - 1), 0)),
                  const(g), const(w), const(w_qk), const(w_gate), const(conv_w), const(conv_b)],
        out_specs=[row(N), row(nqk), row(ng)],
        out_shape=[jax.ShapeDtypeStruct((M, N), BF16), jax.ShapeDtypeStruct((M, nqk), BF16),
                   jax.ShapeDtypeStruct((M, ng), F32)],
        scratch_shapes=[pltpu.VMEM((tm, K), BF16)],
        compiler_params=_params("parallel"),
        name="in_proj",
    )(x, x, x, g, w, w_qk, w_gate, conv_w, conv_b)


def _conv3_rows(p_ref, r, rows, w_ref, b_ref):
    L = p_ref.shape[1]
    r0 = pl.multiple_of(r * rows, rows)
    u = p_ref[0, pl.ds(r0, rows), :].astype(F32)
    row = lax.broadcasted_iota(jnp.int32, u.shape, 0)
    before = p_ref[0, pl.ds(pl.multiple_of(jnp.maximum(r0 - 16, 0), 16), 16), :].astype(F32)[15:16, :]
    after = p_ref[0, pl.ds(pl.multiple_of(jnp.minimum(r0 + rows, L - 16), 16), 16), :].astype(F32)[0:1, :]
    before = jnp.where(r == 0, 0.0, before)
    after = jnp.where(r == L // rows - 1, 0.0, after)
    prev = jnp.where(row == 0, before, pltpu.roll(u, 1, 0))
    nxt = jnp.where(row == rows - 1, after, pltpu.roll(u, rows - 1, 0))
    return prev * w_ref[0:1, :] + u * w_ref[1:2, :] + nxt * w_ref[2:3, :] + b_ref[...]


def _filter_kernel(feat_ref, w1_ref, b1_ref, f1_ref, w2_ref, b2_ref, f2_ref, w3_ref, dec_ref, dir_ref,
                   o_ref, *, L):
    tl = o_ref.shape[0]
    hp = lax.Precision.HIGHEST
    pos = (lax.broadcasted_iota(jnp.int32, (tl, LANES), 0) + pl.program_id(0) * tl).astype(F32)
    lane = lax.broadcasted_iota(jnp.int32, (tl, LANES), 1)
    t = pos * (1.0 / (L - 1))
    arg = feat_ref[...] * (pos * (2.0 * math.pi / L))
    z = jnp.where(lane < FILTER_BANDS, jnp.cos(arg),
                  jnp.where(lane < 2 * FILTER_BANDS, -jnp.sin(arg),
                            jnp.where(lane == 2 * FILTER_BANDS, t, 0.0)))
    hid = jnp.sin(f1_ref[...] * (jnp.dot(z, w1_ref[...], precision=hp, preferred_element_type=F32) + b1_ref[...]))
    hid = jnp.sin(f2_ref[...] * (jnp.dot(hid, w2_ref[...], precision=hp, preferred_element_type=F32) + b2_ref[...]))
    filt = jnp.dot(hid, w3_ref[...], precision=hp, preferred_element_type=F32)
    filt = filt * jnp.exp(-t[:, 0:1] * dec_ref[...])
    filt = jnp.where(pos[:, 0:1] == 0.0, filt * dir_ref[...], filt)
    o_ref[...] = filt


def _hyena_filters(L, w1, b1, fr1, w2, b2, fr2, w3, d_hyena, tl=512):
    n_emb, n_hid = w1.shape
    n_out = w3.shape[1]
    bands = jnp.linspace(1e-4, FILTER_BANDS - 1, FILTER_BANDS, dtype=F32)
    feat = jnp.zeros((1, LANES), F32).at[0, :FILTER_BANDS].set(bands).at[0, FILTER_BANDS:2 * FILTER_BANDS].set(bands)
    w1p = jnp.zeros((LANES, n_hid), F32).at[:n_emb - 1].set(w1[1:]).at[n_emb - 1].set(w1[0])
    max_decay = math.log(DECAY_TARGET) / SHORT_DECAY_PCT
    min_decay = math.log(DECAY_TARGET) / LONG_DECAY_PCT
    deltas = jnp.abs(jnp.linspace(min_decay, max_decay, d_hyena, dtype=F32))
    reps = n_out // d_hyena
    dec = jnp.tile(deltas, reps).reshape(1, n_out)
    dirmask = jnp.tile(jnp.concatenate([jnp.ones((d_hyena,), F32), jnp.zeros((d_hyena,), F32)]), reps // 2)
    full = lambda shape: pl.BlockSpec(shape, lambda i: (0,) * len(shape))
    return pl.pallas_call(
        functools.partial(_filter_kernel, L=L),
        grid=(L // tl,),
        in_specs=[full((1, LANES)), full((LANES, n_hid)), full((1, n_hid)), full((1, n_hid)),
                  full((n_hid, n_hid)), full((1, n_hid)), full((1, n_hid)), full((n_hid, n_out)),
                  full((1, n_out)), full((1, n_out))],
        out_specs=pl.BlockSpec((tl, n_out), lambda i: (i, 0)),
        out_shape=jax.ShapeDtypeStruct((L, n_out), F32),
        compiler_params=_params("parallel"),
        name="hyena_filters",
    )(feat, w1p, b1.reshape(1, -1), fr1.reshape(1, -1), w2, b2.reshape(1, -1), fr2.reshape(1, -1), w3,
      dec, dirmask.reshape(1, n_out))


MID_ROWS = 4096
SIDE_PAD = 16


def _fft_tables(L):
    N = 2 * L
    N2 = LANES
    N1 = N // N2
    H1 = N1 // 2
    k1 = np.arange(H1)[:, None]
    n1 = np.arange(H1)[None, :]
    a1 = 2.0 * np.pi * ((k1 * n1) % N1) / N1
    c1, s1 = np.cos(a1), np.sin(a1)
    sign = np.where(np.arange(H1) % 2 == 0, 1.0, -1.0)[None, :]
    pad = np.zeros((SIDE_PAD - 1, H1))
    m1 = np.concatenate([c1, -s1, sign, pad], axis=0)
    wgt = np.where(np.arange(H1) == 0, 1.0, 2.0)[:, None]
    minv = np.concatenate([(wgt * c1).T, (-wgt * s1).T, sign.T, pad.T], axis=1)
    n2 = np.arange(N2)[:, None]
    k2 = np.arange(N2)[None, :]
    a2 = 2.0 * np.pi * ((n2 * k2) % N2) / N2
    cg, sg = np.cos(a2), np.sin(a2)
    g2 = np.block([[cg, -sg], [sg, cg]])
    g2i = np.block([[cg, sg], [-sg, cg]])
    at = 2.0 * np.pi * (np.arange(H1 + 8)[:, None] * np.arange(N2)[None, :]) / N
    tw = np.concatenate([np.cos(at), -np.sin(at)], axis=1)
    as_bf = lambda a: jnp.asarray(a.astype(np.float32)).astype(BF16)
    return dict(m1=as_bf(m1), minv=as_bf(minv), g2=as_bf(g2), g2i=as_bf(g2i),
                tw=jnp.asarray(tw.astype(np.float32)), N1=N1, H1=H1)


S_PITCH = LANES + 8
GROUP = 8
STAGE_UNROLL = 8


SLAB_UNROLL = 4


def _slab_transpose_in(get_slab, s_ref, h1):
    def body(j, carry):
        for u in range(SLAB_UNROLL):
            i = j * SLAB_UNROLL + u
            s_ref[pl.ds(pl.multiple_of(i * S_PITCH, 8), LANES), :] = get_slab(i).T
        return carry

    lax.fori_loop(0, h1 // SLAB_UNROLL, body, 0)


def _slab_transpose_out(s_ref, put_slab, h1):
    def body(j, carry):
        for u in range(SLAB_UNROLL):
            i = j * SLAB_UNROLL + u
            put_slab(i, s_ref[pl.ds(pl.multiple_of(i * S_PITCH, 8), LANES), :].T)
        return carry

    lax.fori_loop(0, h1 // SLAB_UNROLL, body, 0)


def _slab_rows(i):
    return pl.ds(pl.multiple_of(i * LANES, LANES), LANES)


def _fft_stage_a(s_ref, nch, m1, tw_ref, ab_ref, side_ref, n1):
    h1 = n1 // 2
    twr = tw_ref[0:h1, 0:LANES]
    twi = tw_ref[0:h1, LANES:2 * LANES]
    tsr = tw_ref[h1:h1 + 1, 0:LANES]
    tsi = tw_ref[h1:h1 + 1, LANES:2 * LANES]

    def body(g, carry):
        for u in range(STAGE_UNROLL):
            c0 = (g * STAGE_UNROLL + u) * GROUP
            rhs = jnp.concatenate([s_ref[pl.ds(c0 + i, h1, stride=S_PITCH), :] for i in range(GROUP)], axis=1)
            res = jnp.dot(m1, rhs.astype(BF16), preferred_element_type=F32)
            for i in range(GROUP):
                ar = res[0:h1, LANES * i:LANES * (i + 1)]
                ai = res[h1:2 * h1, LANES * i:LANES * (i + 1)]
                rows = pl.ds(pl.multiple_of((c0 + i) * h1, h1), h1)
                ab_ref[rows, 0:LANES] = ar * twr - ai * twi
                ab_ref[rows, LANES:2 * LANES] = ar * twi + ai * twr
            a_side = jnp.concatenate([res[2 * h1:2 * h1 + 1, LANES * i:LANES * (i + 1)] for i in range(GROUP)],
                                     axis=0)
            rows = pl.ds(pl.multiple_of(c0, GROUP), GROUP)
            side_ref[rows, 0:LANES] = a_side * tsr
            side_ref[rows, LANES:2 * LANES] = a_side * tsi
        return carry

    lax.fori_loop(0, nch // (GROUP * STAGE_UNROLL), body, 0)


def _cmul(a, b):
    ar, ai = a[:, 0:LANES], a[:, LANES:2 * LANES]
    br, bi = b[:, 0:LANES], b[:, LANES:2 * LANES]
    return jnp.concatenate([ar * br - ai * bi, ar * bi + ai * br], axis=1)


def _fft_mid(ab_ref, side_ref, get_h, get_h_side, g2, g2i, nrows):
    chunk = min(MID_ROWS, nrows)

    def body(r, carry):
        rows = pl.ds(pl.multiple_of(r * chunk, chunk), chunk)
        spec = _bdot(ab_ref[rows, :], g2)
        ab_ref[rows, :] = _bdot(_cmul(spec, get_h(rows)), g2i)
        return carry

    lax.fori_loop(0, nrows // chunk, body, 0)
    side_ref[...] = _bdot(_cmul(_bdot(side_ref[...], g2), get_h_side()), g2i)


def _fft_stage_a_inv(ab_ref, side_ref, nch, minv, tw_ref, s_ref, n1):
    h1 = n1 // 2
    twr = tw_ref[0:h1, 0:LANES]
    twi = tw_ref[0:h1, LANES:2 * LANES]
    tsr = tw_ref[h1:h1 + 1, 0:LANES]
    tsi = tw_ref[h1:h1 + 1, LANES:2 * LANES]
    first_row = lax.broadcasted_iota(jnp.int32, (SIDE_PAD, GROUP * LANES), 0) == 0

    def body(g, carry):
        for u in range(STAGE_UNROLL):
            c0 = (g * STAGE_UNROLL + u) * GROUP
            blk = ab_ref[pl.ds(pl.multiple_of(c0 * h1, GROUP * h1), GROUP * h1), :]
            re, im = [], []
            for i in range(GROUP):
                br = blk[i * h1:(i + 1) * h1, 0:LANES]
                bi = blk[i * h1:(i + 1) * h1, LANES:2 * LANES]
                re.append(br * twr + bi * twi)
                im.append(bi * twr - br * twi)
            side = side_ref[pl.ds(pl.multiple_of(c0, GROUP), GROUP), :]
            side_re = side[:, 0:LANES] * tsr + side[:, LANES:2 * LANES] * tsi
            side_row = jnp.concatenate([side_re[i:i + 1, :] for i in range(GROUP)], axis=1)
            side_blk = jnp.where(first_row, jnp.broadcast_to(side_row, first_row.shape), 0.0)
            rhs = jnp.concatenate([jnp.concatenate(re, axis=1), jnp.concatenate(im, axis=1), side_blk], axis=0)
            y = jnp.dot(minv, rhs.astype(BF16), preferred_element_type=F32)
            for i in range(GROUP):
                s_ref[pl.ds(c0 + i, h1, stride=S_PITCH), :] = y[:, LANES * i:LANES * (i + 1)]
        return carry

    lax.fori_loop(0, nch // (GROUP * STAGE_UNROLL), body, 0)


def _filter_spectrum_kernel(hf_ref, hb_ref, m1_ref, g2_ref, tw_ref, o_ref, oside_ref,
                            s_ref, af_ref, ab_ref, afs_ref, abs_ref, *, n1, inv_n):
    nch = hf_ref.shape[1]
    h1 = n1 // 2
    m1, g2 = m1_ref[...], g2_ref[...]
    _slab_transpose_in(lambda i: hf_ref[_slab_rows(i), :], s_ref, h1)
    _fft_stage_a(s_ref, nch, m1, tw_ref, af_ref, afs_ref, n1)
    _slab_transpose_in(lambda i: hb_ref[_slab_rows(i), :], s_ref, h1)
    _fft_stage_a(s_ref, nch, m1, tw_ref, ab_ref, abs_ref, n1)

    def two_sided(fwd, bwd):
        sf = _bdot(fwd, g2)
        sb = _bdot(bwd, g2)
        return jnp.concatenate([sf[:, 0:LANES] + sb[:, 0:LANES], sf[:, LANES:] - sb[:, LANES:]], axis=1) * inv_n

    chunk = min(MID_ROWS, nch * h1)

    def body(r, carry):
        rows = pl.ds(pl.multiple_of(r * chunk, chunk), chunk)
        o_ref[rows, :] = two_sided(af_ref[rows, :], ab_ref[rows, :])
        return carry

    lax.fori_loop(0, (nch * h1) // chunk, body, 0)
    oside_ref[...] = two_sided(afs_ref[...], abs_ref[...])


def _filter_spectrum(filt, L, d_hyena, tabs):
    h1 = tabs["H1"]
    order = filt.shape[1] // (2 * d_hyena)
    cb = LANES
    nblk = d_hyena // cb
    full = lambda a: pl.BlockSpec(a.shape, lambda o, c: (0,) * a.ndim)
    main = pltpu.VMEM((cb * h1, 2 * LANES), F32)
    side = pltpu.VMEM((cb, 2 * LANES), F32)
    return pl.pallas_call(
        functools.partial(_filter_spectrum_kernel, n1=tabs["N1"], inv_n=1.0 / (2 * L)),
        grid=(order, nblk),
        in_specs=[pl.BlockSpec((L, cb), lambda o, c: (0, o * 2 * nblk + c)),
                  pl.BlockSpec((L, cb), lambda o, c: (0, o * 2 * nblk + nblk + c)),
                  full(tabs["m1"]), full(tabs["g2"]), full(tabs["tw"])],
        out_specs=[pl.BlockSpec((cb * h1, 2 * LANES), lambda o, c: (o * nblk + c, 0)),
                   pl.BlockSpec((cb, 2 * LANES), lambda o, c: (o * nblk + c, 0))],
        out_shape=[jax.ShapeDtypeStruct((order * d_hyena * h1, 2 * LANES), F32),
                   jax.ShapeDtypeStruct((order * d_hyena, 2 * LANES), F32)],
        scratch_shapes=[pltpu.VMEM((h1 * S_PITCH, LANES), F32), main, main, side, side],
        compiler_params=_params("parallel", "parallel"),
        name="filter_spectrum",
    )(filt, filt, tabs["m1"], tabs["g2"], tabs["tw"])


def _hyena_kernel(pv_ref, px1_ref, px2_ref, wv_ref, wx1_ref, wx2_ref, bv_ref, bx1_ref, bx2_ref, h_ref, hside_ref,
                  d_ref, m1_ref, minv_ref, g2_ref, g2i_ref, tw_ref, o_ref, s_ref, ab_ref, side_ref, v_ref, z_ref,
                  *, n1):
    nch = pv_ref.shape[2]
    h1 = n1 // 2
    m1, minv, g2, g2i = m1_ref[...], minv_ref[...], g2_ref[...], g2i_ref[...]

    def long_conv(get_slab, order, put_slab):
        _slab_transpose_in(get_slab, s_ref, h1)
        _fft_stage_a(s_ref, nch, m1, tw_ref, ab_ref, side_ref, n1)
        _fft_mid(ab_ref, side_ref, lambda rows: h_ref[order, rows, :], lambda: hside_ref[order], g2, g2i, nch * h1)
        _fft_stage_a_inv(ab_ref, side_ref, nch, minv, tw_ref, s_ref, n1)
        _slab_transpose_out(s_ref, put_slab, h1)

    def get_v(i):
        v = _conv3_rows(pv_ref, i, LANES, wv_ref, bv_ref)
        v_ref[_slab_rows(i), :] = v
        return v

    def put1(i, conv):
        rows = _slab_rows(i)
        x1 = _conv3_rows(px1_ref, i, LANES, wx1_ref, bx1_ref)
        z_ref[rows, :] = x1 * (conv + d_ref[0:1, :] * v_ref[rows, :])

    long_conv(get_v, 0, put1)

    def put2(i, conv):
        rows = _slab_rows(i)
        x2 = _conv3_rows(px2_ref, i, LANES, wx2_ref, bx2_ref)
        o_ref[0, rows, :] = (x2 * (conv + d_ref[1:2, :] * z_ref[rows, :])).astype(o_ref.dtype)

    long_conv(lambda i: z_ref[_slab_rows(i), :], 1, put2)


def _hyena(proj, conv_w, conv_b, spec, spec_side, skip, tabs):
    B, L, _ = proj.shape
    D = skip.shape[1]
    n1, h1 = tabs["N1"], tabs["H1"]
    cb = LANES
    nblk = D // cb
    spec = spec.reshape(2, D * h1, 2 * LANES)
    spec_side = spec_side.reshape(2, D, 2 * LANES)
    col = lambda part: pl.BlockSpec((1, L, cb), lambda c, b: (b, 0, part * nblk + c))
    wcol = lambda part, rows: pl.BlockSpec((rows, cb), lambda c, b: (0, part * nblk + c))
    full = lambda a: pl.BlockSpec(a.shape, lambda c, b: (0,) * a.ndim)
    conv_b = conv_b.reshape(1, 3 * D)
    return pl.pallas_call(
        functools.partial(_hyena_kernel, n1=n1),
        grid=(nblk, B),
        in_specs=[col(0), col(1), col(2), wcol(0, 3), wcol(1, 3), wcol(2, 3), wcol(0, 1), wcol(1, 1), wcol(2, 1),
                  pl.BlockSpec((2, cb * h1, 2 * LANES), lambda c, b: (0, c, 0), pipeline_mode=pl.Buffered(1)),
                  pl.BlockSpec((2, cb, 2 * LANES), lambda c, b: (0, c, 0)),
                  pl.BlockSpec((2, cb), lambda c, b: (0, c)),
                  full(tabs["m1"]), full(tabs["minv"]), full(tabs["g2"]), full(tabs["g2i"]), full(tabs["tw"])],
        out_specs=pl.BlockSpec((1, L, cb), lambda c, b: (b, 0, c)),
        out_shape=jax.ShapeDtypeStruct((B, L, D), BF16),
        scratch_shapes=[pltpu.VMEM((h1 * S_PITCH, LANES), F32), pltpu.VMEM((cb * h1, 2 * LANES), F32),
                        pltpu.VMEM((cb, 2 * LANES), F32), pltpu.VMEM((L, cb), F32), pltpu.VMEM((L, cb), F32)],
        compiler_params=_params("parallel", "arbitrary"),
        name="hyena_fftconv",
    )(proj, proj, proj, conv_w, conv_w, conv_w, conv_b, conv_b, conv_b, spec, spec_side, skip.astype(F32),
      tabs["m1"], tabs["minv"], tabs["g2"], tabs["g2i"], tabs["tw"])


ML_CHUNK = 256


def _log_sigmoid(x):
    return jnp.minimum(x, 0.0) - jnp.log1p(jnp.exp(-jnp.abs(x)))


def _mlstm_direction(q, k, v, i_row, cum_row, logf_row, c_ref, m_ref, mask, scale):
    c, dh = q.shape
    g_row = i_row - cum_row
    f_col_rep = jnp.broadcast_to(cum_row, (LANES, c)).T
    m_prev = m_ref[0:1, 0:1]
    e = jnp.where(mask, g_row, -jnp.inf)
    a = jnp.maximum(jnp.max(e, axis=1, keepdims=True), m_prev)
    w = jnp.exp(e - a)
    sc = lax.dot_general(q.astype(BF16), k.astype(BF16), (((1,), (1,)), ((), ())), preferred_element_type=F32)
    p = sc * (w * scale)
    vaug = jnp.concatenate([v, jnp.ones_like(v)], axis=1)
    inter = jnp.exp(m_prev - a)
    tot = _bdot(p, vaug) + inter * _bdot(q, c_ref[...])
    num, den = tot[:, 0:dh], tot[:, dh:2 * dh]
    m_t = f_col_rep + a
    h = num / jnp.maximum(jnp.abs(den), jnp.exp(-m_t))
    a_end = jnp.maximum(jnp.max(g_row, axis=1, keepdims=True), m_prev)
    f_end = jnp.sum(logf_row, axis=1, keepdims=True)
    we = jnp.exp(g_row - a_end) * scale
    kw = k.astype(F32).T * we
    c_ref[...] = jnp.exp(m_prev - a_end) * c_ref[...] + _bdot(kw, vaug)
    m_ref[...] = jnp.broadcast_to(f_end + a_end, m_ref.shape)
    return h


def _mlstm_kernel(qf_ref, kf_ref, vf_ref, gf_ref, qb_ref, kb_ref, vb_ref, gb_ref, bias_ref,
                  hf_ref, hb_ref, c_ref, m_ref, *, heads, scale):
    @pl.when(pl.program_id(1) == 0)
    def _():
        c_ref[...] = jnp.zeros_like(c_ref)
        m_ref[...] = jnp.zeros_like(m_ref)

    c = qf_ref.shape[1]
    dh = qf_ref.shape[2] // heads
    r = lax.broadcasted_iota(jnp.int32, (c, c), 0)
    s = lax.broadcasted_iota(jnp.int32, (c, c), 1)
    mask_f = s <= r
    mask_b = s >= r

    def gate_rows(g_ref, mask_t):
        g = g_ref[0] + bias_ref[...]
        logf = _log_sigmoid(g)
        tri = mask_t.astype(BF16)
        cum = sum(jnp.dot(part, tri, preferred_element_type=F32) for part in _split3(logf))
        return g, logf, cum

    g_f, logf_f, cum_f = gate_rows(gf_ref, mask_b)
    g_b, logf_b, cum_b = gate_rows(gb_ref, mask_f)
    for h in range(heads):
        sl = slice(h * dh, (h + 1) * dh)
        fi, ff, bi, bf = h, heads + h, 2 * heads + h, 3 * heads + h
        hf_ref[0, :, sl] = _mlstm_direction(
            qf_ref[0, :, sl], kf_ref[0, :, sl], vf_ref[0, :, sl], g_f[fi:fi + 1], cum_f[ff:ff + 1],
            logf_f[ff:ff + 1], c_ref.at[h], m_ref.at[h], mask_f, scale).astype(hf_ref.dtype)
        hb_ref[0, :, sl] = _mlstm_direction(
            qb_ref[0, :, sl], kb_ref[0, :, sl], vb_ref[0, :, sl], g_b[bi:bi + 1], cum_b[bf:bf + 1],
            logf_b[bf:bf + 1], c_ref.at[heads + h], m_ref.at[heads + h], mask_b, scale).astype(hb_ref.dtype)


def _mlstm(qk, proj, v_col0, gates, gate_b, d_ml):
    B, L, _ = qk.shape
    H = MLSTM_HEADS
    dh = d_ml // H
    c = min(ML_CHUNK, L)
    nc = L // c
    g = gates.transpose(0, 2, 1)
    bias = jnp.broadcast_to(gate_b.astype(F32)[:, None], (4 * H, c))
    vb0 = v_col0 // d_ml
    specs = []
    for pos in (lambda ci: ci, lambda ci: nc - 1 - ci):
        specs += [pl.BlockSpec((1, c, d_ml), lambda bi, ci, pos=pos: (bi, pos(ci), 0)),
                  pl.BlockSpec((1, c, d_ml), lambda bi, ci, pos=pos: (bi, pos(ci), 1)),
                  pl.BlockSpec((1, c, d_ml), lambda bi, ci, pos=pos: (bi, pos(ci), vb0)),
                  pl.BlockSpec((1, 4 * H, c), lambda bi, ci, pos=pos: (bi, 0, pos(ci)))]
    specs.append(pl.BlockSpec((4 * H, c), lambda bi, ci: (0, 0)))
    out_shape = jax.ShapeDtypeStruct((B, L, d_ml), BF16)
    return pl.pallas_call(
        functools.partial(_mlstm_kernel, heads=H, scale=dh ** -0.5),
        grid=(B, nc),
        in_specs=specs,
        out_specs=[pl.BlockSpec((1, c, d_ml), lambda bi, ci: (bi, ci, 0)),
                   pl.BlockSpec((1, c, d_ml), lambda bi, ci: (bi, nc - 1 - ci, 0))],
        out_shape=[out_shape, out_shape],
        scratch_shapes=[pltpu.VMEM((2 * H, dh, 2 * dh), F32), pltpu.VMEM((2 * H, 1, LANES), F32)],
        compiler_params=_params("parallel", "arbitrary"),
        name="mlstm_scan",
    )(qk, qk, proj, g, qk, qk, proj, g, bias)


def _group_rms(y, gain, bd, group):
    ss = jnp.dot((y * y).astype(BF16), bd, preferred_element_type=F32)
    return y * lax.rsqrt(ss * (1.0 / group) + EPS) * gain


def _block_diag_ones(n, group):
    idx = np.arange(n) // group
    return jnp.asarray((idx[:, None] == idx[None, :]).astype(np.float32)).astype(BF16)


def _mix_xattn_kernel(z_ref, hf_ref, hb_ref, o_ref, x_ref, mem_ref, ghy_ref, gml_ref, why_ref, wml_ref, bdh_ref,
                      bdm_ref, gx_ref, gm_ref, wq_ref, wkv_ref, wo_ref, out_ref, kv_ref,
                      *, hy_group, ml_group, heads, scale):
    @pl.when(pl.program_id(1) == 0)
    def _():
        kv_ref[...] = _bdot(_rms(mem_ref[0], gm_ref[...]), wkv_ref[...]).astype(kv_ref.dtype)

    y_hy = _group_rms(z_ref[0].astype(F32), ghy_ref[...], bdh_ref[...], hy_group)
    h_sum = hf_ref[0].astype(F32) + hb_ref[0].astype(F32)
    y_ml = _group_rms(jax.nn.sigmoid(o_ref[0].astype(F32)) * h_sum, gml_ref[...], bdm_ref[...], ml_group)
    h1 = x_ref[0] + _bdot(y_hy, why_ref[...]) + _bdot(y_ml, wml_ref[...])

    D = h1.shape[1]
    dh = D // heads
    q = _bdot(_rms(h1, gx_ref[...]), wq_ref[...])
    outs = []
    for hd in range(heads):
        qh = q[:, hd * dh:(hd + 1) * dh]
        kh = kv_ref[:, hd * dh:(hd + 1) * dh]
        vh = kv_ref[:, D + hd * dh:D + (hd + 1) * dh]
        s = lax.dot_general(qh.astype(BF16), kh, (((1,), (1,)), ((), ())), preferred_element_type=F32) * scale
        e = jnp.exp(s - jnp.max(s, axis=-1, keepdims=True))
        p = e / jnp.sum(e, axis=-1, keepdims=True)
        outs.append(_bdot(p, vh))
    out_ref[0] = h1 + _bdot(jnp.concatenate(outs, axis=1), wo_ref[...])


def _mix_xattn(z, hf, hb, proj, o_col0, x, mem, g_hy, g_ml, w_out, g_x, g_mem, wq, wk, wv, wo, tm=512):
    B, L, d_hy = z.shape
    d_ml = hf.shape[2]
    D = x.shape[2]
    nm = mem.shape[1]
    ob = o_col0 // d_ml
    row = lambda n: pl.BlockSpec((1, tm, n), lambda bi, i: (bi, i, 0))
    const = lambda a: pl.BlockSpec(a.shape, lambda bi, i: (0,) * a.ndim, pipeline_mode=pl.Buffered(1))
    hy_group, ml_group = d_hy // HYENA_GROUPS, d_ml // MLSTM_HEADS
    consts = [g_hy.reshape(1, -1), g_ml.reshape(1, -1), w_out[:d_hy].astype(BF16), w_out[d_hy:].astype(BF16),
              _block_diag_ones(d_hy, hy_group), _block_diag_ones(d_ml, ml_group), g_x.reshape(1, D),
              g_mem.reshape(1, D), wq.astype(BF16), jnp.concatenate([wk, wv], axis=1).astype(BF16), wo.astype(BF16)]
    return pl.pallas_call(
        functools.partial(_mix_xattn_kernel, hy_group=hy_group, ml_group=ml_group, heads=XATTN_HEADS,
                          scale=(D // XATTN_HEADS) ** -0.5),
        grid=(B, L // tm),
        in_specs=[row(d_hy), row(d_ml), row(d_ml), pl.BlockSpec((1, tm, d_ml), lambda bi, i: (bi, i, ob)), row(D),
                  pl.BlockSpec((1, nm, D), lambda bi, i: (bi, 0, 0))] + [const(a) for a in consts],
        out_specs=row(D),
        out_shape=jax.ShapeDtypeStruct((B, L, D), F32),
        scratch_shapes=[pltpu.VMEM((nm, 2 * D), BF16)],
        compiler_params=_params("parallel", "arbitrary"),
        name="mix_xattn",
    )(z, hf, hb, proj, x, mem, *consts)


def _mlp_kernel(h_ref, g_ref, w1_ref, w2_ref, gf_ref, out_ref, xn_ref, acc_ref, *, final_norm):
    j = pl.program_id(1)

    @pl.when(j == 0)
    def _():
        x = h_ref[...]
        ms = jnp.mean(x * x, axis=-1, keepdims=True)
        xn_ref[...] = (x * lax.rsqrt(ms + EPS) * g_ref[...]).astype(BF16)
        acc_ref[...] = x

    a = jnp.maximum(jnp.dot(xn_ref[...], w1_ref[...], preferred_element_type=F32), 0.0)
    acc_ref[...] += _bdot(a * a, w2_ref[...])

    @pl.when(j == pl.num_programs(1) - 1)
    def _():
        y = acc_ref[...]
        if final_norm:
            ms = jnp.mean(y * y, axis=-1, keepdims=True)
            y = y * lax.rsqrt(ms + EPS) * gf_ref[...]
        out_ref[...] = y


def _mlp(h, g, w1, w2, gf, final_norm, tm=1024, tf=1024):
    M, D = h.shape
    dff = w1.shape[1]
    return pl.pallas_call(
        functools.partial(_mlp_kernel, final_norm=final_norm),
        grid=(M // tm, dff // tf),
        in_specs=[pl.BlockSpec((tm, D), lambda i, j: (i, 0)), pl.BlockSpec((1, D), lambda i, j: (0, 0)),
                  pl.BlockSpec((D, tf), lambda i, j: (0, j)), pl.BlockSpec((tf, D), lambda i, j: (j, 0)),
                  pl.BlockSpec((1, D), lambda i, j: (0, 0))],
        out_specs=pl.BlockSpec((tm, D), lambda i, j: (i, 0)),
        out_shape=jax.ShapeDtypeStruct((M, D), F32),
        scratch_shapes=[pltpu.VMEM((tm, D), BF16), pltpu.VMEM((tm, D), F32)],
        compiler_params=_params("parallel", "arbitrary"),
        name="mlp",
    )(h, g.reshape(1, D), w1.astype(BF16), w2.astype(BF16), gf.reshape(1, D))


def _pad_cols(w, n):
    return jnp.pad(w, ((0, 0), (0, n - w.shape[1])))


def _layer(h, mem, p, l):
    B, L, D = h.shape
    M = B * L
    d_hy = p["hy_norm_g"].shape[1]
    d_ml = p["ml_norm_g"].shape[1]
    hy_cols = 3 * d_hy
    qk_cols = 2 * d_ml
    n_gate = 4 * MLSTM_HEADS
    w_in = p["w_in"][l]
    qk0 = hy_cols
    vo0 = qk0 + qk_cols
    g0 = vo0 + 2 * d_ml
    w_main = jnp.concatenate([w_in[:, :hy_cols], w_in[:, vo0:g0]], axis=1).astype(BF16)
    v_col0 = hy_cols
    o_col0 = hy_cols + d_ml
    n_in = hy_cols + 2 * d_ml
    proj, qk, gates = _in_proj(h.reshape(M, D), p["norm_mix_g"][l], w_main, w_in[:, qk0:vo0].astype(BF16),
                               _pad_cols(w_in[:, g0:g0 + n_gate], LANES).astype(BF16),
                               p["ml_conv_w"][l], p["ml_conv_b"][l], seq_len=L, tm=min(1024, L))
    proj = proj.reshape(B, L, n_in)
    qk = qk.reshape(B, L, qk_cols)
    gates = gates[:, :n_gate].reshape(B, L, n_gate)

    tabs = _fft_tables(L)
    filt = _hyena_filters(L, p["hy_filt_w1"][l], p["hy_filt_b1"][l], p["hy_filt_freq1"][l], p["hy_filt_w2"][l],
                          p["hy_filt_b2"][l], p["hy_filt_freq2"][l], p["hy_filt_w3"][l], d_hy,
                          tl=min(512, L))
    spec, spec_side = _filter_spectrum(filt, L, d_hy, tabs)
    z_hy = _hyena(proj, p["hy_conv_w"][l], p["hy_conv_b"][l], spec, spec_side, p["hy_skip"][l], tabs)

    h_f, h_b = _mlstm(qk, proj, v_col0, gates, p["ml_gate_b"][l], d_ml)

    return _mix_xattn(z_hy, h_f, h_b, proj, o_col0, h, mem, p["hy_norm_g"][l], p["ml_norm_g"][l], p["w_out"][l],
                      p["norm_x_g"][l], p["norm_mem_g"][l], p["xa_wq"][l], p["xa_wk"][l], p["xa_wv"][l],
                      p["xa_wo"][l], tm=min(512, L))


def kernel(x, mem, norm_mix_g, w_in, hy_conv_w, hy_conv_b, hy_filt_w1, hy_filt_b1, hy_filt_freq1, hy_filt_w2,
           hy_filt_b2, hy_filt_freq2, hy_filt_w3, hy_skip, hy_norm_g, ml_conv_w, ml_conv_b, ml_gate_b, ml_norm_g,
           w_out, norm_x_g, norm_mem_g, xa_wq, xa_wk, xa_wv, xa_wo, norm_ff_g, ff_w1, ff_w2, final_norm_g):
    p = dict(norm_mix_g=norm_mix_g, w_in=w_in, hy_conv_w=hy_conv_w, hy_conv_b=hy_conv_b, hy_filt_w1=hy_filt_w1,
             hy_filt_b1=hy_filt_b1, hy_filt_freq1=hy_filt_freq1, hy_filt_w2=hy_filt_w2, hy_filt_b2=hy_filt_b2,
             hy_filt_freq2=hy_filt_freq2, hy_filt_w3=hy_filt_w3, hy_skip=hy_skip, hy_norm_g=hy_norm_g,
             ml_conv_w=ml_conv_w, ml_conv_b=ml_conv_b, ml_gate_b=ml_gate_b, ml_norm_g=ml_norm_g, w_out=w_out,
             norm_x_g=norm_x_g, norm_mem_g=norm_mem_g, xa_wq=xa_wq, xa_wk=xa_wk, xa_wv=xa_wv, xa_wo=xa_wo)
    B, L, D = x.shape
    depth = w_in.shape[0]
    h = x
    for l in range(depth):
        h = _layer(h, mem, p, l)
        h = _mlp(h.reshape(B * L, D), norm_ff_g[l], ff_w1[l], ff_w2[l], final_norm_g, final_norm=l == depth - 1,
                 tm=min(1024, B * L)).reshape(B, L, D)
    return h
```

```python
import functools
import math

import numpy as np
import jax
import jax.numpy as jnp
from jax import lax
from jax.experimental import pallas as pl
from jax.experimental.pallas import tpu as pltpu

F32 = jnp.float32
BF16 = jnp.bfloat16

EPS = 1e-6
HYENA_GROUPS = 8
MLSTM_HEADS = 4
XATTN_HEADS = 4
FILTER_BANDS = 16
DECAY_TARGET = 1e-2
SHORT_DECAY_PCT = 0.3
LONG_DECAY_PCT = 1.5

LANES = 128
VMEM_LIMIT = 56 * 1024 * 1024


def _params(*sem):
    return pltpu.CompilerParams(dimension_semantics=sem, vmem_limit_bytes=VMEM_LIMIT)


def _bdot(a, b):
    return jnp.dot(a.astype(BF16), b.astype(BF16), preferred_element_type=F32)


def _split3(x):
    hi = x.astype(BF16)
    r = x - hi.astype(F32)
    mid = r.astype(BF16)
    lo = (r - mid.astype(F32)).astype(BF16)
    return hi, mid, lo


IN_CHUNK = 512
HALO = 8


def _rms(x, g):
    return x * lax.rsqrt(jnp.mean(x * x, axis=-1, keepdims=True) + EPS) * g


def _in_proj_kernel(x_ref, xb_ref, xa_ref, g_ref, w_ref, wqk_ref, wg_ref, cw_ref, cb_ref,
                    proj_ref, qk_ref, og_ref, xn_ref, *, tiles_per_seq):
    tm = x_ref.shape[0]
    g = g_ref[...]
    x = x_ref[...]
    xn_ref[...] = (x * g).astype(BF16)
    r = lax.rsqrt(jnp.mean(x * x, axis=-1, keepdims=True) + EPS)
    og_ref[...] = jnp.dot(xn_ref[...], wg_ref[...], preferred_element_type=F32) * r
    for c in range(w_ref.shape[1] // IN_CHUNK):
        cols = slice(c * IN_CHUNK, (c + 1) * IN_CHUNK)
        proj_ref[:, cols] = (jnp.dot(xn_ref[...], w_ref[:, cols], preferred_element_type=F32) * r
                             ).astype(proj_ref.dtype)

    halo = _rms(jnp.concatenate([xb_ref[...], xa_ref[...]], axis=0), g).astype(BF16)
    pos = pl.program_id(0) % tiles_per_seq
    row = lax.broadcasted_iota(jnp.int32, (tm, IN_CHUNK), 0)
    for c in range(wqk_ref.shape[1] // IN_CHUNK):
        cols = slice(c * IN_CHUNK, (c + 1) * IN_CHUNK)
        u = jnp.dot(xn_ref[...], wqk_ref[:, cols], preferred_element_type=F32) * r
        uh = jnp.dot(halo, wqk_ref[:, cols], preferred_element_type=F32)
        before = jnp.where(pos == 0, 0.0, uh[HALO - 1:HALO, :])
        after = jnp.where(pos == tiles_per_seq - 1, 0.0, uh[HALO:HALO + 1, :])
        prev = jnp.where(row == 0, before, pltpu.roll(u, 1, 0))
        nxt = jnp.where(row == tm - 1, after, pltpu.roll(u, tm - 1, 0))
        y = prev * cw_ref[0:1, cols] + u * cw_ref[1:2, cols] + nxt * cw_ref[2:3, cols] + cb_ref[:, cols]
        qk_ref[:, cols] = (y * jax.nn.sigmoid(y)).astype(qk_ref.dtype)


def _in_proj(x, g, w, w_qk, w_gate, conv_w, conv_b, seq_len, tm):
    M, K = x.shape
    N, nqk, ng = w.shape[1], w_qk.shape[1], w_gate.shape[1]
    hb = tm // HALO
    const = lambda a: pl.BlockSpec(a.shape, lambda i: (0,) * a.ndim, pipeline_mode=pl.Buffered(1))
    row = lambda n: pl.BlockSpec((tm, n), lambda i: (i, 0))
    g = g.reshape(1, K)
    conv_b = conv_b.reshape(1, nqk)
    return pl.pallas_call(
        functools.partial(_in_proj_kernel, tiles_per_seq=seq_len // tm),
        grid=(M // tm,),
        in_specs=[row(K),
                  pl.BlockSpec((HALO, K), lambda i: (jnp.maximum(i * hb - 1, 0), 0)),
                  pl.BlockSpec((HALO, K), lambda i: (jnp.minimum((i + 1) * hb, M // HALO - 1), 0)),
                  const(g), const(w), const(w_qk), const(w_gate), const(conv_w), const(conv_b)],
        out_specs=[row(N), row(nqk), row(ng)],
        out_shape=[jax.ShapeDtypeStruct((M, N), BF16), jax.ShapeDtypeStruct((M, nqk), BF16),
                   jax.ShapeDtypeStruct((M, ng), F32)],
        scratch_shapes=[pltpu.VMEM((tm, K), BF16)],
        compiler_params=_params("parallel"),
        name="in_proj",
    )(x, x, x, g, w, w_qk, w_gate, conv_w, conv_b)


def _conv3_rows(p_ref, r, rows, w_ref, b_ref):
    L = p_ref.shape[1]
    r0 = pl.multiple_of(r * rows, rows)
    u = p_ref[0, pl.ds(r0, rows), :].astype(F32)
    row = lax.broadcasted_iota(jnp.int32, u.shape, 0)
    before = p_ref[0, pl.ds(pl.multiple_of(jnp.maximum(r0 - 16, 0), 16), 16), :].astype(F32)[15:16, :]
    after = p_ref[0, pl.ds(pl.multiple_of(jnp.minimum(r0 + rows, L - 16), 16), 16), :].astype(F32)[0:1, :]
    before = jnp.where(r == 0, 0.0, before)
    after = jnp.where(r == L // rows - 1, 0.0, after)
    prev = jnp.where(row == 0, before, pltpu.roll(u, 1, 0))
    nxt = jnp.where(row == rows - 1, after, pltpu.roll(u, rows - 1, 0))
    return prev * w_ref[0:1, :] + u * w_ref[1:2, :] + nxt * w_ref[2:3, :] + b_ref[...]


def _filter_kernel(feat_ref, w1_ref, b1_ref, f1_ref, w2_ref, b2_ref, f2_ref, w3_ref, dec_ref, dir_ref,
                   o_ref, *, L):
    tl = o_ref.shape[0]
    hp = lax.Precision.HIGHEST
    pos = (lax.broadcasted_iota(jnp.int32, (tl, LANES), 0) + pl.program_id(0) * tl).astype(F32)
    lane = lax.broadcasted_iota(jnp.int32, (tl, LANES), 1)
    t = pos * (1.0 / (L - 1))
    arg = feat_ref[...] * (pos * (2.0 * math.pi / L))
    z = jnp.where(lane < FILTER_BANDS, jnp.cos(arg),
                  jnp.where(lane < 2 * FILTER_BANDS, -jnp.sin(arg),
                            jnp.where(lane == 2 * FILTER_BANDS, t, 0.0)))
    hid = jnp.sin(f1_ref[...] * (jnp.dot(z, w1_ref[...], precision=hp, preferred_element_type=F32) + b1_ref[...]))
    hid = jnp.sin(f2_ref[...] * (jnp.dot(hid, w2_ref[...], precision=hp, preferred_element_type=F32) + b2_ref[...]))
    filt = jnp.dot(hid, w3_ref[...], precision=hp, preferred_element_type=F32)
    filt = filt * jnp.exp(-t[:, 0:1] * dec_ref[...])
    filt = jnp.where(pos[:, 0:1] == 0.0, filt * dir_ref[...], filt)
    o_ref[...] = filt


def _hyena_filters(L, w1, b1, fr1, w2, b2, fr2, w3, d_hyena, tl=512):
    n_emb, n_hid = w1.shape
    n_out = w3.shape[1]
    bands = jnp.linspace(1e-4, FILTER_BANDS - 1, FILTER_BANDS, dtype=F32)
    feat = jnp.zeros((1, LANES), F32).at[0, :FILTER_BANDS].set(bands).at[0, FILTER_BANDS:2 * FILTER_BANDS].set(bands)
    w1p = jnp.zeros((LANES, n_hid), F32).at[:n_emb - 1].set(w1[1:]).at[n_emb - 1].set(w1[0])
    max_decay = math.log(DECAY_TARGET) / SHORT_DECAY_PCT
    min_decay = math.log(DECAY_TARGET) / LONG_DECAY_PCT
    deltas = jnp.abs(jnp.linspace(min_decay, max_decay, d_hyena, dtype=F32))
    reps = n_out // d_hyena
    dec = jnp.tile(deltas, reps).reshape(1, n_out)
    dirmask = jnp.tile(jnp.concatenate([jnp.ones((d_hyena,), F32), jnp.zeros((d_hyena,), F32)]), reps // 2)
    full = lambda shape: pl.BlockSpec(shape, lambda i: (0,) * len(shape))
    return pl.pallas_call(
        functools.partial(_filter_kernel, L=L),
        grid=(L // tl,),
        in_specs=[full((1, LANES)), full((LANES, n_hid)), full((1, n_hid)), full((1, n_hid)),
                  full((n_hid, n_hid)), full((1, n_hid)), full((1, n_hid)), full((n_hid, n_out)),
                  full((1, n_out)), full((1, n_out))],
        out_specs=pl.BlockSpec((tl, n_out), lambda i: (i, 0)),
        out_shape=jax.ShapeDtypeStruct((L, n_out), F32),
        compiler_params=_params("parallel"),
        name="hyena_filters",
    )(feat, w1p, b1.reshape(1, -1), fr1.reshape(1, -1), w2, b2.reshape(1, -1), fr2.reshape(1, -1), w3,
      dec, dirmask.reshape(1, n_out))


MID_ROWS = 4096
SIDE_PAD = 16


def _fft_tables(L):
    N = 2 * L
    N2 = LANES
    N1 = N // N2
    H1 = N1 // 2
    k1 = np.arange(H1)[:, None]
    n1 = np.arange(H1)[None, :]
    a1 = 2.0 * np.pi * ((k1 * n1) % N1) / N1
    c1, s1 = np.cos(a1), np.sin(a1)
    sign = np.where(np.arange(H1) % 2 == 0, 1.0, -1.0)[None, :]
    pad = np.zeros((SIDE_PAD - 1, H1))
    m1 = np.concatenate([c1, -s1, sign, pad], axis=0)
    wgt = np.where(np.arange(H1) == 0, 1.0, 2.0)[:, None]
    minv = np.concatenate([(wgt * c1).T, (-wgt * s1).T, sign.T, pad.T], axis=1)
    n2 = np.arange(N2)[:, None]
    k2 = np.arange(N2)[None, :]
    a2 = 2.0 * np.pi * ((n2 * k2) % N2) / N2
    cg, sg = np.cos(a2), np.sin(a2)
    g2 = np.block([[cg, -sg], [sg, cg]])
    g2i = np.block([[cg, sg], [-sg, cg]])
    at = 2.0 * np.pi * (np.arange(H1 + 8)[:, None] * np.arange(N2)[None, :]) / N
    tw = np.concatenate([np.cos(at), -np.sin(at)], axis=1)
    as_bf = lambda a: jnp.asarray(a.astype(np.float32)).astype(BF16)
    return dict(m1=as_bf(m1), minv=as_bf(minv), g2=as_bf(g2), g2i=as_bf(g2i),
                tw=jnp.asarray(tw.astype(np.float32)), N1=N1, H1=H1)


S_PITCH = LANES + 8
GROUP = 8
STAGE_UNROLL = 8


SLAB_UNROLL = 4


def _slab_transpose_in(get_slab, s_ref, h1):
    def body(j, carry):
        for u in range(SLAB_UNROLL):
            i = j * SLAB_UNROLL + u
            s_ref[pl.ds(pl.multiple_of(i * S_PITCH, 8), LANES), :] = get_slab(i).T
        return carry

    lax.fori_loop(0, h1 // SLAB_UNROLL, body, 0)


def _slab_transpose_out(s_ref, put_slab, h1):
    def body(j, carry):
        for u in range(SLAB_UNROLL):
            i = j * SLAB_UNROLL + u
            put_slab(i, s_ref[pl.ds(pl.multiple_of(i * S_PITCH, 8), LANES), :].T)
        return carry

    lax.fori_loop(0, h1 // SLAB_UNROLL, body, 0)


def _slab_rows(i):
    return pl.ds(pl.multiple_of(i * LANES, LANES), LANES)


def _fft_stage_a(s_ref, nch, m1, tw_ref, ab_ref, side_ref, n1):
    h1 = n1 // 2
    twr = tw_ref[0:h1, 0:LANES]
    twi = tw_ref[0:h1, LANES:2 * LANES]
    tsr = tw_ref[h1:h1 + 1, 0:LANES]
    tsi = tw_ref[h1:h1 + 1, LANES:2 * LANES]

    def body(g, carry):
        for u in range(STAGE_UNROLL):
            c0 = (g * STAGE_UNROLL + u) * GROUP
            rhs = jnp.concatenate([s_ref[pl.ds(c0 + i, h1, stride=S_PITCH), :] for i in range(GROUP)], axis=1)
            res = jnp.dot(m1, rhs.astype(BF16), preferred_element_type=F32)
            for i in range(GROUP):
                ar = res[0:h1, LANES * i:LANES * (i + 1)]
                ai = res[h1:2 * h1, LANES * i:LANES * (i + 1)]
                rows = pl.ds(pl.multiple_of((c0 + i) * h1, h1), h1)
                ab_ref[rows, 0:LANES] = ar * twr - ai * twi
                ab_ref[rows, LANES:2 * LANES] = ar * twi + ai * twr
            a_side = jnp.concatenate([res[2 * h1:2 * h1 + 1, LANES * i:LANES * (i + 1)] for i in range(GROUP)],
                                     axis=0)
            rows = pl.ds(pl.multiple_of(c0, GROUP), GROUP)
            side_ref[rows, 0:LANES] = a_side * tsr
            side_ref[rows, LANES:2 * LANES] = a_side * tsi
        return carry

    lax.fori_loop(0, nch // (GROUP * STAGE_UNROLL), body, 0)


def _cmul(a, b):
    ar, ai = a[:, 0:LANES], a[:, LANES:2 * LANES]
    br, bi = b[:, 0:LANES], b[:, LANES:2 * LANES]
    return jnp.concatenate([ar * br - ai * bi, ar * bi + ai * br], axis=1)


def _fft_mid(ab_ref, side_ref, get_h, get_h_side, g2, g2i, nrows):
    chunk = min(MID_ROWS, nrows)

    def body(r, carry):
        rows = pl.ds(pl.multiple_of(r * chunk, chunk), chunk)
        spec = _bdot(ab_ref[rows, :], g2)
        ab_ref[rows, :] = _bdot(_cmul(spec, get_h(rows)), g2i)
        return carry

    lax.fori_loop(0, nrows // chunk, body, 0)
    side_ref[...] = _bdot(_cmul(_bdot(side_ref[...], g2), get_h_side()), g2i)


def _fft_stage_a_inv(ab_ref, side_ref, nch, minv, tw_ref, s_ref, n1):
    h1 = n1 // 2
    twr = tw_ref[0:h1, 0:LANES]
    twi = tw_ref[0:h1, LANES:2 * LANES]
    tsr = tw_ref[h1:h1 + 1, 0:LANES]
    tsi = tw_ref[h1:h1 + 1, LANES:2 * LANES]
    first_row = lax.broadcasted_iota(jnp.int32, (SIDE_PAD, GROUP * LANES), 0) == 0

    def body(g, carry):
        for u in range(STAGE_UNROLL):
            c0 = (g * STAGE_UNROLL + u) * GROUP
            blk = ab_ref[pl.ds(pl.multiple_of(c0 * h1, GROUP * h1), GROUP * h1), :]
            re, im = [], []
            for i in range(GROUP):
                br = blk[i * h1:(i + 1) * h1, 0:LANES]
                bi = blk[i * h1:(i + 1) * h1, LANES:2 * LANES]
                re.append(br * twr + bi * twi)
                im.append(bi * twr - br * twi)
            side = side_ref[pl.ds(pl.multiple_of(c0, GROUP), GROUP), :]
            side_re = side[:, 0:LANES] * tsr + side[:, LANES:2 * LANES] * tsi
            side_row = jnp.concatenate([side_re[i:i + 1, :] for i in range(GROUP)], axis=1)
            side_blk = jnp.where(first_row, jnp.broadcast_to(side_row, first_row.shape), 0.0)
            rhs = jnp.concatenate([jnp.concatenate(re, axis=1), jnp.concatenate(im, axis=1), side_blk], axis=0)
            y = jnp.dot(minv, rhs.astype(BF16), preferred_element_type=F32)
            for i in range(GROUP):
                s_ref[pl.ds(c0 + i, h1, stride=S_PITCH), :] = y[:, LANES * i:LANES * (i + 1)]
        return carry

    lax.fori_loop(0, nch // (GROUP * STAGE_UNROLL), body, 0)


def _filter_spectrum_kernel(hf_ref, hb_ref, m1_ref, g2_ref, tw_ref, o_ref, oside_ref,
                            s_ref, af_ref, ab_ref, afs_ref, abs_ref, *, n1, inv_n):
    nch = hf_ref.shape[1]
    h1 = n1 // 2
    m1, g2 = m1_ref[...], g2_ref[...]
    _slab_transpose_in(lambda i: hf_ref[_slab_rows(i), :], s_ref, h1)
    _fft_stage_a(s_ref, nch, m1, tw_ref, af_ref, afs_ref, n1)
    _slab_transpose_in(lambda i: hb_ref[_slab_rows(i), :], s_ref, h1)
    _fft_stage_a(s_ref, nch, m1, tw_ref, ab_ref, abs_ref, n1)

    def two_sided(fwd, bwd):
        sf = _bdot(fwd, g2)
        sb = _bdot(bwd, g2)
        return jnp.concatenate([sf[:, 0:LANES] + sb[:, 0:LANES], sf[:, LANES:] - sb[:, LANES:]], axis=1) * inv_n

    chunk = min(MID_ROWS, nch * h1)

    def body(r, carry):
        rows = pl.ds(pl.multiple_of(r * chunk, chunk), chunk)
        o_ref[rows, :] = two_sided(af_ref[rows, :], ab_ref[rows, :])
        return carry

    lax.fori_loop(0, (nch * h1) // chunk, body, 0)
    oside_ref[...] = two_sided(afs_ref[...], abs_ref[...])


def _filter_spectrum(filt, L, d_hyena, tabs):
    h1 = tabs["H1"]
    order = filt.shape[1] // (2 * d_hyena)
    cb = LANES
    nblk = d_hyena // cb
    full = lambda a: pl.BlockSpec(a.shape, lambda o, c: (0,) * a.ndim)
    main = pltpu.VMEM((cb * h1, 2 * LANES), F32)
    side = pltpu.VMEM((cb, 2 * LANES), F32)
    return pl.pallas_call(
        functools.partial(_filter_spectrum_kernel, n1=tabs["N1"], inv_n=1.0 / (2 * L)),
        grid=(order, nblk),
        in_specs=[pl.BlockSpec((L, cb), lambda o, c: (0, o * 2 * nblk + c)),
                  pl.BlockSpec((L, cb), lambda o, c: (0, o * 2 * nblk + nblk + c)),
                  full(tabs["m1"]), full(tabs["g2"]), full(tabs["tw"])],
        out_specs=[pl.BlockSpec((cb * h1, 2 * LANES), lambda o, c: (o * nblk + c, 0)),
                   pl.BlockSpec((cb, 2 * LANES), lambda o, c: (o * nblk + c, 0))],
        out_shape=[jax.ShapeDtypeStruct((order * d_hyena * h1, 2 * LANES), F32),
                   jax.ShapeDtypeStruct((order * d_hyena, 2 * LANES), F32)],
        scratch_shapes=[pltpu.VMEM((h1 * S_PITCH, LANES), F32), main, main, side, side],
        compiler_params=_params("parallel", "parallel"),
        name="filter_spectrum",
    )(filt, filt, tabs["m1"], tabs["g2"], tabs["tw"])


def _hyena_kernel(pv_ref, px1_ref, px2_ref, wv_ref, wx1_ref, wx2_ref, bv_ref, bx1_ref, bx2_ref, h_ref, hside_ref,
                  d_ref, m1_ref, minv_ref, g2_ref, g2i_ref, tw_ref, o_ref, s_ref, ab_ref, side_ref, v_ref, z_ref,
                  *, n1):
    nch = pv_ref.shape[2]
    h1 = n1 // 2
    m1, minv, g2, g2i = m1_ref[...], minv_ref[...], g2_ref[...], g2i_ref[...]

    def long_conv(get_slab, order, put_slab):
        _slab_transpose_in(get_slab, s_ref, h1)
        _fft_stage_a(s_ref, nch, m1, tw_ref, ab_ref, side_ref, n1)
        _fft_mid(ab_ref, side_ref, lambda rows: h_ref[order, rows, :], lambda: hside_ref[order], g2, g2i, nch * h1)
        _fft_stage_a_inv(ab_ref, side_ref, nch, minv, tw_ref, s_ref, n1)
        _slab_transpose_out(s_ref, put_slab, h1)

    def get_v(i):
        v = _conv3_rows(pv_ref, i, LANES, wv_ref, bv_ref)
        v_ref[_slab_rows(i), :] = v
        return v

    def put1(i, conv):
        rows = _slab_rows(i)
        x1 = _conv3_rows(px1_ref, i, LANES, wx1_ref, bx1_ref)
        z_ref[rows, :] = x1 * (conv + d_ref[0:1, :] * v_ref[rows, :])

    long_conv(get_v, 0, put1)

    def put2(i, conv):
        rows = _slab_rows(i)
        x2 = _conv3_rows(px2_ref, i, LANES, wx2_ref, bx2_ref)
        o_ref[0, rows, :] = (x2 * (conv + d_ref[1:2, :] * z_ref[rows, :])).astype(o_ref.dtype)

    long_conv(lambda i: z_ref[_slab_rows(i), :], 1, put2)


def _hyena(proj, conv_w, conv_b, spec, spec_side, skip, tabs):
    B, L, _ = proj.shape
    D = skip.shape[1]
    n1, h1 = tabs["N1"], tabs["H1"]
    cb = LANES
    nblk = D // cb
    spec = spec.reshape(2, D * h1, 2 * LANES)
    spec_side = spec_side.reshape(2, D, 2 * LANES)
    col = lambda part: pl.BlockSpec((1, L, cb), lambda c, b: (b, 0, part * nblk + c))
    wcol = lambda part, rows: pl.BlockSpec((rows, cb), lambda c, b: (0, part * nblk + c))
    full = lambda a: pl.BlockSpec(a.shape, lambda c, b: (0,) * a.ndim)
    conv_b = conv_b.reshape(1, 3 * D)
    return pl.pallas_call(
        functools.partial(_hyena_kernel, n1=n1),
        grid=(nblk, B),
        in_specs=[col(0), col(1), col(2), wcol(0, 3), wcol(1, 3), wcol(2, 3), wcol(0, 1), wcol(1, 1), wcol(2, 1),
                  pl.BlockSpec((2, cb * h1, 2 * LANES), lambda c, b: (0, c, 0), pipeline_mode=pl.Buffered(1)),
                  pl.BlockSpec((2, cb, 2 * LANES), lambda c, b: (0, c, 0)),
                  pl.BlockSpec((2, cb), lambda c, b: (0, c)),
                  full(tabs["m1"]), full(tabs["minv"]), full(tabs["g2"]), full(tabs["g2i"]), full(tabs["tw"])],
        out_specs=pl.BlockSpec((1, L, cb), lambda c, b: (b, 0, c)),
        out_shape=jax.ShapeDtypeStruct((B, L, D), BF16),
        scratch_shapes=[pltpu.VMEM((h1 * S_PITCH, LANES), F32), pltpu.VMEM((cb * h1, 2 * LANES), F32),
                        pltpu.VMEM((cb, 2 * LANES), F32), pltpu.VMEM((L, cb), F32), pltpu.VMEM((L, cb), F32)],
        compiler_params=_params("parallel", "arbitrary"),
        name="hyena_fftconv",
    )(proj, proj, proj, conv_w, conv_w, conv_w, conv_b, conv_b, conv_b, spec, spec_side, skip.astype(F32),
      tabs["m1"], tabs["minv"], tabs["g2"], tabs["g2i"], tabs["tw"])


ML_CHUNK = 256
ML_CHUNKS_PER_STEP = 2


def _log_sigmoid(x):
    return jnp.minimum(x, 0.0) - jnp.log1p(jnp.exp(-jnp.abs(x)))


def _mlstm_direction(q, k, v, i_row, cum_row, logf_row, c_ref, m_ref, mask, scale):
    c, dh = q.shape
    g_row = i_row - cum_row
    f_col_rep = jnp.broadcast_to(cum_row, (LANES, c)).T
    m_prev = m_ref[0:1, 0:1]
    e = jnp.where(mask, g_row, -jnp.inf)
    a = jnp.maximum(jnp.max(e, axis=1, keepdims=True), m_prev)
    w = jnp.exp(e - a)
    sc = lax.dot_general(q.astype(BF16), k.astype(BF16), (((1,), (1,)), ((), ())), preferred_element_type=F32)
    p = sc * (w * scale)
    vaug = jnp.concatenate([v, jnp.ones_like(v)], axis=1)
    inter = jnp.exp(m_prev - a)
    tot = _bdot(p, vaug) + inter * _bdot(q, c_ref[...])
    num, den = tot[:, 0:dh], tot[:, dh:2 * dh]
    m_t = f_col_rep + a
    h = num / jnp.maximum(jnp.abs(den), jnp.exp(-m_t))
    a_end = jnp.maximum(jnp.max(g_row, axis=1, keepdims=True), m_prev)
    f_end = jnp.sum(logf_row, axis=1, keepdims=True)
    we = jnp.exp(g_row - a_end) * scale
    kw = k.astype(F32).T * we
    c_ref[...] = jnp.exp(m_prev - a_end) * c_ref[...] + _bdot(kw, vaug)
    m_ref[...] = jnp.broadcast_to(f_end + a_end, m_ref.shape)
    return h


def _mlstm_kernel(qf_ref, kf_ref, vf_ref, gf_ref, qb_ref, kb_ref, vb_ref, gb_ref, bias_ref,
                  hf_ref, hb_ref, c_ref, m_ref, *, heads, scale, chunk):
    @pl.when(pl.program_id(1) == 0)
    def _():
        c_ref[...] = jnp.zeros_like(c_ref)
        m_ref[...] = jnp.zeros_like(m_ref)

    c = chunk
    nsub = qf_ref.shape[1] // c
    dh = qf_ref.shape[2] // heads
    r = lax.broadcasted_iota(jnp.int32, (c, c), 0)
    s = lax.broadcasted_iota(jnp.int32, (c, c), 1)
    mask_f = s <= r
    mask_b = s >= r

    def gate_rows(g, mask_t):
        g = g + bias_ref[...]
        logf = _log_sigmoid(g)
        tri = mask_t.astype(BF16)
        cum = sum(jnp.dot(part, tri, preferred_element_type=F32) for part in _split3(logf))
        return g, logf, cum

    for u in range(nsub):
        rf = slice(u * c, (u + 1) * c)
        rb = slice((nsub - 1 - u) * c, (nsub - u) * c)
        g_f, logf_f, cum_f = gate_rows(gf_ref[0, :, rf], mask_b)
        g_b, logf_b, cum_b = gate_rows(gb_ref[0, :, rb], mask_f)
        for h in range(heads):
            sl = slice(h * dh, (h + 1) * dh)
            fi, ff, bi, bf = h, heads + h, 2 * heads + h, 3 * heads + h
            hf_ref[0, rf, sl] = _mlstm_direction(
                qf_ref[0, rf, sl], kf_ref[0, rf, sl], vf_ref[0, rf, sl], g_f[fi:fi + 1], cum_f[ff:ff + 1],
                logf_f[ff:ff + 1], c_ref.at[h], m_ref.at[h], mask_f, scale).astype(hf_ref.dtype)
            hb_ref[0, rb, sl] = _mlstm_direction(
                qb_ref[0, rb, sl], kb_ref[0, rb, sl], vb_ref[0, rb, sl], g_b[bi:bi + 1], cum_b[bf:bf + 1],
                logf_b[bf:bf + 1], c_ref.at[heads + h], m_ref.at[heads + h], mask_b, scale).astype(hb_ref.dtype)


def _mlstm(qk, proj, v_col0, gates, gate_b, d_ml):
    B, L, _ = qk.shape
    H = MLSTM_HEADS
    dh = d_ml // H
    c = min(ML_CHUNK, L)
    rows = min(ML_CHUNKS_PER_STEP * c, L)
    nb = L // rows
    g = gates.transpose(0, 2, 1)
    bias = jnp.broadcast_to(gate_b.astype(F32)[:, None], (4 * H, c))
    vb0 = v_col0 // d_ml
    specs = []
    for pos in (lambda ci: ci, lambda ci: nb - 1 - ci):
        specs += [pl.BlockSpec((1, rows, d_ml), lambda bi, ci, pos=pos: (bi, pos(ci), 0)),
                  pl.BlockSpec((1, rows, d_ml), lambda bi, ci, pos=pos: (bi, pos(ci), 1)),
                  pl.BlockSpec((1, rows, d_ml), lambda bi, ci, pos=pos: (bi, pos(ci), vb0)),
                  pl.BlockSpec((1, 4 * H, rows), lambda bi, ci, pos=pos: (bi, 0, pos(ci)))]
    specs.append(pl.BlockSpec((4 * H, c), lambda bi, ci: (0, 0)))
    out_shape = jax.ShapeDtypeStruct((B, L, d_ml), BF16)
    return pl.pallas_call(
        functools.partial(_mlstm_kernel, heads=H, scale=dh ** -0.5, chunk=c),
        grid=(B, nb),
        in_specs=specs,
        out_specs=[pl.BlockSpec((1, rows, d_ml), lambda bi, ci: (bi, ci, 0)),
                   pl.BlockSpec((1, rows, d_ml), lambda bi, ci: (bi, nb - 1 - ci, 0))],
        out_shape=[out_shape, out_shape],
        scratch_shapes=[pltpu.VMEM((2 * H, dh, 2 * dh), F32), pltpu.VMEM((2 * H, 1, LANES), F32)],
        compiler_params=_params("parallel", "arbitrary"),
        name="mlstm_scan",
    )(qk, qk, proj, g, qk, qk, proj, g, bias)


def _group_rms(y, gain, bd, group):
    ss = jnp.dot((y * y).astype(BF16), bd, preferred_element_type=F32)
    return y * lax.rsqrt(ss * (1.0 / group) + EPS) * gain


def _block_diag_ones(n, group):
    idx = np.arange(n) // group
    return jnp.asarray((idx[:, None] == idx[None, :]).astype(np.float32)).astype(BF16)


def _mix_xattn_kernel(z_ref, hf_ref, hb_ref, o_ref, x_ref, mem_ref, ghy_ref, gml_ref, why_ref, wml_ref, bdh_ref,
                      bdm_ref, gx_ref, gm_ref, wq_ref, wkv_ref, wo_ref, out_ref, kv_ref,
                      *, hy_group, ml_group, heads, scale):
    @pl.when(pl.program_id(1) == 0)
    def _():
        kv_ref[...] = _bdot(_rms(mem_ref[0], gm_ref[...]), wkv_ref[...]).astype(kv_ref.dtype)

    y_hy = _group_rms(z_ref[0].astype(F32), ghy_ref[...], bdh_ref[...], hy_group)
    h_sum = hf_ref[0].astype(F32) + hb_ref[0].astype(F32)
    y_ml = _group_rms(jax.nn.sigmoid(o_ref[0].astype(F32)) * h_sum, gml_ref[...], bdm_ref[...], ml_group)
    h1 = x_ref[0] + _bdot(y_hy, why_ref[...]) + _bdot(y_ml, wml_ref[...])

    D = h1.shape[1]
    dh = D // heads
    q = _bdot(_rms(h1, gx_ref[...]), wq_ref[...])
    outs = []
    for hd in range(heads):
        qh = q[:, hd * dh:(hd + 1) * dh]
        kh = kv_ref[:, hd * dh:(hd + 1) * dh]
        vh = kv_ref[:, D + hd * dh:D + (hd + 1) * dh]
        s = lax.dot_general(qh.astype(BF16), kh, (((1,), (1,)), ((), ())), preferred_element_type=F32) * scale
        e = jnp.exp(s - jnp.max(s, axis=-1, keepdims=True))
        p = e / jnp.sum(e, axis=-1, keepdims=True)
        outs.append(_bdot(p, vh))
    out_ref[0] = h1 + _bdot(jnp.concatenate(outs, axis=1), wo_ref[...])


def _mix_xattn(z, hf, hb, proj, o_col0, x, mem, g_hy, g_ml, w_out, g_x, g_mem, wq, wk, wv, wo, tm=512):
    B, L, d_hy = z.shape
    d_ml = hf.shape[2]
    D = x.shape[2]
    nm = mem.shape[1]
    ob = o_col0 // d_ml
    row = lambda n: pl.BlockSpec((1, tm, n), lambda bi, i: (bi, i, 0))
    const = lambda a: pl.BlockSpec(a.shape, lambda bi, i: (0,) * a.ndim, pipeline_mode=pl.Buffered(1))
    hy_group, ml_group = d_hy // HYENA_GROUPS, d_ml // MLSTM_HEADS
    consts = [g_hy.reshape(1, -1), g_ml.reshape(1, -1), w_out[:d_hy].astype(BF16), w_out[d_hy:].astype(BF16),
              _block_diag_ones(d_hy, hy_group), _block_diag_ones(d_ml, ml_group), g_x.reshape(1, D),
              g_mem.reshape(1, D), wq.astype(BF16), jnp.concatenate([wk, wv], axis=1).astype(BF16), wo.astype(BF16)]
    return pl.pallas_call(
        functools.partial(_mix_xattn_kernel, hy_group=hy_group, ml_group=ml_group, heads=XATTN_HEADS,
                          scale=(D // XATTN_HEADS) ** -0.5),
        grid=(B, L // tm),
        in_specs=[row(d_hy), row(d_ml), row(d_ml), pl.BlockSpec((1, tm, d_ml), lambda bi, i: (bi, i, ob)), row(D),
                  pl.BlockSpec((1, nm, D), lambda bi, i: (bi, 0, 0))] + [const(a) for a in consts],
        out_specs=row(D),
        out_shape=jax.ShapeDtypeStruct((B, L, D), F32),
        scratch_shapes=[pltpu.VMEM((nm, 2 * D), BF16)],
        compiler_params=_params("parallel", "arbitrary"),
        name="mix_xattn",
    )(z, hf, hb, proj, x, mem, *consts)


MLP_CHUNK = 1024


def _mlp_kernel(h_ref, g_ref, w1_ref, w2_ref, gf_ref, out_ref, xn_ref, *, final_norm):
    j = pl.program_id(1)

    @pl.when(j == 0)
    def _():
        x = h_ref[...]
        xn_ref[...] = _rms(x, g_ref[...]).astype(BF16)
        out_ref[...] = x

    for c in range(w1_ref.shape[1] // MLP_CHUNK):
        cols = slice(c * MLP_CHUNK, (c + 1) * MLP_CHUNK)
        a = jnp.maximum(jnp.dot(xn_ref[...], w1_ref[:, cols], preferred_element_type=F32), 0.0)
        out_ref[...] += _bdot(a * a, w2_ref[cols, :])

    if final_norm:
        @pl.when(j == pl.num_programs(1) - 1)
        def _():
            out_ref[...] = _rms(out_ref[...], gf_ref[...])


def _mlp(h, g, w1, w2, gf, final_norm, tm=1024, tf=2048):
    M, D = h.shape
    dff = w1.shape[1]
    return pl.pallas_call(
        functools.partial(_mlp_kernel, final_norm=final_norm),
        grid=(M // tm, dff // tf),
        in_specs=[pl.BlockSpec((tm, D), lambda i, j: (i, 0)), pl.BlockSpec((1, D), lambda i, j: (0, 0)),
                  pl.BlockSpec((D, tf), lambda i, j: (0, j)), pl.BlockSpec((tf, D), lambda i, j: (j, 0)),
                  pl.BlockSpec((1, D), lambda i, j: (0, 0))],
        out_specs=pl.BlockSpec((tm, D), lambda i, j: (i, 0)),
        out_shape=jax.ShapeDtypeStruct((M, D), F32),
        scratch_shapes=[pltpu.VMEM((tm, D), BF16)],
        compiler_params=_params("parallel", "arbitrary"),
        name="mlp",
    )(h, g.reshape(1, D), w1.astype(BF16), w2.astype(BF16), gf.reshape(1, D))


def _pad_cols(w, n):
    return jnp.pad(w, ((0, 0), (0, n - w.shape[1])))


def _layer(h, mem, p, l):
    B, L, D = h.shape
    M = B * L
    d_hy = p["hy_norm_g"].shape[1]
    d_ml = p["ml_norm_g"].shape[1]
    hy_cols = 3 * d_hy
    qk_cols = 2 * d_ml
    n_gate = 4 * MLSTM_HEADS
    w_in = p["w_in"][l]
    qk0 = hy_cols
    vo0 = qk0 + qk_cols
    g0 = vo0 + 2 * d_ml
    w_main = jnp.concatenate([w_in[:, :hy_cols], w_in[:, vo0:g0]], axis=1).astype(BF16)
    v_col0 = hy_cols
    o_col0 = hy_cols + d_ml
    n_in = hy_cols + 2 * d_ml
    proj, qk, gates = _in_proj(h.reshape(M, D), p["norm_mix_g"][l], w_main, w_in[:, qk0:vo0].astype(BF16),
                               _pad_cols(w_in[:, g0:g0 + n_gate], LANES).astype(BF16),
                               p["ml_conv_w"][l], p["ml_conv_b"][l], seq_len=L, tm=min(1024, L))
    proj = proj.reshape(B, L, n_in)
    qk = qk.reshape(B, L, qk_cols)
    gates = gates[:, :n_gate].reshape(B, L, n_gate)

    tabs = _fft_tables(L)
    filt = _hyena_filters(L, p["hy_filt_w1"][l], p["hy_filt_b1"][l], p["hy_filt_freq1"][l], p["hy_filt_w2"][l],
                          p["hy_filt_b2"][l], p["hy_filt_freq2"][l], p["hy_filt_w3"][l], d_hy,
                          tl=min(512, L))
    spec, spec_side = _filter_spectrum(filt, L, d_hy, tabs)
    z_hy = _hyena(proj, p["hy_conv_w"][l], p["hy_conv_b"][l], spec, spec_side, p["hy_skip"][l], tabs)

    h_f, h_b = _mlstm(qk, proj, v_col0, gates, p["ml_gate_b"][l], d_ml)

    return _mix_xattn(z_hy, h_f, h_b, proj, o_col0, h, mem, p["hy_norm_g"][l], p["ml_norm_g"][l], p["w_out"][l],
                      p["norm_x_g"][l], p["norm_mem_g"][l], p["xa_wq"][l], p["xa_wk"][l], p["xa_wv"][l],
                      p["xa_wo"][l], tm=min(1024, L))


def kernel(x, mem, norm_mix_g, w_in, hy_conv_w, hy_conv_b, hy_filt_w1, hy_filt_b1, hy_filt_freq1, hy_filt_w2,
           hy_filt_b2, hy_filt_freq2, hy_filt_w3, hy_skip, hy_norm_g, ml_conv_w, ml_conv_b, ml_gate_b, ml_norm_g,
           w_out, norm_x_g, norm_mem_g, xa_wq, xa_wk, xa_wv, xa_wo, norm_ff_g, ff_w1, ff_w2, final_norm_g):
    p = dict(norm_mix_g=norm_mix_g, w_in=w_in, hy_conv_w=hy_conv_w, hy_conv_b=hy_conv_b, hy_filt_w1=hy_filt_w1,
             hy_filt_b1=hy_filt_b1, hy_filt_freq1=hy_filt_freq1, hy_filt_w2=hy_filt_w2, hy_filt_b2=hy_filt_b2,
             hy_filt_freq2=hy_filt_freq2, hy_filt_w3=hy_filt_w3, hy_skip=hy_skip, hy_norm_g=hy_norm_g,
             ml_conv_w=ml_conv_w, ml_conv_b=ml_conv_b, ml_gate_b=ml_gate_b, ml_norm_g=ml_norm_g, w_out=w_out,
             norm_x_g=norm_x_g, norm_mem_g=norm_mem_g, xa_wq=xa_wq, xa_wk=xa_wk, xa_wv=xa_wv, xa_wo=xa_wo)
    B, L, D = x.shape
    depth = w_in.shape[0]
    h = x
    for l in range(depth):
        h = _layer(h, mem, p, l)
        h = _mlp(h.reshape(B * L, D), norm_ff_g[l], ff_w1[l], ff_w2[l], final_norm_g, final_norm=l == depth - 1,
                 tm=min(1024, B * L)).reshape(B, L, D)
    return h
```

```python
import functools
import math

import numpy as np
import jax
import jax.numpy as jnp
from jax import lax
from jax.experimental import pallas as pl
from jax.experimental.pallas import tpu as pltpu

F32 = jnp.float32
BF16 = jnp.bfloat16

EPS = 1e-6
HYENA_GROUPS = 8
MLSTM_HEADS = 4
XATTN_HEADS = 4
FILTER_BANDS = 16
DECAY_TARGET = 1e-2
SHORT_DECAY_PCT = 0.3
LONG_DECAY_PCT = 1.5

LANES = 128
VMEM_LIMIT = 56 * 1024 * 1024


def _params(*sem):
    return pltpu.CompilerParams(dimension_semantics=sem, vmem_limit_bytes=VMEM_LIMIT)


def _bdot(a, b):
    return jnp.dot(a.astype(BF16), b.astype(BF16), preferred_element_type=F32)


def _split3(x):
    hi = x.astype(BF16)
    r = x - hi.astype(F32)
    mid = r.astype(BF16)
    lo = (r - mid.astype(F32)).astype(BF16)
    return hi, mid, lo


IN_CHUNK = 512
HALO = 8


def _rms(x, g):
    return x * lax.rsqrt(jnp.mean(x * x, axis=-1, keepdims=True) + EPS) * g


def _in_proj_kernel(x_ref, xb_ref, xa_ref, g_ref, wc_ref, wp_ref, wg_ref, cw_ref, cb_ref,
                    proj_ref, qk_ref, og_ref, xn_ref, *, tiles_per_seq, n_hy):
    tm = x_ref.shape[0]
    g = g_ref[...]
    x = x_ref[...]
    xn_ref[...] = (x * g).astype(BF16)
    r = lax.rsqrt(jnp.mean(x * x, axis=-1, keepdims=True) + EPS)
    og_ref[...] = jnp.dot(xn_ref[...], wg_ref[...], preferred_element_type=F32) * r
    for c in range(wp_ref.shape[1] // IN_CHUNK):
        cols = slice(c * IN_CHUNK, (c + 1) * IN_CHUNK)
        out_cols = slice(n_hy + c * IN_CHUNK, n_hy + (c + 1) * IN_CHUNK)
        proj_ref[:, out_cols] = (jnp.dot(xn_ref[...], wp_ref[:, cols], preferred_element_type=F32) * r
                                 ).astype(proj_ref.dtype)

    halo = _rms(jnp.concatenate([xb_ref[...], xa_ref[...]], axis=0), g).astype(BF16)
    pos = pl.program_id(0) % tiles_per_seq
    row = lax.broadcasted_iota(jnp.int32, (tm, IN_CHUNK), 0)
    for c in range(wc_ref.shape[1] // IN_CHUNK):
        cols = slice(c * IN_CHUNK, (c + 1) * IN_CHUNK)
        u = jnp.dot(xn_ref[...], wc_ref[:, cols], preferred_element_type=F32) * r
        uh = jnp.dot(halo, wc_ref[:, cols], preferred_element_type=F32)
        before = jnp.where(pos == 0, 0.0, uh[HALO - 1:HALO, :])
        after = jnp.where(pos == tiles_per_seq - 1, 0.0, uh[HALO:HALO + 1, :])
        prev = jnp.where(row == 0, before, pltpu.roll(u, 1, 0))
        nxt = jnp.where(row == tm - 1, after, pltpu.roll(u, tm - 1, 0))
        y = prev * cw_ref[0:1, cols] + u * cw_ref[1:2, cols] + nxt * cw_ref[2:3, cols] + cb_ref[:, cols]
        if c * IN_CHUNK < n_hy:
            proj_ref[:, cols] = y.astype(proj_ref.dtype)
        else:
            qk_cols = slice(c * IN_CHUNK - n_hy, (c + 1) * IN_CHUNK - n_hy)
            qk_ref[:, qk_cols] = (y * jax.nn.sigmoid(y)).astype(qk_ref.dtype)


def _in_proj(x, g, w_conv, w_plain, w_gate, conv_w, conv_b, n_hy, seq_len, tm):
    M, K = x.shape
    nc, npl, ng = w_conv.shape[1], w_plain.shape[1], w_gate.shape[1]
    hb = tm // HALO
    const = lambda a: pl.BlockSpec(a.shape, lambda i: (0,) * a.ndim, pipeline_mode=pl.Buffered(1))
    row = lambda n: pl.BlockSpec((tm, n), lambda i: (i, 0))
    g = g.reshape(1, K)
    conv_b = conv_b.reshape(1, nc)
    return pl.pallas_call(
        functools.partial(_in_proj_kernel, tiles_per_seq=seq_len // tm, n_hy=n_hy),
        grid=(M // tm,),
        in_specs=[row(K),
                  pl.BlockSpec((HALO, K), lambda i: (jnp.maximum(i * hb - 1, 0), 0)),
                  pl.BlockSpec((HALO, K), lambda i: (jnp.minimum((i + 1) * hb, M // HALO - 1), 0)),
                  const(g), const(w_conv), const(w_plain), const(w_gate), const(conv_w), const(conv_b)],
        out_specs=[row(n_hy + npl), row(nc - n_hy), row(ng)],
        out_shape=[jax.ShapeDtypeStruct((M, n_hy + npl), BF16), jax.ShapeDtypeStruct((M, nc - n_hy), BF16),
                   jax.ShapeDtypeStruct((M, ng), F32)],
        scratch_shapes=[pltpu.VMEM((tm, K), BF16)],
        compiler_params=_params("parallel"),
        name="in_proj",
    )(x, x, x, g, w_conv, w_plain, w_gate, conv_w, conv_b)


def _filter_kernel(feat_ref, w1_ref, b1_ref, f1_ref, w2_ref, b2_ref, f2_ref, w3_ref, dec_ref, dir_ref,
                   o_ref, *, L):
    tl = o_ref.shape[0]
    hp = lax.Precision.HIGHEST
    pos = (lax.broadcasted_iota(jnp.int32, (tl, LANES), 0) + pl.program_id(0) * tl).astype(F32)
    lane = lax.broadcasted_iota(jnp.int32, (tl, LANES), 1)
    t = pos * (1.0 / (L - 1))
    arg = feat_ref[...] * (pos * (2.0 * math.pi / L))
    z = jnp.where(lane < FILTER_BANDS, jnp.cos(arg),
                  jnp.where(lane < 2 * FILTER_BANDS, -jnp.sin(arg),
                            jnp.where(lane == 2 * FILTER_BANDS, t, 0.0)))
    hid = jnp.sin(f1_ref[...] * (jnp.dot(z, w1_ref[...], precision=hp, preferred_element_type=F32) + b1_ref[...]))
    hid = jnp.sin(f2_ref[...] * (jnp.dot(hid, w2_ref[...], precision=hp, preferred_element_type=F32) + b2_ref[...]))
    filt = jnp.dot(hid, w3_ref[...], precision=hp, preferred_element_type=F32)
    filt = filt * jnp.exp(-t[:, 0:1] * dec_ref[...])
    filt = jnp.where(pos[:, 0:1] == 0.0, filt * dir_ref[...], filt)
    o_ref[...] = filt


def _hyena_filters(L, w1, b1, fr1, w2, b2, fr2, w3, d_hyena, tl=512):
    n_emb, n_hid = w1.shape
    n_out = w3.shape[1]
    bands = jnp.linspace(1e-4, FILTER_BANDS - 1, FILTER_BANDS, dtype=F32)
    feat = jnp.zeros((1, LANES), F32).at[0, :FILTER_BANDS].set(bands).at[0, FILTER_BANDS:2 * FILTER_BANDS].set(bands)
    w1p = jnp.zeros((LANES, n_hid), F32).at[:n_emb - 1].set(w1[1:]).at[n_emb - 1].set(w1[0])
    max_decay = math.log(DECAY_TARGET) / SHORT_DECAY_PCT
    min_decay = math.log(DECAY_TARGET) / LONG_DECAY_PCT
    deltas = jnp.abs(jnp.linspace(min_decay, max_decay, d_hyena, dtype=F32))
    reps = n_out // d_hyena
    dec = jnp.tile(deltas, reps).reshape(1, n_out)
    dirmask = jnp.tile(jnp.concatenate([jnp.ones((d_hyena,), F32), jnp.zeros((d_hyena,), F32)]), reps // 2)
    full = lambda shape: pl.BlockSpec(shape, lambda i: (0,) * len(shape))
    return pl.pallas_call(
        functools.partial(_filter_kernel, L=L),
        grid=(L // tl,),
        in_specs=[full((1, LANES)), full((LANES, n_hid)), full((1, n_hid)), full((1, n_hid)),
                  full((n_hid, n_hid)), full((1, n_hid)), full((1, n_hid)), full((n_hid, n_out)),
                  full((1, n_out)), full((1, n_out))],
        out_specs=pl.BlockSpec((tl, n_out), lambda i: (i, 0)),
        out_shape=jax.ShapeDtypeStruct((L, n_out), F32),
        compiler_params=_params("parallel"),
        name="hyena_filters",
    )(feat, w1p, b1.reshape(1, -1), fr1.reshape(1, -1), w2, b2.reshape(1, -1), fr2.reshape(1, -1), w3,
      dec, dirmask.reshape(1, n_out))


MID_ROWS = 4096
SIDE_PAD = 16


def _fft_tables(L):
    N = 2 * L
    N2 = LANES
    N1 = N // N2
    H1 = N1 // 2
    k1 = np.arange(H1)[:, None]
    n1 = np.arange(H1)[None, :]
    a1 = 2.0 * np.pi * ((k1 * n1) % N1) / N1
    c1, s1 = np.cos(a1), np.sin(a1)
    sign = np.where(np.arange(H1) % 2 == 0, 1.0, -1.0)[None, :]
    pad = np.zeros((SIDE_PAD - 1, H1))
    m1 = np.concatenate([c1, -s1, sign, pad], axis=0)
    wgt = np.where(np.arange(H1) == 0, 1.0, 2.0)[:, None]
    minv = np.concatenate([(wgt * c1).T, (-wgt * s1).T, sign.T, pad.T], axis=1)
    n2 = np.arange(N2)[:, None]
    k2 = np.arange(N2)[None, :]
    a2 = 2.0 * np.pi * ((n2 * k2) % N2) / N2
    cg, sg = np.cos(a2), np.sin(a2)
    g2 = np.block([[cg, -sg], [sg, cg]])
    g2i = np.block([[cg, sg], [-sg, cg]])
    at = 2.0 * np.pi * (np.arange(H1 + 8)[:, None] * np.arange(N2)[None, :]) / N
    tw = np.concatenate([np.cos(at), -np.sin(at)], axis=1)
    as_bf = lambda a: jnp.asarray(a.astype(np.float32)).astype(BF16)
    return dict(m1=as_bf(m1), minv=as_bf(minv), g2=as_bf(g2), g2i=as_bf(g2i),
                tw=jnp.asarray(tw.astype(np.float32)), N1=N1, H1=H1)


S_PITCH = LANES + 8
GROUP = 8
STAGE_UNROLL = 8


SLAB_UNROLL = 4


def _slab_scratch_rows(i):
    return pl.ds(pl.multiple_of(i * S_PITCH, 8), LANES)


def _slab_transpose_one(get_slab, dst_refs, i):
    t = get_slab(i).T
    for dst in dst_refs:
        dst[_slab_scratch_rows(i), :] = t


def _slab_transpose_in(get_slab, dst_refs, h1):
    def body(j, carry):
        for u in range(SLAB_UNROLL):
            _slab_transpose_one(get_slab, dst_refs, j * SLAB_UNROLL + u)
        return carry

    lax.fori_loop(0, h1 // SLAB_UNROLL, body, 0)


def _slab_transpose_out(s_ref, put_slab, h1):
    def body(j, carry):
        for u in range(SLAB_UNROLL):
            i = j * SLAB_UNROLL + u
            put_slab(i, s_ref[_slab_scratch_rows(i), :].T)
        return carry

    lax.fori_loop(0, h1 // SLAB_UNROLL, body, 0)


def _slab_map(fn, h1):
    def body(j, carry):
        for u in range(SLAB_UNROLL):
            fn(_slab_scratch_rows(j * SLAB_UNROLL + u))
        return carry

    lax.fori_loop(0, h1 // SLAB_UNROLL, body, 0)


def _slab_rows(i):
    return pl.ds(pl.multiple_of(i * LANES, LANES), LANES)


def _fft_stage_a(s_ref, nch, m1, tw_ref, ab_ref, side_ref, n1, side_work=None):
    h1 = n1 // 2
    twr = tw_ref[0:h1, 0:LANES]
    twi = tw_ref[0:h1, LANES:2 * LANES]
    tsr = tw_ref[h1:h1 + 1, 0:LANES]
    tsi = tw_ref[h1:h1 + 1, LANES:2 * LANES]

    def body(g, carry):
        for u in range(STAGE_UNROLL):
            c0 = (g * STAGE_UNROLL + u) * GROUP
            rhs = jnp.concatenate([s_ref[pl.ds(c0 + i, h1, stride=S_PITCH), :] for i in range(GROUP)], axis=1)
            res = jnp.dot(m1, rhs.astype(BF16), preferred_element_type=F32)
            for i in range(GROUP):
                ar = res[0:h1, LANES * i:LANES * (i + 1)]
                ai = res[h1:2 * h1, LANES * i:LANES * (i + 1)]
                rows = pl.ds(pl.multiple_of((c0 + i) * h1, h1), h1)
                ab_ref[rows, 0:LANES] = ar * twr - ai * twi
                ab_ref[rows, LANES:2 * LANES] = ar * twi + ai * twr
            a_side = jnp.concatenate([res[2 * h1:2 * h1 + 1, LANES * i:LANES * (i + 1)] for i in range(GROUP)],
                                     axis=0)
            rows = pl.ds(pl.multiple_of(c0, GROUP), GROUP)
            side_ref[rows, 0:LANES] = a_side * tsr
            side_ref[rows, LANES:2 * LANES] = a_side * tsi
        if side_work is not None:
            side_work(g, n_iter)
        return carry

    n_iter = nch // (GROUP * STAGE_UNROLL)
    lax.fori_loop(0, n_iter, body, 0)


def _cmul(a, b):
    ar, ai = a[:, 0:LANES], a[:, LANES:2 * LANES]
    br, bi = b[:, 0:LANES], b[:, LANES:2 * LANES]
    return jnp.concatenate([ar * br - ai * bi, ar * bi + ai * br], axis=1)


def _fft_mid(ab_ref, side_ref, get_h, get_h_side, g2, g2i, nrows):
    chunk = min(MID_ROWS, nrows)

    def body(r, carry):
        rows = pl.ds(pl.multiple_of(r * chunk, chunk), chunk)
        spec = _bdot(ab_ref[rows, :], g2)
        ab_ref[rows, :] = _bdot(_cmul(spec, get_h(rows)), g2i)
        return carry

    lax.fori_loop(0, nrows // chunk, body, 0)
    side_ref[...] = _bdot(_cmul(_bdot(side_ref[...], g2), get_h_side()), g2i)


def _fft_stage_a_inv(ab_ref, side_ref, nch, minv, tw_ref, s_ref, n1):
    h1 = n1 // 2
    twr = tw_ref[0:h1, 0:LANES]
    twi = tw_ref[0:h1, LANES:2 * LANES]
    tsr = tw_ref[h1:h1 + 1, 0:LANES]
    tsi = tw_ref[h1:h1 + 1, LANES:2 * LANES]
    first_row = lax.broadcasted_iota(jnp.int32, (SIDE_PAD, GROUP * LANES), 0) == 0

    def body(g, carry):
        for u in range(STAGE_UNROLL):
            c0 = (g * STAGE_UNROLL + u) * GROUP
            blk = ab_ref[pl.ds(pl.multiple_of(c0 * h1, GROUP * h1), GROUP * h1), :]
            re, im = [], []
            for i in range(GROUP):
                br = blk[i * h1:(i + 1) * h1, 0:LANES]
                bi = blk[i * h1:(i + 1) * h1, LANES:2 * LANES]
                re.append(br * twr + bi * twi)
                im.append(bi * twr - br * twi)
            side = side_ref[pl.ds(pl.multiple_of(c0, GROUP), GROUP), :]
            side_re = side[:, 0:LANES] * tsr + side[:, LANES:2 * LANES] * tsi
            side_row = jnp.concatenate([side_re[i:i + 1, :] for i in range(GROUP)], axis=1)
            side_blk = jnp.where(first_row, jnp.broadcast_to(side_row, first_row.shape), 0.0)
            rhs = jnp.concatenate([jnp.concatenate(re, axis=1), jnp.concatenate(im, axis=1), side_blk], axis=0)
            y = jnp.dot(minv, rhs.astype(BF16), preferred_element_type=F32)
            for i in range(GROUP):
                s_ref[pl.ds(c0 + i, h1, stride=S_PITCH), :] = y[:, LANES * i:LANES * (i + 1)]
        return carry

    lax.fori_loop(0, nch // (GROUP * STAGE_UNROLL), body, 0)


def _filter_spectrum_kernel(hf_ref, hb_ref, m1_ref, g2_ref, tw_ref, o_ref, oside_ref,
                            s_ref, af_ref, ab_ref, afs_ref, abs_ref, *, n1, inv_n):
    nch = hf_ref.shape[1]
    h1 = n1 // 2
    m1, g2 = m1_ref[...], g2_ref[...]
    _slab_transpose_in(lambda i: hf_ref[_slab_rows(i), :], [s_ref], h1)
    _fft_stage_a(s_ref, nch, m1, tw_ref, af_ref, afs_ref, n1)
    _slab_transpose_in(lambda i: hb_ref[_slab_rows(i), :], [s_ref], h1)
    _fft_stage_a(s_ref, nch, m1, tw_ref, ab_ref, abs_ref, n1)

    def two_sided(fwd, bwd):
        sf = _bdot(fwd, g2)
        sb = _bdot(bwd, g2)
        return jnp.concatenate([sf[:, 0:LANES] + sb[:, 0:LANES], sf[:, LANES:] - sb[:, LANES:]], axis=1) * inv_n

    chunk = min(MID_ROWS, nch * h1)

    def body(r, carry):
        rows = pl.ds(pl.multiple_of(r * chunk, chunk), chunk)
        o_ref[rows, :] = two_sided(af_ref[rows, :], ab_ref[rows, :])
        return carry

    lax.fori_loop(0, (nch * h1) // chunk, body, 0)
    oside_ref[...] = two_sided(afs_ref[...], abs_ref[...])


def _filter_spectrum(filt, L, d_hyena, tabs):
    h1 = tabs["H1"]
    order = filt.shape[1] // (2 * d_hyena)
    cb = LANES
    nblk = d_hyena // cb
    full = lambda a: pl.BlockSpec(a.shape, lambda o, c: (0,) * a.ndim)
    main = pltpu.VMEM((cb * h1, 2 * LANES), F32)
    side = pltpu.VMEM((cb, 2 * LANES), F32)
    return pl.pallas_call(
        functools.partial(_filter_spectrum_kernel, n1=tabs["N1"], inv_n=1.0 / (2 * L)),
        grid=(order, nblk),
        in_specs=[pl.BlockSpec((L, cb), lambda o, c: (0, o * 2 * nblk + c)),
                  pl.BlockSpec((L, cb), lambda o, c: (0, o * 2 * nblk + nblk + c)),
                  full(tabs["m1"]), full(tabs["g2"]), full(tabs["tw"])],
        out_specs=[pl.BlockSpec((cb * h1, 2 * LANES), lambda o, c: (o * nblk + c, 0)),
                   pl.BlockSpec((cb, 2 * LANES), lambda o, c: (o * nblk + c, 0))],
        out_shape=[jax.ShapeDtypeStruct((order * d_hyena * h1, 2 * LANES), F32),
                   jax.ShapeDtypeStruct((order * d_hyena, 2 * LANES), F32)],
        scratch_shapes=[pltpu.VMEM((h1 * S_PITCH, LANES), F32), main, main, side, side],
        compiler_params=_params("parallel", "parallel"),
        name="filter_spectrum",
    )(filt, filt, tabs["m1"], tabs["g2"], tabs["tw"])


def _hyena_kernel(v_ref, x1_ref, x2_ref, h_ref, hside_ref, d_ref, m1_ref, minv_ref, g2_ref, g2i_ref, tw_ref,
                  o_ref, s_ref, sv_ref, sx_ref, sz_ref, ab_ref, side_ref, *, n1):
    nch = v_ref.shape[2]
    h1 = n1 // 2
    m1, minv, g2, g2i = m1_ref[...], minv_ref[...], g2_ref[...], g2i_ref[...]

    def slab_of(p_ref):
        return lambda i: p_ref[0, _slab_rows(i), :].astype(F32)

    def gate_transposer(p_ref):
        def work(g, n_iter):
            per_iter = h1 // n_iter
            for u in range(per_iter):
                _slab_transpose_one(slab_of(p_ref), [sx_ref], g * per_iter + u)
        return work

    def long_conv(order, gate_ref, skip_ref, keep_ref=None):
        _fft_stage_a(s_ref, nch, m1, tw_ref, ab_ref, side_ref, n1, side_work=gate_transposer(gate_ref))
        _fft_mid(ab_ref, side_ref, lambda rows: h_ref[order, rows, :], lambda: hside_ref[order], g2, g2i, nch * h1)
        _fft_stage_a_inv(ab_ref, side_ref, nch, minv, tw_ref, s_ref, n1)
        d_slab = jnp.broadcast_to(d_ref[:, order:order + 1], (nch, LANES))

        def gate(rows):
            val = sx_ref[rows, :] * (s_ref[rows, :] + d_slab * skip_ref[rows, :])
            s_ref[rows, :] = val
            if keep_ref is not None:
                keep_ref[rows, :] = val

        _slab_map(gate, h1)

    _slab_transpose_in(slab_of(v_ref), [s_ref, sv_ref], h1)
    long_conv(0, x1_ref, sv_ref, keep_ref=sz_ref)
    long_conv(1, x2_ref, sz_ref)

    def put(i, out):
        o_ref[0, _slab_rows(i), :] = out.astype(o_ref.dtype)

    _slab_transpose_out(s_ref, put, h1)


def _hyena(proj, spec, spec_side, skip, tabs):
    B, L, _ = proj.shape
    D = skip.shape[1]
    n1, h1 = tabs["N1"], tabs["H1"]
    cb = LANES
    nblk = D // cb
    spec = spec.reshape(2, D * h1, 2 * LANES)
    spec_side = spec_side.reshape(2, D, 2 * LANES)
    col = lambda part: pl.BlockSpec((1, L, cb), lambda c, b: (b, 0, part * nblk + c))
    full = lambda a: pl.BlockSpec(a.shape, lambda c, b: (0,) * a.ndim)
    slabs = pltpu.VMEM((h1 * S_PITCH, LANES), F32)
    return pl.pallas_call(
        functools.partial(_hyena_kernel, n1=n1),
        grid=(nblk, B),
        in_specs=[col(0), col(1), col(2),
                  pl.BlockSpec((2, cb * h1, 2 * LANES), lambda c, b: (0, c, 0), pipeline_mode=pl.Buffered(1)),
                  pl.BlockSpec((2, cb, 2 * LANES), lambda c, b: (0, c, 0)),
                  pl.BlockSpec((cb, 2), lambda c, b: (c, 0)),
                  full(tabs["m1"]), full(tabs["minv"]), full(tabs["g2"]), full(tabs["g2i"]), full(tabs["tw"])],
        out_specs=pl.BlockSpec((1, L, cb), lambda c, b: (b, 0, c)),
        out_shape=jax.ShapeDtypeStruct((B, L, D), BF16),
        scratch_shapes=[slabs, slabs, slabs, slabs, pltpu.VMEM((cb * h1, 2 * LANES), F32),
                        pltpu.VMEM((cb, 2 * LANES), F32)],
        compiler_params=_params("parallel", "arbitrary"),
        name="hyena_fftconv",
    )(proj, proj, proj, spec, spec_side, skip.astype(F32).T,
      tabs["m1"], tabs["minv"], tabs["g2"], tabs["g2i"], tabs["tw"])


ML_CHUNK = 256
ML_CHUNKS_PER_STEP = 1


def _log_sigmoid(x):
    return jnp.minimum(x, 0.0) - jnp.log1p(jnp.exp(-jnp.abs(x)))


def _mlstm_direction(q, k, v, i_row, cum_row, logf_row, c_ref, m_ref, mask, scale):
    c, dh = q.shape
    g_row = i_row - cum_row
    f_col_rep = jnp.broadcast_to(cum_row, (LANES, c)).T
    m_prev = m_ref[0:1, 0:1]
    e = jnp.where(mask, g_row, -jnp.inf)
    a = jnp.maximum(jnp.max(e, axis=1, keepdims=True), m_prev)
    w = jnp.exp(e - a)
    sc = lax.dot_general(q.astype(BF16), k.astype(BF16), (((1,), (1,)), ((), ())), preferred_element_type=F32)
    p = sc * (w * scale)
    vaug = jnp.concatenate([v, jnp.ones_like(v)], axis=1)
    inter = jnp.exp(m_prev - a)
    tot = _bdot(p, vaug) + inter * _bdot(q, c_ref[...])
    num, den = tot[:, 0:dh], tot[:, dh:2 * dh]
    m_t = f_col_rep + a
    h = num / jnp.maximum(jnp.abs(den), jnp.exp(-m_t))
    a_end = jnp.maximum(jnp.max(g_row, axis=1, keepdims=True), m_prev)
    f_end = jnp.sum(logf_row, axis=1, keepdims=True)
    we = jnp.exp(g_row - a_end) * scale
    kw = k.astype(F32).T * we
    c_ref[...] = jnp.exp(m_prev - a_end) * c_ref[...] + _bdot(kw, vaug)
    m_ref[...] = jnp.broadcast_to(f_end + a_end, m_ref.shape)
    return h


def _mlstm_kernel(qf_ref, kf_ref, vf_ref, gf_ref, qb_ref, kb_ref, vb_ref, gb_ref, bias_ref,
                  hf_ref, hb_ref, c_ref, m_ref, *, heads, scale, chunk):
    @pl.when(pl.program_id(1) == 0)
    def _():
        c_ref[...] = jnp.zeros_like(c_ref)
        m_ref[...] = jnp.zeros_like(m_ref)

    c = chunk
    nsub = qf_ref.shape[1] // c
    dh = qf_ref.shape[2] // heads
    r = lax.broadcasted_iota(jnp.int32, (c, c), 0)
    s = lax.broadcasted_iota(jnp.int32, (c, c), 1)
    mask_f = s <= r
    mask_b = s >= r

    def gate_rows(g, mask_t):
        g = g + bias_ref[...]
        logf = _log_sigmoid(g)
        tri = mask_t.astype(BF16)
        cum = sum(jnp.dot(part, tri, preferred_element_type=F32) for part in _split3(logf))
        return g, logf, cum

    for u in range(nsub):
        rf = slice(u * c, (u + 1) * c)
        rb = slice((nsub - 1 - u) * c, (nsub - u) * c)
        g_f, logf_f, cum_f = gate_rows(gf_ref[0, :, rf], mask_b)
        g_b, logf_b, cum_b = gate_rows(gb_ref[0, :, rb], mask_f)
        for h in range(heads):
            sl = slice(h * dh, (h + 1) * dh)
            fi, ff, bi, bf = h, heads + h, 2 * heads + h, 3 * heads + h
            hf_ref[0, rf, sl] = _mlstm_direction(
                qf_ref[0, rf, sl], kf_ref[0, rf, sl], vf_ref[0, rf, sl], g_f[fi:fi + 1], cum_f[ff:ff + 1],
                logf_f[ff:ff + 1], c_ref.at[h], m_ref.at[h], mask_f, scale).astype(hf_ref.dtype)
            hb_ref[0, rb, sl] = _mlstm_direction(
                qb_ref[0, rb, sl], kb_ref[0, rb, sl], vb_ref[0, rb, sl], g_b[bi:bi + 1], cum_b[bf:bf + 1],
                logf_b[bf:bf + 1], c_ref.at[heads + h], m_ref.at[heads + h], mask_b, scale).astype(hb_ref.dtype)


def _mlstm(qk, proj, v_col0, gates, gate_b, d_ml):
    B, L, _ = qk.shape
    H = MLSTM_HEADS
    dh = d_ml // H
    c = min(ML_CHUNK, L)
    rows = min(ML_CHUNKS_PER_STEP * c, L)
    nb = L // rows
    g = gates.transpose(0, 2, 1)
    bias = jnp.broadcast_to(gate_b.astype(F32)[:, None], (4 * H, c))
    vb0 = v_col0 // d_ml
    specs = []
    for pos in (lambda ci: ci, lambda ci: nb - 1 - ci):
        specs += [pl.BlockSpec((1, rows, d_ml), lambda bi, ci, pos=pos: (bi, pos(ci), 0)),
                  pl.BlockSpec((1, rows, d_ml), lambda bi, ci, pos=pos: (bi, pos(ci), 1)),
                  pl.BlockSpec((1, rows, d_ml), lambda bi, ci, pos=pos: (bi, pos(ci), vb0)),
                  pl.BlockSpec((1, 4 * H, rows), lambda bi, ci, pos=pos: (bi, 0, pos(ci)))]
    specs.append(pl.BlockSpec((4 * H, c), lambda bi, ci: (0, 0)))
    out_shape = jax.ShapeDtypeStruct((B, L, d_ml), BF16)
    return pl.pallas_call(
        functools.partial(_mlstm_kernel, heads=H, scale=dh ** -0.5, chunk=c),
        grid=(B, nb),
        in_specs=specs,
        out_specs=[pl.BlockSpec((1, rows, d_ml), lambda bi, ci: (bi, ci, 0)),
                   pl.BlockSpec((1, rows, d_ml), lambda bi, ci: (bi, nb - 1 - ci, 0))],
        out_shape=[out_shape, out_shape],
        scratch_shapes=[pltpu.VMEM((2 * H, dh, 2 * dh), F32), pltpu.VMEM((2 * H, 1, LANES), F32)],
        compiler_params=_params("parallel", "arbitrary"),
        name="mlstm_scan",
    )(qk, qk, proj, g, qk, qk, proj, g, bias)


def _group_rms(y, gain, bd, group):
    ss = jnp.dot((y * y).astype(BF16), bd, preferred_element_type=F32)
    return y * lax.rsqrt(ss * (1.0 / group) + EPS) * gain


def _block_diag_ones(n, group):
    idx = np.arange(n) // group
    return jnp.asarray((idx[:, None] == idx[None, :]).astype(np.float32)).astype(BF16)


def _mix_xattn_kernel(z_ref, hf_ref, hb_ref, o_ref, x_ref, mem_ref, ghy_ref, gml_ref, why_ref, wml_ref, bdh_ref,
                      bdm_ref, gx_ref, gm_ref, wq_ref, wkv_ref, wo_ref, out_ref, kv_ref,
                      *, hy_group, ml_group, heads, scale):
    @pl.when(pl.program_id(1) == 0)
    def _():
        kv_ref[...] = _bdot(_rms(mem_ref[0], gm_ref[...]), wkv_ref[...]).astype(kv_ref.dtype)

    y_hy = _group_rms(z_ref[0].astype(F32), ghy_ref[...], bdh_ref[...], hy_group)
    h_sum = hf_ref[0].astype(F32) + hb_ref[0].astype(F32)
    y_ml = _group_rms(jax.nn.sigmoid(o_ref[0].astype(F32)) * h_sum, gml_ref[...], bdm_ref[...], ml_group)
    h1 = x_ref[0] + _bdot(y_hy, why_ref[...]) + _bdot(y_ml, wml_ref[...])

    D = h1.shape[1]
    dh = D // heads
    q = _bdot(_rms(h1, gx_ref[...]), wq_ref[...])
    outs = []
    for hd in range(heads):
        qh = q[:, hd * dh:(hd + 1) * dh]
        kh = kv_ref[:, hd * dh:(hd + 1) * dh]
        vh = kv_ref[:, D + hd * dh:D + (hd + 1) * dh]
        s = lax.dot_general(qh.astype(BF16), kh, (((1,), (1,)), ((), ())), preferred_element_type=F32) * scale
        e = jnp.exp(s - jnp.max(s, axis=-1, keepdims=True))
        p = e / jnp.sum(e, axis=-1, keepdims=True)
        outs.append(_bdot(p, vh))
    out_ref[0] = h1 + _bdot(jnp.concatenate(outs, axis=1), wo_ref[...])


def _mix_xattn(z, hf, hb, proj, o_col0, x, mem, g_hy, g_ml, w_out, g_x, g_mem, wq, wk, wv, wo, tm=512):
    B, L, d_hy = z.shape
    d_ml = hf.shape[2]
    D = x.shape[2]
    nm = mem.shape[1]
    ob = o_col0 // d_ml
    row = lambda n: pl.BlockSpec((1, tm, n), lambda bi, i: (bi, i, 0))
    const = lambda a: pl.BlockSpec(a.shape, lambda bi, i: (0,) * a.ndim, pipeline_mode=pl.Buffered(1))
    hy_group, ml_group = d_hy // HYENA_GROUPS, d_ml // MLSTM_HEADS
    consts = [g_hy.reshape(1, -1), g_ml.reshape(1, -1), w_out[:d_hy].astype(BF16), w_out[d_hy:].astype(BF16),
              _block_diag_ones(d_hy, hy_group), _block_diag_ones(d_ml, ml_group), g_x.reshape(1, D),
              g_mem.reshape(1, D), wq.astype(BF16), jnp.concatenate([wk, wv], axis=1).astype(BF16), wo.astype(BF16)]
    return pl.pallas_call(
        functools.partial(_mix_xattn_kernel, hy_group=hy_group, ml_group=ml_group, heads=XATTN_HEADS,
                          scale=(D // XATTN_HEADS) ** -0.5),
        grid=(B, L // tm),
        in_specs=[row(d_hy), row(d_ml), row(d_ml), pl.BlockSpec((1, tm, d_ml), lambda bi, i: (bi, i, ob)), row(D),
                  pl.BlockSpec((1, nm, D), lambda bi, i: (bi, 0, 0))] + [const(a) for a in consts],
        out_specs=row(D),
        out_shape=jax.ShapeDtypeStruct((B, L, D), F32),
        scratch_shapes=[pltpu.VMEM((nm, 2 * D), BF16)],
        compiler_params=_params("parallel", "arbitrary"),
        name="mix_xattn",
    )(z, hf, hb, proj, x, mem, *consts)


MLP_CHUNK = 1024


def _mlp_kernel(h_ref, g_ref, w1_ref, w2_ref, gf_ref, out_ref, xn_ref, *, final_norm):
    j = pl.program_id(1)

    @pl.when(j == 0)
    def _():
        x = h_ref[...]
        xn_ref[...] = _rms(x, g_ref[...]).astype(BF16)
        out_ref[...] = x

    for c in range(w1_ref.shape[1] // MLP_CHUNK):
        cols = slice(c * MLP_CHUNK, (c + 1) * MLP_CHUNK)
        a = jnp.maximum(jnp.dot(xn_ref[...], w1_ref[:, cols], preferred_element_type=F32), 0.0)
        out_ref[...] += _bdot(a * a, w2_ref[cols, :])

    if final_norm:
        @pl.when(j == pl.num_programs(1) - 1)
        def _():
            out_ref[...] = _rms(out_ref[...], gf_ref[...])


def _mlp(h, g, w1, w2, gf, final_norm, tm=1024, tf=2048):
    M, D = h.shape
    dff = w1.shape[1]
    return pl.pallas_call(
        functools.partial(_mlp_kernel, final_norm=final_norm),
        grid=(M // tm, dff // tf),
        in_specs=[pl.BlockSpec((tm, D), lambda i, j: (i, 0)), pl.BlockSpec((1, D), lambda i, j: (0, 0)),
                  pl.BlockSpec((D, tf), lambda i, j: (0, j)), pl.BlockSpec((tf, D), lambda i, j: (j, 0)),
                  pl.BlockSpec((1, D), lambda i, j: (0, 0))],
        out_specs=pl.BlockSpec((tm, D), lambda i, j: (i, 0)),
        out_shape=jax.ShapeDtypeStruct((M, D), F32),
        scratch_shapes=[pltpu.VMEM((tm, D), BF16)],
        compiler_params=_params("parallel", "arbitrary"),
        name="mlp",
    )(h, g.reshape(1, D), w1.astype(BF16), w2.astype(BF16), gf.reshape(1, D))


def _pad_cols(w, n):
    return jnp.pad(w, ((0, 0), (0, n - w.shape[1])))


def _layer(h, mem, p, l):
    B, L, D = h.shape
    M = B * L
    d_hy = p["hy_norm_g"].shape[1]
    d_ml = p["ml_norm_g"].shape[1]
    hy_cols = 3 * d_hy
    qk_cols = 2 * d_ml
    n_gate = 4 * MLSTM_HEADS
    w_in = p["w_in"][l]
    vo0 = hy_cols + qk_cols
    g0 = vo0 + 2 * d_ml
    v_col0 = hy_cols
    o_col0 = hy_cols + d_ml
    n_in = hy_cols + 2 * d_ml
    proj, qk, gates = _in_proj(h.reshape(M, D), p["norm_mix_g"][l], w_in[:, :vo0].astype(BF16),
                               w_in[:, vo0:g0].astype(BF16),
                               _pad_cols(w_in[:, g0:g0 + n_gate], LANES).astype(BF16),
                               jnp.concatenate([p["hy_conv_w"][l], p["ml_conv_w"][l]], axis=1),
                               jnp.concatenate([p["hy_conv_b"][l], p["ml_conv_b"][l]]),
                               n_hy=hy_cols, seq_len=L, tm=min(1024, L))
    proj = proj.reshape(B, L, n_in)
    qk = qk.reshape(B, L, qk_cols)
    gates = gates[:, :n_gate].reshape(B, L, n_gate)

    tabs = _fft_tables(L)
    filt = _hyena_filters(L, p["hy_filt_w1"][l], p["hy_filt_b1"][l], p["hy_filt_freq1"][l], p["hy_filt_w2"][l],
                          p["hy_filt_b2"][l], p["hy_filt_freq2"][l], p["hy_filt_w3"][l], d_hy,
                          tl=min(512, L))
    spec, spec_side = _filter_spectrum(filt, L, d_hy, tabs)
    z_hy = _hyena(proj, spec, spec_side, p["hy_skip"][l], tabs)

    h_f, h_b = _mlstm(qk, proj, v_col0, gates, p["ml_gate_b"][l], d_ml)

    return _mix_xattn(z_hy, h_f, h_b, proj, o_col0, h, mem, p["hy_norm_g"][l], p["ml_norm_g"][l], p["w_out"][l],
                      p["norm_x_g"][l], p["norm_mem_g"][l], p["xa_wq"][l], p["xa_wk"][l], p["xa_wv"][l],
                      p["xa_wo"][l], tm=min(1024, L))


def kernel(x, mem, norm_mix_g, w_in, hy_conv_w, hy_conv_b, hy_filt_w1, hy_filt_b1, hy_filt_freq1, hy_filt_w2,
           hy_filt_b2, hy_filt_freq2, hy_filt_w3, hy_skip, hy_norm_g, ml_conv_w, ml_conv_b, ml_gate_b, ml_norm_g,
           w_out, norm_x_g, norm_mem_g, xa_wq, xa_wk, xa_wv, xa_wo, norm_ff_g, ff_w1, ff_w2, final_norm_g):
    p = dict(norm_mix_g=norm_mix_g, w_in=w_in, hy_conv_w=hy_conv_w, hy_conv_b=hy_conv_b, hy_filt_w1=hy_filt_w1,
             hy_filt_b1=hy_filt_b1, hy_filt_freq1=hy_filt_freq1, hy_filt_w2=hy_filt_w2, hy_filt_b2=hy_filt_b2,
             hy_filt_freq2=hy_filt_freq2, hy_filt_w3=hy_filt_w3, hy_skip=hy_skip, hy_norm_g=hy_norm_g,
             ml_conv_w=ml_conv_w, ml_conv_b=ml_conv_b, ml_gate_b=ml_gate_b, ml_norm_g=ml_norm_g, w_out=w_out,
             norm_x_g=norm_x_g, norm_mem_g=norm_mem_g, xa_wq=xa_wq, xa_wk=xa_wk, xa_wv=xa_wv, xa_wo=xa_wo)
    B, L, D = x.shape
    depth = w_in.shape[0]
    h = x
    for l in range(depth):
        h = _layer(h, mem, p, l)
        h = _mlp(h.reshape(B * L, D), norm_ff_g[l], ff_w1[l], ff_w2[l], final_norm_g, final_norm=l == depth - 1,
                 tm=min(1024, B * L)).reshape(B, L, D)
    return h
```

```python
import functools
import math

import numpy as np
import jax
import jax.numpy as jnp
from jax import lax
from jax.experimental import pallas as pl
from jax.experimental.pallas import tpu as pltpu

F32 = jnp.float32
BF16 = jnp.bfloat16

EPS = 1e-6
HYENA_GROUPS = 8
MLSTM_HEADS = 4
XATTN_HEADS = 4
FILTER_BANDS = 16
DECAY_TARGET = 1e-2
SHORT_DECAY_PCT = 0.3
LONG_DECAY_PCT = 1.5

LANES = 128
VMEM_LIMIT = 56 * 1024 * 1024


def _params(*sem):
    return pltpu.CompilerParams(dimension_semantics=sem, vmem_limit_bytes=VMEM_LIMIT)


def _bdot(a, b):
    return jnp.dot(a.astype(BF16), b.astype(BF16), preferred_element_type=F32)


def _split3(x):
    hi = x.astype(BF16)
    r = x - hi.astype(F32)
    mid = r.astype(BF16)
    lo = (r - mid.astype(F32)).astype(BF16)
    return hi, mid, lo


IN_CHUNK = 512
HALO = 8


def _rms(x, g):
    return x * lax.rsqrt(jnp.mean(x * x, axis=-1, keepdims=True) + EPS) * g


def _in_proj_kernel(x_ref, xb_ref, xa_ref, g_ref, wc_ref, wp_ref, wg_ref, cw_ref, cb_ref,
                    proj_ref, qk_ref, og_ref, xn_ref, *, tiles_per_seq, n_hy):
    tm = x_ref.shape[0]
    g = g_ref[...]
    x = x_ref[...]
    xn_ref[...] = (x * g).astype(BF16)
    r = lax.rsqrt(jnp.mean(x * x, axis=-1, keepdims=True) + EPS)
    og_ref[...] = jnp.dot(xn_ref[...], wg_ref[...], preferred_element_type=F32) * r
    for c in range(wp_ref.shape[1] // IN_CHUNK):
        cols = slice(c * IN_CHUNK, (c + 1) * IN_CHUNK)
        out_cols = slice(n_hy + c * IN_CHUNK, n_hy + (c + 1) * IN_CHUNK)
        proj_ref[:, out_cols] = (jnp.dot(xn_ref[...], wp_ref[:, cols], preferred_element_type=F32) * r
                                 ).astype(proj_ref.dtype)

    halo = _rms(jnp.concatenate([xb_ref[...], xa_ref[...]], axis=0), g).astype(BF16)
    pos = pl.program_id(0) % tiles_per_seq
    row = lax.broadcasted_iota(jnp.int32, (tm, IN_CHUNK), 0)
    for c in range(wc_ref.shape[1] // IN_CHUNK):
        cols = slice(c * IN_CHUNK, (c + 1) * IN_CHUNK)
        u = jnp.dot(xn_ref[...], wc_ref[:, cols], preferred_element_type=F32) * r
        uh = jnp.dot(halo, wc_ref[:, cols], preferred_element_type=F32)
        before = jnp.where(pos == 0, 0.0, uh[HALO - 1:HALO, :])
        after = jnp.where(pos == tiles_per_seq - 1, 0.0, uh[HALO:HALO + 1, :])
        prev = jnp.where(row == 0, before, pltpu.roll(u, 1, 0))
        nxt = jnp.where(row == tm - 1, after, pltpu.roll(u, tm - 1, 0))
        y = prev * cw_ref[0:1, cols] + u * cw_ref[1:2, cols] + nxt * cw_ref[2:3, cols] + cb_ref[:, cols]
        if c * IN_CHUNK < n_hy:
            proj_ref[:, cols] = y.astype(proj_ref.dtype)
        else:
            qk_cols = slice(c * IN_CHUNK - n_hy, (c + 1) * IN_CHUNK - n_hy)
            qk_ref[:, qk_cols] = (y * jax.nn.sigmoid(y)).astype(qk_ref.dtype)


def _in_proj(x, g, w_conv, w_plain, w_gate, conv_w, conv_b, n_hy, seq_len, tm):
    M, K = x.shape
    nc, npl, ng = w_conv.shape[1], w_plain.shape[1], w_gate.shape[1]
    hb = tm // HALO
    const = lambda a: pl.BlockSpec(a.shape, lambda i: (0,) * a.ndim, pipeline_mode=pl.Buffered(1))
    row = lambda n: pl.BlockSpec((tm, n), lambda i: (i, 0))
    g = g.reshape(1, K)
    conv_b = conv_b.reshape(1, nc)
    return pl.pallas_call(
        functools.partial(_in_proj_kernel, tiles_per_seq=seq_len // tm, n_hy=n_hy),
        grid=(M // tm,),
        in_specs=[row(K),
                  pl.BlockSpec((HALO, K), lambda i: (jnp.maximum(i * hb - 1, 0), 0)),
                  pl.BlockSpec((HALO, K), lambda i: (jnp.minimum((i + 1) * hb, M // HALO - 1), 0)),
                  const(g), const(w_conv), const(w_plain), const(w_gate), const(conv_w), const(conv_b)],
        out_specs=[row(n_hy + npl), row(nc - n_hy), row(ng)],
        out_shape=[jax.ShapeDtypeStruct((M, n_hy + npl), BF16), jax.ShapeDtypeStruct((M, nc - n_hy), BF16),
                   jax.ShapeDtypeStruct((M, ng), F32)],
        scratch_shapes=[pltpu.VMEM((tm, K), BF16)],
        compiler_params=_params("parallel"),
        name="in_proj",
    )(x, x, x, g, w_conv, w_plain, w_gate, conv_w, conv_b)


def _filter_kernel(feat_ref, w1_ref, b1_ref, f1_ref, w2_ref, b2_ref, f2_ref, w3h_ref, w3l_ref, dec_ref, dir_ref,
                   o_ref, *, L):
    tl = o_ref.shape[0]
    hp = lax.Precision.HIGHEST
    pos = (lax.broadcasted_iota(jnp.int32, (tl, LANES), 0) + pl.program_id(0) * tl).astype(F32)
    lane = lax.broadcasted_iota(jnp.int32, (tl, LANES), 1)
    t = pos * (1.0 / (L - 1))
    arg = feat_ref[...] * (pos * (2.0 * math.pi / L))
    z = jnp.where(lane < FILTER_BANDS, jnp.cos(arg),
                  jnp.where(lane < 2 * FILTER_BANDS, -jnp.sin(arg),
                            jnp.where(lane == 2 * FILTER_BANDS, t, 0.0)))
    hid = jnp.sin(f1_ref[...] * (jnp.dot(z, w1_ref[...], precision=hp, preferred_element_type=F32) + b1_ref[...]))
    hid = jnp.sin(f2_ref[...] * (jnp.dot(hid, w2_ref[...], precision=hp, preferred_element_type=F32) + b2_ref[...]))
    hid_hi = hid.astype(BF16)
    hid_lo = (hid - hid_hi.astype(F32)).astype(BF16)
    filt = (jnp.dot(hid_hi, w3h_ref[...], preferred_element_type=F32)
            + jnp.dot(hid_hi, w3l_ref[...], preferred_element_type=F32)
            + jnp.dot(hid_lo, w3h_ref[...], preferred_element_type=F32))
    filt = filt * jnp.exp(-t[:, 0:1] * dec_ref[...])
    filt = jnp.where(pos[:, 0:1] == 0.0, filt * dir_ref[...], filt)
    o_ref[...] = filt


def _hyena_filters(L, w1, b1, fr1, w2, b2, fr2, w3, d_hyena, tl=512):
    n_emb, n_hid = w1.shape
    n_out = w3.shape[1]
    bands = jnp.linspace(1e-4, FILTER_BANDS - 1, FILTER_BANDS, dtype=F32)
    feat = jnp.zeros((1, LANES), F32).at[0, :FILTER_BANDS].set(bands).at[0, FILTER_BANDS:2 * FILTER_BANDS].set(bands)
    w1p = jnp.zeros((LANES, n_hid), F32).at[:n_emb - 1].set(w1[1:]).at[n_emb - 1].set(w1[0])
    max_decay = math.log(DECAY_TARGET) / SHORT_DECAY_PCT
    min_decay = math.log(DECAY_TARGET) / LONG_DECAY_PCT
    deltas = jnp.abs(jnp.linspace(min_decay, max_decay, d_hyena, dtype=F32))
    reps = n_out // d_hyena
    dec = jnp.tile(deltas, reps).reshape(1, n_out)
    dirmask = jnp.tile(jnp.concatenate([jnp.ones((d_hyena,), F32), jnp.zeros((d_hyena,), F32)]), reps // 2)
    w3_hi = w3.astype(BF16)
    w3_lo = (w3 - w3_hi.astype(F32)).astype(BF16)
    full = lambda shape: pl.BlockSpec(shape, lambda i: (0,) * len(shape))
    return pl.pallas_call(
        functools.partial(_filter_kernel, L=L),
        grid=(L // tl,),
        in_specs=[full((1, LANES)), full((LANES, n_hid)), full((1, n_hid)), full((1, n_hid)),
                  full((n_hid, n_hid)), full((1, n_hid)), full((1, n_hid)), full((n_hid, n_out)),
                  full((n_hid, n_out)), full((1, n_out)), full((1, n_out))],
        out_specs=pl.BlockSpec((tl, n_out), lambda i: (i, 0)),
        out_shape=jax.ShapeDtypeStruct((L, n_out), F32),
        compiler_params=_params("parallel"),
        name="hyena_filters",
    )(feat, w1p, b1.reshape(1, -1), fr1.reshape(1, -1), w2, b2.reshape(1, -1), fr2.reshape(1, -1), w3_hi, w3_lo,
      dec, dirmask.reshape(1, n_out))


MID_ROWS = 4096
SIDE_PAD = 16


def _fft_tables(L):
    N = 2 * L
    N2 = LANES
    N1 = N // N2
    H1 = N1 // 2
    k1 = np.arange(H1)[:, None]
    n1 = np.arange(H1)[None, :]
    a1 = 2.0 * np.pi * ((k1 * n1) % N1) / N1
    c1, s1 = np.cos(a1), np.sin(a1)
    sign = np.where(np.arange(H1) % 2 == 0, 1.0, -1.0)[None, :]
    pad = np.zeros((SIDE_PAD - 1, H1))
    m1 = np.concatenate([c1, -s1, sign, pad], axis=0)
    wgt = np.where(np.arange(H1) == 0, 1.0, 2.0)[:, None]
    minv = np.concatenate([(wgt * c1).T, (-wgt * s1).T, sign.T, pad.T], axis=1)
    n2 = np.arange(N2)[:, None]
    k2 = np.arange(N2)[None, :]
    a2 = 2.0 * np.pi * ((n2 * k2) % N2) / N2
    cg, sg = np.cos(a2), np.sin(a2)
    g2 = np.block([[cg, -sg], [sg, cg]])
    g2i = np.block([[cg, sg], [-sg, cg]])
    at = 2.0 * np.pi * (np.arange(H1 + 8)[:, None] * np.arange(N2)[None, :]) / N
    tw = np.concatenate([np.cos(at), -np.sin(at)], axis=1)
    as_bf = lambda a: jnp.asarray(a.astype(np.float32)).astype(BF16)
    return dict(m1=as_bf(m1), minv=as_bf(minv), g2=as_bf(g2), g2i=as_bf(g2i),
                tw=jnp.asarray(tw.astype(np.float32)), N1=N1, H1=H1)


S_PITCH = LANES + 8
GROUP = 8
STAGE_UNROLL = 8


SLAB_UNROLL = 4


def _slab_scratch_rows(i):
    return pl.ds(pl.multiple_of(i * S_PITCH, 8), LANES)


def _slab_transpose_one(get_slab, dst_refs, i):
    t = get_slab(i).T
    for dst in dst_refs:
        dst[_slab_scratch_rows(i), :] = t


def _slab_transpose_in(get_slab, dst_refs, h1):
    def body(j, carry):
        for u in range(SLAB_UNROLL):
            _slab_transpose_one(get_slab, dst_refs, j * SLAB_UNROLL + u)
        return carry

    lax.fori_loop(0, h1 // SLAB_UNROLL, body, 0)


def _slab_transpose_out(s_ref, put_slab, h1):
    def body(j, carry):
        for u in range(SLAB_UNROLL):
            i = j * SLAB_UNROLL + u
            put_slab(i, s_ref[_slab_scratch_rows(i), :].T)
        return carry

    lax.fori_loop(0, h1 // SLAB_UNROLL, body, 0)


def _slab_map(fn, h1):
    def body(j, carry):
        for u in range(SLAB_UNROLL):
            fn(_slab_scratch_rows(j * SLAB_UNROLL + u))
        return carry

    lax.fori_loop(0, h1 // SLAB_UNROLL, body, 0)


def _slab_rows(i):
    return pl.ds(pl.multiple_of(i * LANES, LANES), LANES)


def _fft_stage_a(s_ref, nch, m1, tw_ref, ab_ref, side_ref, n1, side_work=None):
    h1 = n1 // 2
    twr = tw_ref[0:h1, 0:LANES]
    twi = tw_ref[0:h1, LANES:2 * LANES]
    tsr = tw_ref[h1:h1 + 1, 0:LANES]
    tsi = tw_ref[h1:h1 + 1, LANES:2 * LANES]

    def body(g, carry):
        for u in range(STAGE_UNROLL):
            c0 = (g * STAGE_UNROLL + u) * GROUP
            rhs = jnp.concatenate([s_ref[pl.ds(c0 + i, h1, stride=S_PITCH), :] for i in range(GROUP)], axis=1)
            res = jnp.dot(m1, rhs.astype(BF16), preferred_element_type=F32)
            for i in range(GROUP):
                ar = res[0:h1, LANES * i:LANES * (i + 1)]
                ai = res[h1:2 * h1, LANES * i:LANES * (i + 1)]
                rows = pl.ds(pl.multiple_of((c0 + i) * h1, h1), h1)
                ab_ref[rows, 0:LANES] = ar * twr - ai * twi
                ab_ref[rows, LANES:2 * LANES] = ar * twi + ai * twr
            a_side = jnp.concatenate([res[2 * h1:2 * h1 + 1, LANES * i:LANES * (i + 1)] for i in range(GROUP)],
                                     axis=0)
            rows = pl.ds(pl.multiple_of(c0, GROUP), GROUP)
            side_ref[rows, 0:LANES] = a_side * tsr
            side_ref[rows, LANES:2 * LANES] = a_side * tsi
        if side_work is not None:
            side_work(g, n_iter)
        return carry

    n_iter = nch // (GROUP * STAGE_UNROLL)
    lax.fori_loop(0, n_iter, body, 0)


def _cmul(a, b):
    ar, ai = a[:, 0:LANES], a[:, LANES:2 * LANES]
    br, bi = b[:, 0:LANES], b[:, LANES:2 * LANES]
    return jnp.concatenate([ar * br - ai * bi, ar * bi + ai * br], axis=1)


def _fft_mid(ab_ref, side_ref, get_h, get_h_side, g2, g2i, nrows):
    chunk = min(MID_ROWS, nrows)

    def body(r, carry):
        rows = pl.ds(pl.multiple_of(r * chunk, chunk), chunk)
        spec = _bdot(ab_ref[rows, :], g2)
        ab_ref[rows, :] = _bdot(_cmul(spec, get_h(rows)), g2i)
        return carry

    lax.fori_loop(0, nrows // chunk, body, 0)
    side_ref[...] = _bdot(_cmul(_bdot(side_ref[...], g2), get_h_side()), g2i)


def _fft_stage_a_inv(ab_ref, side_ref, nch, minv, tw_ref, s_ref, n1, side_work=None):
    h1 = n1 // 2
    twr = tw_ref[0:h1, 0:LANES]
    twi = tw_ref[0:h1, LANES:2 * LANES]
    tsr = tw_ref[h1:h1 + 1, 0:LANES]
    tsi = tw_ref[h1:h1 + 1, LANES:2 * LANES]
    first_row = lax.broadcasted_iota(jnp.int32, (SIDE_PAD, GROUP * LANES), 0) == 0

    def body(g, carry):
        for u in range(STAGE_UNROLL):
            c0 = (g * STAGE_UNROLL + u) * GROUP
            blk = ab_ref[pl.ds(pl.multiple_of(c0 * h1, GROUP * h1), GROUP * h1), :]
            re, im = [], []
            for i in range(GROUP):
                br = blk[i * h1:(i + 1) * h1, 0:LANES]
                bi = blk[i * h1:(i + 1) * h1, LANES:2 * LANES]
                re.append(br * twr + bi * twi)
                im.append(bi * twr - br * twi)
            side = side_ref[pl.ds(pl.multiple_of(c0, GROUP), GROUP), :]
            side_re = side[:, 0:LANES] * tsr + side[:, LANES:2 * LANES] * tsi
            side_row = jnp.concatenate([side_re[i:i + 1, :] for i in range(GROUP)], axis=1)
            side_blk = jnp.where(first_row, jnp.broadcast_to(side_row, first_row.shape), 0.0)
            rhs = jnp.concatenate([jnp.concatenate(re, axis=1), jnp.concatenate(im, axis=1), side_blk], axis=0)
            y = jnp.dot(minv, rhs.astype(BF16), preferred_element_type=F32)
            for i in range(GROUP):
                s_ref[pl.ds(c0 + i, h1, stride=S_PITCH), :] = y[:, LANES * i:LANES * (i + 1)]
        if side_work is not None:
            side_work(g, n_iter)
        return carry

    n_iter = nch // (GROUP * STAGE_UNROLL)
    lax.fori_loop(0, n_iter, body, 0)


def _filter_spectrum_kernel(hf_ref, hb_ref, m1_ref, g2_ref, tw_ref, o_ref, oside_ref,
                            s_ref, af_ref, ab_ref, afs_ref, abs_ref, *, n1, inv_n):
    nch = hf_ref.shape[1]
    h1 = n1 // 2
    m1, g2 = m1_ref[...], g2_ref[...]
    _slab_transpose_in(lambda i: hf_ref[_slab_rows(i), :], [s_ref], h1)
    _fft_stage_a(s_ref, nch, m1, tw_ref, af_ref, afs_ref, n1)
    _slab_transpose_in(lambda i: hb_ref[_slab_rows(i), :], [s_ref], h1)
    _fft_stage_a(s_ref, nch, m1, tw_ref, ab_ref, abs_ref, n1)

    def two_sided(fwd, bwd):
        sf = _bdot(fwd, g2)
        sb = _bdot(bwd, g2)
        return jnp.concatenate([sf[:, 0:LANES] + sb[:, 0:LANES], sf[:, LANES:] - sb[:, LANES:]], axis=1) * inv_n

    chunk = min(MID_ROWS, nch * h1)

    def body(r, carry):
        rows = pl.ds(pl.multiple_of(r * chunk, chunk), chunk)
        o_ref[rows, :] = two_sided(af_ref[rows, :], ab_ref[rows, :])
        return carry

    lax.fori_loop(0, (nch * h1) // chunk, body, 0)
    oside_ref[...] = two_sided(afs_ref[...], abs_ref[...])


def _filter_spectrum(filt, L, d_hyena, tabs):
    h1 = tabs["H1"]
    order = filt.shape[1] // (2 * d_hyena)
    cb = LANES
    nblk = d_hyena // cb
    full = lambda a: pl.BlockSpec(a.shape, lambda o, c: (0,) * a.ndim)
    main = pltpu.VMEM((cb * h1, 2 * LANES), F32)
    side = pltpu.VMEM((cb, 2 * LANES), F32)
    return pl.pallas_call(
        functools.partial(_filter_spectrum_kernel, n1=tabs["N1"], inv_n=1.0 / (2 * L)),
        grid=(order, nblk),
        in_specs=[pl.BlockSpec((L, cb), lambda o, c: (0, o * 2 * nblk + c)),
                  pl.BlockSpec((L, cb), lambda o, c: (0, o * 2 * nblk + nblk + c)),
                  full(tabs["m1"]), full(tabs["g2"]), full(tabs["tw"])],
        out_specs=[pl.BlockSpec((cb * h1, 2 * LANES), lambda o, c: (o * nblk + c, 0)),
                   pl.BlockSpec((cb, 2 * LANES), lambda o, c: (o * nblk + c, 0))],
        out_shape=[jax.ShapeDtypeStruct((order * d_hyena * h1, 2 * LANES), F32),
                   jax.ShapeDtypeStruct((order * d_hyena, 2 * LANES), F32)],
        scratch_shapes=[pltpu.VMEM((h1 * S_PITCH, LANES), F32), main, main, side, side],
        compiler_params=_params("parallel", "parallel"),
        name="filter_spectrum",
    )(filt, filt, tabs["m1"], tabs["g2"], tabs["tw"])


def _hyena_kernel(v_ref, vnext_ref, x1_ref, x2_ref, h_ref, hside_ref, d_ref, m1_ref, minv_ref, g2_ref, g2i_ref,
                  tw_ref, o_ref, s_ref, sv_ref, sn_ref, sx_ref, sz_ref, ab_ref, side_ref, *, n1):
    nch = v_ref.shape[2]
    h1 = n1 // 2
    m1, minv, g2, g2i = m1_ref[...], minv_ref[...], g2_ref[...], g2i_ref[...]

    def slab_of(p_ref):
        return lambda i: p_ref[0, _slab_rows(i), :].astype(F32)

    def transposer(p_ref, dst_ref):
        def work(g, n_iter):
            per_iter = h1 // n_iter
            for u in range(per_iter):
                _slab_transpose_one(slab_of(p_ref), [dst_ref], g * per_iter + u)
        return work

    def long_conv(order, src_ref, gate_ref, skip_ref, keep_ref=None, inv_side_work=None):
        _fft_stage_a(src_ref, nch, m1, tw_ref, ab_ref, side_ref, n1, side_work=transposer(gate_ref, sx_ref))
        _fft_mid(ab_ref, side_ref, lambda rows: h_ref[order, rows, :], lambda: hside_ref[order], g2, g2i, nch * h1)
        _fft_stage_a_inv(ab_ref, side_ref, nch, minv, tw_ref, s_ref, n1, side_work=inv_side_work)
        d_slab = jnp.broadcast_to(d_ref[:, order:order + 1], (nch, LANES))

        def gate(rows):
            val = sx_ref[rows, :] * (s_ref[rows, :] + d_slab * skip_ref[rows, :])
            s_ref[rows, :] = val
            if keep_ref is not None:
                keep_ref[rows, :] = val

        _slab_map(gate, h1)

    first = pl.program_id(1) == 0

    @pl.when(first)
    def _():
        _slab_transpose_in(slab_of(v_ref), [sv_ref], h1)

    @pl.when(jnp.logical_not(first))
    def _():
        def copy(rows):
            sv_ref[rows, :] = sn_ref[rows, :]

        _slab_map(copy, h1)

    long_conv(0, sv_ref, x1_ref, sv_ref, keep_ref=sz_ref, inv_side_work=transposer(vnext_ref, sn_ref))
    long_conv(1, s_ref, x2_ref, sz_ref)

    def put(i, out):
        o_ref[0, _slab_rows(i), :] = out.astype(o_ref.dtype)

    _slab_transpose_out(s_ref, put, h1)


def _hyena(proj, spec, spec_side, skip, tabs):
    B, L, _ = proj.shape
    D = skip.shape[1]
    n1, h1 = tabs["N1"], tabs["H1"]
    cb = LANES
    nblk = D // cb
    spec = spec.reshape(2, D * h1, 2 * LANES)
    spec_side = spec_side.reshape(2, D, 2 * LANES)
    col = lambda part: pl.BlockSpec((1, L, cb), lambda c, b: (b, 0, part * nblk + c))
    full = lambda a: pl.BlockSpec(a.shape, lambda c, b: (0,) * a.ndim)
    slabs = pltpu.VMEM((h1 * S_PITCH, LANES), F32)
    return pl.pallas_call(
        functools.partial(_hyena_kernel, n1=n1),
        grid=(nblk, B),
        in_specs=[col(0),
                  pl.BlockSpec((1, L, cb), lambda c, b: (jnp.minimum(b + 1, B - 1), 0, c)),
                  col(1), col(2),
                  pl.BlockSpec((2, cb * h1, 2 * LANES), lambda c, b: (0, c, 0), pipeline_mode=pl.Buffered(1)),
                  pl.BlockSpec((2, cb, 2 * LANES), lambda c, b: (0, c, 0)),
                  pl.BlockSpec((cb, 2), lambda c, b: (c, 0)),
                  full(tabs["m1"]), full(tabs["minv"]), full(tabs["g2"]), full(tabs["g2i"]), full(tabs["tw"])],
        out_specs=pl.BlockSpec((1, L, cb), lambda c, b: (b, 0, c)),
        out_shape=jax.ShapeDtypeStruct((B, L, D), BF16),
        scratch_shapes=[slabs, slabs, slabs, slabs, slabs, pltpu.VMEM((cb * h1, 2 * LANES), F32),
                        pltpu.VMEM((cb, 2 * LANES), F32)],
        compiler_params=_params("parallel", "arbitrary"),
        name="hyena_fftconv",
    )(proj, proj, proj, proj, spec, spec_side, skip.astype(F32).T,
      tabs["m1"], tabs["minv"], tabs["g2"], tabs["g2i"], tabs["tw"])


ML_CHUNK = 256
ML_CHUNKS_PER_STEP = 1


def _log_sigmoid(x):
    return jnp.minimum(x, 0.0) - jnp.log1p(jnp.exp(-jnp.abs(x)))


def _mlstm_direction(q, k, v, i_row, cum_row, logf_row, c_ref, m_ref, mask, scale):
    c, dh = q.shape
    g_row = i_row - cum_row
    f_col_rep = jnp.broadcast_to(cum_row, (LANES, c)).T
    m_prev = m_ref[0:1, 0:1]
    e = jnp.where(mask, g_row, -jnp.inf)
    a = jnp.maximum(jnp.max(e, axis=1, keepdims=True), m_prev)
    w = jnp.exp(e - a)
    sc = lax.dot_general(q.astype(BF16), k.astype(BF16), (((1,), (1,)), ((), ())), preferred_element_type=F32)
    p = sc * (w * scale)
    vaug = jnp.concatenate([v, jnp.ones_like(v)], axis=1)
    inter = jnp.exp(m_prev - a)
    tot = _bdot(p, vaug) + inter * _bdot(q, c_ref[...])
    num, den = tot[:, 0:dh], tot[:, dh:2 * dh]
    m_t = f_col_rep + a
    h = num / jnp.maximum(jnp.abs(den), jnp.exp(-m_t))
    a_end = jnp.maximum(jnp.max(g_row, axis=1, keepdims=True), m_prev)
    f_end = jnp.sum(logf_row, axis=1, keepdims=True)
    we = jnp.exp(g_row - a_end) * scale
    kw = k.astype(F32).T * we
    c_ref[...] = jnp.exp(m_prev - a_end) * c_ref[...] + _bdot(kw, vaug)
    m_ref[...] = jnp.broadcast_to(f_end + a_end, m_ref.shape)
    return h


def _mlstm_kernel(qf_ref, kf_ref, vf_ref, gf_ref, qb_ref, kb_ref, vb_ref, gb_ref, bias_ref,
                  hf_ref, hb_ref, c_ref, m_ref, *, heads, scale, chunk):
    @pl.when(pl.program_id(1) == 0)
    def _():
        c_ref[...] = jnp.zeros_like(c_ref)
        m_ref[...] = jnp.zeros_like(m_ref)

    c = chunk
    nsub = qf_ref.shape[1] // c
    dh = qf_ref.shape[2] // heads
    r = lax.broadcasted_iota(jnp.int32, (c, c), 0)
    s = lax.broadcasted_iota(jnp.int32, (c, c), 1)
    mask_f = s <= r
    mask_b = s >= r

    def gate_rows(g, mask_t):
        g = g + bias_ref[...]
        logf = _log_sigmoid(g)
        tri = mask_t.astype(BF16)
        cum = sum(jnp.dot(part, tri, preferred_element_type=F32) for part in _split3(logf))
        return g, logf, cum

    for u in range(nsub):
        rf = slice(u * c, (u + 1) * c)
        rb = slice((nsub - 1 - u) * c, (nsub - u) * c)
        g_f, logf_f, cum_f = gate_rows(gf_ref[0, :, rf], mask_b)
        g_b, logf_b, cum_b = gate_rows(gb_ref[0, :, rb], mask_f)
        for h in range(heads):
            sl = slice(h * dh, (h + 1) * dh)
            fi, ff, bi, bf = h, heads + h, 2 * heads + h, 3 * heads + h
            hf_ref[0, rf, sl] = _mlstm_direction(
                qf_ref[0, rf, sl], kf_ref[0, rf, sl], vf_ref[0, rf, sl], g_f[fi:fi + 1], cum_f[ff:ff + 1],
                logf_f[ff:ff + 1], c_ref.at[h], m_ref.at[h], mask_f, scale).astype(hf_ref.dtype)
            hb_ref[0, rb, sl] = _mlstm_direction(
                qb_ref[0, rb, sl], kb_ref[0, rb, sl], vb_ref[0, rb, sl], g_b[bi:bi + 1], cum_b[bf:bf + 1],
                logf_b[bf:bf + 1], c_ref.at[heads + h], m_ref.at[heads + h], mask_b, scale).astype(hb_ref.dtype)


def _mlstm(qk, proj, v_col0, gates, gate_b, d_ml):
    B, L, _ = qk.shape
    H = MLSTM_HEADS
    dh = d_ml // H
    c = min(ML_CHUNK, L)
    rows = min(ML_CHUNKS_PER_STEP * c, L)
    nb = L // rows
    g = gates.transpose(0, 2, 1)
    bias = jnp.broadcast_to(gate_b.astype(F32)[:, None], (4 * H, c))
    vb0 = v_col0 // d_ml
    specs = []
    for pos in (lambda ci: ci, lambda ci: nb - 1 - ci):
        specs += [pl.BlockSpec((1, rows, d_ml), lambda bi, ci, pos=pos: (bi, pos(ci), 0)),
                  pl.BlockSpec((1, rows, d_ml), lambda bi, ci, pos=pos: (bi, pos(ci), 1)),
                  pl.BlockSpec((1, rows, d_ml), lambda bi, ci, pos=pos: (bi, pos(ci), vb0)),
                  pl.BlockSpec((1, 4 * H, rows), lambda bi, ci, pos=pos: (bi, 0, pos(ci)))]
    specs.append(pl.BlockSpec((4 * H, c), lambda bi, ci: (0, 0)))
    out_shape = jax.ShapeDtypeStruct((B, L, d_ml), BF16)
    return pl.pallas_call(
        functools.partial(_mlstm_kernel, heads=H, scale=dh ** -0.5, chunk=c),
        grid=(B, nb),
        in_specs=specs,
        out_specs=[pl.BlockSpec((1, rows, d_ml), lambda bi, ci: (bi, ci, 0)),
                   pl.BlockSpec((1, rows, d_ml), lambda bi, ci: (bi, nb - 1 - ci, 0))],
        out_shape=[out_shape, out_shape],
        scratch_shapes=[pltpu.VMEM((2 * H, dh, 2 * dh), F32), pltpu.VMEM((2 * H, 1, LANES), F32)],
        compiler_params=_params("parallel", "arbitrary"),
        name="mlstm_scan",
    )(qk, qk, proj, g, qk, qk, proj, g, bias)


def _group_rms(y, gain, bd, group):
    ss = jnp.dot((y * y).astype(BF16), bd, preferred_element_type=F32)
    return y * lax.rsqrt(ss * (1.0 / group) + EPS) * gain


def _block_diag_ones(n, group):
    idx = np.arange(n) // group
    return jnp.asarray((idx[:, None] == idx[None, :]).astype(np.float32)).astype(BF16)


def _mix_xattn_kernel(z_ref, hf_ref, hb_ref, o_ref, x_ref, mem_ref, ghy_ref, gml_ref, why_ref, wml_ref, bdh_ref,
                      bdm_ref, gx_ref, gm_ref, wq_ref, wkv_ref, wo_ref, out_ref, kv_ref,
                      *, hy_group, ml_group, heads, scale):
    @pl.when(pl.program_id(1) == 0)
    def _():
        kv_ref[...] = _bdot(_rms(mem_ref[0], gm_ref[...]), wkv_ref[...]).astype(kv_ref.dtype)

    y_hy = _group_rms(z_ref[0].astype(F32), ghy_ref[...], bdh_ref[...], hy_group)
    h_sum = hf_ref[0].astype(F32) + hb_ref[0].astype(F32)
    y_ml = _group_rms(jax.nn.sigmoid(o_ref[0].astype(F32)) * h_sum, gml_ref[...], bdm_ref[...], ml_group)
    h1 = x_ref[0] + _bdot(y_hy, why_ref[...]) + _bdot(y_ml, wml_ref[...])

    D = h1.shape[1]
    dh = D // heads
    q = _bdot(_rms(h1, gx_ref[...]), wq_ref[...])
    outs = []
    for hd in range(heads):
        qh = q[:, hd * dh:(hd + 1) * dh]
        kh = kv_ref[:, hd * dh:(hd + 1) * dh]
        vh = kv_ref[:, D + hd * dh:D + (hd + 1) * dh]
        s = lax.dot_general(qh.astype(BF16), kh, (((1,), (1,)), ((), ())), preferred_element_type=F32) * scale
        e = jnp.exp(s - jnp.max(s, axis=-1, keepdims=True))
        p = e / jnp.sum(e, axis=-1, keepdims=True)
        outs.append(_bdot(p, vh))
    out_ref[0] = h1 + _bdot(jnp.concatenate(outs, axis=1), wo_ref[...])


def _mix_xattn(z, hf, hb, proj, o_col0, x, mem, g_hy, g_ml, w_out, g_x, g_mem, wq, wk, wv, wo, tm=512):
    B, L, d_hy = z.shape
    d_ml = hf.shape[2]
    D = x.shape[2]
    nm = mem.shape[1]
    ob = o_col0 // d_ml
    row = lambda n: pl.BlockSpec((1, tm, n), lambda bi, i: (bi, i, 0))
    const = lambda a: pl.BlockSpec(a.shape, lambda bi, i: (0,) * a.ndim, pipeline_mode=pl.Buffered(1))
    hy_group, ml_group = d_hy // HYENA_GROUPS, d_ml // MLSTM_HEADS
    consts = [g_hy.reshape(1, -1), g_ml.reshape(1, -1), w_out[:d_hy].astype(BF16), w_out[d_hy:].astype(BF16),
              _block_diag_ones(d_hy, hy_group), _block_diag_ones(d_ml, ml_group), g_x.reshape(1, D),
              g_mem.reshape(1, D), wq.astype(BF16), jnp.concatenate([wk, wv], axis=1).astype(BF16), wo.astype(BF16)]
    return pl.pallas_call(
        functools.partial(_mix_xattn_kernel, hy_group=hy_group, ml_group=ml_group, heads=XATTN_HEADS,
                          scale=(D // XATTN_HEADS) ** -0.5),
        grid=(B, L // tm),
        in_specs=[row(d_hy), row(d_ml), row(d_ml), pl.BlockSpec((1, tm, d_ml), lambda bi, i: (bi, i, ob)), row(D),
                  pl.BlockSpec((1, nm, D), lambda bi, i: (bi, 0, 0))] + [const(a) for a in consts],
        out_specs=row(D),
        out_shape=jax.ShapeDtypeStruct((B, L, D), F32),
        scratch_shapes=[pltpu.VMEM((nm, 2 * D), BF16)],
        compiler_params=_params("parallel", "arbitrary"),
        name="mix_xattn",
    )(z, hf, hb, proj, x, mem, *consts)


MLP_CHUNK = 1024


def _mlp_kernel(h_ref, g_ref, w1_ref, w2_ref, gf_ref, out_ref, xn_ref, *, final_norm):
    j = pl.program_id(1)

    @pl.when(j == 0)
    def _():
        x = h_ref[...]
        xn_ref[...] = _rms(x, g_ref[...]).astype(BF16)
        out_ref[...] = x

    for c in range(w1_ref.shape[1] // MLP_CHUNK):
        cols = slice(c * MLP_CHUNK, (c + 1) * MLP_CHUNK)
        a = jnp.maximum(jnp.dot(xn_ref[...], w1_ref[:, cols], preferred_element_type=F32), 0.0)
        out_ref[...] += _bdot(a * a, w2_ref[cols, :])

    if final_norm:
        @pl.when(j == pl.num_programs(1) - 1)
        def _():
            out_ref[...] = _rms(out_ref[...], gf_ref[...])


def _mlp(h, g, w1, w2, gf, final_norm, tm=1024, tf=2048):
    M, D = h.shape
    dff = w1.shape[1]
    return pl.pallas_call(
        functools.partial(_mlp_kernel, final_norm=final_norm),
        grid=(M // tm, dff // tf),
        in_specs=[pl.BlockSpec((tm, D), lambda i, j: (i, 0)), pl.BlockSpec((1, D), lambda i, j: (0, 0)),
                  pl.BlockSpec((D, tf), lambda i, j: (0, j)), pl.BlockSpec((tf, D), lambda i, j: (j, 0)),
                  pl.BlockSpec((1, D), lambda i, j: (0, 0))],
        out_specs=pl.BlockSpec((tm, D), lambda i, j: (i, 0)),
        out_shape=jax.ShapeDtypeStruct((M, D), F32),
        scratch_shapes=[pltpu.VMEM((tm, D), BF16)],
        compiler_params=_params("parallel", "arbitrary"),
        name="mlp",
    )(h, g.reshape(1, D), w1.astype(BF16), w2.astype(BF16), gf.reshape(1, D))


def _pad_cols(w, n):
    return jnp.pad(w, ((0, 0), (0, n - w.shape[1])))


def _layer(h, mem, p, l):
    B, L, D = h.shape
    M = B * L
    d_hy = p["hy_norm_g"].shape[1]
    d_ml = p["ml_norm_g"].shape[1]
    hy_cols = 3 * d_hy
    qk_cols = 2 * d_ml
    n_gate = 4 * MLSTM_HEADS
    w_in = p["w_in"][l]
    vo0 = hy_cols + qk_cols
    g0 = vo0 + 2 * d_ml
    v_col0 = hy_cols
    o_col0 = hy_cols + d_ml
    n_in = hy_cols + 2 * d_ml
    proj, qk, gates = _in_proj(h.reshape(M, D), p["norm_mix_g"][l], w_in[:, :vo0].astype(BF16),
                               w_in[:, vo0:g0].astype(BF16),
                               _pad_cols(w_in[:, g0:g0 + n_gate], LANES).astype(BF16),
                               jnp.concatenate([p["hy_conv_w"][l], p["ml_conv_w"][l]], axis=1),
                               jnp.concatenate([p["hy_conv_b"][l], p["ml_conv_b"][l]]),
                               n_hy=hy_cols, seq_len=L, tm=min(1024, L))
    proj = proj.reshape(B, L, n_in)
    qk = qk.reshape(B, L, qk_cols)
    gates = gates[:, :n_gate].reshape(B, L, n_gate)

    tabs = _fft_tables(L)
    filt = _hyena_filters(L, p["hy_filt_w1"][l], p["hy_filt_b1"][l], p["hy_filt_freq1"][l], p["hy_filt_w2"][l],
                          p["hy_filt_b2"][l], p["hy_filt_freq2"][l], p["hy_filt_w3"][l], d_hy,
                          tl=min(512, L))
    spec, spec_side = _filter_spectrum(filt, L, d_hy, tabs)
    z_hy = _hyena(proj, spec, spec_side, p["hy_skip"][l], tabs)

    h_f, h_b = _mlstm(qk, proj, v_col0, gates, p["ml_gate_b"][l], d_ml)

    return _mix_xattn(z_hy, h_f, h_b, proj, o_col0, h, mem, p["hy_norm_g"][l], p["ml_norm_g"][l], p["w_out"][l],
                      p["norm_x_g"][l], p["norm_mem_g"][l], p["xa_wq"][l], p["xa_wk"][l], p["xa_wv"][l],
                      p["xa_wo"][l], tm=min(1024, L))


def kernel(x, mem, norm_mix_g, w_in, hy_conv_w, hy_conv_b, hy_filt_w1, hy_filt_b1, hy_filt_freq1, hy_filt_w2,
           hy_filt_b2, hy_filt_freq2, hy_filt_w3, hy_skip, hy_norm_g, ml_conv_w, ml_conv_b, ml_gate_b, ml_norm_g,
           w_out, norm_x_g, norm_mem_g, xa_wq, xa_wk, xa_wv, xa_wo, norm_ff_g, ff_w1, ff_w2, final_norm_g):
    p = dict(norm_mix_g=norm_mix_g, w_in=w_in, hy_conv_w=hy_conv_w, hy_conv_b=hy_conv_b, hy_filt_w1=hy_filt_w1,
             hy_filt_b1=hy_filt_b1, hy_filt_freq1=hy_filt_freq1, hy_filt_w2=hy_filt_w2, hy_filt_b2=hy_filt_b2,
             hy_filt_freq2=hy_filt_freq2, hy_filt_w3=hy_filt_w3, hy_skip=hy_skip, hy_norm_g=hy_norm_g,
             ml_conv_w=ml_conv_w, ml_conv_b=ml_conv_b, ml_gate_b=ml_gate_b, ml_norm_g=ml_norm_g, w_out=w_out,
             norm_x_g=norm_x_g, norm_mem_g=norm_mem_g, xa_wq=xa_wq, xa_wk=xa_wk, xa_wv=xa_wv, xa_wo=xa_wo)
    B, L, D = x.shape
    depth = w_in.shape[0]
    h = x
    for l in range(depth):
        h = _layer(h, mem, p, l)
        h = _mlp(h.reshape(B * L, D), norm_ff_g[l], ff_w1[l], ff_w2[l], final_norm_g, final_norm=l == depth - 1,
                 tm=min(1024, B * L)).reshape(B, L, D)
    return h
```

```python
import functools
import math

import numpy as np
import jax
import jax.numpy as jnp
from jax import lax
from jax.experimental import pallas as pl
from jax.experimental.pallas import tpu as pltpu

F32 = jnp.float32
BF16 = jnp.bfloat16

EPS = 1e-6
HYENA_GROUPS = 8
MLSTM_HEADS = 4
XATTN_HEADS = 4
FILTER_BANDS = 16
DECAY_TARGET = 1e-2
SHORT_DECAY_PCT = 0.3
LONG_DECAY_PCT = 1.5

LANES = 128
VMEM_LIMIT = 56 * 1024 * 1024


def _params(*sem):
    return pltpu.CompilerParams(dimension_semantics=sem, vmem_limit_bytes=VMEM_LIMIT)


def _bdot(a, b):
    return jnp.dot(a.astype(BF16), b.astype(BF16), preferred_element_type=F32)


def _split3(x):
    hi = x.astype(BF16)
    r = x - hi.astype(F32)
    mid = r.astype(BF16)
    lo = (r - mid.astype(F32)).astype(BF16)
    return hi, mid, lo


IN_CHUNK = 512
HALO = 8


def _rms(x, g):
    return x * lax.rsqrt(jnp.mean(x * x, axis=-1, keepdims=True) + EPS) * g


def _in_proj_kernel(x_ref, xb_ref, xa_ref, g_ref, wc_ref, wp_ref, wg_ref, cw_ref, cb_ref,
                    proj_ref, qk_ref, og_ref, xn_ref, *, tiles_per_seq, n_hy):
    tm = x_ref.shape[0]
    g = g_ref[...]
    x = x_ref[...]
    xn_ref[...] = (x * g).astype(BF16)
    r = lax.rsqrt(jnp.mean(x * x, axis=-1, keepdims=True) + EPS)
    og_ref[...] = (jnp.dot(xn_ref[...], wg_ref[...], preferred_element_type=F32) * r).T[0:og_ref.shape[0], :]
    for c in range(wp_ref.shape[1] // IN_CHUNK):
        cols = slice(c * IN_CHUNK, (c + 1) * IN_CHUNK)
        out_cols = slice(n_hy + c * IN_CHUNK, n_hy + (c + 1) * IN_CHUNK)
        proj_ref[:, out_cols] = (jnp.dot(xn_ref[...], wp_ref[:, cols], preferred_element_type=F32) * r
                                 ).astype(proj_ref.dtype)

    halo = _rms(jnp.concatenate([xb_ref[...], xa_ref[...]], axis=0), g).astype(BF16)
    pos = pl.program_id(0) % tiles_per_seq
    row = lax.broadcasted_iota(jnp.int32, (tm, IN_CHUNK), 0)
    for c in range(wc_ref.shape[1] // IN_CHUNK):
        cols = slice(c * IN_CHUNK, (c + 1) * IN_CHUNK)
        u = jnp.dot(xn_ref[...], wc_ref[:, cols], preferred_element_type=F32) * r
        uh = jnp.dot(halo, wc_ref[:, cols], preferred_element_type=F32)
        before = jnp.where(pos == 0, 0.0, uh[HALO - 1:HALO, :])
        after = jnp.where(pos == tiles_per_seq - 1, 0.0, uh[HALO:HALO + 1, :])
        prev = jnp.where(row == 0, before, pltpu.roll(u, 1, 0))
        nxt = jnp.where(row == tm - 1, after, pltpu.roll(u, tm - 1, 0))
        y = prev * cw_ref[0:1, cols] + u * cw_ref[1:2, cols] + nxt * cw_ref[2:3, cols] + cb_ref[:, cols]
        if c * IN_CHUNK < n_hy:
            proj_ref[:, cols] = y.astype(proj_ref.dtype)
        else:
            qk_cols = slice(c * IN_CHUNK - n_hy, (c + 1) * IN_CHUNK - n_hy)
            qk_ref[:, qk_cols] = (y * jax.nn.sigmoid(y)).astype(qk_ref.dtype)


def _in_proj(x, g, w_conv, w_plain, w_gate, n_gate, conv_w, conv_b, n_hy, seq_len, tm):
    M, K = x.shape
    nc, npl = w_conv.shape[1], w_plain.shape[1]
    hb = tm // HALO
    const = lambda a: pl.BlockSpec(a.shape, lambda i: (0,) * a.ndim, pipeline_mode=pl.Buffered(1))
    row = lambda n: pl.BlockSpec((tm, n), lambda i: (i, 0))
    g = g.reshape(1, K)
    conv_b = conv_b.reshape(1, nc)
    return pl.pallas_call(
        functools.partial(_in_proj_kernel, tiles_per_seq=seq_len // tm, n_hy=n_hy),
        grid=(M // tm,),
        in_specs=[row(K),
                  pl.BlockSpec((HALO, K), lambda i: (jnp.maximum(i * hb - 1, 0), 0)),
                  pl.BlockSpec((HALO, K), lambda i: (jnp.minimum((i + 1) * hb, M // HALO - 1), 0)),
                  const(g), const(w_conv), const(w_plain), const(w_gate), const(conv_w), const(conv_b)],
        out_specs=[row(n_hy + npl), row(nc - n_hy), pl.BlockSpec((n_gate, tm), lambda i: (0, i))],
        out_shape=[jax.ShapeDtypeStruct((M, n_hy + npl), BF16), jax.ShapeDtypeStruct((M, nc - n_hy), BF16),
                   jax.ShapeDtypeStruct((n_gate, M), F32)],
        scratch_shapes=[pltpu.VMEM((tm, K), BF16)],
        compiler_params=_params("parallel"),
        name="in_proj",
    )(x, x, x, g, w_conv, w_plain, w_gate, conv_w, conv_b)


def _filter_kernel(feat_ref, w1_ref, b1_ref, f1_ref, w2_ref, b2_ref, f2_ref, w3h_ref, w3l_ref, dec_ref, dir_ref,
                   o_ref, *, L):
    tl = o_ref.shape[0]
    hp = lax.Precision.HIGHEST
    pos = (lax.broadcasted_iota(jnp.int32, (tl, LANES), 0) + pl.program_id(0) * tl).astype(F32)
    lane = lax.broadcasted_iota(jnp.int32, (tl, LANES), 1)
    t = pos * (1.0 / (L - 1))
    arg = feat_ref[...] * (pos * (2.0 * math.pi / L))
    z = jnp.where(lane < FILTER_BANDS, jnp.cos(arg),
                  jnp.where(lane < 2 * FILTER_BANDS, -jnp.sin(arg),
                            jnp.where(lane == 2 * FILTER_BANDS, t, 0.0)))
    hid = jnp.sin(f1_ref[...] * (jnp.dot(z, w1_ref[...], precision=hp, preferred_element_type=F32) + b1_ref[...]))
    hid = jnp.sin(f2_ref[...] * (jnp.dot(hid, w2_ref[...], precision=hp, preferred_element_type=F32) + b2_ref[...]))
    hid_hi = hid.astype(BF16)
    hid_lo = (hid - hid_hi.astype(F32)).astype(BF16)
    filt = (jnp.dot(hid_hi, w3h_ref[...], preferred_element_type=F32)
            + jnp.dot(hid_hi, w3l_ref[...], preferred_element_type=F32)
            + jnp.dot(hid_lo, w3h_ref[...], preferred_element_type=F32))
    filt = filt * jnp.exp(-t[:, 0:1] * dec_ref[...])
    filt = jnp.where(pos[:, 0:1] == 0.0, filt * dir_ref[...], filt)
    o_ref[...] = filt


def _hyena_filters(L, w1, b1, fr1, w2, b2, fr2, w3, d_hyena, tl=512):
    n_emb, n_hid = w1.shape
    n_out = w3.shape[1]
    bands = jnp.linspace(1e-4, FILTER_BANDS - 1, FILTER_BANDS, dtype=F32)
    feat = jnp.zeros((1, LANES), F32).at[0, :FILTER_BANDS].set(bands).at[0, FILTER_BANDS:2 * FILTER_BANDS].set(bands)
    w1p = jnp.zeros((LANES, n_hid), F32).at[:n_emb - 1].set(w1[1:]).at[n_emb - 1].set(w1[0])
    max_decay = math.log(DECAY_TARGET) / SHORT_DECAY_PCT
    min_decay = math.log(DECAY_TARGET) / LONG_DECAY_PCT
    deltas = jnp.abs(jnp.linspace(min_decay, max_decay, d_hyena, dtype=F32))
    reps = n_out // d_hyena
    dec = jnp.tile(deltas, reps).reshape(1, n_out)
    dirmask = jnp.tile(jnp.concatenate([jnp.ones((d_hyena,), F32), jnp.zeros((d_hyena,), F32)]), reps // 2)
    w3_hi = w3.astype(BF16)
    w3_lo = (w3 - w3_hi.astype(F32)).astype(BF16)
    full = lambda shape: pl.BlockSpec(shape, lambda i: (0,) * len(shape))
    return pl.pallas_call(
        functools.partial(_filter_kernel, L=L),
        grid=(L // tl,),
        in_specs=[full((1, LANES)), full((LANES, n_hid)), full((1, n_hid)), full((1, n_hid)),
                  full((n_hid, n_hid)), full((1, n_hid)), full((1, n_hid)), full((n_hid, n_out)),
                  full((n_hid, n_out)), full((1, n_out)), full((1, n_out))],
        out_specs=pl.BlockSpec((tl, n_out), lambda i: (i, 0)),
        out_shape=jax.ShapeDtypeStruct((L, n_out), F32),
        compiler_params=_params("parallel"),
        name="hyena_filters",
    )(feat, w1p, b1.reshape(1, -1), fr1.reshape(1, -1), w2, b2.reshape(1, -1), fr2.reshape(1, -1), w3_hi, w3_lo,
      dec, dirmask.reshape(1, n_out))


MID_ROWS = 4096
SIDE_PAD = 16


def _fft_tables(L):
    N = 2 * L
    N2 = LANES
    N1 = N // N2
    H1 = N1 // 2
    k1 = np.arange(H1)[:, None]
    n1 = np.arange(H1)[None, :]
    a1 = 2.0 * np.pi * ((k1 * n1) % N1) / N1
    c1, s1 = np.cos(a1), np.sin(a1)
    sign = np.where(np.arange(H1) % 2 == 0, 1.0, -1.0)[None, :]
    pad = np.zeros((SIDE_PAD - 1, H1))
    m1 = np.concatenate([c1, -s1, sign, pad], axis=0)
    wgt = np.where(np.arange(H1) == 0, 1.0, 2.0)[:, None]
    minv = np.concatenate([(wgt * c1).T, (-wgt * s1).T, sign.T, pad.T], axis=1)
    n2 = np.arange(N2)[:, None]
    k2 = np.arange(N2)[None, :]
    a2 = 2.0 * np.pi * ((n2 * k2) % N2) / N2
    cg, sg = np.cos(a2), np.sin(a2)
    g2 = np.block([[cg, -sg], [sg, cg]])
    g2i = np.block([[cg, sg], [-sg, cg]])
    at = 2.0 * np.pi * (np.arange(H1 + 8)[:, None] * np.arange(N2)[None, :]) / N
    tw = np.concatenate([np.cos(at), -np.sin(at)], axis=1)
    as_bf = lambda a: jnp.asarray(a.astype(np.float32)).astype(BF16)
    return dict(m1=as_bf(m1), minv=as_bf(minv), g2=as_bf(g2), g2i=as_bf(g2i),
                tw=jnp.asarray(tw.astype(np.float32)), N1=N1, H1=H1)


S_PITCH = LANES + 8
GROUP = 8
STAGE_UNROLL = 8


SLAB_UNROLL = 4


def _slab_scratch_rows(i):
    return pl.ds(pl.multiple_of(i * S_PITCH, 8), LANES)


def _slab_transpose_one(get_slab, dst_refs, i):
    t = get_slab(i).T
    for dst in dst_refs:
        dst[_slab_scratch_rows(i), :] = t


def _slab_transpose_in(get_slab, dst_refs, h1):
    def body(j, carry):
        for u in range(SLAB_UNROLL):
            _slab_transpose_one(get_slab, dst_refs, j * SLAB_UNROLL + u)
        return carry

    lax.fori_loop(0, h1 // SLAB_UNROLL, body, 0)


def _slab_transpose_out(s_ref, put_slab, h1):
    def body(j, carry):
        for u in range(SLAB_UNROLL):
            i = j * SLAB_UNROLL + u
            put_slab(i, s_ref[_slab_scratch_rows(i), :].T)
        return carry

    lax.fori_loop(0, h1 // SLAB_UNROLL, body, 0)


def _slab_map(fn, h1):
    def body(j, carry):
        for u in range(SLAB_UNROLL):
            fn(_slab_scratch_rows(j * SLAB_UNROLL + u))
        return carry

    lax.fori_loop(0, h1 // SLAB_UNROLL, body, 0)


def _slab_rows(i):
    return pl.ds(pl.multiple_of(i * LANES, LANES), LANES)


def _fft_stage_a(s_ref, nch, m1, tw_ref, ab_ref, side_ref, n1, side_work=None):
    h1 = n1 // 2
    twr = tw_ref[0:h1, 0:LANES]
    twi = tw_ref[0:h1, LANES:2 * LANES]
    tsr = tw_ref[h1:h1 + 1, 0:LANES]
    tsi = tw_ref[h1:h1 + 1, LANES:2 * LANES]

    def body(g, carry):
        for u in range(STAGE_UNROLL):
            c0 = (g * STAGE_UNROLL + u) * GROUP
            rhs = jnp.concatenate([s_ref[pl.ds(c0 + i, h1, stride=S_PITCH), :] for i in range(GROUP)], axis=1)
            res = jnp.dot(m1, rhs.astype(BF16), preferred_element_type=F32)
            for i in range(GROUP):
                ar = res[0:h1, LANES * i:LANES * (i + 1)]
                ai = res[h1:2 * h1, LANES * i:LANES * (i + 1)]
                rows = pl.ds(pl.multiple_of((c0 + i) * h1, h1), h1)
                ab_ref[rows, 0:LANES] = ar * twr - ai * twi
                ab_ref[rows, LANES:2 * LANES] = ar * twi + ai * twr
            a_side = jnp.concatenate([res[2 * h1:2 * h1 + 1, LANES * i:LANES * (i + 1)] for i in range(GROUP)],
                                     axis=0)
            rows = pl.ds(pl.multiple_of(c0, GROUP), GROUP)
            side_ref[rows, 0:LANES] = a_side * tsr
            side_ref[rows, LANES:2 * LANES] = a_side * tsi
        if side_work is not None:
            side_work(g, n_iter)
        return carry

    n_iter = nch // (GROUP * STAGE_UNROLL)
    lax.fori_loop(0, n_iter, body, 0)


def _cmul(a, b):
    ar, ai = a[:, 0:LANES], a[:, LANES:2 * LANES]
    br, bi = b[:, 0:LANES], b[:, LANES:2 * LANES]
    return jnp.concatenate([ar * br - ai * bi, ar * bi + ai * br], axis=1)


def _fft_mid(ab_ref, side_ref, get_h, get_h_side, g2, g2i, nrows):
    chunk = min(MID_ROWS, nrows)

    def body(r, carry):
        rows = pl.ds(pl.multiple_of(r * chunk, chunk), chunk)
        spec = _bdot(ab_ref[rows, :], g2)
        ab_ref[rows, :] = _bdot(_cmul(spec, get_h(rows)), g2i)
        return carry

    lax.fori_loop(0, nrows // chunk, body, 0)
    side_ref[...] = _bdot(_cmul(_bdot(side_ref[...], g2), get_h_side()), g2i)


def _fft_stage_a_inv(ab_ref, side_ref, nch, minv, tw_ref, s_ref, n1, side_work=None):
    h1 = n1 // 2
    twr = tw_ref[0:h1, 0:LANES]
    twi = tw_ref[0:h1, LANES:2 * LANES]
    tsr = tw_ref[h1:h1 + 1, 0:LANES]
    tsi = tw_ref[h1:h1 + 1, LANES:2 * LANES]
    first_row = lax.broadcasted_iota(jnp.int32, (SIDE_PAD, GROUP * LANES), 0) == 0

    def body(g, carry):
        for u in range(STAGE_UNROLL):
            c0 = (g * STAGE_UNROLL + u) * GROUP
            blk = ab_ref[pl.ds(pl.multiple_of(c0 * h1, GROUP * h1), GROUP * h1), :]
            re, im = [], []
            for i in range(GROUP):
                br = blk[i * h1:(i + 1) * h1, 0:LANES]
                bi = blk[i * h1:(i + 1) * h1, LANES:2 * LANES]
                re.append(br * twr + bi * twi)
                im.append(bi * twr - br * twi)
            side = side_ref[pl.ds(pl.multiple_of(c0, GROUP), GROUP), :]
            side_re = side[:, 0:LANES] * tsr + side[:, LANES:2 * LANES] * tsi
            side_row = jnp.concatenate([side_re[i:i + 1, :] for i in range(GROUP)], axis=1)
            side_blk = jnp.where(first_row, jnp.broadcast_to(side_row, first_row.shape), 0.0)
            rhs = jnp.concatenate([jnp.concatenate(re, axis=1), jnp.concatenate(im, axis=1), side_blk], axis=0)
            y = jnp.dot(minv, rhs.astype(BF16), preferred_element_type=F32)
            for i in range(GROUP):
                s_ref[pl.ds(c0 + i, h1, stride=S_PITCH), :] = y[:, LANES * i:LANES * (i + 1)]
        if side_work is not None:
            side_work(g, n_iter)
        return carry

    n_iter = nch // (GROUP * STAGE_UNROLL)
    lax.fori_loop(0, n_iter, body, 0)


def _filter_spectrum_kernel(hf_ref, hb_ref, m1_ref, g2_ref, tw_ref, o_ref, oside_ref,
                            s_ref, af_ref, ab_ref, afs_ref, abs_ref, *, n1, inv_n):
    nch = hf_ref.shape[1]
    h1 = n1 // 2
    m1, g2 = m1_ref[...], g2_ref[...]
    _slab_transpose_in(lambda i: hf_ref[_slab_rows(i), :], [s_ref], h1)
    _fft_stage_a(s_ref, nch, m1, tw_ref, af_ref, afs_ref, n1)
    _slab_transpose_in(lambda i: hb_ref[_slab_rows(i), :], [s_ref], h1)
    _fft_stage_a(s_ref, nch, m1, tw_ref, ab_ref, abs_ref, n1)

    def two_sided(fwd, bwd):
        sf = _bdot(fwd, g2)
        sb = _bdot(bwd, g2)
        return jnp.concatenate([sf[:, 0:LANES] + sb[:, 0:LANES], sf[:, LANES:] - sb[:, LANES:]], axis=1) * inv_n

    chunk = min(MID_ROWS, nch * h1)

    def body(r, carry):
        rows = pl.ds(pl.multiple_of(r * chunk, chunk), chunk)
        o_ref[rows, :] = two_sided(af_ref[rows, :], ab_ref[rows, :])
        return carry

    lax.fori_loop(0, (nch * h1) // chunk, body, 0)
    oside_ref[...] = two_sided(afs_ref[...], abs_ref[...])


def _filter_spectrum(filt, L, d_hyena, tabs):
    h1 = tabs["H1"]
    order = filt.shape[1] // (2 * d_hyena)
    cb = LANES
    nblk = d_hyena // cb
    full = lambda a: pl.BlockSpec(a.shape, lambda o, c: (0,) * a.ndim)
    main = pltpu.VMEM((cb * h1, 2 * LANES), F32)
    side = pltpu.VMEM((cb, 2 * LANES), F32)
    return pl.pallas_call(
        functools.partial(_filter_spectrum_kernel, n1=tabs["N1"], inv_n=1.0 / (2 * L)),
        grid=(order, nblk),
        in_specs=[pl.BlockSpec((L, cb), lambda o, c: (0, o * 2 * nblk + c)),
                  pl.BlockSpec((L, cb), lambda o, c: (0, o * 2 * nblk + nblk + c)),
                  full(tabs["m1"]), full(tabs["g2"]), full(tabs["tw"])],
        out_specs=[pl.BlockSpec((cb * h1, 2 * LANES), lambda o, c: (o * nblk + c, 0)),
                   pl.BlockSpec((cb, 2 * LANES), lambda o, c: (o * nblk + c, 0))],
        out_shape=[jax.ShapeDtypeStruct((order * d_hyena * h1, 2 * LANES), F32),
                   jax.ShapeDtypeStruct((order * d_hyena, 2 * LANES), F32)],
        scratch_shapes=[pltpu.VMEM((h1 * S_PITCH, LANES), F32), main, main, side, side],
        compiler_params=_params("parallel", "parallel"),
        name="filter_spectrum",
    )(filt, filt, tabs["m1"], tabs["g2"], tabs["tw"])


def _hyena_kernel(v_ref, vnext_ref, x1_ref, x2_ref, h_ref, hside_ref, d_ref, m1_ref, minv_ref, g2_ref, g2i_ref,
                  tw_ref, o_ref, s_ref, sv_ref, sn_ref, sx_ref, sz_ref, ab_ref, side_ref, *, n1):
    nch = v_ref.shape[2]
    h1 = n1 // 2
    m1, minv, g2, g2i = m1_ref[...], minv_ref[...], g2_ref[...], g2i_ref[...]

    def slab_of(p_ref):
        return lambda i: p_ref[0, _slab_rows(i), :].astype(F32)

    def transposer(p_ref, dst_ref):
        def work(g, n_iter):
            per_iter = h1 // n_iter
            for u in range(per_iter):
                _slab_transpose_one(slab_of(p_ref), [dst_ref], g * per_iter + u)
        return work

    def long_conv(order, src_ref, gate_ref, skip_ref, keep_ref=None, inv_side_work=None):
        _fft_stage_a(src_ref, nch, m1, tw_ref, ab_ref, side_ref, n1, side_work=transposer(gate_ref, sx_ref))
        _fft_mid(ab_ref, side_ref, lambda rows: h_ref[order, rows, :], lambda: hside_ref[order], g2, g2i, nch * h1)
        _fft_stage_a_inv(ab_ref, side_ref, nch, minv, tw_ref, s_ref, n1, side_work=inv_side_work)
        d_slab = jnp.broadcast_to(d_ref[:, order:order + 1], (nch, LANES))

        def gate(rows):
            val = sx_ref[rows, :] * (s_ref[rows, :] + d_slab * skip_ref[rows, :])
            s_ref[rows, :] = val
            if keep_ref is not None:
                keep_ref[rows, :] = val

        _slab_map(gate, h1)

    first = pl.program_id(1) == 0

    @pl.when(first)
    def _():
        _slab_transpose_in(slab_of(v_ref), [sv_ref], h1)

    @pl.when(jnp.logical_not(first))
    def _():
        def copy(rows):
            sv_ref[rows, :] = sn_ref[rows, :]

        _slab_map(copy, h1)

    long_conv(0, sv_ref, x1_ref, sv_ref, keep_ref=sz_ref, inv_side_work=transposer(vnext_ref, sn_ref))
    long_conv(1, s_ref, x2_ref, sz_ref)

    def put(i, out):
        o_ref[0, _slab_rows(i), :] = out.astype(o_ref.dtype)

    _slab_transpose_out(s_ref, put, h1)


def _hyena(proj, spec, spec_side, skip, tabs):
    B, L, _ = proj.shape
    D = skip.shape[1]
    n1, h1 = tabs["N1"], tabs["H1"]
    cb = LANES
    nblk = D // cb
    spec = spec.reshape(2, D * h1, 2 * LANES)
    spec_side = spec_side.reshape(2, D, 2 * LANES)
    col = lambda part: pl.BlockSpec((1, L, cb), lambda c, b: (b, 0, part * nblk + c))
    full = lambda a: pl.BlockSpec(a.shape, lambda c, b: (0,) * a.ndim)
    slabs = pltpu.VMEM((h1 * S_PITCH, LANES), F32)
    return pl.pallas_call(
        functools.partial(_hyena_kernel, n1=n1),
        grid=(nblk, B),
        in_specs=[col(0),
                  pl.BlockSpec((1, L, cb), lambda c, b: (jnp.minimum(b + 1, B - 1), 0, c)),
                  col(1), col(2),
                  pl.BlockSpec((2, cb * h1, 2 * LANES), lambda c, b: (0, c, 0), pipeline_mode=pl.Buffered(1)),
                  pl.BlockSpec((2, cb, 2 * LANES), lambda c, b: (0, c, 0)),
                  pl.BlockSpec((cb, 2), lambda c, b: (c, 0)),
                  full(tabs["m1"]), full(tabs["minv"]), full(tabs["g2"]), full(tabs["g2i"]), full(tabs["tw"])],
        out_specs=pl.BlockSpec((1, L, cb), lambda c, b: (b, 0, c)),
        out_shape=jax.ShapeDtypeStruct((B, L, D), BF16),
        scratch_shapes=[slabs, slabs, slabs, slabs, slabs, pltpu.VMEM((cb * h1, 2 * LANES), F32),
                        pltpu.VMEM((cb, 2 * LANES), F32)],
        compiler_params=_params("parallel", "arbitrary"),
        name="hyena_fftconv",
    )(proj, proj, proj, proj, spec, spec_side, skip.astype(F32).T,
      tabs["m1"], tabs["minv"], tabs["g2"], tabs["g2i"], tabs["tw"])


ML_CHUNK = 256
ML_CHUNKS_PER_STEP = 1


def _log_sigmoid(x):
    return jnp.minimum(x, 0.0) - jnp.log1p(jnp.exp(-jnp.abs(x)))


def _mlstm_direction(q, k, v, i_row, cum_row, logf_row, c_ref, m_ref, mask, scale):
    c, dh = q.shape
    g_row = i_row - cum_row
    f_col_rep = jnp.broadcast_to(cum_row, (LANES, c)).T
    m_prev = m_ref[0:1, 0:1]
    e = jnp.where(mask, g_row, -jnp.inf)
    a = jnp.maximum(jnp.max(e, axis=1, keepdims=True), m_prev)
    w = jnp.exp(e - a)
    sc = lax.dot_general(q.astype(BF16), k.astype(BF16), (((1,), (1,)), ((), ())), preferred_element_type=F32)
    p = sc * (w * scale)
    vaug = jnp.concatenate([v, jnp.ones_like(v)], axis=1)
    inter = jnp.exp(m_prev - a)
    tot = _bdot(p, vaug) + inter * _bdot(q, c_ref[...])
    num, den = tot[:, 0:dh], tot[:, dh:2 * dh]
    m_t = f_col_rep + a
    h = num / jnp.maximum(jnp.abs(den), jnp.exp(-m_t))
    a_end = jnp.maximum(jnp.max(g_row, axis=1, keepdims=True), m_prev)
    f_end = jnp.sum(logf_row, axis=1, keepdims=True)
    we = jnp.exp(g_row - a_end) * scale
    kw = k.astype(F32).T * we
    c_ref[...] = jnp.exp(m_prev - a_end) * c_ref[...] + _bdot(kw, vaug)
    m_ref[...] = jnp.broadcast_to(f_end + a_end, m_ref.shape)
    return h


def _mlstm_kernel(qf_ref, kf_ref, vf_ref, gf_ref, qb_ref, kb_ref, vb_ref, gb_ref, bias_ref,
                  hf_ref, hb_ref, c_ref, m_ref, *, heads, scale, chunk):
    @pl.when(pl.program_id(1) == 0)
    def _():
        c_ref[...] = jnp.zeros_like(c_ref)
        m_ref[...] = jnp.zeros_like(m_ref)

    c = chunk
    nsub = qf_ref.shape[1] // c
    dh = qf_ref.shape[2] // heads
    r = lax.broadcasted_iota(jnp.int32, (c, c), 0)
    s = lax.broadcasted_iota(jnp.int32, (c, c), 1)
    mask_f = s <= r
    mask_b = s >= r

    def gate_rows(g, mask_t):
        g = g + bias_ref[...]
        logf = _log_sigmoid(g)
        tri = mask_t.astype(BF16)
        cum = sum(jnp.dot(part, tri, preferred_element_type=F32) for part in _split3(logf))
        return g, logf, cum

    for u in range(nsub):
        rf = slice(u * c, (u + 1) * c)
        rb = slice((nsub - 1 - u) * c, (nsub - u) * c)
        g_f, logf_f, cum_f = gate_rows(gf_ref[:, rf], mask_b)
        g_b, logf_b, cum_b = gate_rows(gb_ref[:, rb], mask_f)
        for h in range(heads):
            sl = slice(h * dh, (h + 1) * dh)
            fi, ff, bi, bf = h, heads + h, 2 * heads + h, 3 * heads + h
            hf_ref[0, rf, sl] = _mlstm_direction(
                qf_ref[0, rf, sl], kf_ref[0, rf, sl], vf_ref[0, rf, sl], g_f[fi:fi + 1], cum_f[ff:ff + 1],
                logf_f[ff:ff + 1], c_ref.at[h], m_ref.at[h], mask_f, scale).astype(hf_ref.dtype)
            hb_ref[0, rb, sl] = _mlstm_direction(
                qb_ref[0, rb, sl], kb_ref[0, rb, sl], vb_ref[0, rb, sl], g_b[bi:bi + 1], cum_b[bf:bf + 1],
                logf_b[bf:bf + 1], c_ref.at[heads + h], m_ref.at[heads + h], mask_b, scale).astype(hb_ref.dtype)


def _mlstm(qk, proj, v_col0, gates, gate_b, d_ml):
    B, L, _ = qk.shape
    H = MLSTM_HEADS
    dh = d_ml // H
    c = min(ML_CHUNK, L)
    rows = min(ML_CHUNKS_PER_STEP * c, L)
    nb = L // rows
    bias = jnp.broadcast_to(gate_b.astype(F32)[:, None], (4 * H, c))
    vb0 = v_col0 // d_ml
    specs = []
    for pos in (lambda ci: ci, lambda ci: nb - 1 - ci):
        specs += [pl.BlockSpec((1, rows, d_ml), lambda bi, ci, pos=pos: (bi, pos(ci), 0)),
                  pl.BlockSpec((1, rows, d_ml), lambda bi, ci, pos=pos: (bi, pos(ci), 1)),
                  pl.BlockSpec((1, rows, d_ml), lambda bi, ci, pos=pos: (bi, pos(ci), vb0)),
                  pl.BlockSpec((4 * H, rows), lambda bi, ci, pos=pos: (0, bi * nb + pos(ci)))]
    specs.append(pl.BlockSpec((4 * H, c), lambda bi, ci: (0, 0)))
    out_shape = jax.ShapeDtypeStruct((B, L, d_ml), BF16)
    return pl.pallas_call(
        functools.partial(_mlstm_kernel, heads=H, scale=dh ** -0.5, chunk=c),
        grid=(B, nb),
        in_specs=specs,
        out_specs=[pl.BlockSpec((1, rows, d_ml), lambda bi, ci: (bi, ci, 0)),
                   pl.BlockSpec((1, rows, d_ml), lambda bi, ci: (bi, nb - 1 - ci, 0))],
        out_shape=[out_shape, out_shape],
        scratch_shapes=[pltpu.VMEM((2 * H, dh, 2 * dh), F32), pltpu.VMEM((2 * H, 1, LANES), F32)],
        compiler_params=_params("parallel", "arbitrary"),
        name="mlstm_scan",
    )(qk, qk, proj, gates, qk, qk, proj, gates, bias)


def _group_rms(y, gain, bd, group):
    sq = (y * y).astype(BF16)
    w = bd.shape[0]
    ss = jnp.concatenate([jnp.dot(sq[:, i:i + w], bd, preferred_element_type=F32)
                          for i in range(0, y.shape[1], w)], axis=1)
    return y * lax.rsqrt(ss * (1.0 / group) + EPS) * gain


GN_STRIPE = 256


def _block_diag_ones(n, group):
    idx = np.arange(n) // group
    return jnp.asarray((idx[:, None] == idx[None, :]).astype(np.float32)).astype(BF16)


def _mix_xattn_kernel(z_ref, hf_ref, hb_ref, o_ref, x_ref, mem_ref, ghy_ref, gml_ref, why_ref, wml_ref, bdh_ref,
                      bdm_ref, gx_ref, gm_ref, wq_ref, wkv_ref, wo_ref, out_ref, kv_ref,
                      *, hy_group, ml_group, heads, scale):
    @pl.when(pl.program_id(1) == 0)
    def _():
        kv_ref[...] = _bdot(_rms(mem_ref[0], gm_ref[...]), wkv_ref[...]).astype(kv_ref.dtype)

    y_hy = _group_rms(z_ref[0].astype(F32), ghy_ref[...], bdh_ref[...], hy_group)
    h_sum = hf_ref[0].astype(F32) + hb_ref[0].astype(F32)
    y_ml = _group_rms(jax.nn.sigmoid(o_ref[0].astype(F32)) * h_sum, gml_ref[...], bdm_ref[...], ml_group)
    h1 = x_ref[0] + _bdot(y_hy, why_ref[...]) + _bdot(y_ml, wml_ref[...])

    D = h1.shape[1]
    dh = D // heads
    q = _bdot(_rms(h1, gx_ref[...]), wq_ref[...])
    outs = []
    for hd in range(heads):
        qh = q[:, hd * dh:(hd + 1) * dh]
        kh = kv_ref[:, hd * dh:(hd + 1) * dh]
        vh = kv_ref[:, D + hd * dh:D + (hd + 1) * dh]
        s = lax.dot_general(qh.astype(BF16), kh, (((1,), (1,)), ((), ())), preferred_element_type=F32) * scale
        e = jnp.exp(s - jnp.max(s, axis=-1, keepdims=True))
        p = e / jnp.sum(e, axis=-1, keepdims=True)
        outs.append(_bdot(p, vh))
    out_ref[0] = h1 + _bdot(jnp.concatenate(outs, axis=1), wo_ref[...])


def _mix_xattn(z, hf, hb, proj, o_col0, x, mem, g_hy, g_ml, w_out, g_x, g_mem, wq, wk, wv, wo, tm=512):
    B, L, d_hy = z.shape
    d_ml = hf.shape[2]
    D = x.shape[2]
    nm = mem.shape[1]
    ob = o_col0 // d_ml
    row = lambda n: pl.BlockSpec((1, tm, n), lambda bi, i: (bi, i, 0))
    const = lambda a: pl.BlockSpec(a.shape, lambda bi, i: (0,) * a.ndim, pipeline_mode=pl.Buffered(1))
    hy_group, ml_group = d_hy // HYENA_GROUPS, d_ml // MLSTM_HEADS
    consts = [g_hy.reshape(1, -1), g_ml.reshape(1, -1), w_out[:d_hy].astype(BF16), w_out[d_hy:].astype(BF16),
              _block_diag_ones(GN_STRIPE, hy_group), _block_diag_ones(GN_STRIPE, ml_group), g_x.reshape(1, D),
              g_mem.reshape(1, D), wq.astype(BF16), jnp.concatenate([wk, wv], axis=1).astype(BF16), wo.astype(BF16)]
    return pl.pallas_call(
        functools.partial(_mix_xattn_kernel, hy_group=hy_group, ml_group=ml_group, heads=XATTN_HEADS,
                          scale=(D // XATTN_HEADS) ** -0.5),
        grid=(B, L // tm),
        in_specs=[row(d_hy), row(d_ml), row(d_ml), pl.BlockSpec((1, tm, d_ml), lambda bi, i: (bi, i, ob)), row(D),
                  pl.BlockSpec((1, nm, D), lambda bi, i: (bi, 0, 0))] + [const(a) for a in consts],
        out_specs=row(D),
        out_shape=jax.ShapeDtypeStruct((B, L, D), F32),
        scratch_shapes=[pltpu.VMEM((nm, 2 * D), BF16)],
        compiler_params=_params("parallel", "arbitrary"),
        name="mix_xattn",
    )(z, hf, hb, proj, x, mem, *consts)


MLP_CHUNK = 1024


def _mlp_kernel(h_ref, g_ref, w1_ref, w2_ref, gf_ref, out_ref, xn_ref, *, final_norm):
    j = pl.program_id(1)

    @pl.when(j == 0)
    def _():
        x = h_ref[...]
        xn_ref[...] = _rms(x, g_ref[...]).astype(BF16)
        out_ref[...] = x

    for c in range(w1_ref.shape[1] // MLP_CHUNK):
        cols = slice(c * MLP_CHUNK, (c + 1) * MLP_CHUNK)
        a = jnp.maximum(jnp.dot(xn_ref[...], w1_ref[:, cols], preferred_element_type=F32), 0.0)
        out_ref[...] += _bdot(a * a, w2_ref[cols, :])

    if final_norm:
        @pl.when(j == pl.num_programs(1) - 1)
        def _():
            out_ref[...] = _rms(out_ref[...], gf_ref[...])


def _mlp(h, g, w1, w2, gf, final_norm, tm=1024, tf=2048):
    M, D = h.shape
    dff = w1.shape[1]
    return pl.pallas_call(
        functools.partial(_mlp_kernel, final_norm=final_norm),
        grid=(M // tm, dff // tf),
        in_specs=[pl.BlockSpec((tm, D), lambda i, j: (i, 0)), pl.BlockSpec((1, D), lambda i, j: (0, 0)),
                  pl.BlockSpec((D, tf), lambda i, j: (0, j)), pl.BlockSpec((tf, D), lambda i, j: (j, 0)),
                  pl.BlockSpec((1, D), lambda i, j: (0, 0))],
        out_specs=pl.BlockSpec((tm, D), lambda i, j: (i, 0)),
        out_shape=jax.ShapeDtypeStruct((M, D), F32),
        scratch_shapes=[pltpu.VMEM((tm, D), BF16)],
        compiler_params=_params("parallel", "arbitrary"),
        name="mlp",
    )(h, g.reshape(1, D), w1.astype(BF16), w2.astype(BF16), gf.reshape(1, D))


def _pad_cols(w, n):
    return jnp.pad(w, ((0, 0), (0, n - w.shape[1])))


def _layer(h, mem, p, l):
    B, L, D = h.shape
    M = B * L
    d_hy = p["hy_norm_g"].shape[1]
    d_ml = p["ml_norm_g"].shape[1]
    hy_cols = 3 * d_hy
    qk_cols = 2 * d_ml
    n_gate = 4 * MLSTM_HEADS
    w_in = p["w_in"][l]
    vo0 = hy_cols + qk_cols
    g0 = vo0 + 2 * d_ml
    v_col0 = hy_cols
    o_col0 = hy_cols + d_ml
    n_in = hy_cols + 2 * d_ml
    proj, qk, gates = _in_proj(h.reshape(M, D), p["norm_mix_g"][l], w_in[:, :vo0].astype(BF16),
                               w_in[:, vo0:g0].astype(BF16),
                               _pad_cols(w_in[:, g0:g0 + n_gate], LANES).astype(BF16), n_gate,
                               jnp.concatenate([p["hy_conv_w"][l], p["ml_conv_w"][l]], axis=1),
                               jnp.concatenate([p["hy_conv_b"][l], p["ml_conv_b"][l]]),
                               n_hy=hy_cols, seq_len=L, tm=min(1024, L))
    proj = proj.reshape(B, L, n_in)
    qk = qk.reshape(B, L, qk_cols)

    tabs = _fft_tables(L)
    filt = _hyena_filters(L, p["hy_filt_w1"][l], p["hy_filt_b1"][l], p["hy_filt_freq1"][l], p["hy_filt_w2"][l],
                          p["hy_filt_b2"][l], p["hy_filt_freq2"][l], p["hy_filt_w3"][l], d_hy,
                          tl=min(512, L))
    spec, spec_side = _filter_spectrum(filt, L, d_hy, tabs)
    z_hy = _hyena(proj, spec, spec_side, p["hy_skip"][l], tabs)

    h_f, h_b = _mlstm(qk, proj, v_col0, gates, p["ml_gate_b"][l], d_ml)

    return _mix_xattn(z_hy, h_f, h_b, proj, o_col0, h, mem, p["hy_norm_g"][l], p["ml_norm_g"][l], p["w_out"][l],
                      p["norm_x_g"][l], p["norm_mem_g"][l], p["xa_wq"][l], p["xa_wk"][l], p["xa_wv"][l],
                      p["xa_wo"][l], tm=min(1024, L))


def kernel(x, mem, norm_mix_g, w_in, hy_conv_w, hy_conv_b, hy_filt_w1, hy_filt_b1, hy_filt_freq1, hy_filt_w2,
           hy_filt_b2, hy_filt_freq2, hy_filt_w3, hy_skip, hy_norm_g, ml_conv_w, ml_conv_b, ml_gate_b, ml_norm_g,
           w_out, norm_x_g, norm_mem_g, xa_wq, xa_wk, xa_wv, xa_wo, norm_ff_g, ff_w1, ff_w2, final_norm_g):
    p = dict(norm_mix_g=norm_mix_g, w_in=w_in, hy_conv_w=hy_conv_w, hy_conv_b=hy_conv_b, hy_filt_w1=hy_filt_w1,
             hy_filt_b1=hy_filt_b1, hy_filt_freq1=hy_filt_freq1, hy_filt_w2=hy_filt_w2, hy_filt_b2=hy_filt_b2,
             hy_filt_freq2=hy_filt_freq2, hy_filt_w3=hy_filt_w3, hy_skip=hy_skip, hy_norm_g=hy_norm_g,
             ml_conv_w=ml_conv_w, ml_conv_b=ml_conv_b, ml_gate_b=ml_gate_b, ml_norm_g=ml_norm_g, w_out=w_out,
             norm_x_g=norm_x_g, norm_mem_g=norm_mem_g, xa_wq=xa_wq, xa_wk=xa_wk, xa_wv=xa_wv, xa_wo=xa_wo)
    B, L, D = x.shape
    depth = w_in.shape[0]
    h = x
    for l in range(depth):
        h = _layer(h, mem, p, l)
        h = _mlp(h.reshape(B * L, D), norm_ff_g[l], ff_w1[l], ff_w2[l], final_norm_g, final_norm=l == depth - 1,
                 tm=min(1024, B * L)).reshape(B, L, D)
    return h
```

```python
import functools
import math

import numpy as np
import jax
import jax.numpy as jnp
from jax import lax
from jax.experimental import pallas as pl
from jax.experimental.pallas import tpu as pltpu

F32 = jnp.float32
BF16 = jnp.bfloat16

EPS = 1e-6
HYENA_GROUPS = 8
MLSTM_HEADS = 4
XATTN_HEADS = 4
FILTER_BANDS = 16
DECAY_TARGET = 1e-2
SHORT_DECAY_PCT = 0.3
LONG_DECAY_PCT = 1.5

LANES = 128
VMEM_LIMIT = 56 * 1024 * 1024


def _params(*sem):
    return pltpu.CompilerParams(dimension_semantics=sem, vmem_limit_bytes=VMEM_LIMIT)


def _bdot(a, b):
    return jnp.dot(a.astype(BF16), b.astype(BF16), preferred_element_type=F32)


def _split3(x):
    hi = x.astype(BF16)
    r = x - hi.astype(F32)
    mid = r.astype(BF16)
    lo = (r - mid.astype(F32)).astype(BF16)
    return hi, mid, lo


IN_CHUNK = 512
HALO = 8


def _rms(x, g):
    return x * lax.rsqrt(jnp.mean(x * x, axis=-1, keepdims=True) + EPS) * g


def _in_proj_kernel(x_ref, xb_ref, xa_ref, g_ref, wc_ref, wp_ref, wg_ref, cw_ref, cb_ref,
                    proj_ref, qk_ref, og_ref, xn_ref, *, tiles_per_seq, n_hy):
    tm = x_ref.shape[0]
    g = g_ref[...]
    x = x_ref[...]
    xn_ref[...] = (x * g).astype(BF16)
    r = lax.rsqrt(jnp.mean(x * x, axis=-1, keepdims=True) + EPS)
    og_ref[...] = (jnp.dot(xn_ref[...], wg_ref[...], preferred_element_type=F32) * r).T[0:og_ref.shape[0], :]
    for c in range(wp_ref.shape[1] // IN_CHUNK):
        cols = slice(c * IN_CHUNK, (c + 1) * IN_CHUNK)
        out_cols = slice(n_hy + c * IN_CHUNK, n_hy + (c + 1) * IN_CHUNK)
        proj_ref[:, out_cols] = (jnp.dot(xn_ref[...], wp_ref[:, cols], preferred_element_type=F32) * r
                                 ).astype(proj_ref.dtype)

    halo = _rms(jnp.concatenate([xb_ref[...], xa_ref[...]], axis=0), g).astype(BF16)
    pos = pl.program_id(0) % tiles_per_seq
    row = lax.broadcasted_iota(jnp.int32, (tm, IN_CHUNK), 0)
    for c in range(wc_ref.shape[1] // IN_CHUNK):
        cols = slice(c * IN_CHUNK, (c + 1) * IN_CHUNK)
        u = jnp.dot(xn_ref[...], wc_ref[:, cols], preferred_element_type=F32) * r
        uh = jnp.dot(halo, wc_ref[:, cols], preferred_element_type=F32)
        before = jnp.where(pos == 0, 0.0, uh[HALO - 1:HALO, :])
        after = jnp.where(pos == tiles_per_seq - 1, 0.0, uh[HALO:HALO + 1, :])
        prev = jnp.where(row == 0, before, pltpu.roll(u, 1, 0))
        nxt = jnp.where(row == tm - 1, after, pltpu.roll(u, tm - 1, 0))
        y = prev * cw_ref[0:1, cols] + u * cw_ref[1:2, cols] + nxt * cw_ref[2:3, cols] + cb_ref[:, cols]
        if c * IN_CHUNK < n_hy:
            proj_ref[:, cols] = y.astype(proj_ref.dtype)
        else:
            qk_cols = slice(c * IN_CHUNK - n_hy, (c + 1) * IN_CHUNK - n_hy)
            qk_ref[:, qk_cols] = (y * jax.nn.sigmoid(y)).astype(qk_ref.dtype)


def _in_proj(x, g, w_conv, w_plain, w_gate, n_gate, conv_w, conv_b, n_hy, seq_len, tm):
    M, K = x.shape
    nc, npl = w_conv.shape[1], w_plain.shape[1]
    hb = tm // HALO
    const = lambda a: pl.BlockSpec(a.shape, lambda i: (0,) * a.ndim, pipeline_mode=pl.Buffered(1))
    row = lambda n: pl.BlockSpec((tm, n), lambda i: (i, 0))
    g = g.reshape(1, K)
    conv_b = conv_b.reshape(1, nc)
    return pl.pallas_call(
        functools.partial(_in_proj_kernel, tiles_per_seq=seq_len // tm, n_hy=n_hy),
        grid=(M // tm,),
        in_specs=[row(K),
                  pl.BlockSpec((HALO, K), lambda i: (jnp.maximum(i * hb - 1, 0), 0)),
                  pl.BlockSpec((HALO, K), lambda i: (jnp.minimum((i + 1) * hb, M // HALO - 1), 0)),
                  const(g), const(w_conv), const(w_plain), const(w_gate), const(conv_w), const(conv_b)],
        out_specs=[row(n_hy + npl), row(nc - n_hy), pl.BlockSpec((n_gate, tm), lambda i: (0, i))],
        out_shape=[jax.ShapeDtypeStruct((M, n_hy + npl), BF16), jax.ShapeDtypeStruct((M, nc - n_hy), BF16),
                   jax.ShapeDtypeStruct((n_gate, M), F32)],
        scratch_shapes=[pltpu.VMEM((tm, K), BF16)],
        compiler_params=_params("parallel"),
        name="in_proj",
    )(x, x, x, g, w_conv, w_plain, w_gate, conv_w, conv_b)


def _filter_kernel(feat_ref, w1_ref, b1_ref, f1_ref, w2_ref, b2_ref, f2_ref, w3h_ref, w3l_ref, dec_ref, dir_ref,
                   o_ref, *, L):
    tl = o_ref.shape[0]
    hp = lax.Precision.HIGHEST
    pos = (lax.broadcasted_iota(jnp.int32, (tl, LANES), 0) + pl.program_id(0) * tl).astype(F32)
    lane = lax.broadcasted_iota(jnp.int32, (tl, LANES), 1)
    t = pos * (1.0 / (L - 1))
    arg = feat_ref[...] * (pos * (2.0 * math.pi / L))
    z = jnp.where(lane < FILTER_BANDS, jnp.cos(arg),
                  jnp.where(lane < 2 * FILTER_BANDS, -jnp.sin(arg),
                            jnp.where(lane == 2 * FILTER_BANDS, t, 0.0)))
    hid = jnp.sin(f1_ref[...] * (jnp.dot(z, w1_ref[...], precision=hp, preferred_element_type=F32) + b1_ref[...]))
    hid = jnp.sin(f2_ref[...] * (jnp.dot(hid, w2_ref[...], precision=hp, preferred_element_type=F32) + b2_ref[...]))
    hid_hi = hid.astype(BF16)
    hid_lo = (hid - hid_hi.astype(F32)).astype(BF16)
    filt = (jnp.dot(hid_hi, w3h_ref[...], preferred_element_type=F32)
            + jnp.dot(hid_hi, w3l_ref[...], preferred_element_type=F32)
            + jnp.dot(hid_lo, w3h_ref[...], preferred_element_type=F32))
    filt = filt * jnp.exp(-t[:, 0:1] * dec_ref[...])
    filt = jnp.where(pos[:, 0:1] == 0.0, filt * dir_ref[...], filt)
    o_ref[...] = filt


def _hyena_filters(L, w1, b1, fr1, w2, b2, fr2, w3, d_hyena, tl=512):
    n_emb, n_hid = w1.shape
    n_out = w3.shape[1]
    bands = jnp.linspace(1e-4, FILTER_BANDS - 1, FILTER_BANDS, dtype=F32)
    feat = jnp.zeros((1, LANES), F32).at[0, :FILTER_BANDS].set(bands).at[0, FILTER_BANDS:2 * FILTER_BANDS].set(bands)
    w1p = jnp.zeros((LANES, n_hid), F32).at[:n_emb - 1].set(w1[1:]).at[n_emb - 1].set(w1[0])
    max_decay = math.log(DECAY_TARGET) / SHORT_DECAY_PCT
    min_decay = math.log(DECAY_TARGET) / LONG_DECAY_PCT
    deltas = jnp.abs(jnp.linspace(min_decay, max_decay, d_hyena, dtype=F32))
    reps = n_out // d_hyena
    dec = jnp.tile(deltas, reps).reshape(1, n_out)
    dirmask = jnp.tile(jnp.concatenate([jnp.ones((d_hyena,), F32), jnp.zeros((d_hyena,), F32)]), reps // 2)
    w3_hi = w3.astype(BF16)
    w3_lo = (w3 - w3_hi.astype(F32)).astype(BF16)
    full = lambda shape: pl.BlockSpec(shape, lambda i: (0,) * len(shape))
    return pl.pallas_call(
        functools.partial(_filter_kernel, L=L),
        grid=(L // tl,),
        in_specs=[full((1, LANES)), full((LANES, n_hid)), full((1, n_hid)), full((1, n_hid)),
                  full((n_hid, n_hid)), full((1, n_hid)), full((1, n_hid)), full((n_hid, n_out)),
                  full((n_hid, n_out)), full((1, n_out)), full((1, n_out))],
        out_specs=pl.BlockSpec((tl, n_out), lambda i: (i, 0)),
        out_shape=jax.ShapeDtypeStruct((L, n_out), F32),
        compiler_params=_params("parallel"),
        name="hyena_filters",
    )(feat, w1p, b1.reshape(1, -1), fr1.reshape(1, -1), w2, b2.reshape(1, -1), fr2.reshape(1, -1), w3_hi, w3_lo,
      dec, dirmask.reshape(1, n_out))


MID_ROWS = 4096
SIDE_PAD = 16


def _fft_tables(L):
    N = 2 * L
    N2 = LANES
    N1 = N // N2
    H1 = N1 // 2
    k1 = np.arange(H1)[:, None]
    n1 = np.arange(H1)[None, :]
    a1 = 2.0 * np.pi * ((k1 * n1) % N1) / N1
    c1, s1 = np.cos(a1), np.sin(a1)
    sign = np.where(np.arange(H1) % 2 == 0, 1.0, -1.0)[None, :]
    pad = np.zeros((SIDE_PAD - 1, H1))
    m1 = np.concatenate([c1, -s1, sign, pad], axis=0)
    wgt = np.where(np.arange(H1) == 0, 1.0, 2.0)[:, None]
    minv = np.concatenate([(wgt * c1).T, (-wgt * s1).T, sign.T, pad.T], axis=1)
    n2 = np.arange(N2)[:, None]
    k2 = np.arange(N2)[None, :]
    a2 = 2.0 * np.pi * ((n2 * k2) % N2) / N2
    cg, sg = np.cos(a2), np.sin(a2)
    g2 = np.block([[cg, -sg], [sg, cg]])
    g2i = np.block([[cg, sg], [-sg, cg]])
    at = 2.0 * np.pi * (np.arange(H1 + 8)[:, None] * np.arange(N2)[None, :]) / N
    tw = np.concatenate([np.cos(at), -np.sin(at)], axis=1)
    as_bf = lambda a: jnp.asarray(a.astype(np.float32)).astype(BF16)
    return dict(m1=as_bf(m1), minv=as_bf(minv), g2=as_bf(g2), g2i=as_bf(g2i),
                tw=jnp.asarray(tw.astype(np.float32)), N1=N1, H1=H1)


S_PITCH = LANES + 8
GROUP = 8
STAGE_UNROLL = 8


SLAB_UNROLL = 4


def _slab_scratch_rows(i):
    return pl.ds(pl.multiple_of(i * S_PITCH, 8), LANES)


def _slab_transpose_one(get_slab, dst_refs, i):
    t = get_slab(i).T
    for dst in dst_refs:
        dst[_slab_scratch_rows(i), :] = t


def _slab_transpose_in(get_slab, dst_refs, h1):
    def body(j, carry):
        for u in range(SLAB_UNROLL):
            _slab_transpose_one(get_slab, dst_refs, j * SLAB_UNROLL + u)
        return carry

    lax.fori_loop(0, h1 // SLAB_UNROLL, body, 0)


def _slab_transpose_out(s_ref, put_slab, h1):
    def body(j, carry):
        for u in range(SLAB_UNROLL):
            i = j * SLAB_UNROLL + u
            put_slab(i, s_ref[_slab_scratch_rows(i), :].T)
        return carry

    lax.fori_loop(0, h1 // SLAB_UNROLL, body, 0)


def _slab_map(fn, h1):
    def body(j, carry):
        for u in range(SLAB_UNROLL):
            fn(_slab_scratch_rows(j * SLAB_UNROLL + u))
        return carry

    lax.fori_loop(0, h1 // SLAB_UNROLL, body, 0)


def _slab_rows(i):
    return pl.ds(pl.multiple_of(i * LANES, LANES), LANES)


def _fft_stage_a(s_ref, nch, m1, tw_ref, ab_ref, side_ref, n1, side_work=None):
    h1 = n1 // 2
    twr = tw_ref[0:h1, 0:LANES]
    twi = tw_ref[0:h1, LANES:2 * LANES]
    tsr = tw_ref[h1:h1 + 1, 0:LANES]
    tsi = tw_ref[h1:h1 + 1, LANES:2 * LANES]

    def body(g, carry):
        for u in range(STAGE_UNROLL):
            c0 = (g * STAGE_UNROLL + u) * GROUP
            rhs = jnp.concatenate([s_ref[pl.ds(c0 + i, h1, stride=S_PITCH), :] for i in range(GROUP)], axis=1)
            res = jnp.dot(m1, rhs.astype(BF16), preferred_element_type=F32)
            for i in range(GROUP):
                ar = res[0:h1, LANES * i:LANES * (i + 1)]
                ai = res[h1:2 * h1, LANES * i:LANES * (i + 1)]
                rows = pl.ds(pl.multiple_of((c0 + i) * h1, h1), h1)
                ab_ref[rows, 0:LANES] = ar * twr - ai * twi
                ab_ref[rows, LANES:2 * LANES] = ar * twi + ai * twr
            a_side = jnp.concatenate([res[2 * h1:2 * h1 + 1, LANES * i:LANES * (i + 1)] for i in range(GROUP)],
                                     axis=0)
            rows = pl.ds(pl.multiple_of(c0, GROUP), GROUP)
            side_ref[rows, 0:LANES] = a_side * tsr
            side_ref[rows, LANES:2 * LANES] = a_side * tsi
        if side_work is not None:
            side_work(g, n_iter)
        return carry

    n_iter = nch // (GROUP * STAGE_UNROLL)
    lax.fori_loop(0, n_iter, body, 0)


def _cmul(a, b):
    ar, ai = a[:, 0:LANES], a[:, LANES:2 * LANES]
    br, bi = b[:, 0:LANES], b[:, LANES:2 * LANES]
    return jnp.concatenate([ar * br - ai * bi, ar * bi + ai * br], axis=1)


def _fft_mid(ab_ref, side_ref, get_h, get_h_side, g2, g2i, nrows):
    chunk = min(MID_ROWS, nrows)

    def body(r, carry):
        rows = pl.ds(pl.multiple_of(r * chunk, chunk), chunk)
        spec = _bdot(ab_ref[rows, :], g2)
        ab_ref[rows, :] = _bdot(_cmul(spec, get_h(rows)), g2i)
        return carry

    lax.fori_loop(0, nrows // chunk, body, 0)
    side_ref[...] = _bdot(_cmul(_bdot(side_ref[...], g2), get_h_side()), g2i)


def _fft_stage_a_inv(ab_ref, side_ref, nch, minv, tw_ref, s_ref, n1, side_work=None):
    h1 = n1 // 2
    twr = tw_ref[0:h1, 0:LANES]
    twi = tw_ref[0:h1, LANES:2 * LANES]
    tsr = tw_ref[h1:h1 + 1, 0:LANES]
    tsi = tw_ref[h1:h1 + 1, LANES:2 * LANES]
    first_row = lax.broadcasted_iota(jnp.int32, (SIDE_PAD, GROUP * LANES), 0) == 0

    def body(g, carry):
        for u in range(STAGE_UNROLL):
            c0 = (g * STAGE_UNROLL + u) * GROUP
            blk = ab_ref[pl.ds(pl.multiple_of(c0 * h1, GROUP * h1), GROUP * h1), :]
            re, im = [], []
            for i in range(GROUP):
                br = blk[i * h1:(i + 1) * h1, 0:LANES]
                bi = blk[i * h1:(i + 1) * h1, LANES:2 * LANES]
                re.append(br * twr + bi * twi)
                im.append(bi * twr - br * twi)
            side = side_ref[pl.ds(pl.multiple_of(c0, GROUP), GROUP), :]
            side_re = side[:, 0:LANES] * tsr + side[:, LANES:2 * LANES] * tsi
            side_row = jnp.concatenate([side_re[i:i + 1, :] for i in range(GROUP)], axis=1)
            side_blk = jnp.where(first_row, jnp.broadcast_to(side_row, first_row.shape), 0.0)
            rhs = jnp.concatenate([jnp.concatenate(re, axis=1), jnp.concatenate(im, axis=1), side_blk], axis=0)
            y = jnp.dot(minv, rhs.astype(BF16), preferred_element_type=F32)
            for i in range(GROUP):
                s_ref[pl.ds(c0 + i, h1, stride=S_PITCH), :] = y[:, LANES * i:LANES * (i + 1)]
        if side_work is not None:
            side_work(g, n_iter)
        return carry

    n_iter = nch // (GROUP * STAGE_UNROLL)
    lax.fori_loop(0, n_iter, body, 0)


def _filter_spectrum_kernel(hf_ref, hb_ref, m1_ref, g2_ref, tw_ref, o_ref, oside_ref,
                            s_ref, s2_ref, af_ref, ab_ref, afs_ref, abs_ref, *, n1, inv_n):
    nch = hf_ref.shape[1]
    h1 = n1 // 2
    m1, g2 = m1_ref[...], g2_ref[...]

    def transpose_backward(g, n_iter):
        per_iter = h1 // n_iter
        for u in range(per_iter):
            _slab_transpose_one(lambda i: hb_ref[_slab_rows(i), :], [s2_ref], g * per_iter + u)

    _slab_transpose_in(lambda i: hf_ref[_slab_rows(i), :], [s_ref], h1)
    _fft_stage_a(s_ref, nch, m1, tw_ref, af_ref, afs_ref, n1, side_work=transpose_backward)
    _fft_stage_a(s2_ref, nch, m1, tw_ref, ab_ref, abs_ref, n1)

    def two_sided(fwd, bwd):
        sf = _bdot(fwd, g2)
        sb = _bdot(bwd, g2)
        return jnp.concatenate([sf[:, 0:LANES] + sb[:, 0:LANES], sf[:, LANES:] - sb[:, LANES:]], axis=1) * inv_n

    chunk = min(MID_ROWS, nch * h1)

    def body(r, carry):
        rows = pl.ds(pl.multiple_of(r * chunk, chunk), chunk)
        o_ref[rows, :] = two_sided(af_ref[rows, :], ab_ref[rows, :])
        return carry

    lax.fori_loop(0, (nch * h1) // chunk, body, 0)
    oside_ref[...] = two_sided(afs_ref[...], abs_ref[...])


def _filter_spectrum(filt, L, d_hyena, tabs):
    h1 = tabs["H1"]
    order = filt.shape[1] // (2 * d_hyena)
    cb = LANES
    nblk = d_hyena // cb
    full = lambda a: pl.BlockSpec(a.shape, lambda o, c: (0,) * a.ndim)
    main = pltpu.VMEM((cb * h1, 2 * LANES), F32)
    side = pltpu.VMEM((cb, 2 * LANES), F32)
    return pl.pallas_call(
        functools.partial(_filter_spectrum_kernel, n1=tabs["N1"], inv_n=1.0 / (2 * L)),
        grid=(order, nblk),
        in_specs=[pl.BlockSpec((L, cb), lambda o, c: (0, o * 2 * nblk + c)),
                  pl.BlockSpec((L, cb), lambda o, c: (0, o * 2 * nblk + nblk + c)),
                  full(tabs["m1"]), full(tabs["g2"]), full(tabs["tw"])],
        out_specs=[pl.BlockSpec((cb * h1, 2 * LANES), lambda o, c: (o * nblk + c, 0)),
                   pl.BlockSpec((cb, 2 * LANES), lambda o, c: (o * nblk + c, 0))],
        out_shape=[jax.ShapeDtypeStruct((order * d_hyena * h1, 2 * LANES), F32),
                   jax.ShapeDtypeStruct((order * d_hyena, 2 * LANES), F32)],
        scratch_shapes=[pltpu.VMEM((h1 * S_PITCH, LANES), F32), pltpu.VMEM((h1 * S_PITCH, LANES), F32),
                        main, main, side, side],
        compiler_params=_params("parallel", "parallel"),
        name="filter_spectrum",
    )(filt, filt, tabs["m1"], tabs["g2"], tabs["tw"])


def _hyena_kernel(v_ref, vnext_ref, x1_ref, x2_ref, h_ref, hside_ref, d_ref, m1_ref, minv_ref, g2_ref, g2i_ref,
                  tw_ref, o_ref, s_ref, sv_ref, sn_ref, sx_ref, sz_ref, ab_ref, side_ref, *, n1):
    nch = v_ref.shape[2]
    h1 = n1 // 2
    m1, minv, g2, g2i = m1_ref[...], minv_ref[...], g2_ref[...], g2i_ref[...]

    def slab_of(p_ref):
        return lambda i: p_ref[0, _slab_rows(i), :].astype(F32)

    def transposer(p_ref, dst_ref):
        def work(g, n_iter):
            per_iter = h1 // n_iter
            for u in range(per_iter):
                _slab_transpose_one(slab_of(p_ref), [dst_ref], g * per_iter + u)
        return work

    def long_conv(order, src_ref, gate_ref, skip_ref, keep_ref=None, inv_side_work=None):
        _fft_stage_a(src_ref, nch, m1, tw_ref, ab_ref, side_ref, n1, side_work=transposer(gate_ref, sx_ref))
        _fft_mid(ab_ref, side_ref, lambda rows: h_ref[order, rows, :], lambda: hside_ref[order], g2, g2i, nch * h1)
        _fft_stage_a_inv(ab_ref, side_ref, nch, minv, tw_ref, s_ref, n1, side_work=inv_side_work)
        d_slab = jnp.broadcast_to(d_ref[:, order:order + 1], (nch, LANES))

        def gate(rows):
            val = sx_ref[rows, :] * (s_ref[rows, :] + d_slab * skip_ref[rows, :])
            s_ref[rows, :] = val
            if keep_ref is not None:
                keep_ref[rows, :] = val

        _slab_map(gate, h1)

    first = pl.program_id(1) == 0

    @pl.when(first)
    def _():
        _slab_transpose_in(slab_of(v_ref), [sv_ref], h1)

    @pl.when(jnp.logical_not(first))
    def _():
        def copy(rows):
            sv_ref[rows, :] = sn_ref[rows, :]

        _slab_map(copy, h1)

    long_conv(0, sv_ref, x1_ref, sv_ref, keep_ref=sz_ref, inv_side_work=transposer(vnext_ref, sn_ref))
    long_conv(1, s_ref, x2_ref, sz_ref)

    def put(i, out):
        o_ref[0, _slab_rows(i), :] = out.astype(o_ref.dtype)

    _slab_transpose_out(s_ref, put, h1)


def _hyena(proj, spec, spec_side, skip, tabs):
    B, L, _ = proj.shape
    D = skip.shape[1]
    n1, h1 = tabs["N1"], tabs["H1"]
    cb = LANES
    nblk = D // cb
    spec = spec.reshape(2, D * h1, 2 * LANES)
    spec_side = spec_side.reshape(2, D, 2 * LANES)
    col = lambda part: pl.BlockSpec((1, L, cb), lambda c, b: (b, 0, part * nblk + c))
    full = lambda a: pl.BlockSpec(a.shape, lambda c, b: (0,) * a.ndim)
    slabs = pltpu.VMEM((h1 * S_PITCH, LANES), F32)
    return pl.pallas_call(
        functools.partial(_hyena_kernel, n1=n1),
        grid=(nblk, B),
        in_specs=[col(0),
                  pl.BlockSpec((1, L, cb), lambda c, b: (jnp.minimum(b + 1, B - 1), 0, c)),
                  col(1), col(2),
                  pl.BlockSpec((2, cb * h1, 2 * LANES), lambda c, b: (0, c, 0), pipeline_mode=pl.Buffered(1)),
                  pl.BlockSpec((2, cb, 2 * LANES), lambda c, b: (0, c, 0)),
                  pl.BlockSpec((cb, 2), lambda c, b: (c, 0)),
                  full(tabs["m1"]), full(tabs["minv"]), full(tabs["g2"]), full(tabs["g2i"]), full(tabs["tw"])],
        out_specs=pl.BlockSpec((1, L, cb), lambda c, b: (b, 0, c)),
        out_shape=jax.ShapeDtypeStruct((B, L, D), BF16),
        scratch_shapes=[slabs, slabs, slabs, slabs, slabs, pltpu.VMEM((cb * h1, 2 * LANES), F32),
                        pltpu.VMEM((cb, 2 * LANES), F32)],
        compiler_params=_params("parallel", "arbitrary"),
        name="hyena_fftconv",
    )(proj, proj, proj, proj, spec, spec_side, skip.astype(F32).T,
      tabs["m1"], tabs["minv"], tabs["g2"], tabs["g2i"], tabs["tw"])


ML_CHUNK = 512
ML_CHUNKS_PER_STEP = 1


def _log_sigmoid(x):
    return jnp.minimum(x, 0.0) - jnp.log1p(jnp.exp(-jnp.abs(x)))


def _mlstm_direction(q, k, v, i_row, cum_row, logf_row, c_ref, m_ref, mask, scale):
    c, dh = q.shape
    g_row = i_row - cum_row
    f_col_rep = jnp.broadcast_to(cum_row, (LANES, c)).T
    m_prev = m_ref[0:1, 0:1]
    e = jnp.where(mask, g_row, -jnp.inf)
    a = jnp.maximum(jnp.max(e, axis=1, keepdims=True), m_prev)
    w = jnp.exp(e - a)
    sc = lax.dot_general(q.astype(BF16), k.astype(BF16), (((1,), (1,)), ((), ())), preferred_element_type=F32)
    p = sc * (w * scale)
    vaug = jnp.concatenate([v, jnp.ones_like(v)], axis=1)
    inter = jnp.exp(m_prev - a)
    tot = _bdot(p, vaug) + inter * _bdot(q, c_ref[...])
    num, den = tot[:, 0:dh], tot[:, dh:2 * dh]
    m_t = f_col_rep + a
    h = num / jnp.maximum(jnp.abs(den), jnp.exp(-m_t))
    a_end = jnp.maximum(jnp.max(g_row, axis=1, keepdims=True), m_prev)
    f_end = jnp.sum(logf_row, axis=1, keepdims=True)
    we = jnp.exp(g_row - a_end) * scale
    kw = k.astype(F32).T * we
    c_ref[...] = jnp.exp(m_prev - a_end) * c_ref[...] + _bdot(kw, vaug)
    m_ref[...] = jnp.broadcast_to(f_end + a_end, m_ref.shape)
    return h


def _mlstm_kernel(qf_ref, kf_ref, vf_ref, gf_ref, qb_ref, kb_ref, vb_ref, gb_ref, bias_ref,
                  hf_ref, hb_ref, c_ref, m_ref, *, heads, scale, chunk):
    @pl.when(pl.program_id(1) == 0)
    def _():
        c_ref[...] = jnp.zeros_like(c_ref)
        m_ref[...] = jnp.zeros_like(m_ref)

    c = chunk
    nsub = qf_ref.shape[1] // c
    dh = qf_ref.shape[2] // heads
    r = lax.broadcasted_iota(jnp.int32, (c, c), 0)
    s = lax.broadcasted_iota(jnp.int32, (c, c), 1)
    mask_f = s <= r
    mask_b = s >= r

    def gate_rows(g, mask_t):
        g = g + bias_ref[...]
        logf = _log_sigmoid(g)
        tri = mask_t.astype(BF16)
        cum = sum(jnp.dot(part, tri, preferred_element_type=F32) for part in _split3(logf))
        return g, logf, cum

    for u in range(nsub):
        rf = slice(u * c, (u + 1) * c)
        rb = slice((nsub - 1 - u) * c, (nsub - u) * c)
        g_f, logf_f, cum_f = gate_rows(gf_ref[:, rf], mask_b)
        g_b, logf_b, cum_b = gate_rows(gb_ref[:, rb], mask_f)
        for h in range(heads):
            sl = slice(h * dh, (h + 1) * dh)
            fi, ff, bi, bf = h, heads + h, 2 * heads + h, 3 * heads + h
            hf_ref[0, rf, sl] = _mlstm_direction(
                qf_ref[0, rf, sl], kf_ref[0, rf, sl], vf_ref[0, rf, sl], g_f[fi:fi + 1], cum_f[ff:ff + 1],
                logf_f[ff:ff + 1], c_ref.at[h], m_ref.at[h], mask_f, scale).astype(hf_ref.dtype)
            hb_ref[0, rb, sl] = _mlstm_direction(
                qb_ref[0, rb, sl], kb_ref[0, rb, sl], vb_ref[0, rb, sl], g_b[bi:bi + 1], cum_b[bf:bf + 1],
                logf_b[bf:bf + 1], c_ref.at[heads + h], m_ref.at[heads + h], mask_b, scale).astype(hb_ref.dtype)


def _mlstm(qk, proj, v_col0, gates, gate_b, d_ml):
    B, L, _ = qk.shape
    H = MLSTM_HEADS
    dh = d_ml // H
    c = min(ML_CHUNK, L)
    rows = min(ML_CHUNKS_PER_STEP * c, L)
    nb = L // rows
    bias = jnp.broadcast_to(gate_b.astype(F32)[:, None], (4 * H, c))
    vb0 = v_col0 // d_ml
    specs = []
    for pos in (lambda ci: ci, lambda ci: nb - 1 - ci):
        specs += [pl.BlockSpec((1, rows, d_ml), lambda bi, ci, pos=pos: (bi, pos(ci), 0)),
                  pl.BlockSpec((1, rows, d_ml), lambda bi, ci, pos=pos: (bi, pos(ci), 1)),
                  pl.BlockSpec((1, rows, d_ml), lambda bi, ci, pos=pos: (bi, pos(ci), vb0)),
                  pl.BlockSpec((4 * H, rows), lambda bi, ci, pos=pos: (0, bi * nb + pos(ci)))]
    specs.append(pl.BlockSpec((4 * H, c), lambda bi, ci: (0, 0)))
    out_shape = jax.ShapeDtypeStruct((B, L, d_ml), BF16)
    return pl.pallas_call(
        functools.partial(_mlstm_kernel, heads=H, scale=dh ** -0.5, chunk=c),
        grid=(B, nb),
        in_specs=specs,
        out_specs=[pl.BlockSpec((1, rows, d_ml), lambda bi, ci: (bi, ci, 0)),
                   pl.BlockSpec((1, rows, d_ml), lambda bi, ci: (bi, nb - 1 - ci, 0))],
        out_shape=[out_shape, out_shape],
        scratch_shapes=[pltpu.VMEM((2 * H, dh, 2 * dh), F32), pltpu.VMEM((2 * H, 1, LANES), F32)],
        compiler_params=_params("parallel", "arbitrary"),
        name="mlstm_scan",
    )(qk, qk, proj, gates, qk, qk, proj, gates, bias)


def _group_rms(y, gain, bd, group):
    sq = (y * y).astype(BF16)
    w = bd.shape[0]
    ss = jnp.concatenate([jnp.dot(sq[:, i:i + w], bd, preferred_element_type=F32)
                          for i in range(0, y.shape[1], w)], axis=1)
    return y * lax.rsqrt(ss * (1.0 / group) + EPS) * gain


GN_STRIPE = 256


def _block_diag_ones(n, group):
    idx = np.arange(n) // group
    return jnp.asarray((idx[:, None] == idx[None, :]).astype(np.float32)).astype(BF16)


def _mix_xattn_kernel(z_ref, hf_ref, hb_ref, o_ref, x_ref, mem_ref, ghy_ref, gml_ref, why_ref, wml_ref, bdh_ref,
                      bdm_ref, gx_ref, gm_ref, wq_ref, wkv_ref, wo_ref, out_ref, kv_ref,
                      *, hy_group, ml_group, heads, scale):
    @pl.when(pl.program_id(1) == 0)
    def _():
        kv_ref[...] = _bdot(_rms(mem_ref[0], gm_ref[...]), wkv_ref[...]).astype(kv_ref.dtype)

    y_hy = _group_rms(z_ref[0].astype(F32), ghy_ref[...], bdh_ref[...], hy_group)
    h_sum = hf_ref[0].astype(F32) + hb_ref[0].astype(F32)
    y_ml = _group_rms(jax.nn.sigmoid(o_ref[0].astype(F32)) * h_sum, gml_ref[...], bdm_ref[...], ml_group)
    h1 = x_ref[0] + _bdot(y_hy, why_ref[...]) + _bdot(y_ml, wml_ref[...])

    D = h1.shape[1]
    dh = D // heads
    q = _bdot(_rms(h1, gx_ref[...]), wq_ref[...])
    outs = []
    for hd in range(heads):
        qh = q[:, hd * dh:(hd + 1) * dh]
        kh = kv_ref[:, hd * dh:(hd + 1) * dh]
        vh = kv_ref[:, D + hd * dh:D + (hd + 1) * dh]
        s = lax.dot_general(qh.astype(BF16), kh, (((1,), (1,)), ((), ())), preferred_element_type=F32) * scale
        e = jnp.exp(s - jnp.max(s, axis=-1, keepdims=True))
        p = e / jnp.sum(e, axis=-1, keepdims=True)
        outs.append(_bdot(p, vh))
    out_ref[0] = h1 + _bdot(jnp.concatenate(outs, axis=1), wo_ref[...])


def _mix_xattn(z, hf, hb, proj, o_col0, x, mem, g_hy, g_ml, w_out, g_x, g_mem, wq, wk, wv, wo, tm=512):
    B, L, d_hy = z.shape
    d_ml = hf.shape[2]
    D = x.shape[2]
    nm = mem.shape[1]
    ob = o_col0 // d_ml
    row = lambda n: pl.BlockSpec((1, tm, n), lambda bi, i: (bi, i, 0))
    const = lambda a: pl.BlockSpec(a.shape, lambda bi, i: (0,) * a.ndim, pipeline_mode=pl.Buffered(1))
    hy_group, ml_group = d_hy // HYENA_GROUPS, d_ml // MLSTM_HEADS
    consts = [g_hy.reshape(1, -1), g_ml.reshape(1, -1), w_out[:d_hy].astype(BF16), w_out[d_hy:].astype(BF16),
              _block_diag_ones(GN_STRIPE, hy_group), _block_diag_ones(GN_STRIPE, ml_group), g_x.reshape(1, D),
              g_mem.reshape(1, D), wq.astype(BF16), jnp.concatenate([wk, wv], axis=1).astype(BF16), wo.astype(BF16)]
    return pl.pallas_call(
        functools.partial(_mix_xattn_kernel, hy_group=hy_group, ml_group=ml_group, heads=XATTN_HEADS,
                          scale=(D // XATTN_HEADS) ** -0.5),
        grid=(B, L // tm),
        in_specs=[row(d_hy), row(d_ml), row(d_ml), pl.BlockSpec((1, tm, d_ml), lambda bi, i: (bi, i, ob)), row(D),
                  pl.BlockSpec((1, nm, D), lambda bi, i: (bi, 0, 0))] + [const(a) for a in consts],
        out_specs=row(D),
        out_shape=jax.ShapeDtypeStruct((B, L, D), F32),
        scratch_shapes=[pltpu.VMEM((nm, 2 * D), BF16)],
        compiler_params=_params("parallel", "arbitrary"),
        name="mix_xattn",
    )(z, hf, hb, proj, x, mem, *consts)


MLP_CHUNK = 1024


def _mlp_kernel(h_ref, g_ref, w1_ref, w2_ref, gf_ref, out_ref, xn_ref, *, final_norm):
    j = pl.program_id(1)

    @pl.when(j == 0)
    def _():
        x = h_ref[...]
        xn_ref[...] = _rms(x, g_ref[...]).astype(BF16)
        out_ref[...] = x

    for c in range(w1_ref.shape[1] // MLP_CHUNK):
        cols = slice(c * MLP_CHUNK, (c + 1) * MLP_CHUNK)
        a = jnp.maximum(jnp.dot(xn_ref[...], w1_ref[:, cols], preferred_element_type=F32), 0.0)
        out_ref[...] += _bdot(a * a, w2_ref[cols, :])

    if final_norm:
        @pl.when(j == pl.num_programs(1) - 1)
        def _():
            out_ref[...] = _rms(out_ref[...], gf_ref[...])


def _mlp(h, g, w1, w2, gf, final_norm, tm=1024, tf=2048):
    M, D = h.shape
    dff = w1.shape[1]
    return pl.pallas_call(
        functools.partial(_mlp_kernel, final_norm=final_norm),
        grid=(M // tm, dff // tf),
        in_specs=[pl.BlockSpec((tm, D), lambda i, j: (i, 0)), pl.BlockSpec((1, D), lambda i, j: (0, 0)),
                  pl.BlockSpec((D, tf), lambda i, j: (0, j)), pl.BlockSpec((tf, D), lambda i, j: (j, 0)),
                  pl.BlockSpec((1, D), lambda i, j: (0, 0))],
        out_specs=pl.BlockSpec((tm, D), lambda i, j: (i, 0)),
        out_shape=jax.ShapeDtypeStruct((M, D), F32),
        scratch_shapes=[pltpu.VMEM((tm, D), BF16)],
        compiler_params=_params("parallel", "arbitrary"),
        name="mlp",
    )(h, g.reshape(1, D), w1.astype(BF16), w2.astype(BF16), gf.reshape(1, D))


def _pad_cols(w, n):
    return jnp.pad(w, ((0, 0), (0, n - w.shape[1])))


def _layer(h, mem, p, l):
    B, L, D = h.shape
    M = B * L
    d_hy = p["hy_norm_g"].shape[1]
    d_ml = p["ml_norm_g"].shape[1]
    hy_cols = 3 * d_hy
    qk_cols = 2 * d_ml
    n_gate = 4 * MLSTM_HEADS
    w_in = p["w_in"][l]
    vo0 = hy_cols + qk_cols
    g0 = vo0 + 2 * d_ml
    v_col0 = hy_cols
    o_col0 = hy_cols + d_ml
    n_in = hy_cols + 2 * d_ml
    proj, qk, gates = _in_proj(h.reshape(M, D), p["norm_mix_g"][l], w_in[:, :vo0].astype(BF16),
                               w_in[:, vo0:g0].astype(BF16),
                               _pad_cols(w_in[:, g0:g0 + n_gate], LANES).astype(BF16), n_gate,
                               jnp.concatenate([p["hy_conv_w"][l], p["ml_conv_w"][l]], axis=1),
                               jnp.concatenate([p["hy_conv_b"][l], p["ml_conv_b"][l]]),
                               n_hy=hy_cols, seq_len=L, tm=min(1024, L))
    proj = proj.reshape(B, L, n_in)
    qk = qk.reshape(B, L, qk_cols)

    tabs = _fft_tables(L)
    filt = _hyena_filters(L, p["hy_filt_w1"][l], p["hy_filt_b1"][l], p["hy_filt_freq1"][l], p["hy_filt_w2"][l],
                          p["hy_filt_b2"][l], p["hy_filt_freq2"][l], p["hy_filt_w3"][l], d_hy,
                          tl=min(512, L))
    spec, spec_side = _filter_spectrum(filt, L, d_hy, tabs)
    z_hy = _hyena(proj, spec, spec_side, p["hy_skip"][l], tabs)

    h_f, h_b = _mlstm(qk, proj, v_col0, gates, p["ml_gate_b"][l], d_ml)

    return _mix_xattn(z_hy, h_f, h_b, proj, o_col0, h, mem, p["hy_norm_g"][l], p["ml_norm_g"][l], p["w_out"][l],
                      p["norm_x_g"][l], p["norm_mem_g"][l], p["xa_wq"][l], p["xa_wk"][l], p["xa_wv"][l],
                      p["xa_wo"][l], tm=min(1024, L))


def kernel(x, mem, norm_mix_g, w_in, hy_conv_w, hy_conv_b, hy_filt_w1, hy_filt_b1, hy_filt_freq1, hy_filt_w2,
           hy_filt_b2, hy_filt_freq2, hy_filt_w3, hy_skip, hy_norm_g, ml_conv_w, ml_conv_b, ml_gate_b, ml_norm_g,
           w_out, norm_x_g, norm_mem_g, xa_wq, xa_wk, xa_wv, xa_wo, norm_ff_g, ff_w1, ff_w2, final_norm_g):
    p = dict(norm_mix_g=norm_mix_g, w_in=w_in, hy_conv_w=hy_conv_w, hy_conv_b=hy_conv_b, hy_filt_w1=hy_filt_w1,
             hy_filt_b1=hy_filt_b1, hy_filt_freq1=hy_filt_freq1, hy_filt_w2=hy_filt_w2, hy_filt_b2=hy_filt_b2,
             hy_filt_freq2=hy_filt_freq2, hy_filt_w3=hy_filt_w3, hy_skip=hy_skip, hy_norm_g=hy_norm_g,
             ml_conv_w=ml_conv_w, ml_conv_b=ml_conv_b, ml_gate_b=ml_gate_b, ml_norm_g=ml_norm_g, w_out=w_out,
             norm_x_g=norm_x_g, norm_mem_g=norm_mem_g, xa_wq=xa_wq, xa_wk=xa_wk, xa_wv=xa_wv, xa_wo=xa_wo)
    B, L, D = x.shape
    depth = w_in.shape[0]
    h = x
    for l in range(depth):
        h = _layer(h, mem, p, l)
        h = _mlp(h.reshape(B * L, D), norm_ff_g[l], ff_w1[l], ff_w2[l], final_norm_g, final_norm=l == depth - 1,
                 tm=min(1024, B * L)).reshape(B, L, D)
    return h
```

```python
import functools
import math

import numpy as np
import jax
import jax.numpy as jnp
from jax import lax
from jax.experimental import pallas as pl
from jax.experimental.pallas import tpu as pltpu

F32 = jnp.float32
BF16 = jnp.bfloat16

EPS = 1e-6
HYENA_GROUPS = 8
MLSTM_HEADS = 4
XATTN_HEADS = 4
FILTER_BANDS = 16
DECAY_TARGET = 1e-2
SHORT_DECAY_PCT = 0.3
LONG_DECAY_PCT = 1.5

LANES = 128
VMEM_LIMIT = 56 * 1024 * 1024


def _params(*sem):
    return pltpu.CompilerParams(dimension_semantics=sem, vmem_limit_bytes=VMEM_LIMIT)


def _bdot(a, b):
    return jnp.dot(a.astype(BF16), b.astype(BF16), preferred_element_type=F32)


def _split3(x):
    hi = x.astype(BF16)
    r = x - hi.astype(F32)
    mid = r.astype(BF16)
    lo = (r - mid.astype(F32)).astype(BF16)
    return hi, mid, lo


IN_CHUNK = 512
HALO = 8


def _rms(x, g):
    return x * lax.rsqrt(jnp.mean(x * x, axis=-1, keepdims=True) + EPS) * g


def _in_proj_kernel(x_ref, xb_ref, xa_ref, g_ref, wc_ref, wp_ref, wg_ref, cw_ref, cb_ref,
                    proj_ref, qk_ref, og_ref, xn_ref, *, tiles_per_seq, n_hy):
    tm = x_ref.shape[0]
    g = g_ref[...]
    x = x_ref[...]
    xn_ref[...] = (x * g).astype(BF16)
    r = lax.rsqrt(jnp.mean(x * x, axis=-1, keepdims=True) + EPS)
    og_ref[...] = (jnp.dot(xn_ref[...], wg_ref[...], preferred_element_type=F32) * r).T[0:og_ref.shape[0], :]
    for c in range(wp_ref.shape[1] // IN_CHUNK):
        cols = slice(c * IN_CHUNK, (c + 1) * IN_CHUNK)
        out_cols = slice(n_hy + c * IN_CHUNK, n_hy + (c + 1) * IN_CHUNK)
        proj_ref[:, out_cols] = (jnp.dot(xn_ref[...], wp_ref[:, cols], preferred_element_type=F32) * r
                                 ).astype(proj_ref.dtype)

    halo = _rms(jnp.concatenate([xb_ref[...], xa_ref[...]], axis=0), g).astype(BF16)
    pos = pl.program_id(0) % tiles_per_seq
    row = lax.broadcasted_iota(jnp.int32, (tm, IN_CHUNK), 0)
    for c in range(wc_ref.shape[1] // IN_CHUNK):
        cols = slice(c * IN_CHUNK, (c + 1) * IN_CHUNK)
        u = jnp.dot(xn_ref[...], wc_ref[:, cols], preferred_element_type=F32) * r
        uh = jnp.dot(halo, wc_ref[:, cols], preferred_element_type=F32)
        before = jnp.where(pos == 0, 0.0, uh[HALO - 1:HALO, :])
        after = jnp.where(pos == tiles_per_seq - 1, 0.0, uh[HALO:HALO + 1, :])
        prev = jnp.where(row == 0, before, pltpu.roll(u, 1, 0))
        nxt = jnp.where(row == tm - 1, after, pltpu.roll(u, tm - 1, 0))
        y = prev * cw_ref[0:1, cols] + u * cw_ref[1:2, cols] + nxt * cw_ref[2:3, cols] + cb_ref[:, cols]
        if c * IN_CHUNK < n_hy:
            proj_ref[:, cols] = y.astype(proj_ref.dtype)
        else:
            qk_cols = slice(c * IN_CHUNK - n_hy, (c + 1) * IN_CHUNK - n_hy)
            qk_ref[:, qk_cols] = (y * jax.nn.sigmoid(y)).astype(qk_ref.dtype)


def _in_proj(x, g, w_conv, w_plain, w_gate, n_gate, conv_w, conv_b, n_hy, seq_len, tm):
    M, K = x.shape
    nc, npl = w_conv.shape[1], w_plain.shape[1]
    hb = tm // HALO
    const = lambda a: pl.BlockSpec(a.shape, lambda i: (0,) * a.ndim, pipeline_mode=pl.Buffered(1))
    row = lambda n: pl.BlockSpec((tm, n), lambda i: (i, 0))
    g = g.reshape(1, K)
    conv_b = conv_b.reshape(1, nc)
    return pl.pallas_call(
        functools.partial(_in_proj_kernel, tiles_per_seq=seq_len // tm, n_hy=n_hy),
        grid=(M // tm,),
        in_specs=[row(K),
                  pl.BlockSpec((HALO, K), lambda i: (jnp.maximum(i * hb - 1, 0), 0)),
                  pl.BlockSpec((HALO, K), lambda i: (jnp.minimum((i + 1) * hb, M // HALO - 1), 0)),
                  const(g), const(w_conv), const(w_plain), const(w_gate), const(conv_w), const(conv_b)],
        out_specs=[row(n_hy + npl), row(nc - n_hy), pl.BlockSpec((n_gate, tm), lambda i: (0, i))],
        out_shape=[jax.ShapeDtypeStruct((M, n_hy + npl), BF16), jax.ShapeDtypeStruct((M, nc - n_hy), BF16),
                   jax.ShapeDtypeStruct((n_gate, M), F32)],
        scratch_shapes=[pltpu.VMEM((tm, K), BF16)],
        compiler_params=_params("parallel"),
        name="in_proj",
    )(x, x, x, g, w_conv, w_plain, w_gate, conv_w, conv_b)


def _filter_kernel(feat_ref, w1_ref, b1_ref, f1_ref, w2_ref, b2_ref, f2_ref, w3h_ref, w3l_ref, dec_ref, dir_ref,
                   o_ref, *, L):
    tl = o_ref.shape[0]
    hp = lax.Precision.HIGHEST
    pos = (lax.broadcasted_iota(jnp.int32, (tl, LANES), 0) + pl.program_id(0) * tl).astype(F32)
    lane = lax.broadcasted_iota(jnp.int32, (tl, LANES), 1)
    t = pos * (1.0 / (L - 1))
    arg = feat_ref[...] * (pos * (2.0 * math.pi / L))
    z = jnp.where(lane < FILTER_BANDS, jnp.cos(arg),
                  jnp.where(lane < 2 * FILTER_BANDS, -jnp.sin(arg),
                            jnp.where(lane == 2 * FILTER_BANDS, t, 0.0)))
    hid = jnp.sin(f1_ref[...] * (jnp.dot(z, w1_ref[...], precision=hp, preferred_element_type=F32) + b1_ref[...]))
    hid = jnp.sin(f2_ref[...] * (jnp.dot(hid, w2_ref[...], precision=hp, preferred_element_type=F32) + b2_ref[...]))
    hid_hi = hid.astype(BF16)
    hid_lo = (hid - hid_hi.astype(F32)).astype(BF16)
    filt = (jnp.dot(hid_hi, w3h_ref[...], preferred_element_type=F32)
            + jnp.dot(hid_hi, w3l_ref[...], preferred_element_type=F32)
            + jnp.dot(hid_lo, w3h_ref[...], preferred_element_type=F32))
    filt = filt * jnp.exp(-t[:, 0:1] * dec_ref[...])
    filt = jnp.where(pos[:, 0:1] == 0.0, filt * dir_ref[...], filt)
    o_ref[...] = filt


def _hyena_filters(L, w1, b1, fr1, w2, b2, fr2, w3, d_hyena, tl=512):
    n_emb, n_hid = w1.shape
    n_out = w3.shape[1]
    bands = jnp.linspace(1e-4, FILTER_BANDS - 1, FILTER_BANDS, dtype=F32)
    feat = jnp.zeros((1, LANES), F32).at[0, :FILTER_BANDS].set(bands).at[0, FILTER_BANDS:2 * FILTER_BANDS].set(bands)
    w1p = jnp.zeros((LANES, n_hid), F32).at[:n_emb - 1].set(w1[1:]).at[n_emb - 1].set(w1[0])
    max_decay = math.log(DECAY_TARGET) / SHORT_DECAY_PCT
    min_decay = math.log(DECAY_TARGET) / LONG_DECAY_PCT
    deltas = jnp.abs(jnp.linspace(min_decay, max_decay, d_hyena, dtype=F32))
    reps = n_out // d_hyena
    dec = jnp.tile(deltas, reps).reshape(1, n_out)
    dirmask = jnp.tile(jnp.concatenate([jnp.ones((d_hyena,), F32), jnp.zeros((d_hyena,), F32)]), reps // 2)
    w3_hi = w3.astype(BF16)
    w3_lo = (w3 - w3_hi.astype(F32)).astype(BF16)
    full = lambda shape: pl.BlockSpec(shape, lambda i: (0,) * len(shape))
    return pl.pallas_call(
        functools.partial(_filter_kernel, L=L),
        grid=(L // tl,),
        in_specs=[full((1, LANES)), full((LANES, n_hid)), full((1, n_hid)), full((1, n_hid)),
                  full((n_hid, n_hid)), full((1, n_hid)), full((1, n_hid)), full((n_hid, n_out)),
                  full((n_hid, n_out)), full((1, n_out)), full((1, n_out))],
        out_specs=pl.BlockSpec((tl, n_out), lambda i: (i, 0)),
        out_shape=jax.ShapeDtypeStruct((L, n_out), F32),
        compiler_params=_params("parallel"),
        name="hyena_filters",
    )(feat, w1p, b1.reshape(1, -1), fr1.reshape(1, -1), w2, b2.reshape(1, -1), fr2.reshape(1, -1), w3_hi, w3_lo,
      dec, dirmask.reshape(1, n_out))


MID_ROWS = 4096
SIDE_PAD = 16


def _fft_tables(L):
    N = 2 * L
    N2 = LANES
    N1 = N // N2
    H1 = N1 // 2
    k1 = np.arange(H1)[:, None]
    n1 = np.arange(H1)[None, :]
    a1 = 2.0 * np.pi * ((k1 * n1) % N1) / N1
    c1, s1 = np.cos(a1), np.sin(a1)
    sign = np.where(np.arange(H1) % 2 == 0, 1.0, -1.0)[None, :]
    pad = np.zeros((SIDE_PAD - 1, H1))
    m1 = np.concatenate([c1, -s1, sign, pad], axis=0)
    wgt = np.where(np.arange(H1) == 0, 1.0, 2.0)[:, None]
    minv = np.concatenate([(wgt * c1).T, (-wgt * s1).T, sign.T, pad.T], axis=1)
    n2 = np.arange(N2)[:, None]
    k2 = np.arange(N2)[None, :]
    a2 = 2.0 * np.pi * ((n2 * k2) % N2) / N2
    cg, sg = np.cos(a2), np.sin(a2)
    g2 = np.block([[cg, -sg], [sg, cg]])
    g2i = np.block([[cg, sg], [-sg, cg]])
    at = 2.0 * np.pi * (np.arange(H1 + 8)[:, None] * np.arange(N2)[None, :]) / N
    tw = np.concatenate([np.cos(at), -np.sin(at)], axis=1)
    as_bf = lambda a: jnp.asarray(a.astype(np.float32)).astype(BF16)
    return dict(m1=as_bf(m1), minv=as_bf(minv), g2=as_bf(g2), g2i=as_bf(g2i),
                tw=jnp.asarray(tw.astype(np.float32)), N1=N1, H1=H1)


S_PITCH = LANES + 8
GROUP = 8
STAGE_UNROLL = 16


SLAB_UNROLL = 8


def _slab_scratch_rows(i):
    return pl.ds(pl.multiple_of(i * S_PITCH, 8), LANES)


def _slab_transpose_one(get_slab, dst_refs, i):
    t = get_slab(i).T
    for dst in dst_refs:
        dst[_slab_scratch_rows(i), :] = t


def _slab_transpose_in(get_slab, dst_refs, h1):
    def body(j, carry):
        for u in range(SLAB_UNROLL):
            _slab_transpose_one(get_slab, dst_refs, j * SLAB_UNROLL + u)
        return carry

    lax.fori_loop(0, h1 // SLAB_UNROLL, body, 0)


def _slab_transpose_out(s_ref, put_slab, h1):
    def body(j, carry):
        for u in range(SLAB_UNROLL):
            i = j * SLAB_UNROLL + u
            put_slab(i, s_ref[_slab_scratch_rows(i), :].T)
        return carry

    lax.fori_loop(0, h1 // SLAB_UNROLL, body, 0)


def _slab_map(fn, h1):
    def body(j, carry):
        for u in range(SLAB_UNROLL):
            fn(_slab_scratch_rows(j * SLAB_UNROLL + u))
        return carry

    lax.fori_loop(0, h1 // SLAB_UNROLL, body, 0)


def _slab_rows(i):
    return pl.ds(pl.multiple_of(i * LANES, LANES), LANES)


def _fft_stage_a(s_ref, nch, m1, tw_ref, ab_ref, side_ref, n1, side_work=None):
    h1 = n1 // 2
    twr = tw_ref[0:h1, 0:LANES]
    twi = tw_ref[0:h1, LANES:2 * LANES]
    tsr = tw_ref[h1:h1 + 1, 0:LANES]
    tsi = tw_ref[h1:h1 + 1, LANES:2 * LANES]

    def body(g, carry):
        for u in range(STAGE_UNROLL):
            c0 = (g * STAGE_UNROLL + u) * GROUP
            rhs = jnp.concatenate([s_ref[pl.ds(c0 + i, h1, stride=S_PITCH), :] for i in range(GROUP)], axis=1)
            res = jnp.dot(m1, rhs.astype(BF16), preferred_element_type=F32)
            for i in range(GROUP):
                ar = res[0:h1, LANES * i:LANES * (i + 1)]
                ai = res[h1:2 * h1, LANES * i:LANES * (i + 1)]
                rows = pl.ds(pl.multiple_of((c0 + i) * h1, h1), h1)
                ab_ref[rows, 0:LANES] = ar * twr - ai * twi
                ab_ref[rows, LANES:2 * LANES] = ar * twi + ai * twr
            a_side = jnp.concatenate([res[2 * h1:2 * h1 + 1, LANES * i:LANES * (i + 1)] for i in range(GROUP)],
                                     axis=0)
            rows = pl.ds(pl.multiple_of(c0, GROUP), GROUP)
            side_ref[rows, 0:LANES] = a_side * tsr
            side_ref[rows, LANES:2 * LANES] = a_side * tsi
        if side_work is not None:
            side_work(g, n_iter)
        return carry

    n_iter = nch // (GROUP * STAGE_UNROLL)
    lax.fori_loop(0, n_iter, body, 0)


def _cmul(a, b):
    ar, ai = a[:, 0:LANES], a[:, LANES:2 * LANES]
    br, bi = b[:, 0:LANES], b[:, LANES:2 * LANES]
    return jnp.concatenate([ar * br - ai * bi, ar * bi + ai * br], axis=1)


def _fft_mid(ab_ref, side_ref, get_h, get_h_side, g2, g2i, nrows):
    chunk = min(MID_ROWS, nrows)

    def body(r, carry):
        rows = pl.ds(pl.multiple_of(r * chunk, chunk), chunk)
        spec = _bdot(ab_ref[rows, :], g2)
        ab_ref[rows, :] = _bdot(_cmul(spec, get_h(rows)), g2i)
        return carry

    lax.fori_loop(0, nrows // chunk, body, 0)
    side_ref[...] = _bdot(_cmul(_bdot(side_ref[...], g2), get_h_side()), g2i)


def _fft_stage_a_inv(ab_ref, side_ref, nch, minv, tw_ref, s_ref, n1, side_work=None):
    h1 = n1 // 2
    twr = tw_ref[0:h1, 0:LANES]
    twi = tw_ref[0:h1, LANES:2 * LANES]
    tsr = tw_ref[h1:h1 + 1, 0:LANES]
    tsi = tw_ref[h1:h1 + 1, LANES:2 * LANES]
    first_row = lax.broadcasted_iota(jnp.int32, (SIDE_PAD, GROUP * LANES), 0) == 0

    def body(g, carry):
        for u in range(STAGE_UNROLL):
            c0 = (g * STAGE_UNROLL + u) * GROUP
            blk = ab_ref[pl.ds(pl.multiple_of(c0 * h1, GROUP * h1), GROUP * h1), :]
            re, im = [], []
            for i in range(GROUP):
                br = blk[i * h1:(i + 1) * h1, 0:LANES]
                bi = blk[i * h1:(i + 1) * h1, LANES:2 * LANES]
                re.append(br * twr + bi * twi)
                im.append(bi * twr - br * twi)
            side = side_ref[pl.ds(pl.multiple_of(c0, GROUP), GROUP), :]
            side_re = side[:, 0:LANES] * tsr + side[:, LANES:2 * LANES] * tsi
            side_row = jnp.concatenate([side_re[i:i + 1, :] for i in range(GROUP)], axis=1)
            side_blk = jnp.where(first_row, jnp.broadcast_to(side_row, first_row.shape), 0.0)
            rhs = jnp.concatenate([jnp.concatenate(re, axis=1), jnp.concatenate(im, axis=1), side_blk], axis=0)
            y = jnp.dot(minv, rhs.astype(BF16), preferred_element_type=F32)
            for i in range(GROUP):
                s_ref[pl.ds(c0 + i, h1, stride=S_PITCH), :] = y[:, LANES * i:LANES * (i + 1)]
        if side_work is not None:
            side_work(g, n_iter)
        return carry

    n_iter = nch // (GROUP * STAGE_UNROLL)
    lax.fori_loop(0, n_iter, body, 0)


def _filter_spectrum_kernel(hf_ref, hb_ref, m1_ref, g2_ref, tw_ref, o_ref, oside_ref,
                            s_ref, s2_ref, af_ref, ab_ref, afs_ref, abs_ref, *, n1, inv_n):
    nch = hf_ref.shape[1]
    h1 = n1 // 2
    m1, g2 = m1_ref[...], g2_ref[...]

    def transpose_backward(g, n_iter):
        per_iter = h1 // n_iter
        for u in range(per_iter):
            _slab_transpose_one(lambda i: hb_ref[_slab_rows(i), :], [s2_ref], g * per_iter + u)

    _slab_transpose_in(lambda i: hf_ref[_slab_rows(i), :], [s_ref], h1)
    _fft_stage_a(s_ref, nch, m1, tw_ref, af_ref, afs_ref, n1, side_work=transpose_backward)
    _fft_stage_a(s2_ref, nch, m1, tw_ref, ab_ref, abs_ref, n1)

    def two_sided(fwd, bwd):
        sf = _bdot(fwd, g2)
        sb = _bdot(bwd, g2)
        return jnp.concatenate([sf[:, 0:LANES] + sb[:, 0:LANES], sf[:, LANES:] - sb[:, LANES:]], axis=1) * inv_n

    chunk = min(MID_ROWS, nch * h1)

    def body(r, carry):
        rows = pl.ds(pl.multiple_of(r * chunk, chunk), chunk)
        o_ref[rows, :] = two_sided(af_ref[rows, :], ab_ref[rows, :])
        return carry

    lax.fori_loop(0, (nch * h1) // chunk, body, 0)
    oside_ref[...] = two_sided(afs_ref[...], abs_ref[...])


def _filter_spectrum(filt, L, d_hyena, tabs):
    h1 = tabs["H1"]
    order = filt.shape[1] // (2 * d_hyena)
    cb = LANES
    nblk = d_hyena // cb
    full = lambda a: pl.BlockSpec(a.shape, lambda o, c: (0,) * a.ndim)
    main = pltpu.VMEM((cb * h1, 2 * LANES), F32)
    side = pltpu.VMEM((cb, 2 * LANES), F32)
    return pl.pallas_call(
        functools.partial(_filter_spectrum_kernel, n1=tabs["N1"], inv_n=1.0 / (2 * L)),
        grid=(order, nblk),
        in_specs=[pl.BlockSpec((L, cb), lambda o, c: (0, o * 2 * nblk + c)),
                  pl.BlockSpec((L, cb), lambda o, c: (0, o * 2 * nblk + nblk + c)),
                  full(tabs["m1"]), full(tabs["g2"]), full(tabs["tw"])],
        out_specs=[pl.BlockSpec((cb * h1, 2 * LANES), lambda o, c: (o * nblk + c, 0)),
                   pl.BlockSpec((cb, 2 * LANES), lambda o, c: (o * nblk + c, 0))],
        out_shape=[jax.ShapeDtypeStruct((order * d_hyena * h1, 2 * LANES), F32),
                   jax.ShapeDtypeStruct((order * d_hyena, 2 * LANES), F32)],
        scratch_shapes=[pltpu.VMEM((h1 * S_PITCH, LANES), F32), pltpu.VMEM((h1 * S_PITCH, LANES), F32),
                        main, main, side, side],
        compiler_params=_params("parallel", "parallel"),
        name="filter_spectrum",
    )(filt, filt, tabs["m1"], tabs["g2"], tabs["tw"])


def _hyena_kernel(v_ref, vnext_ref, x1_ref, x2_ref, h_ref, hside_ref, d_ref, m1_ref, minv_ref, g2_ref, g2i_ref,
                  tw_ref, o_ref, s_ref, sv_ref, sn_ref, sx_ref, sz_ref, ab_ref, side_ref, *, n1):
    nch = v_ref.shape[2]
    h1 = n1 // 2
    m1, minv, g2, g2i = m1_ref[...], minv_ref[...], g2_ref[...], g2i_ref[...]

    def slab_of(p_ref):
        return lambda i: p_ref[0, _slab_rows(i), :].astype(F32)

    def transposer(p_ref, dst_ref):
        def work(g, n_iter):
            per_iter = h1 // n_iter
            for u in range(per_iter):
                _slab_transpose_one(slab_of(p_ref), [dst_ref], g * per_iter + u)
        return work

    def long_conv(order, src_ref, gate_ref, skip_ref, keep_ref=None, inv_side_work=None):
        _fft_stage_a(src_ref, nch, m1, tw_ref, ab_ref, side_ref, n1, side_work=transposer(gate_ref, sx_ref))
        _fft_mid(ab_ref, side_ref, lambda rows: h_ref[order, rows, :], lambda: hside_ref[order], g2, g2i, nch * h1)
        _fft_stage_a_inv(ab_ref, side_ref, nch, minv, tw_ref, s_ref, n1, side_work=inv_side_work)
        d_slab = jnp.broadcast_to(d_ref[:, order:order + 1], (nch, LANES))

        def gate(rows):
            val = sx_ref[rows, :] * (s_ref[rows, :] + d_slab * skip_ref[rows, :])
            s_ref[rows, :] = val
            if keep_ref is not None:
                keep_ref[rows, :] = val

        _slab_map(gate, h1)

    first = pl.program_id(1) == 0

    @pl.when(first)
    def _():
        _slab_transpose_in(slab_of(v_ref), [sv_ref], h1)

    @pl.when(jnp.logical_not(first))
    def _():
        def copy(rows):
            sv_ref[rows, :] = sn_ref[rows, :]

        _slab_map(copy, h1)

    long_conv(0, sv_ref, x1_ref, sv_ref, keep_ref=sz_ref, inv_side_work=transposer(vnext_ref, sn_ref))
    long_conv(1, s_ref, x2_ref, sz_ref)

    def put(i, out):
        o_ref[0, _slab_rows(i), :] = out.astype(o_ref.dtype)

    _slab_transpose_out(s_ref, put, h1)


def _hyena(proj, spec, spec_side, skip, tabs):
    B, L, _ = proj.shape
    D = skip.shape[1]
    n1, h1 = tabs["N1"], tabs["H1"]
    cb = LANES
    nblk = D // cb
    spec = spec.reshape(2, D * h1, 2 * LANES)
    spec_side = spec_side.reshape(2, D, 2 * LANES)
    col = lambda part: pl.BlockSpec((1, L, cb), lambda c, b: (b, 0, part * nblk + c))
    full = lambda a: pl.BlockSpec(a.shape, lambda c, b: (0,) * a.ndim)
    slabs = pltpu.VMEM((h1 * S_PITCH, LANES), F32)
    return pl.pallas_call(
        functools.partial(_hyena_kernel, n1=n1),
        grid=(nblk, B),
        in_specs=[col(0),
                  pl.BlockSpec((1, L, cb), lambda c, b: (jnp.minimum(b + 1, B - 1), 0, c)),
                  col(1), col(2),
                  pl.BlockSpec((2, cb * h1, 2 * LANES), lambda c, b: (0, c, 0), pipeline_mode=pl.Buffered(1)),
                  pl.BlockSpec((2, cb, 2 * LANES), lambda c, b: (0, c, 0)),
                  pl.BlockSpec((cb, 2), lambda c, b: (c, 0)),
                  full(tabs["m1"]), full(tabs["minv"]), full(tabs["g2"]), full(tabs["g2i"]), full(tabs["tw"])],
        out_specs=pl.BlockSpec((1, L, cb), lambda c, b: (b, 0, c)),
        out_shape=jax.ShapeDtypeStruct((B, L, D), BF16),
        scratch_shapes=[slabs, slabs, slabs, slabs, slabs, pltpu.VMEM((cb * h1, 2 * LANES), F32),
                        pltpu.VMEM((cb, 2 * LANES), F32)],
        compiler_params=_params("parallel", "arbitrary"),
        name="hyena_fftconv",
    )(proj, proj, proj, proj, spec, spec_side, skip.astype(F32).T,
      tabs["m1"], tabs["minv"], tabs["g2"], tabs["g2i"], tabs["tw"])


ML_CHUNK = 512
ML_CHUNKS_PER_STEP = 1


def _log_sigmoid(x):
    return jnp.minimum(x, 0.0) - jnp.log1p(jnp.exp(-jnp.abs(x)))


def _mlstm_direction(q, k, v, i_row, cum_row, logf_row, c_ref, m_ref, mask, scale):
    c, dh = q.shape
    g_row = i_row - cum_row
    f_col_rep = jnp.broadcast_to(cum_row, (LANES, c)).T
    m_prev = m_ref[0:1, 0:1]
    log_scale = math.log(scale)
    e = jnp.where(mask, g_row + log_scale, -jnp.inf)
    a = jnp.maximum(jnp.max(e, axis=1, keepdims=True) - log_scale, m_prev)
    sc = lax.dot_general(q.astype(BF16), k.astype(BF16), (((1,), (1,)), ((), ())), preferred_element_type=F32)
    p = sc * jnp.exp(e - a)
    vaug = jnp.concatenate([v, jnp.ones_like(v)], axis=1)
    inter = jnp.exp(m_prev - a)
    tot = _bdot(p, vaug) + inter * _bdot(q, c_ref[...])
    num, den = tot[:, 0:dh], tot[:, dh:2 * dh]
    m_t = f_col_rep + a
    h = num / jnp.maximum(jnp.abs(den), jnp.exp(-m_t))
    a_end = jnp.maximum(jnp.max(g_row, axis=1, keepdims=True), m_prev)
    f_end = jnp.sum(logf_row, axis=1, keepdims=True)
    we = jnp.exp(g_row - a_end) * scale
    kw = k.astype(F32).T * we
    c_ref[...] = jnp.exp(m_prev - a_end) * c_ref[...] + _bdot(kw, vaug)
    m_ref[...] = jnp.broadcast_to(f_end + a_end, m_ref.shape)
    return h


def _mlstm_kernel(qf_ref, kf_ref, vf_ref, gf_ref, qb_ref, kb_ref, vb_ref, gb_ref, bias_ref,
                  hf_ref, hb_ref, c_ref, m_ref, *, heads, scale, chunk):
    @pl.when(pl.program_id(1) == 0)
    def _():
        c_ref[...] = jnp.zeros_like(c_ref)
        m_ref[...] = jnp.zeros_like(m_ref)

    c = chunk
    nsub = qf_ref.shape[1] // c
    dh = qf_ref.shape[2] // heads
    r = lax.broadcasted_iota(jnp.int32, (c, c), 0)
    s = lax.broadcasted_iota(jnp.int32, (c, c), 1)
    mask_f = s <= r
    mask_b = s >= r

    def gate_rows(g, mask_t):
        g = g + bias_ref[...]
        logf = _log_sigmoid(g)
        tri = mask_t.astype(BF16)
        cum = sum(jnp.dot(part, tri, preferred_element_type=F32) for part in _split3(logf))
        return g, logf, cum

    for u in range(nsub):
        rf = slice(u * c, (u + 1) * c)
        rb = slice((nsub - 1 - u) * c, (nsub - u) * c)
        g_f, logf_f, cum_f = gate_rows(gf_ref[:, rf], mask_b)
        g_b, logf_b, cum_b = gate_rows(gb_ref[:, rb], mask_f)
        for h in range(heads):
            sl = slice(h * dh, (h + 1) * dh)
            fi, ff, bi, bf = h, heads + h, 2 * heads + h, 3 * heads + h
            hf_ref[0, rf, sl] = _mlstm_direction(
                qf_ref[0, rf, sl], kf_ref[0, rf, sl], vf_ref[0, rf, sl], g_f[fi:fi + 1], cum_f[ff:ff + 1],
                logf_f[ff:ff + 1], c_ref.at[h], m_ref.at[h], mask_f, scale).astype(hf_ref.dtype)
            hb_ref[0, rb, sl] = _mlstm_direction(
                qb_ref[0, rb, sl], kb_ref[0, rb, sl], vb_ref[0, rb, sl], g_b[bi:bi + 1], cum_b[bf:bf + 1],
                logf_b[bf:bf + 1], c_ref.at[heads + h], m_ref.at[heads + h], mask_b, scale).astype(hb_ref.dtype)


def _mlstm(qk, proj, v_col0, gates, gate_b, d_ml):
    B, L, _ = qk.shape
    H = MLSTM_HEADS
    dh = d_ml // H
    c = min(ML_CHUNK, L)
    rows = min(ML_CHUNKS_PER_STEP * c, L)
    nb = L // rows
    bias = jnp.broadcast_to(gate_b.astype(F32)[:, None], (4 * H, c))
    vb0 = v_col0 // d_ml
    specs = []
    for pos in (lambda ci: ci, lambda ci: nb - 1 - ci):
        specs += [pl.BlockSpec((1, rows, d_ml), lambda bi, ci, pos=pos: (bi, pos(ci), 0)),
                  pl.BlockSpec((1, rows, d_ml), lambda bi, ci, pos=pos: (bi, pos(ci), 1)),
                  pl.BlockSpec((1, rows, d_ml), lambda bi, ci, pos=pos: (bi, pos(ci), vb0)),
                  pl.BlockSpec((4 * H, rows), lambda bi, ci, pos=pos: (0, bi * nb + pos(ci)))]
    specs.append(pl.BlockSpec((4 * H, c), lambda bi, ci: (0, 0)))
    out_shape = jax.ShapeDtypeStruct((B, L, d_ml), BF16)
    return pl.pallas_call(
        functools.partial(_mlstm_kernel, heads=H, scale=dh ** -0.5, chunk=c),
        grid=(B, nb),
        in_specs=specs,
        out_specs=[pl.BlockSpec((1, rows, d_ml), lambda bi, ci: (bi, ci, 0)),
                   pl.BlockSpec((1, rows, d_ml), lambda bi, ci: (bi, nb - 1 - ci, 0))],
        out_shape=[out_shape, out_shape],
        scratch_shapes=[pltpu.VMEM((2 * H, dh, 2 * dh), F32), pltpu.VMEM((2 * H, 1, LANES), F32)],
        compiler_params=_params("parallel", "arbitrary"),
        name="mlstm_scan",
    )(qk, qk, proj, gates, qk, qk, proj, gates, bias)


def _group_rms(y, gain, bd, group):
    sq = (y * y).astype(BF16)
    w = bd.shape[0]
    ss = jnp.concatenate([jnp.dot(sq[:, i:i + w], bd, preferred_element_type=F32)
                          for i in range(0, y.shape[1], w)], axis=1)
    return y * lax.rsqrt(ss * (1.0 / group) + EPS) * gain


GN_STRIPE = 256


def _block_diag_ones(n, group):
    idx = np.arange(n) // group
    return jnp.asarray((idx[:, None] == idx[None, :]).astype(np.float32)).astype(BF16)


def _mix_xattn_kernel(z_ref, hf_ref, hb_ref, o_ref, x_ref, mem_ref, ghy_ref, gml_ref, why_ref, wml_ref, bdh_ref,
                      bdm_ref, gx_ref, gm_ref, wq_ref, wkv_ref, wo_ref, out_ref, kv_ref,
                      *, hy_group, ml_group, heads, scale):
    @pl.when(pl.program_id(1) == 0)
    def _():
        kv_ref[...] = _bdot(_rms(mem_ref[0], gm_ref[...]), wkv_ref[...]).astype(kv_ref.dtype)

    y_hy = _group_rms(z_ref[0].astype(F32), ghy_ref[...], bdh_ref[...], hy_group)
    h_sum = hf_ref[0].astype(F32) + hb_ref[0].astype(F32)
    y_ml = _group_rms(jax.nn.sigmoid(o_ref[0].astype(F32)) * h_sum, gml_ref[...], bdm_ref[...], ml_group)
    h1 = x_ref[0] + _bdot(y_hy, why_ref[...]) + _bdot(y_ml, wml_ref[...])

    D = h1.shape[1]
    dh = D // heads
    q = _bdot(_rms(h1, gx_ref[...]), wq_ref[...])
    outs = []
    for hd in range(heads):
        qh = q[:, hd * dh:(hd + 1) * dh]
        kh = kv_ref[:, hd * dh:(hd + 1) * dh]
        vh = kv_ref[:, D + hd * dh:D + (hd + 1) * dh]
        s = lax.dot_general(qh.astype(BF16), kh, (((1,), (1,)), ((), ())), preferred_element_type=F32) * scale
        e = jnp.exp(s - jnp.max(s, axis=-1, keepdims=True))
        p = e / jnp.sum(e, axis=-1, keepdims=True)
        outs.append(_bdot(p, vh))
    out_ref[0] = h1 + _bdot(jnp.concatenate(outs, axis=1), wo_ref[...])


def _mix_xattn(z, hf, hb, proj, o_col0, x, mem, g_hy, g_ml, w_out, g_x, g_mem, wq, wk, wv, wo, tm=512):
    B, L, d_hy = z.shape
    d_ml = hf.shape[2]
    D = x.shape[2]
    nm = mem.shape[1]
    ob = o_col0 // d_ml
    row = lambda n: pl.BlockSpec((1, tm, n), lambda bi, i: (bi, i, 0))
    const = lambda a: pl.BlockSpec(a.shape, lambda bi, i: (0,) * a.ndim, pipeline_mode=pl.Buffered(1))
    hy_group, ml_group = d_hy // HYENA_GROUPS, d_ml // MLSTM_HEADS
    consts = [g_hy.reshape(1, -1), g_ml.reshape(1, -1), w_out[:d_hy].astype(BF16), w_out[d_hy:].astype(BF16),
              _block_diag_ones(GN_STRIPE, hy_group), _block_diag_ones(GN_STRIPE, ml_group), g_x.reshape(1, D),
              g_mem.reshape(1, D), wq.astype(BF16), jnp.concatenate([wk, wv], axis=1).astype(BF16), wo.astype(BF16)]
    return pl.pallas_call(
        functools.partial(_mix_xattn_kernel, hy_group=hy_group, ml_group=ml_group, heads=XATTN_HEADS,
                          scale=(D // XATTN_HEADS) ** -0.5),
        grid=(B, L // tm),
        in_specs=[row(d_hy), row(d_ml), row(d_ml), pl.BlockSpec((1, tm, d_ml), lambda bi, i: (bi, i, ob)), row(D),
                  pl.BlockSpec((1, nm, D), lambda bi, i: (bi, 0, 0))] + [const(a) for a in consts],
        out_specs=row(D),
        out_shape=jax.ShapeDtypeStruct((B, L, D), F32),
        scratch_shapes=[pltpu.VMEM((nm, 2 * D), BF16)],
        compiler_params=_params("parallel", "arbitrary"),
        name="mix_xattn",
    )(z, hf, hb, proj, x, mem, *consts)


MLP_CHUNK = 1024


def _mlp_kernel(h_ref, g_ref, w1_ref, w2_ref, gf_ref, out_ref, xn_ref, *, final_norm):
    j = pl.program_id(1)

    @pl.when(j == 0)
    def _():
        x = h_ref[...]
        xn_ref[...] = _rms(x, g_ref[...]).astype(BF16)
        out_ref[...] = x

    for c in range(w1_ref.shape[1] // MLP_CHUNK):
        cols = slice(c * MLP_CHUNK, (c + 1) * MLP_CHUNK)
        a = jnp.maximum(jnp.dot(xn_ref[...], w1_ref[:, cols], preferred_element_type=F32), 0.0)
        out_ref[...] += _bdot(a * a, w2_ref[cols, :])

    if final_norm:
        @pl.when(j == pl.num_programs(1) - 1)
        def _():
            out_ref[...] = _rms(out_ref[...], gf_ref[...])


def _mlp(h, g, w1, w2, gf, final_norm, tm=1024, tf=2048):
    M, D = h.shape
    dff = w1.shape[1]
    return pl.pallas_call(
        functools.partial(_mlp_kernel, final_norm=final_norm),
        grid=(M // tm, dff // tf),
        in_specs=[pl.BlockSpec((tm, D), lambda i, j: (i, 0)), pl.BlockSpec((1, D), lambda i, j: (0, 0)),
                  pl.BlockSpec((D, tf), lambda i, j: (0, j)), pl.BlockSpec((tf, D), lambda i, j: (j, 0)),
                  pl.BlockSpec((1, D), lambda i, j: (0, 0))],
        out_specs=pl.BlockSpec((tm, D), lambda i, j: (i, 0)),
        out_shape=jax.ShapeDtypeStruct((M, D), F32),
        scratch_shapes=[pltpu.VMEM((tm, D), BF16)],
        compiler_params=_params("parallel", "arbitrary"),
        name="mlp",
    )(h, g.reshape(1, D), w1.astype(BF16), w2.astype(BF16), gf.reshape(1, D))


def _pad_cols(w, n):
    return jnp.pad(w, ((0, 0), (0, n - w.shape[1])))


def _layer(h, mem, p, l):
    B, L, D = h.shape
    M = B * L
    d_hy = p["hy_norm_g"].shape[1]
    d_ml = p["ml_norm_g"].shape[1]
    hy_cols = 3 * d_hy
    qk_cols = 2 * d_ml
    n_gate = 4 * MLSTM_HEADS
    w_in = p["w_in"][l]
    vo0 = hy_cols + qk_cols
    g0 = vo0 + 2 * d_ml
    v_col0 = hy_cols
    o_col0 = hy_cols + d_ml
    n_in = hy_cols + 2 * d_ml
    proj, qk, gates = _in_proj(h.reshape(M, D), p["norm_mix_g"][l], w_in[:, :vo0].astype(BF16),
                               w_in[:, vo0:g0].astype(BF16),
                               _pad_cols(w_in[:, g0:g0 + n_gate], LANES).astype(BF16), n_gate,
                               jnp.concatenate([p["hy_conv_w"][l], p["ml_conv_w"][l]], axis=1),
                               jnp.concatenate([p["hy_conv_b"][l], p["ml_conv_b"][l]]),
                               n_hy=hy_cols, seq_len=L, tm=min(1024, L))
    proj = proj.reshape(B, L, n_in)
    qk = qk.reshape(B, L, qk_cols)

    tabs = _fft_tables(L)
    filt = _hyena_filters(L, p["hy_filt_w1"][l], p["hy_filt_b1"][l], p["hy_filt_freq1"][l], p["hy_filt_w2"][l],
                          p["hy_filt_b2"][l], p["hy_filt_freq2"][l], p["hy_filt_w3"][l], d_hy,
                          tl=min(512, L))
    spec, spec_side = _filter_spectrum(filt, L, d_hy, tabs)
    z_hy = _hyena(proj, spec, spec_side, p["hy_skip"][l], tabs)

    h_f, h_b = _mlstm(qk, proj, v_col0, gates, p["ml_gate_b"][l], d_ml)

    return _mix_xattn(z_hy, h_f, h_b, proj, o_col0, h, mem, p["hy_norm_g"][l], p["ml_norm_g"][l], p["w_out"][l],
                      p["norm_x_g"][l], p["norm_mem_g"][l], p["xa_wq"][l], p["xa_wk"][l], p["xa_wv"][l],
                      p["xa_wo"][l], tm=min(1024, L))


def kernel(x, mem, norm_mix_g, w_in, hy_conv_w, hy_conv_b, hy_filt_w1, hy_filt_b1, hy_filt_freq1, hy_filt_w2,
           hy_filt_b2, hy_filt_freq2, hy_filt_w3, hy_skip, hy_norm_g, ml_conv_w, ml_conv_b, ml_gate_b, ml_norm_g,
           w_out, norm_x_g, norm_mem_g, xa_wq, xa_wk, xa_wv, xa_wo, norm_ff_g, ff_w1, ff_w2, final_norm_g):
    p = dict(norm_mix_g=norm_mix_g, w_in=w_in, hy_conv_w=hy_conv_w, hy_conv_b=hy_conv_b, hy_filt_w1=hy_filt_w1,
             hy_filt_b1=hy_filt_b1, hy_filt_freq1=hy_filt_freq1, hy_filt_w2=hy_filt_w2, hy_filt_b2=hy_filt_b2,
             hy_filt_freq2=hy_filt_freq2, hy_filt_w3=hy_filt_w3, hy_skip=hy_skip, hy_norm_g=hy_norm_g,
             ml_conv_w=ml_conv_w, ml_conv_b=ml_conv_b, ml_gate_b=ml_gate_b, ml_norm_g=ml_norm_g, w_out=w_out,
             norm_x_g=norm_x_g, norm_mem_g=norm_mem_g, xa_wq=xa_wq, xa_wk=xa_wk, xa_wv=xa_wv, xa_wo=xa_wo)
    B, L, D = x.shape
    depth = w_in.shape[0]
    h = x
    for l in range(depth):
        h = _layer(h, mem, p, l)
        h = _mlp(h.reshape(B * L, D), norm_ff_g[l], ff_w1[l], ff_w2[l], final_norm_g, final_norm=l == depth - 1,
                 tm=min(1024, B * L)).reshape(B, L, D)
    return h
```

```python
import functools
import math

import numpy as np
import jax
import jax.numpy as jnp
from jax import lax
from jax.experimental import pallas as pl
from jax.experimental.pallas import tpu as pltpu

F32 = jnp.float32
BF16 = jnp.bfloat16

EPS = 1e-6
HYENA_GROUPS = 8
MLSTM_HEADS = 4
XATTN_HEADS = 4
FILTER_BANDS = 16
DECAY_TARGET = 1e-2
SHORT_DECAY_PCT = 0.3
LONG_DECAY_PCT = 1.5

LANES = 128
VMEM_LIMIT = 56 * 1024 * 1024


def _params(*sem):
    return pltpu.CompilerParams(dimension_semantics=sem, vmem_limit_bytes=VMEM_LIMIT)


def _bdot(a, b):
    return jnp.dot(a.astype(BF16), b.astype(BF16), preferred_element_type=F32)


def _split3(x):
    hi = x.astype(BF16)
    r = x - hi.astype(F32)
    mid = r.astype(BF16)
    lo = (r - mid.astype(F32)).astype(BF16)
    return hi, mid, lo


IN_CHUNK = 512
HALO = 8


def _rms(x, g):
    return x * lax.rsqrt(jnp.mean(x * x, axis=-1, keepdims=True) + EPS) * g


def _in_proj_kernel(x_ref, xb_ref, xa_ref, g_ref, wc_ref, wp_ref, wg_ref, cw_ref, cb_ref,
                    proj_ref, qk_ref, og_ref, xn_ref, *, tiles_per_seq, n_hy):
    tm = x_ref.shape[0]
    g = g_ref[...]
    x = x_ref[...]
    xn_ref[...] = (x * g).astype(BF16)
    r = lax.rsqrt(jnp.mean(x * x, axis=-1, keepdims=True) + EPS)
    og_ref[...] = (jnp.dot(xn_ref[...], wg_ref[...], preferred_element_type=F32) * r).T[0:og_ref.shape[0], :]
    for c in range(wp_ref.shape[1] // IN_CHUNK):
        cols = slice(c * IN_CHUNK, (c + 1) * IN_CHUNK)
        out_cols = slice(n_hy + c * IN_CHUNK, n_hy + (c + 1) * IN_CHUNK)
        proj_ref[:, out_cols] = (jnp.dot(xn_ref[...], wp_ref[:, cols], preferred_element_type=F32) * r
                                 ).astype(proj_ref.dtype)

    halo = _rms(jnp.concatenate([xb_ref[...], xa_ref[...]], axis=0), g).astype(BF16)
    pos = pl.program_id(0) % tiles_per_seq
    row = lax.broadcasted_iota(jnp.int32, (tm, IN_CHUNK), 0)
    for c in range(wc_ref.shape[1] // IN_CHUNK):
        cols = slice(c * IN_CHUNK, (c + 1) * IN_CHUNK)
        u = jnp.dot(xn_ref[...], wc_ref[:, cols], preferred_element_type=F32) * r
        uh = jnp.dot(halo, wc_ref[:, cols], preferred_element_type=F32)
        before = jnp.where(pos == 0, 0.0, uh[HALO - 1:HALO, :])
        after = jnp.where(pos == tiles_per_seq - 1, 0.0, uh[HALO:HALO + 1, :])
        prev = jnp.where(row == 0, before, pltpu.roll(u, 1, 0))
        nxt = jnp.where(row == tm - 1, after, pltpu.roll(u, tm - 1, 0))
        y = prev * cw_ref[0:1, cols] + u * cw_ref[1:2, cols] + nxt * cw_ref[2:3, cols] + cb_ref[:, cols]
        if c * IN_CHUNK < n_hy:
            proj_ref[:, cols] = y.astype(proj_ref.dtype)
        else:
            qk_cols = slice(c * IN_CHUNK - n_hy, (c + 1) * IN_CHUNK - n_hy)
            qk_ref[:, qk_cols] = (y * jax.nn.sigmoid(y)).astype(qk_ref.dtype)


def _in_proj(x, g, w_conv, w_plain, w_gate, n_gate, conv_w, conv_b, n_hy, seq_len, tm):
    M, K = x.shape
    nc, npl = w_conv.shape[1], w_plain.shape[1]
    hb = tm // HALO
    const = lambda a: pl.BlockSpec(a.shape, lambda i: (0,) * a.ndim, pipeline_mode=pl.Buffered(1))
    row = lambda n: pl.BlockSpec((tm, n), lambda i: (i, 0))
    g = g.reshape(1, K)
    conv_b = conv_b.reshape(1, nc)
    return pl.pallas_call(
        functools.partial(_in_proj_kernel, tiles_per_seq=seq_len // tm, n_hy=n_hy),
        grid=(M // tm,),
        in_specs=[row(K),
                  pl.BlockSpec((HALO, K), lambda i: (jnp.maximum(i * hb - 1, 0), 0)),
                  pl.BlockSpec((HALO, K), lambda i: (jnp.minimum((i + 1) * hb, M // HALO - 1), 0)),
                  const(g), const(w_conv), const(w_plain), const(w_gate), const(conv_w), const(conv_b)],
        out_specs=[row(n_hy + npl), row(nc - n_hy), pl.BlockSpec((n_gate, tm), lambda i: (0, i))],
        out_shape=[jax.ShapeDtypeStruct((M, n_hy + npl), BF16), jax.ShapeDtypeStruct((M, nc - n_hy), BF16),
                   jax.ShapeDtypeStruct((n_gate, M), F32)],
        scratch_shapes=[pltpu.VMEM((tm, K), BF16)],
        compiler_params=_params("parallel"),
        name="in_proj",
    )(x, x, x, g, w_conv, w_plain, w_gate, conv_w, conv_b)


def _filter_kernel(feat_ref, w1_ref, b1_ref, f1_ref, w2_ref, b2_ref, f2_ref, w3h_ref, w3l_ref, dec_ref, dir_ref,
                   o_ref, *, L):
    tl = o_ref.shape[0]
    hp = lax.Precision.HIGHEST
    pos = (lax.broadcasted_iota(jnp.int32, (tl, LANES), 0) + pl.program_id(0) * tl).astype(F32)
    lane = lax.broadcasted_iota(jnp.int32, (tl, LANES), 1)
    t = pos * (1.0 / (L - 1))
    arg = feat_ref[...] * (pos * (2.0 * math.pi / L))
    z = jnp.where(lane < FILTER_BANDS, jnp.cos(arg),
                  jnp.where(lane < 2 * FILTER_BANDS, -jnp.sin(arg),
                            jnp.where(lane == 2 * FILTER_BANDS, t, 0.0)))
    hid = jnp.sin(f1_ref[...] * (jnp.dot(z, w1_ref[...], precision=hp, preferred_element_type=F32) + b1_ref[...]))
    hid = jnp.sin(f2_ref[...] * (jnp.dot(hid, w2_ref[...], precision=hp, preferred_element_type=F32) + b2_ref[...]))
    hid_hi = hid.astype(BF16)
    hid_lo = (hid - hid_hi.astype(F32)).astype(BF16)
    filt = (jnp.dot(hid_hi, w3h_ref[...], preferred_element_type=F32)
            + jnp.dot(hid_hi, w3l_ref[...], preferred_element_type=F32)
            + jnp.dot(hid_lo, w3h_ref[...], preferred_element_type=F32))
    filt = filt * jnp.exp(-t[:, 0:1] * dec_ref[...])
    filt = jnp.where(pos[:, 0:1] == 0.0, filt * dir_ref[...], filt)
    o_ref[...] = filt


def _hyena_filters(L, w1, b1, fr1, w2, b2, fr2, w3, d_hyena, tl=512):
    n_emb, n_hid = w1.shape
    n_out = w3.shape[1]
    bands = jnp.linspace(1e-4, FILTER_BANDS - 1, FILTER_BANDS, dtype=F32)
    feat = jnp.zeros((1, LANES), F32).at[0, :FILTER_BANDS].set(bands).at[0, FILTER_BANDS:2 * FILTER_BANDS].set(bands)
    w1p = jnp.zeros((LANES, n_hid), F32).at[:n_emb - 1].set(w1[1:]).at[n_emb - 1].set(w1[0])
    max_decay = math.log(DECAY_TARGET) / SHORT_DECAY_PCT
    min_decay = math.log(DECAY_TARGET) / LONG_DECAY_PCT
    deltas = jnp.abs(jnp.linspace(min_decay, max_decay, d_hyena, dtype=F32))
    reps = n_out // d_hyena
    dec = jnp.tile(deltas, reps).reshape(1, n_out)
    dirmask = jnp.tile(jnp.concatenate([jnp.ones((d_hyena,), F32), jnp.zeros((d_hyena,), F32)]), reps // 2)
    w3_hi = w3.astype(BF16)
    w3_lo = (w3 - w3_hi.astype(F32)).astype(BF16)
    full = lambda shape: pl.BlockSpec(shape, lambda i: (0,) * len(shape))
    return pl.pallas_call(
        functools.partial(_filter_kernel, L=L),
        grid=(L // tl,),
        in_specs=[full((1, LANES)), full((LANES, n_hid)), full((1, n_hid)), full((1, n_hid)),
                  full((n_hid, n_hid)), full((1, n_hid)), full((1, n_hid)), full((n_hid, n_out)),
                  full((n_hid, n_out)), full((1, n_out)), full((1, n_out))],
        out_specs=pl.BlockSpec((tl, n_out), lambda i: (i, 0)),
        out_shape=jax.ShapeDtypeStruct((L, n_out), F32),
        compiler_params=_params("parallel"),
        name="hyena_filters",
    )(feat, w1p, b1.reshape(1, -1), fr1.reshape(1, -1), w2, b2.reshape(1, -1), fr2.reshape(1, -1), w3_hi, w3_lo,
      dec, dirmask.reshape(1, n_out))


MID_ROWS = 4096
SIDE_PAD = 16


def _fft_tables(L):
    N = 2 * L
    N2 = LANES
    N1 = N // N2
    H1 = N1 // 2
    k1 = np.arange(H1)[:, None]
    n1 = np.arange(H1)[None, :]
    a1 = 2.0 * np.pi * ((k1 * n1) % N1) / N1
    c1, s1 = np.cos(a1), np.sin(a1)
    sign = np.where(np.arange(H1) % 2 == 0, 1.0, -1.0)[None, :]
    pad = np.zeros((SIDE_PAD - 1, H1))
    m1 = np.concatenate([c1, -s1, sign, pad], axis=0)
    wgt = np.where(np.arange(H1) == 0, 1.0, 2.0)[:, None]
    minv = np.concatenate([(wgt * c1).T, (-wgt * s1).T, sign.T, pad.T], axis=1)
    n2 = np.arange(N2)[:, None]
    k2 = np.arange(N2)[None, :]
    a2 = 2.0 * np.pi * ((n2 * k2) % N2) / N2
    cg, sg = np.cos(a2), np.sin(a2)
    g2 = np.block([[cg, -sg], [sg, cg]])
    g2i = np.block([[cg, sg], [-sg, cg]])
    at = 2.0 * np.pi * (np.arange(H1 + 8)[:, None] * np.arange(N2)[None, :]) / N
    tw = np.concatenate([np.cos(at), -np.sin(at)], axis=1)
    as_bf = lambda a: jnp.asarray(a.astype(np.float32)).astype(BF16)
    return dict(m1=as_bf(m1), minv=as_bf(minv), g2=as_bf(g2), g2i=as_bf(g2i),
                tw=jnp.asarray(tw.astype(np.float32)), N1=N1, H1=H1)


S_PITCH = LANES + 8
GROUP = 8
STAGE_UNROLL = 16


SLAB_UNROLL = 32


def _slab_scratch_rows(i):
    return pl.ds(pl.multiple_of(i * S_PITCH, 8), LANES)


def _slab_transpose_one(get_slab, dst_refs, i):
    t = get_slab(i).T
    for dst in dst_refs:
        dst[_slab_scratch_rows(i), :] = t


def _slab_transpose_in(get_slab, dst_refs, h1):
    def body(j, carry):
        for u in range(SLAB_UNROLL):
            _slab_transpose_one(get_slab, dst_refs, j * SLAB_UNROLL + u)
        return carry

    lax.fori_loop(0, h1 // SLAB_UNROLL, body, 0)


def _slab_transpose_out(s_ref, put_slab, h1):
    def body(j, carry):
        for u in range(SLAB_UNROLL):
            i = j * SLAB_UNROLL + u
            put_slab(i, s_ref[_slab_scratch_rows(i), :].T)
        return carry

    lax.fori_loop(0, h1 // SLAB_UNROLL, body, 0)


def _slab_map(fn, h1):
    def body(j, carry):
        for u in range(SLAB_UNROLL):
            fn(_slab_scratch_rows(j * SLAB_UNROLL + u))
        return carry

    lax.fori_loop(0, h1 // SLAB_UNROLL, body, 0)


def _slab_rows(i):
    return pl.ds(pl.multiple_of(i * LANES, LANES), LANES)


def _fft_stage_a(s_ref, nch, m1, tw_ref, ab_ref, side_ref, n1, side_work=None):
    h1 = n1 // 2
    twr = tw_ref[0:h1, 0:LANES]
    twi = tw_ref[0:h1, LANES:2 * LANES]
    tsr = tw_ref[h1:h1 + 1, 0:LANES]
    tsi = tw_ref[h1:h1 + 1, LANES:2 * LANES]

    def body(g, carry):
        for u in range(STAGE_UNROLL):
            c0 = (g * STAGE_UNROLL + u) * GROUP
            rhs = jnp.concatenate([s_ref[pl.ds(c0 + i, h1, stride=S_PITCH), :] for i in range(GROUP)], axis=1)
            res = jnp.dot(m1, rhs.astype(BF16), preferred_element_type=F32)
            for i in range(GROUP):
                ar = res[0:h1, LANES * i:LANES * (i + 1)]
                ai = res[h1:2 * h1, LANES * i:LANES * (i + 1)]
                rows = pl.ds(pl.multiple_of((c0 + i) * h1, h1), h1)
                ab_ref[rows, 0:LANES] = ar * twr - ai * twi
                ab_ref[rows, LANES:2 * LANES] = ar * twi + ai * twr
            a_side = jnp.concatenate([res[2 * h1:2 * h1 + 1, LANES * i:LANES * (i + 1)] for i in range(GROUP)],
                                     axis=0)
            rows = pl.ds(pl.multiple_of(c0, GROUP), GROUP)
            side_ref[rows, 0:LANES] = a_side * tsr
            side_ref[rows, LANES:2 * LANES] = a_side * tsi
        if side_work is not None:
            side_work(g, n_iter)
        return carry

    n_iter = nch // (GROUP * STAGE_UNROLL)
    lax.fori_loop(0, n_iter, body, 0)


def _cmul(a, b):
    ar, ai = a[:, 0:LANES], a[:, LANES:2 * LANES]
    br, bi = b[:, 0:LANES], b[:, LANES:2 * LANES]
    return jnp.concatenate([ar * br - ai * bi, ar * bi + ai * br], axis=1)


def _fft_mid(ab_ref, side_ref, get_h, get_h_side, g2, g2i, nrows):
    chunk = min(MID_ROWS, nrows)

    def body(r, carry):
        rows = pl.ds(pl.multiple_of(r * chunk, chunk), chunk)
        spec = _bdot(ab_ref[rows, :], g2)
        ab_ref[rows, :] = _bdot(_cmul(spec, get_h(rows)), g2i)
        return carry

    lax.fori_loop(0, nrows // chunk, body, 0)
    side_ref[...] = _bdot(_cmul(_bdot(side_ref[...], g2), get_h_side()), g2i)


def _fft_stage_a_inv(ab_ref, side_ref, nch, minv, tw_ref, s_ref, n1, side_work=None):
    h1 = n1 // 2
    twr = tw_ref[0:h1, 0:LANES]
    twi = tw_ref[0:h1, LANES:2 * LANES]
    tsr = tw_ref[h1:h1 + 1, 0:LANES]
    tsi = tw_ref[h1:h1 + 1, LANES:2 * LANES]
    first_row = lax.broadcasted_iota(jnp.int32, (SIDE_PAD, GROUP * LANES), 0) == 0

    def body(g, carry):
        for u in range(STAGE_UNROLL):
            c0 = (g * STAGE_UNROLL + u) * GROUP
            blk = ab_ref[pl.ds(pl.multiple_of(c0 * h1, GROUP * h1), GROUP * h1), :]
            re, im = [], []
            for i in range(GROUP):
                br = blk[i * h1:(i + 1) * h1, 0:LANES]
                bi = blk[i * h1:(i + 1) * h1, LANES:2 * LANES]
                re.append(br * twr + bi * twi)
                im.append(bi * twr - br * twi)
            side = side_ref[pl.ds(pl.multiple_of(c0, GROUP), GROUP), :]
            side_re = side[:, 0:LANES] * tsr + side[:, LANES:2 * LANES] * tsi
            side_row = jnp.concatenate([side_re[i:i + 1, :] for i in range(GROUP)], axis=1)
            side_blk = jnp.where(first_row, jnp.broadcast_to(side_row, first_row.shape), 0.0)
            rhs = jnp.concatenate([jnp.concatenate(re, axis=1), jnp.concatenate(im, axis=1), side_blk], axis=0)
            y = jnp.dot(minv, rhs.astype(BF16), preferred_element_type=F32)
            for i in range(GROUP):
                s_ref[pl.ds(c0 + i, h1, stride=S_PITCH), :] = y[:, LANES * i:LANES * (i + 1)]
        if side_work is not None:
            side_work(g, n_iter)
        return carry

    n_iter = nch // (GROUP * STAGE_UNROLL)
    lax.fori_loop(0, n_iter, body, 0)


def _filter_spectrum_kernel(hf_ref, hb_ref, m1_ref, g2_ref, tw_ref, o_ref, oside_ref,
                            s_ref, s2_ref, af_ref, ab_ref, afs_ref, abs_ref, *, n1, inv_n):
    nch = hf_ref.shape[1]
    h1 = n1 // 2
    m1, g2 = m1_ref[...], g2_ref[...]

    def transpose_backward(g, n_iter):
        per_iter = h1 // n_iter
        for u in range(per_iter):
            _slab_transpose_one(lambda i: hb_ref[_slab_rows(i), :], [s2_ref], g * per_iter + u)

    _slab_transpose_in(lambda i: hf_ref[_slab_rows(i), :], [s_ref], h1)
    _fft_stage_a(s_ref, nch, m1, tw_ref, af_ref, afs_ref, n1, side_work=transpose_backward)
    _fft_stage_a(s2_ref, nch, m1, tw_ref, ab_ref, abs_ref, n1)

    def two_sided(fwd, bwd):
        sf = _bdot(fwd, g2)
        sb = _bdot(bwd, g2)
        return jnp.concatenate([sf[:, 0:LANES] + sb[:, 0:LANES], sf[:, LANES:] - sb[:, LANES:]], axis=1) * inv_n

    chunk = min(MID_ROWS, nch * h1)

    def body(r, carry):
        rows = pl.ds(pl.multiple_of(r * chunk, chunk), chunk)
        o_ref[rows, :] = two_sided(af_ref[rows, :], ab_ref[rows, :])
        return carry

    lax.fori_loop(0, (nch * h1) // chunk, body, 0)
    oside_ref[...] = two_sided(afs_ref[...], abs_ref[...])


def _filter_spectrum(filt, L, d_hyena, tabs):
    h1 = tabs["H1"]
    order = filt.shape[1] // (2 * d_hyena)
    cb = LANES
    nblk = d_hyena // cb
    full = lambda a: pl.BlockSpec(a.shape, lambda o, c: (0,) * a.ndim)
    main = pltpu.VMEM((cb * h1, 2 * LANES), F32)
    side = pltpu.VMEM((cb, 2 * LANES), F32)
    return pl.pallas_call(
        functools.partial(_filter_spectrum_kernel, n1=tabs["N1"], inv_n=1.0 / (2 * L)),
        grid=(order, nblk),
        in_specs=[pl.BlockSpec((L, cb), lambda o, c: (0, o * 2 * nblk + c)),
                  pl.BlockSpec((L, cb), lambda o, c: (0, o * 2 * nblk + nblk + c)),
                  full(tabs["m1"]), full(tabs["g2"]), full(tabs["tw"])],
        out_specs=[pl.BlockSpec((cb * h1, 2 * LANES), lambda o, c: (o * nblk + c, 0)),
                   pl.BlockSpec((cb, 2 * LANES), lambda o, c: (o * nblk + c, 0))],
        out_shape=[jax.ShapeDtypeStruct((order * d_hyena * h1, 2 * LANES), F32),
                   jax.ShapeDtypeStruct((order * d_hyena, 2 * LANES), F32)],
        scratch_shapes=[pltpu.VMEM((h1 * S_PITCH, LANES), F32), pltpu.VMEM((h1 * S_PITCH, LANES), F32),
                        main, main, side, side],
        compiler_params=_params("parallel", "parallel"),
        name="filter_spectrum",
    )(filt, filt, tabs["m1"], tabs["g2"], tabs["tw"])


def _hyena_kernel(v_ref, vnext_ref, x1_ref, x2_ref, h_ref, hside_ref, d_ref, m1_ref, minv_ref, g2_ref, g2i_ref,
                  tw_ref, o_ref, s_ref, sv_ref, sn_ref, sx_ref, sz_ref, ab_ref, side_ref, *, n1):
    nch = v_ref.shape[2]
    h1 = n1 // 2
    m1, minv, g2, g2i = m1_ref[...], minv_ref[...], g2_ref[...], g2i_ref[...]

    def slab_of(p_ref):
        return lambda i: p_ref[0, _slab_rows(i), :].astype(F32)

    def transposer(p_ref, dst_ref):
        def work(g, n_iter):
            per_iter = h1 // n_iter
            for u in range(per_iter):
                _slab_transpose_one(slab_of(p_ref), [dst_ref], g * per_iter + u)
        return work

    def long_conv(order, src_ref, gate_ref, skip_ref, keep_ref=None, inv_side_work=None):
        _fft_stage_a(src_ref, nch, m1, tw_ref, ab_ref, side_ref, n1, side_work=transposer(gate_ref, sx_ref))
        _fft_mid(ab_ref, side_ref, lambda rows: h_ref[order, rows, :], lambda: hside_ref[order], g2, g2i, nch * h1)
        _fft_stage_a_inv(ab_ref, side_ref, nch, minv, tw_ref, s_ref, n1, side_work=inv_side_work)
        d_slab = jnp.broadcast_to(d_ref[:, order:order + 1], (nch, LANES))

        def gate(rows):
            val = sx_ref[rows, :] * (s_ref[rows, :] + d_slab * skip_ref[rows, :])
            s_ref[rows, :] = val
            if keep_ref is not None:
                keep_ref[rows, :] = val

        _slab_map(gate, h1)

    first = pl.program_id(1) == 0

    @pl.when(first)
    def _():
        _slab_transpose_in(slab_of(v_ref), [sv_ref], h1)

    @pl.when(jnp.logical_not(first))
    def _():
        def copy(rows):
            sv_ref[rows, :] = sn_ref[rows, :]

        _slab_map(copy, h1)

    long_conv(0, sv_ref, x1_ref, sv_ref, keep_ref=sz_ref, inv_side_work=transposer(vnext_ref, sn_ref))
    long_conv(1, s_ref, x2_ref, sz_ref)

    def put(i, out):
        o_ref[0, _slab_rows(i), :] = out.astype(o_ref.dtype)

    _slab_transpose_out(s_ref, put, h1)


def _hyena(proj, spec, spec_side, skip, tabs):
    B, L, _ = proj.shape
    D = skip.shape[1]
    n1, h1 = tabs["N1"], tabs["H1"]
    cb = LANES
    nblk = D // cb
    spec = spec.reshape(2, D * h1, 2 * LANES)
    spec_side = spec_side.reshape(2, D, 2 * LANES)
    col = lambda part: pl.BlockSpec((1, L, cb), lambda c, b: (b, 0, part * nblk + c))
    full = lambda a: pl.BlockSpec(a.shape, lambda c, b: (0,) * a.ndim)
    slabs = pltpu.VMEM((h1 * S_PITCH, LANES), F32)
    return pl.pallas_call(
        functools.partial(_hyena_kernel, n1=n1),
        grid=(nblk, B),
        in_specs=[col(0),
                  pl.BlockSpec((1, L, cb), lambda c, b: (jnp.minimum(b + 1, B - 1), 0, c)),
                  col(1), col(2),
                  pl.BlockSpec((2, cb * h1, 2 * LANES), lambda c, b: (0, c, 0), pipeline_mode=pl.Buffered(1)),
                  pl.BlockSpec((2, cb, 2 * LANES), lambda c, b: (0, c, 0)),
                  pl.BlockSpec((cb, 2), lambda c, b: (c, 0)),
                  full(tabs["m1"]), full(tabs["minv"]), full(tabs["g2"]), full(tabs["g2i"]), full(tabs["tw"])],
        out_specs=pl.BlockSpec((1, L, cb), lambda c, b: (b, 0, c)),
        out_shape=jax.ShapeDtypeStruct((B, L, D), BF16),
        scratch_shapes=[slabs, slabs, slabs, slabs, slabs, pltpu.VMEM((cb * h1, 2 * LANES), F32),
                        pltpu.VMEM((cb, 2 * LANES), F32)],
        compiler_params=_params("parallel", "arbitrary"),
        name="hyena_fftconv",
    )(proj, proj, proj, proj, spec, spec_side, skip.astype(F32).T,
      tabs["m1"], tabs["minv"], tabs["g2"], tabs["g2i"], tabs["tw"])


ML_CHUNK = 512
ML_CHUNKS_PER_STEP = 1


def _log_sigmoid(x):
    return jnp.minimum(x, 0.0) - jnp.log1p(jnp.exp(-jnp.abs(x)))


def _mlstm_direction(q, k, v, i_row, cum_row, logf_row, c_ref, m_ref, mask, scale):
    c, dh = q.shape
    g_row = i_row - cum_row
    f_col_rep = jnp.broadcast_to(cum_row, (LANES, c)).T
    m_prev = m_ref[0:1, 0:1]
    log_scale = math.log(scale)
    e = jnp.where(mask, g_row + log_scale, -jnp.inf)
    a = jnp.maximum(jnp.max(e, axis=1, keepdims=True) - log_scale, m_prev)
    sc = lax.dot_general(q.astype(BF16), k.astype(BF16), (((1,), (1,)), ((), ())), preferred_element_type=F32)
    p = sc * jnp.exp(e - a)
    vaug = jnp.concatenate([v, jnp.ones_like(v)], axis=1)
    inter = jnp.exp(m_prev - a)
    tot = _bdot(p, vaug) + inter * _bdot(q, c_ref[...])
    num, den = tot[:, 0:dh], tot[:, dh:2 * dh]
    m_t = f_col_rep + a
    h = num / jnp.maximum(jnp.abs(den), jnp.exp(-m_t))
    a_end = jnp.maximum(jnp.max(g_row, axis=1, keepdims=True), m_prev)
    f_end = jnp.sum(logf_row, axis=1, keepdims=True)
    we = jnp.exp(g_row - a_end) * scale
    kw = k.astype(F32).T * we
    c_ref[...] = jnp.exp(m_prev - a_end) * c_ref[...] + _bdot(kw, vaug)
    m_ref[...] = jnp.broadcast_to(f_end + a_end, m_ref.shape)
    return h


def _mlstm_kernel(qf_ref, kf_ref, vf_ref, gf_ref, qb_ref, kb_ref, vb_ref, gb_ref, bias_ref,
                  hf_ref, hb_ref, c_ref, m_ref, *, heads, scale, chunk):
    @pl.when(pl.program_id(1) == 0)
    def _():
        c_ref[...] = jnp.zeros_like(c_ref)
        m_ref[...] = jnp.zeros_like(m_ref)

    c = chunk
    nsub = qf_ref.shape[1] // c
    dh = qf_ref.shape[2] // heads
    r = lax.broadcasted_iota(jnp.int32, (c, c), 0)
    s = lax.broadcasted_iota(jnp.int32, (c, c), 1)
    mask_f = s <= r
    mask_b = s >= r

    def gate_rows(g, mask_t):
        g = g + bias_ref[...]
        logf = _log_sigmoid(g)
        tri = mask_t.astype(BF16)
        cum = sum(jnp.dot(part, tri, preferred_element_type=F32) for part in _split3(logf))
        return g, logf, cum

    for u in range(nsub):
        rf = slice(u * c, (u + 1) * c)
        rb = slice((nsub - 1 - u) * c, (nsub - u) * c)
        g_f, logf_f, cum_f = gate_rows(gf_ref[:, rf], mask_b)
        g_b, logf_b, cum_b = gate_rows(gb_ref[:, rb], mask_f)
        for h in range(heads):
            sl = slice(h * dh, (h + 1) * dh)
            fi, ff, bi, bf = h, heads + h, 2 * heads + h, 3 * heads + h
            hf_ref[0, rf, sl] = _mlstm_direction(
                qf_ref[0, rf, sl], kf_ref[0, rf, sl], vf_ref[0, rf, sl], g_f[fi:fi + 1], cum_f[ff:ff + 1],
                logf_f[ff:ff + 1], c_ref.at[h], m_ref.at[h], mask_f, scale).astype(hf_ref.dtype)
            hb_ref[0, rb, sl] = _mlstm_direction(
                qb_ref[0, rb, sl], kb_ref[0, rb, sl], vb_ref[0, rb, sl], g_b[bi:bi + 1], cum_b[bf:bf + 1],
                logf_b[bf:bf + 1], c_ref.at[heads + h], m_ref.at[heads + h], mask_b, scale).astype(hb_ref.dtype)


def _mlstm(qk, proj, v_col0, gates, gate_b, d_ml):
    B, L, _ = qk.shape
    H = MLSTM_HEADS
    dh = d_ml // H
    c = min(ML_CHUNK, L)
    rows = min(ML_CHUNKS_PER_STEP * c, L)
    nb = L // rows
    bias = jnp.broadcast_to(gate_b.astype(F32)[:, None], (4 * H, c))
    vb0 = v_col0 // d_ml
    specs = []
    for pos in (lambda ci: ci, lambda ci: nb - 1 - ci):
        specs += [pl.BlockSpec((1, rows, d_ml), lambda bi, ci, pos=pos: (bi, pos(ci), 0)),
                  pl.BlockSpec((1, rows, d_ml), lambda bi, ci, pos=pos: (bi, pos(ci), 1)),
                  pl.BlockSpec((1, rows, d_ml), lambda bi, ci, pos=pos: (bi, pos(ci), vb0)),
                  pl.BlockSpec((4 * H, rows), lambda bi, ci, pos=pos: (0, bi * nb + pos(ci)))]
    specs.append(pl.BlockSpec((4 * H, c), lambda bi, ci: (0, 0)))
    out_shape = jax.ShapeDtypeStruct((B, L, d_ml), BF16)
    return pl.pallas_call(
        functools.partial(_mlstm_kernel, heads=H, scale=dh ** -0.5, chunk=c),
        grid=(B, nb),
        in_specs=specs,
        out_specs=[pl.BlockSpec((1, rows, d_ml), lambda bi, ci: (bi, ci, 0)),
                   pl.BlockSpec((1, rows, d_ml), lambda bi, ci: (bi, nb - 1 - ci, 0))],
        out_shape=[out_shape, out_shape],
        scratch_shapes=[pltpu.VMEM((2 * H, dh, 2 * dh), F32), pltpu.VMEM((2 * H, 1, LANES), F32)],
        compiler_params=_params("parallel", "arbitrary"),
        name="mlstm_scan",
    )(qk, qk, proj, gates, qk, qk, proj, gates, bias)


def _group_rms(y, gain, bd, group):
    sq = (y * y).astype(BF16)
    w = bd.shape[0]
    ss = jnp.concatenate([jnp.dot(sq[:, i:i + w], bd, preferred_element_type=F32)
                          for i in range(0, y.shape[1], w)], axis=1)
    return y * lax.rsqrt(ss * (1.0 / group) + EPS) * gain


GN_STRIPE = 256


def _block_diag_ones(n, group):
    idx = np.arange(n) // group
    return jnp.asarray((idx[:, None] == idx[None, :]).astype(np.float32)).astype(BF16)


def _mix_xattn_kernel(z_ref, hf_ref, hb_ref, o_ref, x_ref, mem_ref, ghy_ref, gml_ref, why_ref, wml_ref, bdh_ref,
                      bdm_ref, gx_ref, gm_ref, wq_ref, wkv_ref, wo_ref, out_ref, kv_ref,
                      *, hy_group, ml_group, heads, scale):
    @pl.when(pl.program_id(1) == 0)
    def _():
        kv_ref[...] = _bdot(_rms(mem_ref[0], gm_ref[...]), wkv_ref[...]).astype(kv_ref.dtype)

    y_hy = _group_rms(z_ref[0].astype(F32), ghy_ref[...], bdh_ref[...], hy_group)
    h_sum = hf_ref[0].astype(F32) + hb_ref[0].astype(F32)
    y_ml = _group_rms(jax.nn.sigmoid(o_ref[0].astype(F32)) * h_sum, gml_ref[...], bdm_ref[...], ml_group)
    h1 = x_ref[0] + _bdot(y_hy, why_ref[...]) + _bdot(y_ml, wml_ref[...])

    D = h1.shape[1]
    dh = D // heads
    q = _bdot(_rms(h1, gx_ref[...]), wq_ref[...])
    outs = []
    for hd in range(heads):
        qh = q[:, hd * dh:(hd + 1) * dh]
        kh = kv_ref[:, hd * dh:(hd + 1) * dh]
        vh = kv_ref[:, D + hd * dh:D + (hd + 1) * dh]
        s = lax.dot_general(qh.astype(BF16), kh, (((1,), (1,)), ((), ())), preferred_element_type=F32) * scale
        e = jnp.exp(s - jnp.max(s, axis=-1, keepdims=True))
        p = e / jnp.sum(e, axis=-1, keepdims=True)
        outs.append(_bdot(p, vh))
    out_ref[0] = h1 + _bdot(jnp.concatenate(outs, axis=1), wo_ref[...])


def _mix_xattn(z, hf, hb, proj, o_col0, x, mem, g_hy, g_ml, w_out, g_x, g_mem, wq, wk, wv, wo, tm=512):
    B, L, d_hy = z.shape
    d_ml = hf.shape[2]
    D = x.shape[2]
    nm = mem.shape[1]
    ob = o_col0 // d_ml
    row = lambda n: pl.BlockSpec((1, tm, n), lambda bi, i: (bi, i, 0))
    const = lambda a: pl.BlockSpec(a.shape, lambda bi, i: (0,) * a.ndim, pipeline_mode=pl.Buffered(1))
    hy_group, ml_group = d_hy // HYENA_GROUPS, d_ml // MLSTM_HEADS
    consts = [g_hy.reshape(1, -1), g_ml.reshape(1, -1), w_out[:d_hy].astype(BF16), w_out[d_hy:].astype(BF16),
              _block_diag_ones(GN_STRIPE, hy_group), _block_diag_ones(GN_STRIPE, ml_group), g_x.reshape(1, D),
              g_mem.reshape(1, D), wq.astype(BF16), jnp.concatenate([wk, wv], axis=1).astype(BF16), wo.astype(BF16)]
    return pl.pallas_call(
        functools.partial(_mix_xattn_kernel, hy_group=hy_group, ml_group=ml_group, heads=XATTN_HEADS,
                          scale=(D // XATTN_HEADS) ** -0.5),
        grid=(B, L // tm),
        in_specs=[row(d_hy), row(d_ml), row(d_ml), pl.BlockSpec((1, tm, d_ml), lambda bi, i: (bi, i, ob)), row(D),
                  pl.BlockSpec((1, nm, D), lambda bi, i: (bi, 0, 0))] + [const(a) for a in consts],
        out_specs=row(D),
        out_shape=jax.ShapeDtypeStruct((B, L, D), F32),
        scratch_shapes=[pltpu.VMEM((nm, 2 * D), BF16)],
        compiler_params=_params("parallel", "arbitrary"),
        name="mix_xattn",
    )(z, hf, hb, proj, x, mem, *consts)


MLP_CHUNK = 1024


def _mlp_kernel(h_ref, g_ref, w1_ref, w2_ref, gf_ref, out_ref, xn_ref, *, final_norm):
    j = pl.program_id(1)

    @pl.when(j == 0)
    def _():
        x = h_ref[...]
        xn_ref[...] = _rms(x, g_ref[...]).astype(BF16)
        out_ref[...] = x

    for c in range(w1_ref.shape[1] // MLP_CHUNK):
        cols = slice(c * MLP_CHUNK, (c + 1) * MLP_CHUNK)
        a = jnp.maximum(jnp.dot(xn_ref[...], w1_ref[:, cols], preferred_element_type=F32), 0.0)
        out_ref[...] += _bdot(a * a, w2_ref[cols, :])

    if final_norm:
        @pl.when(j == pl.num_programs(1) - 1)
        def _():
            out_ref[...] = _rms(out_ref[...], gf_ref[...])


def _mlp(h, g, w1, w2, gf, final_norm, tm=1024, tf=2048):
    M, D = h.shape
    dff = w1.shape[1]
    return pl.pallas_call(
        functools.partial(_mlp_kernel, final_norm=final_norm),
        grid=(M // tm, dff // tf),
        in_specs=[pl.BlockSpec((tm, D), lambda i, j: (i, 0)), pl.BlockSpec((1, D), lambda i, j: (0, 0)),
                  pl.BlockSpec((D, tf), lambda i, j: (0, j)), pl.BlockSpec((tf, D), lambda i, j: (j, 0)),
                  pl.BlockSpec((1, D), lambda i, j: (0, 0))],
        out_specs=pl.BlockSpec((tm, D), lambda i, j: (i, 0)),
        out_shape=jax.ShapeDtypeStruct((M, D), F32),
        scratch_shapes=[pltpu.VMEM((tm, D), BF16)],
        compiler_params=_params("parallel", "arbitrary"),
        name="mlp",
    )(h, g.reshape(1, D), w1.astype(BF16), w2.astype(BF16), gf.reshape(1, D))


def _pad_cols(w, n):
    return jnp.pad(w, ((0, 0), (0, n - w.shape[1])))


def _layer(h, mem, p, l):
    B, L, D = h.shape
    M = B * L
    d_hy = p["hy_norm_g"].shape[1]
    d_ml = p["ml_norm_g"].shape[1]
    hy_cols = 3 * d_hy
    qk_cols = 2 * d_ml
    n_gate = 4 * MLSTM_HEADS
    w_in = p["w_in"][l]
    vo0 = hy_cols + qk_cols
    g0 = vo0 + 2 * d_ml
    v_col0 = hy_cols
    o_col0 = hy_cols + d_ml
    n_in = hy_cols + 2 * d_ml
    proj, qk, gates = _in_proj(h.reshape(M, D), p["norm_mix_g"][l], w_in[:, :vo0].astype(BF16),
                               w_in[:, vo0:g0].astype(BF16),
                               _pad_cols(w_in[:, g0:g0 + n_gate], LANES).astype(BF16), n_gate,
                               jnp.concatenate([p["hy_conv_w"][l], p["ml_conv_w"][l]], axis=1),
                               jnp.concatenate([p["hy_conv_b"][l], p["ml_conv_b"][l]]),
                               n_hy=hy_cols, seq_len=L, tm=min(1024, L))
    proj = proj.reshape(B, L, n_in)
    qk = qk.reshape(B, L, qk_cols)

    tabs = _fft_tables(L)
    filt = _hyena_filters(L, p["hy_filt_w1"][l], p["hy_filt_b1"][l], p["hy_filt_freq1"][l], p["hy_filt_w2"][l],
                          p["hy_filt_b2"][l], p["hy_filt_freq2"][l], p["hy_filt_w3"][l], d_hy,
                          tl=min(512, L))
    spec, spec_side = _filter_spectrum(filt, L, d_hy, tabs)
    z_hy = _hyena(proj, spec, spec_side, p["hy_skip"][l], tabs)

    h_f, h_b = _mlstm(qk, proj, v_col0, gates, p["ml_gate_b"][l], d_ml)

    return _mix_xattn(z_hy, h_f, h_b, proj, o_col0, h, mem, p["hy_norm_g"][l], p["ml_norm_g"][l], p["w_out"][l],
                      p["norm_x_g"][l], p["norm_mem_g"][l], p["xa_wq"][l], p["xa_wk"][l], p["xa_wv"][l],
                      p["xa_wo"][l], tm=min(1024, L))


def kernel(x, mem, norm_mix_g, w_in, hy_conv_w, hy_conv_b, hy_filt_w1, hy_filt_b1, hy_filt_freq1, hy_filt_w2,
           hy_filt_b2, hy_filt_freq2, hy_filt_w3, hy_skip, hy_norm_g, ml_conv_w, ml_conv_b, ml_gate_b, ml_norm_g,
           w_out, norm_x_g, norm_mem_g, xa_wq, xa_wk, xa_wv, xa_wo, norm_ff_g, ff_w1, ff_w2, final_norm_g):
    p = dict(norm_mix_g=norm_mix_g, w_in=w_in, hy_conv_w=hy_conv_w, hy_conv_b=hy_conv_b, hy_filt_w1=hy_filt_w1,
             hy_filt_b1=hy_filt_b1, hy_filt_freq1=hy_filt_freq1, hy_filt_w2=hy_filt_w2, hy_filt_b2=hy_filt_b2,
             hy_filt_freq2=hy_filt_freq2, hy_filt_w3=hy_filt_w3, hy_skip=hy_skip, hy_norm_g=hy_norm_g,
             ml_conv_w=ml_conv_w, ml_conv_b=ml_conv_b, ml_gate_b=ml_gate_b, ml_norm_g=ml_norm_g, w_out=w_out,
             norm_x_g=norm_x_g, norm_mem_g=norm_mem_g, xa_wq=xa_wq, xa_wk=xa_wk, xa_wv=xa_wv, xa_wo=xa_wo)
    B, L, D = x.shape
    depth = w_in.shape[0]
    h = x
    for l in range(depth):
        h = _layer(h, mem, p, l)
        h = _mlp(h.reshape(B * L, D), norm_ff_g[l], ff_w1[l], ff_w2[l], final_norm_g, final_norm=l == depth - 1,
                 tm=min(1024, B * L)).reshape(B, L, D)
    return h
```

```python
import functools
import math

import numpy as np
import jax
import jax.numpy as jnp
from jax import lax
from jax.experimental import pallas as pl
from jax.experimental.pallas import tpu as pltpu

F32 = jnp.float32
BF16 = jnp.bfloat16

EPS = 1e-6
HYENA_GROUPS = 8
MLSTM_HEADS = 4
XATTN_HEADS = 4
FILTER_BANDS = 16
DECAY_TARGET = 1e-2
SHORT_DECAY_PCT = 0.3
LONG_DECAY_PCT = 1.5

LANES = 128
VMEM_LIMIT = 56 * 1024 * 1024

IN_PROJ_ROWS = 1024
FILTER_ROWS = 512
MIX_XATTN_ROWS = 1024
MLP_ROWS = 1024
MLP_HIDDEN = 2048


def _params(*sem):
    return pltpu.CompilerParams(dimension_semantics=sem, vmem_limit_bytes=VMEM_LIMIT)


def _bdot(a, b):
    return jnp.dot(a.astype(BF16), b.astype(BF16), preferred_element_type=F32)


def _split3(x):
    hi = x.astype(BF16)
    r = x - hi.astype(F32)
    mid = r.astype(BF16)
    lo = (r - mid.astype(F32)).astype(BF16)
    return hi, mid, lo


IN_CHUNK = 512
HALO = 8


def _rms(x, g):
    return x * lax.rsqrt(jnp.mean(x * x, axis=-1, keepdims=True) + EPS) * g


def _in_proj_kernel(x_ref, xb_ref, xa_ref, g_ref, wc_ref, wp_ref, wg_ref, cw_ref, cb_ref,
                    proj_ref, qk_ref, og_ref, xn_ref, *, tiles_per_seq, n_hy):
    tm = x_ref.shape[0]
    g = g_ref[...]
    x = x_ref[...]
    xn_ref[...] = (x * g).astype(BF16)
    r = lax.rsqrt(jnp.mean(x * x, axis=-1, keepdims=True) + EPS)
    og_ref[...] = (jnp.dot(xn_ref[...], wg_ref[...], preferred_element_type=F32) * r).T[0:og_ref.shape[0], :]
    for c in range(wp_ref.shape[1] // IN_CHUNK):
        cols = slice(c * IN_CHUNK, (c + 1) * IN_CHUNK)
        out_cols = slice(n_hy + c * IN_CHUNK, n_hy + (c + 1) * IN_CHUNK)
        proj_ref[:, out_cols] = (jnp.dot(xn_ref[...], wp_ref[:, cols], preferred_element_type=F32) * r
                                 ).astype(proj_ref.dtype)

    halo = _rms(jnp.concatenate([xb_ref[...], xa_ref[...]], axis=0), g).astype(BF16)
    pos = pl.program_id(0) % tiles_per_seq
    row = lax.broadcasted_iota(jnp.int32, (tm, IN_CHUNK), 0)
    for c in range(wc_ref.shape[1] // IN_CHUNK):
        cols = slice(c * IN_CHUNK, (c + 1) * IN_CHUNK)
        u = jnp.dot(xn_ref[...], wc_ref[:, cols], preferred_element_type=F32) * r
        uh = jnp.dot(halo, wc_ref[:, cols], preferred_element_type=F32)
        before = jnp.where(pos == 0, 0.0, uh[HALO - 1:HALO, :])
        after = jnp.where(pos == tiles_per_seq - 1, 0.0, uh[HALO:HALO + 1, :])
        prev = jnp.where(row == 0, before, pltpu.roll(u, 1, 0))
        nxt = jnp.where(row == tm - 1, after, pltpu.roll(u, tm - 1, 0))
        y = prev * cw_ref[0:1, cols] + u * cw_ref[1:2, cols] + nxt * cw_ref[2:3, cols] + cb_ref[:, cols]
        if c * IN_CHUNK < n_hy:
            proj_ref[:, cols] = y.astype(proj_ref.dtype)
        else:
            qk_cols = slice(c * IN_CHUNK - n_hy, (c + 1) * IN_CHUNK - n_hy)
            qk_ref[:, qk_cols] = (y * jax.nn.sigmoid(y)).astype(qk_ref.dtype)


def _in_proj(x, g, w_conv, w_plain, w_gate, n_gate, conv_w, conv_b, n_hy, seq_len):
    M, K = x.shape
    nc, npl = w_conv.shape[1], w_plain.shape[1]
    tm = min(IN_PROJ_ROWS, seq_len)
    hb = tm // HALO
    const = lambda a: pl.BlockSpec(a.shape, lambda i: (0,) * a.ndim, pipeline_mode=pl.Buffered(1))
    row = lambda n: pl.BlockSpec((tm, n), lambda i: (i, 0))
    g = g.reshape(1, K)
    conv_b = conv_b.reshape(1, nc)
    return pl.pallas_call(
        functools.partial(_in_proj_kernel, tiles_per_seq=seq_len // tm, n_hy=n_hy),
        grid=(M // tm,),
        in_specs=[row(K),
                  pl.BlockSpec((HALO, K), lambda i: (jnp.maximum(i * hb - 1, 0), 0)),
                  pl.BlockSpec((HALO, K), lambda i: (jnp.minimum((i + 1) * hb, M // HALO - 1), 0)),
                  const(g), const(w_conv), const(w_plain), const(w_gate), const(conv_w), const(conv_b)],
        out_specs=[row(n_hy + npl), row(nc - n_hy), pl.BlockSpec((n_gate, tm), lambda i: (0, i))],
        out_shape=[jax.ShapeDtypeStruct((M, n_hy + npl), BF16), jax.ShapeDtypeStruct((M, nc - n_hy), BF16),
                   jax.ShapeDtypeStruct((n_gate, M), F32)],
        scratch_shapes=[pltpu.VMEM((tm, K), BF16)],
        compiler_params=_params("parallel"),
        name="in_proj",
    )(x, x, x, g, w_conv, w_plain, w_gate, conv_w, conv_b)


def _filter_kernel(feat_ref, w1_ref, b1_ref, f1_ref, w2_ref, b2_ref, f2_ref, w3h_ref, w3l_ref, dec_ref, dir_ref,
                   o_ref, *, L):
    tl = o_ref.shape[0]
    hp = lax.Precision.HIGHEST
    pos = (lax.broadcasted_iota(jnp.int32, (tl, LANES), 0) + pl.program_id(0) * tl).astype(F32)
    lane = lax.broadcasted_iota(jnp.int32, (tl, LANES), 1)
    t = pos * (1.0 / (L - 1))
    arg = feat_ref[...] * (pos * (2.0 * math.pi / L))
    z = jnp.where(lane < FILTER_BANDS, jnp.cos(arg),
                  jnp.where(lane < 2 * FILTER_BANDS, -jnp.sin(arg),
                            jnp.where(lane == 2 * FILTER_BANDS, t, 0.0)))
    hid = jnp.sin(f1_ref[...] * (jnp.dot(z, w1_ref[...], precision=hp, preferred_element_type=F32) + b1_ref[...]))
    hid = jnp.sin(f2_ref[...] * (jnp.dot(hid, w2_ref[...], precision=hp, preferred_element_type=F32) + b2_ref[...]))
    hid_hi = hid.astype(BF16)
    hid_lo = (hid - hid_hi.astype(F32)).astype(BF16)
    filt = (jnp.dot(hid_hi, w3h_ref[...], preferred_element_type=F32)
            + jnp.dot(hid_hi, w3l_ref[...], preferred_element_type=F32)
            + jnp.dot(hid_lo, w3h_ref[...], preferred_element_type=F32))
    filt = filt * jnp.exp(-t[:, 0:1] * dec_ref[...])
    filt = jnp.where(pos[:, 0:1] == 0.0, filt * dir_ref[...], filt)
    o_ref[...] = filt


def _hyena_filters(L, w1, b1, fr1, w2, b2, fr2, w3, d_hyena):
    tl = min(FILTER_ROWS, L)
    n_emb, n_hid = w1.shape
    n_out = w3.shape[1]
    bands = jnp.linspace(1e-4, FILTER_BANDS - 1, FILTER_BANDS, dtype=F32)
    feat = jnp.zeros((1, LANES), F32).at[0, :FILTER_BANDS].set(bands).at[0, FILTER_BANDS:2 * FILTER_BANDS].set(bands)
    w1p = jnp.zeros((LANES, n_hid), F32).at[:n_emb - 1].set(w1[1:]).at[n_emb - 1].set(w1[0])
    max_decay = math.log(DECAY_TARGET) / SHORT_DECAY_PCT
    min_decay = math.log(DECAY_TARGET) / LONG_DECAY_PCT
    deltas = jnp.abs(jnp.linspace(min_decay, max_decay, d_hyena, dtype=F32))
    reps = n_out // d_hyena
    dec = jnp.tile(deltas, reps).reshape(1, n_out)
    dirmask = jnp.tile(jnp.concatenate([jnp.ones((d_hyena,), F32), jnp.zeros((d_hyena,), F32)]), reps // 2)
    w3_hi = w3.astype(BF16)
    w3_lo = (w3 - w3_hi.astype(F32)).astype(BF16)
    full = lambda shape: pl.BlockSpec(shape, lambda i: (0,) * len(shape))
    return pl.pallas_call(
        functools.partial(_filter_kernel, L=L),
        grid=(L // tl,),
        in_specs=[full((1, LANES)), full((LANES, n_hid)), full((1, n_hid)), full((1, n_hid)),
                  full((n_hid, n_hid)), full((1, n_hid)), full((1, n_hid)), full((n_hid, n_out)),
                  full((n_hid, n_out)), full((1, n_out)), full((1, n_out))],
        out_specs=pl.BlockSpec((tl, n_out), lambda i: (i, 0)),
        out_shape=jax.ShapeDtypeStruct((L, n_out), F32),
        compiler_params=_params("parallel"),
        name="hyena_filters",
    )(feat, w1p, b1.reshape(1, -1), fr1.reshape(1, -1), w2, b2.reshape(1, -1), fr2.reshape(1, -1), w3_hi, w3_lo,
      dec, dirmask.reshape(1, n_out))


MID_ROWS = 4096
SIDE_PAD = 16


def _fft_tables(L):
    N = 2 * L
    N2 = LANES
    N1 = N // N2
    H1 = N1 // 2
    k1 = np.arange(H1)[:, None]
    n1 = np.arange(H1)[None, :]
    a1 = 2.0 * np.pi * ((k1 * n1) % N1) / N1
    c1, s1 = np.cos(a1), np.sin(a1)
    sign = np.where(np.arange(H1) % 2 == 0, 1.0, -1.0)[None, :]
    pad = np.zeros((SIDE_PAD - 1, H1))
    m1 = np.concatenate([c1, -s1, sign, pad], axis=0)
    wgt = np.where(np.arange(H1) == 0, 1.0, 2.0)[:, None]
    minv = np.concatenate([(wgt * c1).T, (-wgt * s1).T, sign.T, pad.T], axis=1)
    n2 = np.arange(N2)[:, None]
    k2 = np.arange(N2)[None, :]
    a2 = 2.0 * np.pi * ((n2 * k2) % N2) / N2
    cg, sg = np.cos(a2), np.sin(a2)
    g2 = np.block([[cg, -sg], [sg, cg]])
    g2i = np.block([[cg, sg], [-sg, cg]])
    at = 2.0 * np.pi * (np.arange(H1 + 8)[:, None] * np.arange(N2)[None, :]) / N
    tw = np.concatenate([np.cos(at), -np.sin(at)], axis=1)
    as_bf = lambda a: jnp.asarray(a.astype(np.float32)).astype(BF16)
    return dict(m1=as_bf(m1), minv=as_bf(minv), g2=as_bf(g2), g2i=as_bf(g2i),
                tw=jnp.asarray(tw.astype(np.float32)), N1=N1, H1=H1)


S_PITCH = LANES + 8
GROUP = 8


def _slab_scratch_rows(i):
    return pl.ds(i * S_PITCH, LANES)


def _slab_rows(i):
    return pl.ds(i * LANES, LANES)


def _slab_transpose_in(get_slab, dst_refs, h1):
    for i in range(h1):
        t = get_slab(i).T
        for dst in dst_refs:
            dst[_slab_scratch_rows(i), :] = t


def _slab_transpose_out(s_ref, put_slab, h1):
    for i in range(h1):
        put_slab(i, s_ref[_slab_scratch_rows(i), :].T)


def _slab_map(fn, h1):
    for i in range(h1):
        fn(_slab_scratch_rows(i))


def _fft_stage_a(s_ref, nch, m1, tw_ref, ab_ref, side_ref, n1):
    h1 = n1 // 2
    twr = tw_ref[0:h1, 0:LANES]
    twi = tw_ref[0:h1, LANES:2 * LANES]
    tsr = tw_ref[h1:h1 + 1, 0:LANES]
    tsi = tw_ref[h1:h1 + 1, LANES:2 * LANES]
    for c0 in range(0, nch, GROUP):
        rhs = jnp.concatenate([s_ref[pl.ds(c0 + i, h1, stride=S_PITCH), :] for i in range(GROUP)], axis=1)
        res = jnp.dot(m1, rhs.astype(BF16), preferred_element_type=F32)
        for i in range(GROUP):
            ar = res[0:h1, LANES * i:LANES * (i + 1)]
            ai = res[h1:2 * h1, LANES * i:LANES * (i + 1)]
            rows = pl.ds((c0 + i) * h1, h1)
            ab_ref[rows, 0:LANES] = ar * twr - ai * twi
            ab_ref[rows, LANES:2 * LANES] = ar * twi + ai * twr
        a_side = jnp.concatenate([res[2 * h1:2 * h1 + 1, LANES * i:LANES * (i + 1)] for i in range(GROUP)],
                                 axis=0)
        side_ref[c0:c0 + GROUP, 0:LANES] = a_side * tsr
        side_ref[c0:c0 + GROUP, LANES:2 * LANES] = a_side * tsi


def _cmul(a, b):
    ar, ai = a[:, 0:LANES], a[:, LANES:2 * LANES]
    br, bi = b[:, 0:LANES], b[:, LANES:2 * LANES]
    return jnp.concatenate([ar * br - ai * bi, ar * bi + ai * br], axis=1)


def _fft_mid(ab_ref, side_ref, get_h, get_h_side, g2, g2i, nrows):
    chunk = min(MID_ROWS, nrows)
    for r0 in range(0, nrows, chunk):
        rows = pl.ds(r0, chunk)
        spec = _bdot(ab_ref[rows, :], g2)
        ab_ref[rows, :] = _bdot(_cmul(spec, get_h(rows)), g2i)
    side_ref[...] = _bdot(_cmul(_bdot(side_ref[...], g2), get_h_side()), g2i)


def _fft_stage_a_inv(ab_ref, side_ref, nch, minv, tw_ref, s_ref, n1):
    h1 = n1 // 2
    twr = tw_ref[0:h1, 0:LANES]
    twi = tw_ref[0:h1, LANES:2 * LANES]
    tsr = tw_ref[h1:h1 + 1, 0:LANES]
    tsi = tw_ref[h1:h1 + 1, LANES:2 * LANES]
    first_row = lax.broadcasted_iota(jnp.int32, (SIDE_PAD, GROUP * LANES), 0) == 0
    for c0 in range(0, nch, GROUP):
        blk = ab_ref[c0 * h1:(c0 + GROUP) * h1, :]
        re, im = [], []
        for i in range(GROUP):
            br = blk[i * h1:(i + 1) * h1, 0:LANES]
            bi = blk[i * h1:(i + 1) * h1, LANES:2 * LANES]
            re.append(br * twr + bi * twi)
            im.append(bi * twr - br * twi)
        side = side_ref[c0:c0 + GROUP, :]
        side_re = side[:, 0:LANES] * tsr + side[:, LANES:2 * LANES] * tsi
        side_row = jnp.concatenate([side_re[i:i + 1, :] for i in range(GROUP)], axis=1)
        side_blk = jnp.where(first_row, jnp.broadcast_to(side_row, first_row.shape), 0.0)
        rhs = jnp.concatenate([jnp.concatenate(re, axis=1), jnp.concatenate(im, axis=1), side_blk], axis=0)
        y = jnp.dot(minv, rhs.astype(BF16), preferred_element_type=F32)
        for i in range(GROUP):
            s_ref[pl.ds(c0 + i, h1, stride=S_PITCH), :] = y[:, LANES * i:LANES * (i + 1)]


def _filter_spectrum_kernel(hf_ref, hb_ref, m1_ref, g2_ref, tw_ref, o_ref, oside_ref,
                            s_ref, s2_ref, af_ref, ab_ref, afs_ref, abs_ref, *, n1, inv_n):
    nch = hf_ref.shape[1]
    h1 = n1 // 2
    m1, g2 = m1_ref[...], g2_ref[...]
    _slab_transpose_in(lambda i: hf_ref[_slab_rows(i), :], [s_ref], h1)
    _slab_transpose_in(lambda i: hb_ref[_slab_rows(i), :], [s2_ref], h1)
    _fft_stage_a(s_ref, nch, m1, tw_ref, af_ref, afs_ref, n1)
    _fft_stage_a(s2_ref, nch, m1, tw_ref, ab_ref, abs_ref, n1)

    def two_sided(fwd, bwd):
        sf = _bdot(fwd, g2)
        sb = _bdot(bwd, g2)
        return jnp.concatenate([sf[:, 0:LANES] + sb[:, 0:LANES], sf[:, LANES:] - sb[:, LANES:]], axis=1) * inv_n

    chunk = min(MID_ROWS, nch * h1)
    for r0 in range(0, nch * h1, chunk):
        rows = pl.ds(r0, chunk)
        o_ref[rows, :] = two_sided(af_ref[rows, :], ab_ref[rows, :])
    oside_ref[...] = two_sided(afs_ref[...], abs_ref[...])


def _filter_spectrum(filt, L, d_hyena, tabs):
    h1 = tabs["H1"]
    order = filt.shape[1] // (2 * d_hyena)
    cb = LANES
    nblk = d_hyena // cb
    full = lambda a: pl.BlockSpec(a.shape, lambda o, c: (0,) * a.ndim)
    main = pltpu.VMEM((cb * h1, 2 * LANES), F32)
    side = pltpu.VMEM((cb, 2 * LANES), F32)
    return pl.pallas_call(
        functools.partial(_filter_spectrum_kernel, n1=tabs["N1"], inv_n=1.0 / (2 * L)),
        grid=(order, nblk),
        in_specs=[pl.BlockSpec((L, cb), lambda o, c: (0, o * 2 * nblk + c)),
                  pl.BlockSpec((L, cb), lambda o, c: (0, o * 2 * nblk + nblk + c)),
                  full(tabs["m1"]), full(tabs["g2"]), full(tabs["tw"])],
        out_specs=[pl.BlockSpec((cb * h1, 2 * LANES), lambda o, c: (o * nblk + c, 0)),
                   pl.BlockSpec((cb, 2 * LANES), lambda o, c: (o * nblk + c, 0))],
        out_shape=[jax.ShapeDtypeStruct((order * d_hyena * h1, 2 * LANES), F32),
                   jax.ShapeDtypeStruct((order * d_hyena, 2 * LANES), F32)],
        scratch_shapes=[pltpu.VMEM((h1 * S_PITCH, LANES), F32), pltpu.VMEM((h1 * S_PITCH, LANES), F32),
                        main, main, side, side],
        compiler_params=_params("parallel", "parallel"),
        name="filter_spectrum",
    )(filt, filt, tabs["m1"], tabs["g2"], tabs["tw"])


def _hyena_kernel(v_ref, vnext_ref, x1_ref, x2_ref, h_ref, hside_ref, d_ref, m1_ref, minv_ref, g2_ref, g2i_ref,
                  tw_ref, o_ref, s_ref, sv_ref, sn_ref, sx_ref, sz_ref, ab_ref, side_ref, *, n1):
    nch = v_ref.shape[2]
    h1 = n1 // 2
    m1, minv, g2, g2i = m1_ref[...], minv_ref[...], g2_ref[...], g2i_ref[...]

    def slab_of(p_ref):
        return lambda i: p_ref[0, _slab_rows(i), :].astype(F32)

    def long_conv(order, src_ref, gate_ref, skip_ref, keep_ref=None, prefetch_ref=None):
        _fft_stage_a(src_ref, nch, m1, tw_ref, ab_ref, side_ref, n1)
        _slab_transpose_in(slab_of(gate_ref), [sx_ref], h1)
        _fft_mid(ab_ref, side_ref, lambda rows: h_ref[order, rows, :], lambda: hside_ref[order], g2, g2i, nch * h1)
        _fft_stage_a_inv(ab_ref, side_ref, nch, minv, tw_ref, s_ref, n1)
        if prefetch_ref is not None:
            _slab_transpose_in(slab_of(prefetch_ref), [sn_ref], h1)
        d_slab = jnp.broadcast_to(d_ref[:, order:order + 1], (nch, LANES))

        def gate(rows):
            val = sx_ref[rows, :] * (s_ref[rows, :] + d_slab * skip_ref[rows, :])
            s_ref[rows, :] = val
            if keep_ref is not None:
                keep_ref[rows, :] = val

        _slab_map(gate, h1)

    first = pl.program_id(1) == 0

    @pl.when(first)
    def _():
        _slab_transpose_in(slab_of(v_ref), [sv_ref], h1)

    @pl.when(jnp.logical_not(first))
    def _():
        def copy(rows):
            sv_ref[rows, :] = sn_ref[rows, :]

        _slab_map(copy, h1)

    long_conv(0, sv_ref, x1_ref, sv_ref, keep_ref=sz_ref, prefetch_ref=vnext_ref)
    long_conv(1, s_ref, x2_ref, sz_ref)

    def put(i, out):
        o_ref[0, _slab_rows(i), :] = out.astype(o_ref.dtype)

    _slab_transpose_out(s_ref, put, h1)


def _hyena(proj, spec, spec_side, skip, tabs):
    B, L, _ = proj.shape
    D = skip.shape[1]
    n1, h1 = tabs["N1"], tabs["H1"]
    cb = LANES
    nblk = D // cb
    spec = spec.reshape(2, D * h1, 2 * LANES)
    spec_side = spec_side.reshape(2, D, 2 * LANES)
    col = lambda part: pl.BlockSpec((1, L, cb), lambda c, b: (b, 0, part * nblk + c))
    full = lambda a: pl.BlockSpec(a.shape, lambda c, b: (0,) * a.ndim)
    slabs = pltpu.VMEM((h1 * S_PITCH, LANES), F32)
    return pl.pallas_call(
        functools.partial(_hyena_kernel, n1=n1),
        grid=(nblk, B),
        in_specs=[col(0),
                  pl.BlockSpec((1, L, cb), lambda c, b: (jnp.minimum(b + 1, B - 1), 0, c)),
                  col(1), col(2),
                  pl.BlockSpec((2, cb * h1, 2 * LANES), lambda c, b: (0, c, 0), pipeline_mode=pl.Buffered(1)),
                  pl.BlockSpec((2, cb, 2 * LANES), lambda c, b: (0, c, 0)),
                  pl.BlockSpec((cb, 2), lambda c, b: (c, 0)),
                  full(tabs["m1"]), full(tabs["minv"]), full(tabs["g2"]), full(tabs["g2i"]), full(tabs["tw"])],
        out_specs=pl.BlockSpec((1, L, cb), lambda c, b: (b, 0, c)),
        out_shape=jax.ShapeDtypeStruct((B, L, D), BF16),
        scratch_shapes=[slabs, slabs, slabs, slabs, slabs, pltpu.VMEM((cb * h1, 2 * LANES), F32),
                        pltpu.VMEM((cb, 2 * LANES), F32)],
        compiler_params=_params("parallel", "arbitrary"),
        name="hyena_fftconv",
    )(proj, proj, proj, proj, spec, spec_side, skip.astype(F32).T,
      tabs["m1"], tabs["minv"], tabs["g2"], tabs["g2i"], tabs["tw"])


ML_CHUNK = 512


def _log_sigmoid(x):
    return jnp.minimum(x, 0.0) - jnp.log1p(jnp.exp(-jnp.abs(x)))


def _mlstm_direction(q, k, v, i_row, cum_row, logf_row, c_ref, m_ref, mask, scale):
    c, dh = q.shape
    g_row = i_row - cum_row
    f_col_rep = jnp.broadcast_to(cum_row, (LANES, c)).T
    m_prev = m_ref[0:1, 0:1]
    log_scale = math.log(scale)
    e = jnp.where(mask, g_row + log_scale, -jnp.inf)
    a = jnp.maximum(jnp.max(e, axis=1, keepdims=True) - log_scale, m_prev)
    sc = lax.dot_general(q.astype(BF16), k.astype(BF16), (((1,), (1,)), ((), ())), preferred_element_type=F32)
    p = sc * jnp.exp(e - a)
    vaug = jnp.concatenate([v, jnp.ones_like(v)], axis=1)
    inter = jnp.exp(m_prev - a)
    tot = _bdot(p, vaug) + inter * _bdot(q, c_ref[...])
    num, den = tot[:, 0:dh], tot[:, dh:2 * dh]
    m_t = f_col_rep + a
    h = num / jnp.maximum(jnp.abs(den), jnp.exp(-m_t))
    a_end = jnp.maximum(jnp.max(g_row, axis=1, keepdims=True), m_prev)
    f_end = jnp.sum(logf_row, axis=1, keepdims=True)
    we = jnp.exp(g_row - a_end) * scale
    kw = k.astype(F32).T * we
    c_ref[...] = jnp.exp(m_prev - a_end) * c_ref[...] + _bdot(kw, vaug)
    m_ref[...] = jnp.broadcast_to(f_end + a_end, m_ref.shape)
    return h


def _mlstm_kernel(qf_ref, kf_ref, vf_ref, gf_ref, qb_ref, kb_ref, vb_ref, gb_ref, bias_ref,
                  hf_ref, hb_ref, c_ref, m_ref, *, heads, scale, chunk):
    @pl.when(pl.program_id(1) == 0)
    def _():
        c_ref[...] = jnp.zeros_like(c_ref)
        m_ref[...] = jnp.zeros_like(m_ref)

    c = chunk
    nsub = qf_ref.shape[1] // c
    dh = qf_ref.shape[2] // heads
    r = lax.broadcasted_iota(jnp.int32, (c, c), 0)
    s = lax.broadcasted_iota(jnp.int32, (c, c), 1)
    mask_f = s <= r
    mask_b = s >= r

    def gate_rows(g, mask_t):
        g = g + bias_ref[...]
        logf = _log_sigmoid(g)
        tri = mask_t.astype(BF16)
        cum = sum(jnp.dot(part, tri, preferred_element_type=F32) for part in _split3(logf))
        return g, logf, cum

    for u in range(nsub):
        rf = slice(u * c, (u + 1) * c)
        rb = slice((nsub - 1 - u) * c, (nsub - u) * c)
        g_f, logf_f, cum_f = gate_rows(gf_ref[:, rf], mask_b)
        g_b, logf_b, cum_b = gate_rows(gb_ref[:, rb], mask_f)
        for h in range(heads):
            sl = slice(h * dh, (h + 1) * dh)
            fi, ff, bi, bf = h, heads + h, 2 * heads + h, 3 * heads + h
            hf_ref[0, rf, sl] = _mlstm_direction(
                qf_ref[0, rf, sl], kf_ref[0, rf, sl], vf_ref[0, rf, sl], g_f[fi:fi + 1], cum_f[ff:ff + 1],
                logf_f[ff:ff + 1], c_ref.at[h], m_ref.at[h], mask_f, scale).astype(hf_ref.dtype)
            hb_ref[0, rb, sl] = _mlstm_direction(
                qb_ref[0, rb, sl], kb_ref[0, rb, sl], vb_ref[0, rb, sl], g_b[bi:bi + 1], cum_b[bf:bf + 1],
                logf_b[bf:bf + 1], c_ref.at[heads + h], m_ref.at[heads + h], mask_b, scale).astype(hb_ref.dtype)


def _mlstm(qk, proj, v_col0, gates, gate_b, d_ml):
    B, L, _ = qk.shape
    H = MLSTM_HEADS
    dh = d_ml // H
    c = min(ML_CHUNK, L)
    rows = c
    nb = L // rows
    bias = jnp.broadcast_to(gate_b.astype(F32)[:, None], (4 * H, c))
    vb0 = v_col0 // d_ml
    specs = []
    for pos in (lambda ci: ci, lambda ci: nb - 1 - ci):
        specs += [pl.BlockSpec((1, rows, d_ml), lambda bi, ci, pos=pos: (bi, pos(ci), 0)),
                  pl.BlockSpec((1, rows, d_ml), lambda bi, ci, pos=pos: (bi, pos(ci), 1)),
                  pl.BlockSpec((1, rows, d_ml), lambda bi, ci, pos=pos: (bi, pos(ci), vb0)),
                  pl.BlockSpec((4 * H, rows), lambda bi, ci, pos=pos: (0, bi * nb + pos(ci)))]
    specs.append(pl.BlockSpec((4 * H, c), lambda bi, ci: (0, 0)))
    out_shape = jax.ShapeDtypeStruct((B, L, d_ml), BF16)
    return pl.pallas_call(
        functools.partial(_mlstm_kernel, heads=H, scale=dh ** -0.5, chunk=c),
        grid=(B, nb),
        in_specs=specs,
        out_specs=[pl.BlockSpec((1, rows, d_ml), lambda bi, ci: (bi, ci, 0)),
                   pl.BlockSpec((1, rows, d_ml), lambda bi, ci: (bi, nb - 1 - ci, 0))],
        out_shape=[out_shape, out_shape],
        scratch_shapes=[pltpu.VMEM((2 * H, dh, 2 * dh), F32), pltpu.VMEM((2 * H, 1, LANES), F32)],
        compiler_params=_params("parallel", "arbitrary"),
        name="mlstm_scan",
    )(qk, qk, proj, gates, qk, qk, proj, gates, bias)


def _group_rms(y, gain, bd, group):
    sq = (y * y).astype(BF16)
    w = bd.shape[0]
    ss = jnp.concatenate([jnp.dot(sq[:, i:i + w], bd, preferred_element_type=F32)
                          for i in range(0, y.shape[1], w)], axis=1)
    return y * lax.rsqrt(ss * (1.0 / group) + EPS) * gain


GN_STRIPE = 256


def _block_diag_ones(n, group):
    idx = np.arange(n) // group
    return jnp.asarray((idx[:, None] == idx[None, :]).astype(np.float32)).astype(BF16)


def _mix_xattn_kernel(z_ref, hf_ref, hb_ref, o_ref, x_ref, mem_ref, ghy_ref, gml_ref, why_ref, wml_ref, bdh_ref,
                      bdm_ref, gx_ref, gm_ref, wq_ref, wkv_ref, wo_ref, out_ref, kv_ref,
                      *, hy_group, ml_group, heads, scale):
    @pl.when(pl.program_id(1) == 0)
    def _():
        kv_ref[...] = _bdot(_rms(mem_ref[0], gm_ref[...]), wkv_ref[...]).astype(kv_ref.dtype)

    y_hy = _group_rms(z_ref[0].astype(F32), ghy_ref[...], bdh_ref[...], hy_group)
    h_sum = hf_ref[0].astype(F32) + hb_ref[0].astype(F32)
    y_ml = _group_rms(jax.nn.sigmoid(o_ref[0].astype(F32)) * h_sum, gml_ref[...], bdm_ref[...], ml_group)
    h1 = x_ref[0] + _bdot(y_hy, why_ref[...]) + _bdot(y_ml, wml_ref[...])

    D = h1.shape[1]
    dh = D // heads
    q = _bdot(_rms(h1, gx_ref[...]), wq_ref[...])
    outs = []
    for hd in range(heads):
        qh = q[:, hd * dh:(hd + 1) * dh]
        kh = kv_ref[:, hd * dh:(hd + 1) * dh]
        vh = kv_ref[:, D + hd * dh:D + (hd + 1) * dh]
        s = lax.dot_general(qh.astype(BF16), kh, (((1,), (1,)), ((), ())), preferred_element_type=F32) * scale
        e = jnp.exp(s - jnp.max(s, axis=-1, keepdims=True))
        p = e / jnp.sum(e, axis=-1, keepdims=True)
        outs.append(_bdot(p, vh))
    out_ref[0] = h1 + _bdot(jnp.concatenate(outs, axis=1), wo_ref[...])


def _mix_xattn(z, hf, hb, proj, o_col0, x, mem, g_hy, g_ml, w_out, g_x, g_mem, wq, wk, wv, wo):
    B, L, d_hy = z.shape
    tm = min(MIX_XATTN_ROWS, L)
    d_ml = hf.shape[2]
    D = x.shape[2]
    nm = mem.shape[1]
    ob = o_col0 // d_ml
    row = lambda n: pl.BlockSpec((1, tm, n), lambda bi, i: (bi, i, 0))
    const = lambda a: pl.BlockSpec(a.shape, lambda bi, i: (0,) * a.ndim, pipeline_mode=pl.Buffered(1))
    hy_group, ml_group = d_hy // HYENA_GROUPS, d_ml // MLSTM_HEADS
    consts = [g_hy.reshape(1, -1), g_ml.reshape(1, -1), w_out[:d_hy].astype(BF16), w_out[d_hy:].astype(BF16),
              _block_diag_ones(GN_STRIPE, hy_group), _block_diag_ones(GN_STRIPE, ml_group), g_x.reshape(1, D),
              g_mem.reshape(1, D), wq.astype(BF16), jnp.concatenate([wk, wv], axis=1).astype(BF16), wo.astype(BF16)]
    return pl.pallas_call(
        functools.partial(_mix_xattn_kernel, hy_group=hy_group, ml_group=ml_group, heads=XATTN_HEADS,
                          scale=(D // XATTN_HEADS) ** -0.5),
        grid=(B, L // tm),
        in_specs=[row(d_hy), row(d_ml), row(d_ml), pl.BlockSpec((1, tm, d_ml), lambda bi, i: (bi, i, ob)), row(D),
                  pl.BlockSpec((1, nm, D), lambda bi, i: (bi, 0, 0))] + [const(a) for a in consts],
        out_specs=row(D),
        out_shape=jax.ShapeDtypeStruct((B, L, D), F32),
        scratch_shapes=[pltpu.VMEM((nm, 2 * D), BF16)],
        compiler_params=_params("parallel", "arbitrary"),
        name="mix_xattn",
    )(z, hf, hb, proj, x, mem, *consts)


MLP_CHUNK = 1024


def _mlp_kernel(h_ref, g_ref, w1_ref, w2_ref, gf_ref, out_ref, xn_ref, *, final_norm):
    j = pl.program_id(1)

    @pl.when(j == 0)
    def _():
        x = h_ref[...]
        xn_ref[...] = _rms(x, g_ref[...]).astype(BF16)
        out_ref[...] = x

    for c in range(w1_ref.shape[1] // MLP_CHUNK):
        cols = slice(c * MLP_CHUNK, (c + 1) * MLP_CHUNK)
        a = jnp.maximum(jnp.dot(xn_ref[...], w1_ref[:, cols], preferred_element_type=F32), 0.0)
        out_ref[...] += _bdot(a * a, w2_ref[cols, :])

    if final_norm:
        @pl.when(j == pl.num_programs(1) - 1)
        def _():
            out_ref[...] = _rms(out_ref[...], gf_ref[...])


def _mlp(h, g, w1, w2, gf, final_norm):
    M, D = h.shape
    dff = w1.shape[1]
    tm, tf = min(MLP_ROWS, M), min(MLP_HIDDEN, dff)
    return pl.pallas_call(
        functools.partial(_mlp_kernel, final_norm=final_norm),
        grid=(M // tm, dff // tf),
        in_specs=[pl.BlockSpec((tm, D), lambda i, j: (i, 0)), pl.BlockSpec((1, D), lambda i, j: (0, 0)),
                  pl.BlockSpec((D, tf), lambda i, j: (0, j)), pl.BlockSpec((tf, D), lambda i, j: (j, 0)),
                  pl.BlockSpec((1, D), lambda i, j: (0, 0))],
        out_specs=pl.BlockSpec((tm, D), lambda i, j: (i, 0)),
        out_shape=jax.ShapeDtypeStruct((M, D), F32),
        scratch_shapes=[pltpu.VMEM((tm, D), BF16)],
        compiler_params=_params("parallel", "arbitrary"),
        name="mlp",
    )(h, g.reshape(1, D), w1.astype(BF16), w2.astype(BF16), gf.reshape(1, D))


def _pad_cols(w, n):
    return jnp.pad(w, ((0, 0), (0, n - w.shape[1])))


def _layer(h, mem, p, l):
    B, L, D = h.shape
    M = B * L
    d_hy = p["hy_norm_g"].shape[1]
    d_ml = p["ml_norm_g"].shape[1]
    hy_cols = 3 * d_hy
    qk_cols = 2 * d_ml
    n_gate = 4 * MLSTM_HEADS
    w_in = p["w_in"][l]
    vo0 = hy_cols + qk_cols
    g0 = vo0 + 2 * d_ml
    v_col0 = hy_cols
    o_col0 = hy_cols + d_ml
    n_in = hy_cols + 2 * d_ml
    proj, qk, gates = _in_proj(h.reshape(M, D), p["norm_mix_g"][l], w_in[:, :vo0].astype(BF16),
                               w_in[:, vo0:g0].astype(BF16),
                               _pad_cols(w_in[:, g0:g0 + n_gate], LANES).astype(BF16), n_gate,
                               jnp.concatenate([p["hy_conv_w"][l], p["ml_conv_w"][l]], axis=1),
                               jnp.concatenate([p["hy_conv_b"][l], p["ml_conv_b"][l]]),
                               n_hy=hy_cols, seq_len=L)
    proj = proj.reshape(B, L, n_in)
    qk = qk.reshape(B, L, qk_cols)

    tabs = _fft_tables(L)
    filt = _hyena_filters(L, p["hy_filt_w1"][l], p["hy_filt_b1"][l], p["hy_filt_freq1"][l], p["hy_filt_w2"][l],
                          p["hy_filt_b2"][l], p["hy_filt_freq2"][l], p["hy_filt_w3"][l], d_hy)
    spec, spec_side = _filter_spectrum(filt, L, d_hy, tabs)
    z_hy = _hyena(proj, spec, spec_side, p["hy_skip"][l], tabs)

    h_f, h_b = _mlstm(qk, proj, v_col0, gates, p["ml_gate_b"][l], d_ml)

    return _mix_xattn(z_hy, h_f, h_b, proj, o_col0, h, mem, p["hy_norm_g"][l], p["ml_norm_g"][l], p["w_out"][l],
                      p["norm_x_g"][l], p["norm_mem_g"][l], p["xa_wq"][l], p["xa_wk"][l], p["xa_wv"][l],
                      p["xa_wo"][l])


def kernel(x, mem, norm_mix_g, w_in, hy_conv_w, hy_conv_b, hy_filt_w1, hy_filt_b1, hy_filt_freq1, hy_filt_w2,
           hy_filt_b2, hy_filt_freq2, hy_filt_w3, hy_skip, hy_norm_g, ml_conv_w, ml_conv_b, ml_gate_b, ml_norm_g,
           w_out, norm_x_g, norm_mem_g, xa_wq, xa_wk, xa_wv, xa_wo, norm_ff_g, ff_w1, ff_w2, final_norm_g):
    p = dict(norm_mix_g=norm_mix_g, w_in=w_in, hy_conv_w=hy_conv_w, hy_conv_b=hy_conv_b, hy_filt_w1=hy_filt_w1,
             hy_filt_b1=hy_filt_b1, hy_filt_freq1=hy_filt_freq1, hy_filt_w2=hy_filt_w2, hy_filt_b2=hy_filt_b2,
             hy_filt_freq2=hy_filt_freq2, hy_filt_w3=hy_filt_w3, hy_skip=hy_skip, hy_norm_g=hy_norm_g,
             ml_conv_w=ml_conv_w, ml_conv_b=ml_conv_b, ml_gate_b=ml_gate_b, ml_norm_g=ml_norm_g, w_out=w_out,
             norm_x_g=norm_x_g, norm_mem_g=norm_mem_g, xa_wq=xa_wq, xa_wk=xa_wk, xa_wv=xa_wv, xa_wo=xa_wo)
    B, L, D = x.shape
    depth = w_in.shape[0]
    h = x
    for l in range(depth):
        h = _layer(h, mem, p, l)
        h = _mlp(h.reshape(B * L, D), norm_ff_g[l], ff_w1[l], ff_w2[l], final_norm_g,
                 final_norm=l == depth - 1).reshape(B, L, D)
    return h
```

```python
import functools
import math

import numpy as np
import jax
import jax.numpy as jnp
from jax import lax
from jax.experimental import pallas as pl
from jax.experimental.pallas import tpu as pltpu

F32 = jnp.float32
BF16 = jnp.bfloat16

EPS = 1e-6
HYENA_GROUPS = 8
MLSTM_HEADS = 4
XATTN_HEADS = 4
FILTER_BANDS = 16
DECAY_TARGET = 1e-2
SHORT_DECAY_PCT = 0.3
LONG_DECAY_PCT = 1.5

LANES = 128
VMEM_LIMIT = 56 * 1024 * 1024

IN_PROJ_ROWS = 1024
FILTER_ROWS = 512
MIX_XATTN_ROWS = 1024
MLP_ROWS = 1024


def _params(*sem):
    return pltpu.CompilerParams(dimension_semantics=sem, vmem_limit_bytes=VMEM_LIMIT)


def _bdot(a, b):
    return jnp.dot(a.astype(BF16), b.astype(BF16), preferred_element_type=F32)


def _split3(x):
    hi = x.astype(BF16)
    r = x - hi.astype(F32)
    mid = r.astype(BF16)
    lo = (r - mid.astype(F32)).astype(BF16)
    return hi, mid, lo


IN_CHUNK = 512
HALO = 8


def _rms(x, g):
    return x * lax.rsqrt(jnp.mean(x * x, axis=-1, keepdims=True) + EPS) * g


def _in_proj_kernel(x_ref, xb_ref, xa_ref, g_ref, wc_ref, wp_ref, wg_ref, cw_ref, cb_ref,
                    proj_ref, qk_ref, og_ref, xn_ref, *, tiles_per_seq, n_hy):
    tm = x_ref.shape[0]
    g = g_ref[...]
    x = x_ref[...]
    xn_ref[...] = (x * g).astype(BF16)
    r = lax.rsqrt(jnp.mean(x * x, axis=-1, keepdims=True) + EPS)
    og_ref[...] = (jnp.dot(xn_ref[...], wg_ref[...], preferred_element_type=F32) * r).T[0:og_ref.shape[0], :]
    for c in range(wp_ref.shape[1] // IN_CHUNK):
        cols = slice(c * IN_CHUNK, (c + 1) * IN_CHUNK)
        out_cols = slice(n_hy + c * IN_CHUNK, n_hy + (c + 1) * IN_CHUNK)
        proj_ref[:, out_cols] = (jnp.dot(xn_ref[...], wp_ref[:, cols], preferred_element_type=F32) * r
                                 ).astype(proj_ref.dtype)

    halo = _rms(jnp.concatenate([xb_ref[...], xa_ref[...]], axis=0), g).astype(BF16)
    pos = pl.program_id(0) % tiles_per_seq
    row = lax.broadcasted_iota(jnp.int32, (tm, IN_CHUNK), 0)
    for c in range(wc_ref.shape[1] // IN_CHUNK):
        cols = slice(c * IN_CHUNK, (c + 1) * IN_CHUNK)
        u = jnp.dot(xn_ref[...], wc_ref[:, cols], preferred_element_type=F32) * r
        uh = jnp.dot(halo, wc_ref[:, cols], preferred_element_type=F32)
        before = jnp.where(pos == 0, 0.0, uh[HALO - 1:HALO, :])
        after = jnp.where(pos == tiles_per_seq - 1, 0.0, uh[HALO:HALO + 1, :])
        prev = jnp.where(row == 0, before, pltpu.roll(u, 1, 0))
        nxt = jnp.where(row == tm - 1, after, pltpu.roll(u, tm - 1, 0))
        y = prev * cw_ref[0:1, cols] + u * cw_ref[1:2, cols] + nxt * cw_ref[2:3, cols] + cb_ref[:, cols]
        if c * IN_CHUNK < n_hy:
            proj_ref[:, cols] = y.astype(proj_ref.dtype)
        else:
            qk_cols = slice(c * IN_CHUNK - n_hy, (c + 1) * IN_CHUNK - n_hy)
            qk_ref[:, qk_cols] = (y * jax.nn.sigmoid(y)).astype(qk_ref.dtype)


def _in_proj(x, g, w_conv, w_plain, w_gate, n_gate, conv_w, conv_b, n_hy, seq_len):
    M, K = x.shape
    nc, npl = w_conv.shape[1], w_plain.shape[1]
    tm = min(IN_PROJ_ROWS, seq_len)
    hb = tm // HALO
    const = lambda a: pl.BlockSpec(a.shape, lambda i: (0,) * a.ndim, pipeline_mode=pl.Buffered(1))
    row = lambda n: pl.BlockSpec((tm, n), lambda i: (i, 0))
    g = g.reshape(1, K)
    conv_b = conv_b.reshape(1, nc)
    return pl.pallas_call(
        functools.partial(_in_proj_kernel, tiles_per_seq=seq_len // tm, n_hy=n_hy),
        grid=(M // tm,),
        in_specs=[row(K),
                  pl.BlockSpec((HALO, K), lambda i: (jnp.maximum(i * hb - 1, 0), 0)),
                  pl.BlockSpec((HALO, K), lambda i: (jnp.minimum((i + 1) * hb, M // HALO - 1), 0)),
                  const(g), const(w_conv), const(w_plain), const(w_gate), const(conv_w), const(conv_b)],
        out_specs=[row(n_hy + npl), row(nc - n_hy), pl.BlockSpec((n_gate, tm), lambda i: (0, i))],
        out_shape=[jax.ShapeDtypeStruct((M, n_hy + npl), BF16), jax.ShapeDtypeStruct((M, nc - n_hy), BF16),
                   jax.ShapeDtypeStruct((n_gate, M), F32)],
        scratch_shapes=[pltpu.VMEM((tm, K), BF16)],
        compiler_params=_params("parallel"),
        name="in_proj",
    )(x, x, x, g, w_conv, w_plain, w_gate, conv_w, conv_b)


def _filter_kernel(feat_ref, w1_ref, b1_ref, f1_ref, w2_ref, b2_ref, f2_ref, w3h_ref, w3l_ref, dec_ref, dir_ref,
                   o_ref, *, L):
    tl = o_ref.shape[0]
    hp = lax.Precision.HIGHEST
    pos = (lax.broadcasted_iota(jnp.int32, (tl, LANES), 0) + pl.program_id(0) * tl).astype(F32)
    lane = lax.broadcasted_iota(jnp.int32, (tl, LANES), 1)
    t = pos * (1.0 / (L - 1))
    arg = feat_ref[...] * (pos * (2.0 * math.pi / L))
    z = jnp.where(lane < FILTER_BANDS, jnp.cos(arg),
                  jnp.where(lane < 2 * FILTER_BANDS, -jnp.sin(arg),
                            jnp.where(lane == 2 * FILTER_BANDS, t, 0.0)))
    hid = jnp.sin(f1_ref[...] * (jnp.dot(z, w1_ref[...], precision=hp, preferred_element_type=F32) + b1_ref[...]))
    hid = jnp.sin(f2_ref[...] * (jnp.dot(hid, w2_ref[...], precision=hp, preferred_element_type=F32) + b2_ref[...]))
    hid_hi = hid.astype(BF16)
    hid_lo = (hid - hid_hi.astype(F32)).astype(BF16)
    filt = (jnp.dot(hid_hi, w3h_ref[...], preferred_element_type=F32)
            + jnp.dot(hid_hi, w3l_ref[...], preferred_element_type=F32)
            + jnp.dot(hid_lo, w3h_ref[...], preferred_element_type=F32))
    filt = filt * jnp.exp(-t[:, 0:1] * dec_ref[...])
    filt = jnp.where(pos[:, 0:1] == 0.0, filt * dir_ref[...], filt)
    o_ref[...] = filt


def _hyena_filters(L, w1, b1, fr1, w2, b2, fr2, w3, d_hyena):
    tl = min(FILTER_ROWS, L)
    n_emb, n_hid = w1.shape
    n_out = w3.shape[1]
    bands = jnp.linspace(1e-4, FILTER_BANDS - 1, FILTER_BANDS, dtype=F32)
    feat = jnp.zeros((1, LANES), F32).at[0, :FILTER_BANDS].set(bands).at[0, FILTER_BANDS:2 * FILTER_BANDS].set(bands)
    w1p = jnp.zeros((LANES, n_hid), F32).at[:n_emb - 1].set(w1[1:]).at[n_emb - 1].set(w1[0])
    max_decay = math.log(DECAY_TARGET) / SHORT_DECAY_PCT
    min_decay = math.log(DECAY_TARGET) / LONG_DECAY_PCT
    deltas = jnp.abs(jnp.linspace(min_decay, max_decay, d_hyena, dtype=F32))
    reps = n_out // d_hyena
    dec = jnp.tile(deltas, reps).reshape(1, n_out)
    dirmask = jnp.tile(jnp.concatenate([jnp.ones((d_hyena,), F32), jnp.zeros((d_hyena,), F32)]), reps // 2)
    w3_hi = w3.astype(BF16)
    w3_lo = (w3 - w3_hi.astype(F32)).astype(BF16)
    full = lambda shape: pl.BlockSpec(shape, lambda i: (0,) * len(shape))
    return pl.pallas_call(
        functools.partial(_filter_kernel, L=L),
        grid=(L // tl,),
        in_specs=[full((1, LANES)), full((LANES, n_hid)), full((1, n_hid)), full((1, n_hid)),
                  full((n_hid, n_hid)), full((1, n_hid)), full((1, n_hid)), full((n_hid, n_out)),
                  full((n_hid, n_out)), full((1, n_out)), full((1, n_out))],
        out_specs=pl.BlockSpec((tl, n_out), lambda i: (i, 0)),
        out_shape=jax.ShapeDtypeStruct((L, n_out), F32),
        compiler_params=_params("parallel"),
        name="hyena_filters",
    )(feat, w1p, b1.reshape(1, -1), fr1.reshape(1, -1), w2, b2.reshape(1, -1), fr2.reshape(1, -1), w3_hi, w3_lo,
      dec, dirmask.reshape(1, n_out))


MID_ROWS = 4096
SIDE_PAD = 16


def _fft_tables(L):
    N = 2 * L
    N2 = LANES
    N1 = N // N2
    H1 = N1 // 2
    k1 = np.arange(H1)[:, None]
    n1 = np.arange(H1)[None, :]
    a1 = 2.0 * np.pi * ((k1 * n1) % N1) / N1
    c1, s1 = np.cos(a1), np.sin(a1)
    sign = np.where(np.arange(H1) % 2 == 0, 1.0, -1.0)[None, :]
    pad = np.zeros((SIDE_PAD - 1, H1))
    m1 = np.concatenate([c1, -s1, sign, pad], axis=0)
    wgt = np.where(np.arange(H1) == 0, 1.0, 2.0)[:, None]
    minv = np.concatenate([(wgt * c1).T, (-wgt * s1).T, sign.T, pad.T], axis=1)
    n2 = np.arange(N2)[:, None]
    k2 = np.arange(N2)[None, :]
    a2 = 2.0 * np.pi * ((n2 * k2) % N2) / N2
    cg, sg = np.cos(a2), np.sin(a2)
    g2 = np.block([[cg, -sg], [sg, cg]])
    g2i = np.block([[cg, sg], [-sg, cg]])
    at = 2.0 * np.pi * (np.arange(H1 + 8)[:, None] * np.arange(N2)[None, :]) / N
    tw = np.concatenate([np.cos(at), -np.sin(at)], axis=1)
    as_bf = lambda a: jnp.asarray(a.astype(np.float32)).astype(BF16)
    return dict(m1=as_bf(m1), minv=as_bf(minv), g2=as_bf(g2), g2i=as_bf(g2i),
                tw=jnp.asarray(tw.astype(np.float32)), N1=N1, H1=H1)


S_PITCH = LANES + 8
GROUP = 8


def _slab_scratch_rows(i):
    return pl.ds(i * S_PITCH, LANES)


def _slab_rows(i):
    return pl.ds(i * LANES, LANES)


def _slab_transpose_in(get_slab, dst_refs, h1):
    for i in range(h1):
        t = get_slab(i).T
        for dst in dst_refs:
            dst[_slab_scratch_rows(i), :] = t


def _slab_transpose_out(s_ref, put_slab, h1):
    for i in range(h1):
        put_slab(i, s_ref[_slab_scratch_rows(i), :].T)


def _slab_map(fn, h1):
    for i in range(h1):
        fn(_slab_scratch_rows(i))


def _fft_stage_a(s_ref, nch, m1, tw_ref, ab_ref, side_ref, n1):
    h1 = n1 // 2
    twr = tw_ref[0:h1, 0:LANES]
    twi = tw_ref[0:h1, LANES:2 * LANES]
    tsr = tw_ref[h1:h1 + 1, 0:LANES]
    tsi = tw_ref[h1:h1 + 1, LANES:2 * LANES]
    for c0 in range(0, nch, GROUP):
        rhs = jnp.concatenate([s_ref[pl.ds(c0 + i, h1, stride=S_PITCH), :] for i in range(GROUP)], axis=1)
        res = jnp.dot(m1, rhs.astype(BF16), preferred_element_type=F32)
        for i in range(GROUP):
            ar = res[0:h1, LANES * i:LANES * (i + 1)]
            ai = res[h1:2 * h1, LANES * i:LANES * (i + 1)]
            rows = pl.ds((c0 + i) * h1, h1)
            ab_ref[rows, 0:LANES] = ar * twr - ai * twi
            ab_ref[rows, LANES:2 * LANES] = ar * twi + ai * twr
        a_side = jnp.concatenate([res[2 * h1:2 * h1 + 1, LANES * i:LANES * (i + 1)] for i in range(GROUP)],
                                 axis=0)
        side_ref[c0:c0 + GROUP, 0:LANES] = a_side * tsr
        side_ref[c0:c0 + GROUP, LANES:2 * LANES] = a_side * tsi


def _cmul(a, b):
    ar, ai = a[:, 0:LANES], a[:, LANES:2 * LANES]
    br, bi = b[:, 0:LANES], b[:, LANES:2 * LANES]
    return jnp.concatenate([ar * br - ai * bi, ar * bi + ai * br], axis=1)


def _fft_mid(ab_ref, side_ref, get_h, get_h_side, g2, g2i, nrows):
    chunk = min(MID_ROWS, nrows)
    for r0 in range(0, nrows, chunk):
        rows = pl.ds(r0, chunk)
        spec = _bdot(ab_ref[rows, :], g2)
        ab_ref[rows, :] = _bdot(_cmul(spec, get_h(rows)), g2i)
    side_ref[...] = _bdot(_cmul(_bdot(side_ref[...], g2), get_h_side()), g2i)


def _fft_stage_a_inv(ab_ref, side_ref, nch, minv, tw_ref, s_ref, n1):
    h1 = n1 // 2
    twr = tw_ref[0:h1, 0:LANES]
    twi = tw_ref[0:h1, LANES:2 * LANES]
    tsr = tw_ref[h1:h1 + 1, 0:LANES]
    tsi = tw_ref[h1:h1 + 1, LANES:2 * LANES]
    first_row = lax.broadcasted_iota(jnp.int32, (SIDE_PAD, GROUP * LANES), 0) == 0
    for c0 in range(0, nch, GROUP):
        blk = ab_ref[c0 * h1:(c0 + GROUP) * h1, :]
        re, im = [], []
        for i in range(GROUP):
            br = blk[i * h1:(i + 1) * h1, 0:LANES]
            bi = blk[i * h1:(i + 1) * h1, LANES:2 * LANES]
            re.append(br * twr + bi * twi)
            im.append(bi * twr - br * twi)
        side = side_ref[c0:c0 + GROUP, :]
        side_re = side[:, 0:LANES] * tsr + side[:, LANES:2 * LANES] * tsi
        side_row = jnp.concatenate([side_re[i:i + 1, :] for i in range(GROUP)], axis=1)
        side_blk = jnp.where(first_row, jnp.broadcast_to(side_row, first_row.shape), 0.0)
        rhs = jnp.concatenate([jnp.concatenate(re, axis=1), jnp.concatenate(im, axis=1), side_blk], axis=0)
        y = jnp.dot(minv, rhs.astype(BF16), preferred_element_type=F32)
        for i in range(GROUP):
            s_ref[pl.ds(c0 + i, h1, stride=S_PITCH), :] = y[:, LANES * i:LANES * (i + 1)]


def _filter_spectrum_kernel(hf_ref, hb_ref, m1_ref, g2_ref, tw_ref, o_ref, oside_ref,
                            s_ref, s2_ref, af_ref, ab_ref, afs_ref, abs_ref, *, n1, inv_n):
    nch = hf_ref.shape[1]
    h1 = n1 // 2
    m1, g2 = m1_ref[...], g2_ref[...]
    _slab_transpose_in(lambda i: hf_ref[_slab_rows(i), :], [s_ref], h1)
    _slab_transpose_in(lambda i: hb_ref[_slab_rows(i), :], [s2_ref], h1)
    _fft_stage_a(s_ref, nch, m1, tw_ref, af_ref, afs_ref, n1)
    _fft_stage_a(s2_ref, nch, m1, tw_ref, ab_ref, abs_ref, n1)

    def two_sided(fwd, bwd):
        sf = _bdot(fwd, g2)
        sb = _bdot(bwd, g2)
        return jnp.concatenate([sf[:, 0:LANES] + sb[:, 0:LANES], sf[:, LANES:] - sb[:, LANES:]], axis=1) * inv_n

    chunk = min(MID_ROWS, nch * h1)
    for r0 in range(0, nch * h1, chunk):
        rows = pl.ds(r0, chunk)
        o_ref[rows, :] = two_sided(af_ref[rows, :], ab_ref[rows, :])
    oside_ref[...] = two_sided(afs_ref[...], abs_ref[...])


def _filter_spectrum(filt, L, d_hyena, tabs):
    h1 = tabs["H1"]
    order = filt.shape[1] // (2 * d_hyena)
    cb = LANES
    nblk = d_hyena // cb
    full = lambda a: pl.BlockSpec(a.shape, lambda o, c: (0,) * a.ndim)
    main = pltpu.VMEM((cb * h1, 2 * LANES), F32)
    side = pltpu.VMEM((cb, 2 * LANES), F32)
    return pl.pallas_call(
        functools.partial(_filter_spectrum_kernel, n1=tabs["N1"], inv_n=1.0 / (2 * L)),
        grid=(order, nblk),
        in_specs=[pl.BlockSpec((L, cb), lambda o, c: (0, o * 2 * nblk + c)),
                  pl.BlockSpec((L, cb), lambda o, c: (0, o * 2 * nblk + nblk + c)),
                  full(tabs["m1"]), full(tabs["g2"]), full(tabs["tw"])],
        out_specs=[pl.BlockSpec((cb * h1, 2 * LANES), lambda o, c: (o * nblk + c, 0)),
                   pl.BlockSpec((cb, 2 * LANES), lambda o, c: (o * nblk + c, 0))],
        out_shape=[jax.ShapeDtypeStruct((order * d_hyena * h1, 2 * LANES), F32),
                   jax.ShapeDtypeStruct((order * d_hyena, 2 * LANES), F32)],
        scratch_shapes=[pltpu.VMEM((h1 * S_PITCH, LANES), F32), pltpu.VMEM((h1 * S_PITCH, LANES), F32),
                        main, main, side, side],
        compiler_params=_params("parallel", "parallel"),
        name="filter_spectrum",
    )(filt, filt, tabs["m1"], tabs["g2"], tabs["tw"])


def _hyena_kernel(v_ref, vnext_ref, x1_ref, x2_ref, h_ref, hside_ref, d_ref, m1_ref, minv_ref, g2_ref, g2i_ref,
                  tw_ref, o_ref, s_ref, sv_ref, sn_ref, sx_ref, sz_ref, ab_ref, side_ref, *, n1):
    nch = v_ref.shape[2]
    h1 = n1 // 2
    m1, minv, g2, g2i = m1_ref[...], minv_ref[...], g2_ref[...], g2i_ref[...]

    def slab_of(p_ref):
        return lambda i: p_ref[0, _slab_rows(i), :].astype(F32)

    def long_conv(order, src_ref, gate_ref, skip_ref, keep_ref=None, prefetch_ref=None):
        _fft_stage_a(src_ref, nch, m1, tw_ref, ab_ref, side_ref, n1)
        _slab_transpose_in(slab_of(gate_ref), [sx_ref], h1)
        _fft_mid(ab_ref, side_ref, lambda rows: h_ref[order, rows, :], lambda: hside_ref[order], g2, g2i, nch * h1)
        _fft_stage_a_inv(ab_ref, side_ref, nch, minv, tw_ref, s_ref, n1)
        if prefetch_ref is not None:
            _slab_transpose_in(slab_of(prefetch_ref), [sn_ref], h1)
        d_slab = jnp.broadcast_to(d_ref[:, order:order + 1], (nch, LANES))

        def gate(rows):
            val = sx_ref[rows, :] * (s_ref[rows, :] + d_slab * skip_ref[rows, :])
            s_ref[rows, :] = val
            if keep_ref is not None:
                keep_ref[rows, :] = val

        _slab_map(gate, h1)

    first = pl.program_id(1) == 0

    @pl.when(first)
    def _():
        _slab_transpose_in(slab_of(v_ref), [sv_ref], h1)

    @pl.when(jnp.logical_not(first))
    def _():
        def copy(rows):
            sv_ref[rows, :] = sn_ref[rows, :]

        _slab_map(copy, h1)

    long_conv(0, sv_ref, x1_ref, sv_ref, keep_ref=sz_ref, prefetch_ref=vnext_ref)
    long_conv(1, s_ref, x2_ref, sz_ref)

    def put(i, out):
        o_ref[0, _slab_rows(i), :] = out.astype(o_ref.dtype)

    _slab_transpose_out(s_ref, put, h1)


def _hyena(proj, spec, spec_side, skip, tabs):
    B, L, _ = proj.shape
    D = skip.shape[1]
    n1, h1 = tabs["N1"], tabs["H1"]
    cb = LANES
    nblk = D // cb
    spec = spec.reshape(2, D * h1, 2 * LANES)
    spec_side = spec_side.reshape(2, D, 2 * LANES)
    col = lambda part: pl.BlockSpec((1, L, cb), lambda c, b: (b, 0, part * nblk + c))
    full = lambda a: pl.BlockSpec(a.shape, lambda c, b: (0,) * a.ndim)
    slabs = pltpu.VMEM((h1 * S_PITCH, LANES), F32)
    return pl.pallas_call(
        functools.partial(_hyena_kernel, n1=n1),
        grid=(nblk, B),
        in_specs=[col(0),
                  pl.BlockSpec((1, L, cb), lambda c, b: (jnp.minimum(b + 1, B - 1), 0, c)),
                  col(1), col(2),
                  pl.BlockSpec((2, cb * h1, 2 * LANES), lambda c, b: (0, c, 0), pipeline_mode=pl.Buffered(1)),
                  pl.BlockSpec((2, cb, 2 * LANES), lambda c, b: (0, c, 0)),
                  pl.BlockSpec((cb, 2), lambda c, b: (c, 0)),
                  full(tabs["m1"]), full(tabs["minv"]), full(tabs["g2"]), full(tabs["g2i"]), full(tabs["tw"])],
        out_specs=pl.BlockSpec((1, L, cb), lambda c, b: (b, 0, c)),
        out_shape=jax.ShapeDtypeStruct((B, L, D), BF16),
        scratch_shapes=[slabs, slabs, slabs, slabs, slabs, pltpu.VMEM((cb * h1, 2 * LANES), F32),
                        pltpu.VMEM((cb, 2 * LANES), F32)],
        compiler_params=_params("parallel", "arbitrary"),
        name="hyena_fftconv",
    )(proj, proj, proj, proj, spec, spec_side, skip.astype(F32).T,
      tabs["m1"], tabs["minv"], tabs["g2"], tabs["g2i"], tabs["tw"])


ML_CHUNK = 512


def _log_sigmoid(x):
    return jnp.minimum(x, 0.0) - jnp.log1p(jnp.exp(-jnp.abs(x)))


def _mlstm_direction(q, k, v, i_row, cum_row, logf_row, c_ref, m_ref, mask, scale):
    c, dh = q.shape
    g_row = i_row - cum_row
    f_col_rep = jnp.broadcast_to(cum_row, (LANES, c)).T
    m_prev = m_ref[0:1, 0:1]
    log_scale = math.log(scale)
    e = jnp.where(mask, g_row + log_scale, -jnp.inf)
    a = jnp.maximum(jnp.max(e, axis=1, keepdims=True) - log_scale, m_prev)
    sc = lax.dot_general(q.astype(BF16), k.astype(BF16), (((1,), (1,)), ((), ())), preferred_element_type=F32)
    p = sc * jnp.exp(e - a)
    vaug = jnp.concatenate([v, jnp.ones_like(v)], axis=1)
    inter = jnp.exp(m_prev - a)
    tot = _bdot(p, vaug) + inter * _bdot(q, c_ref[...])
    num, den = tot[:, 0:dh], tot[:, dh:2 * dh]
    m_t = f_col_rep + a
    h = num / jnp.maximum(jnp.abs(den), jnp.exp(-m_t))
    a_end = jnp.maximum(jnp.max(g_row, axis=1, keepdims=True), m_prev)
    f_end = jnp.sum(logf_row, axis=1, keepdims=True)
    we = jnp.exp(g_row - a_end) * scale
    kw = k.astype(F32).T * we
    c_ref[...] = jnp.exp(m_prev - a_end) * c_ref[...] + _bdot(kw, vaug)
    m_ref[...] = jnp.broadcast_to(f_end + a_end, m_ref.shape)
    return h


def _mlstm_kernel(qf_ref, kf_ref, vf_ref, gf_ref, qb_ref, kb_ref, vb_ref, gb_ref, bias_ref,
                  hf_ref, hb_ref, c_ref, m_ref, *, heads, scale, chunk):
    @pl.when(pl.program_id(1) == 0)
    def _():
        c_ref[...] = jnp.zeros_like(c_ref)
        m_ref[...] = jnp.zeros_like(m_ref)

    c = chunk
    nsub = qf_ref.shape[1] // c
    dh = qf_ref.shape[2] // heads
    r = lax.broadcasted_iota(jnp.int32, (c, c), 0)
    s = lax.broadcasted_iota(jnp.int32, (c, c), 1)
    mask_f = s <= r
    mask_b = s >= r

    def gate_rows(g, mask_t):
        g = g + bias_ref[...]
        logf = _log_sigmoid(g)
        tri = mask_t.astype(BF16)
        cum = sum(jnp.dot(part, tri, preferred_element_type=F32) for part in _split3(logf))
        return g, logf, cum

    for u in range(nsub):
        rf = slice(u * c, (u + 1) * c)
        rb = slice((nsub - 1 - u) * c, (nsub - u) * c)
        g_f, logf_f, cum_f = gate_rows(gf_ref[:, rf], mask_b)
        g_b, logf_b, cum_b = gate_rows(gb_ref[:, rb], mask_f)
        for h in range(heads):
            sl = slice(h * dh, (h + 1) * dh)
            fi, ff, bi, bf = h, heads + h, 2 * heads + h, 3 * heads + h
            hf_ref[0, rf, sl] = _mlstm_direction(
                qf_ref[0, rf, sl], kf_ref[0, rf, sl], vf_ref[0, rf, sl], g_f[fi:fi + 1], cum_f[ff:ff + 1],
                logf_f[ff:ff + 1], c_ref.at[h], m_ref.at[h], mask_f, scale).astype(hf_ref.dtype)
            hb_ref[0, rb, sl] = _mlstm_direction(
                qb_ref[0, rb, sl], kb_ref[0, rb, sl], vb_ref[0, rb, sl], g_b[bi:bi + 1], cum_b[bf:bf + 1],
                logf_b[bf:bf + 1], c_ref.at[heads + h], m_ref.at[heads + h], mask_b, scale).astype(hb_ref.dtype)


def _mlstm(qk, proj, v_col0, gates, gate_b, d_ml):
    B, L, _ = qk.shape
    H = MLSTM_HEADS
    dh = d_ml // H
    c = min(ML_CHUNK, L)
    rows = c
    nb = L // rows
    bias = jnp.broadcast_to(gate_b.astype(F32)[:, None], (4 * H, c))
    vb0 = v_col0 // d_ml
    specs = []
    for pos in (lambda ci: ci, lambda ci: nb - 1 - ci):
        specs += [pl.BlockSpec((1, rows, d_ml), lambda bi, ci, pos=pos: (bi, pos(ci), 0)),
                  pl.BlockSpec((1, rows, d_ml), lambda bi, ci, pos=pos: (bi, pos(ci), 1)),
                  pl.BlockSpec((1, rows, d_ml), lambda bi, ci, pos=pos: (bi, pos(ci), vb0)),
                  pl.BlockSpec((4 * H, rows), lambda bi, ci, pos=pos: (0, bi * nb + pos(ci)))]
    specs.append(pl.BlockSpec((4 * H, c), lambda bi, ci: (0, 0)))
    out_shape = jax.ShapeDtypeStruct((B, L, d_ml), BF16)
    return pl.pallas_call(
        functools.partial(_mlstm_kernel, heads=H, scale=dh ** -0.5, chunk=c),
        grid=(B, nb),
        in_specs=specs,
        out_specs=[pl.BlockSpec((1, rows, d_ml), lambda bi, ci: (bi, ci, 0)),
                   pl.BlockSpec((1, rows, d_ml), lambda bi, ci: (bi, nb - 1 - ci, 0))],
        out_shape=[out_shape, out_shape],
        scratch_shapes=[pltpu.VMEM((2 * H, dh, 2 * dh), F32), pltpu.VMEM((2 * H, 1, LANES), F32)],
        compiler_params=_params("parallel", "arbitrary"),
        name="mlstm_scan",
    )(qk, qk, proj, gates, qk, qk, proj, gates, bias)


def _group_rms(y, gain, bd, group):
    sq = (y * y).astype(BF16)
    w = bd.shape[0]
    ss = jnp.concatenate([jnp.dot(sq[:, i:i + w], bd, preferred_element_type=F32)
                          for i in range(0, y.shape[1], w)], axis=1)
    return y * lax.rsqrt(ss * (1.0 / group) + EPS) * gain


GN_STRIPE = 256


def _block_diag_ones(n, group):
    idx = np.arange(n) // group
    return jnp.asarray((idx[:, None] == idx[None, :]).astype(np.float32)).astype(BF16)


def _mix_xattn_kernel(z_ref, hf_ref, hb_ref, o_ref, x_ref, mem_ref, ghy_ref, gml_ref, why_ref, wml_ref, bdh_ref,
                      bdm_ref, gx_ref, gm_ref, wq_ref, wkv_ref, wo_ref, out_ref, kv_ref,
                      *, hy_group, ml_group, heads, scale):
    @pl.when(pl.program_id(1) == 0)
    def _():
        kv_ref[...] = _bdot(_rms(mem_ref[0], gm_ref[...]), wkv_ref[...]).astype(kv_ref.dtype)

    y_hy = _group_rms(z_ref[0].astype(F32), ghy_ref[...], bdh_ref[...], hy_group)
    h_sum = hf_ref[0].astype(F32) + hb_ref[0].astype(F32)
    y_ml = _group_rms(jax.nn.sigmoid(o_ref[0].astype(F32)) * h_sum, gml_ref[...], bdm_ref[...], ml_group)
    h1 = x_ref[0] + _bdot(y_hy, why_ref[...]) + _bdot(y_ml, wml_ref[...])

    D = h1.shape[1]
    dh = D // heads
    q = _bdot(_rms(h1, gx_ref[...]), wq_ref[...])
    outs = []
    for hd in range(heads):
        qh = q[:, hd * dh:(hd + 1) * dh]
        kh = kv_ref[:, hd * dh:(hd + 1) * dh]
        vh = kv_ref[:, D + hd * dh:D + (hd + 1) * dh]
        s = lax.dot_general(qh.astype(BF16), kh, (((1,), (1,)), ((), ())), preferred_element_type=F32) * scale
        e = jnp.exp(s - jnp.max(s, axis=-1, keepdims=True))
        p = e / jnp.sum(e, axis=-1, keepdims=True)
        outs.append(_bdot(p, vh))
    out_ref[0] = h1 + _bdot(jnp.concatenate(outs, axis=1), wo_ref[...])


def _mix_xattn(z, hf, hb, proj, o_col0, x, mem, g_hy, g_ml, w_out, g_x, g_mem, wq, wk, wv, wo):
    B, L, d_hy = z.shape
    tm = min(MIX_XATTN_ROWS, L)
    d_ml = hf.shape[2]
    D = x.shape[2]
    nm = mem.shape[1]
    ob = o_col0 // d_ml
    row = lambda n: pl.BlockSpec((1, tm, n), lambda bi, i: (bi, i, 0))
    const = lambda a: pl.BlockSpec(a.shape, lambda bi, i: (0,) * a.ndim, pipeline_mode=pl.Buffered(1))
    hy_group, ml_group = d_hy // HYENA_GROUPS, d_ml // MLSTM_HEADS
    consts = [g_hy.reshape(1, -1), g_ml.reshape(1, -1), w_out[:d_hy].astype(BF16), w_out[d_hy:].astype(BF16),
              _block_diag_ones(GN_STRIPE, hy_group), _block_diag_ones(GN_STRIPE, ml_group), g_x.reshape(1, D),
              g_mem.reshape(1, D), wq.astype(BF16), jnp.concatenate([wk, wv], axis=1).astype(BF16), wo.astype(BF16)]
    return pl.pallas_call(
        functools.partial(_mix_xattn_kernel, hy_group=hy_group, ml_group=ml_group, heads=XATTN_HEADS,
                          scale=(D // XATTN_HEADS) ** -0.5),
        grid=(B, L // tm),
        in_specs=[row(d_hy), row(d_ml), row(d_ml), pl.BlockSpec((1, tm, d_ml), lambda bi, i: (bi, i, ob)), row(D),
                  pl.BlockSpec((1, nm, D), lambda bi, i: (bi, 0, 0))] + [const(a) for a in consts],
        out_specs=row(D),
        out_shape=jax.ShapeDtypeStruct((B, L, D), F32),
        scratch_shapes=[pltpu.VMEM((nm, 2 * D), BF16)],
        compiler_params=_params("parallel", "arbitrary"),
        name="mix_xattn",
    )(z, hf, hb, proj, x, mem, *consts)


MLP_CHUNK = 1024


def _mlp_kernel(h_ref, g_ref, w1_ref, w2_ref, gf_ref, out_ref, xn_ref, *, final_norm):
    x = h_ref[...]
    xn_ref[...] = _rms(x, g_ref[...]).astype(BF16)
    out_ref[...] = x
    for c in range(w1_ref.shape[1] // MLP_CHUNK):
        cols = slice(c * MLP_CHUNK, (c + 1) * MLP_CHUNK)
        a = jnp.maximum(jnp.dot(xn_ref[...], w1_ref[:, cols], preferred_element_type=F32), 0.0)
        out_ref[...] += _bdot(a * a, w2_ref[cols, :])
    if final_norm:
        out_ref[...] = _rms(out_ref[...], gf_ref[...])


def _mlp(h, g, w1, w2, gf, final_norm):
    M, D = h.shape
    dff = w1.shape[1]
    tm = min(MLP_ROWS, M)
    const = lambda shape: pl.BlockSpec(shape, lambda i: (0, 0), pipeline_mode=pl.Buffered(1))
    return pl.pallas_call(
        functools.partial(_mlp_kernel, final_norm=final_norm),
        grid=(M // tm,),
        in_specs=[pl.BlockSpec((tm, D), lambda i: (i, 0)), const((1, D)), const((D, dff)), const((dff, D)),
                  const((1, D))],
        out_specs=pl.BlockSpec((tm, D), lambda i: (i, 0)),
        out_shape=jax.ShapeDtypeStruct((M, D), F32),
        scratch_shapes=[pltpu.VMEM((tm, D), BF16)],
        compiler_params=_params("parallel"),
        name="mlp",
    )(h, g.reshape(1, D), w1.astype(BF16), w2.astype(BF16), gf.reshape(1, D))


def _pad_cols(w, n):
    return jnp.pad(w, ((0, 0), (0, n - w.shape[1])))


def _layer(h, mem, p, l):
    B, L, D = h.shape
    M = B * L
    d_hy = p["hy_norm_g"].shape[1]
    d_ml = p["ml_norm_g"].shape[1]
    hy_cols = 3 * d_hy
    qk_cols = 2 * d_ml
    n_gate = 4 * MLSTM_HEADS
    w_in = p["w_in"][l]
    vo0 = hy_cols + qk_cols
    g0 = vo0 + 2 * d_ml
    v_col0 = hy_cols
    o_col0 = hy_cols + d_ml
    n_in = hy_cols + 2 * d_ml
    proj, qk, gates = _in_proj(h.reshape(M, D), p["norm_mix_g"][l], w_in[:, :vo0].astype(BF16),
                               w_in[:, vo0:g0].astype(BF16),
                               _pad_cols(w_in[:, g0:g0 + n_gate], LANES).astype(BF16), n_gate,
                               jnp.concatenate([p["hy_conv_w"][l], p["ml_conv_w"][l]], axis=1),
                               jnp.concatenate([p["hy_conv_b"][l], p["ml_conv_b"][l]]),
                               n_hy=hy_cols, seq_len=L)
    proj = proj.reshape(B, L, n_in)
    qk = qk.reshape(B, L, qk_cols)

    tabs = _fft_tables(L)
    filt = _hyena_filters(L, p["hy_filt_w1"][l], p["hy_filt_b1"][l], p["hy_filt_freq1"][l], p["hy_filt_w2"][l],
                          p["hy_filt_b2"][l], p["hy_filt_freq2"][l], p["hy_filt_w3"][l], d_hy)
    spec, spec_side = _filter_spectrum(filt, L, d_hy, tabs)
    z_hy = _hyena(proj, spec, spec_side, p["hy_skip"][l], tabs)

    h_f, h_b = _mlstm(qk, proj, v_col0, gates, p["ml_gate_b"][l], d_ml)

    return _mix_xattn(z_hy, h_f, h_b, proj, o_col0, h, mem, p["hy_norm_g"][l], p["ml_norm_g"][l], p["w_out"][l],
                      p["norm_x_g"][l], p["norm_mem_g"][l], p["xa_wq"][l], p["xa_wk"][l], p["xa_wv"][l],
                      p["xa_wo"][l])


def kernel(x, mem, norm_mix_g, w_in, hy_conv_w, hy_conv_b, hy_filt_w1, hy_filt_b1, hy_filt_freq1, hy_filt_w2,
           hy_filt_b2, hy_filt_freq2, hy_filt_w3, hy_skip, hy_norm_g, ml_conv_w, ml_conv_b, ml_gate_b, ml_norm_g,
           w_out, norm_x_g, norm_mem_g, xa_wq, xa_wk, xa_wv, xa_wo, norm_ff_g, ff_w1, ff_w2, final_norm_g):
    p = dict(norm_mix_g=norm_mix_g, w_in=w_in, hy_conv_w=hy_conv_w, hy_conv_b=hy_conv_b, hy_filt_w1=hy_filt_w1,
             hy_filt_b1=hy_filt_b1, hy_filt_freq1=hy_filt_freq1, hy_filt_w2=hy_filt_w2, hy_filt_b2=hy_filt_b2,
             hy_filt_freq2=hy_filt_freq2, hy_filt_w3=hy_filt_w3, hy_skip=hy_skip, hy_norm_g=hy_norm_g,
             ml_conv_w=ml_conv_w, ml_conv_b=ml_conv_b, ml_gate_b=ml_gate_b, ml_norm_g=ml_norm_g, w_out=w_out,
             norm_x_g=norm_x_g, norm_mem_g=norm_mem_g, xa_wq=xa_wq, xa_wk=xa_wk, xa_wv=xa_wv, xa_wo=xa_wo)
    B, L, D = x.shape
    depth = w_in.shape[0]
    h = x
    for l in range(depth):
        h = _layer(h, mem, p, l)
        h = _mlp(h.reshape(B * L, D), norm_ff_g[l], ff_w1[l], ff_w2[l], final_norm_g,
                 final_norm=l == depth - 1).reshape(B, L, D)
    return h
```

```python
import functools
import math

import numpy as np
import jax
import jax.numpy as jnp
from jax import lax
from jax.experimental import pallas as pl
from jax.experimental.pallas import tpu as pltpu

F32 = jnp.float32
BF16 = jnp.bfloat16

EPS = 1e-6
HYENA_GROUPS = 8
MLSTM_HEADS = 4
XATTN_HEADS = 4
FILTER_BANDS = 16
DECAY_TARGET = 1e-2
SHORT_DECAY_PCT = 0.3
LONG_DECAY_PCT = 1.5

LANES = 128
VMEM_LIMIT = 56 * 1024 * 1024

IN_PROJ_ROWS = 1024
FILTER_ROWS = 512
MIX_XATTN_ROWS = 1024
MLP_ROWS = 1024


def _params(*sem):
    return pltpu.CompilerParams(dimension_semantics=sem, vmem_limit_bytes=VMEM_LIMIT)


def _bdot(a, b):
    return jnp.dot(a.astype(BF16), b.astype(BF16), preferred_element_type=F32)


def _split3(x):
    hi = x.astype(BF16)
    r = x - hi.astype(F32)
    mid = r.astype(BF16)
    lo = (r - mid.astype(F32)).astype(BF16)
    return hi, mid, lo


IN_CHUNK = 512
HALO = 8


def _rms(x, g):
    return x * lax.rsqrt(jnp.mean(x * x, axis=-1, keepdims=True) + EPS) * g


def _in_proj_kernel(x_ref, xb_ref, xa_ref, g_ref, wc_ref, wp_ref, wg_ref, cw_ref, cb_ref,
                    proj_ref, qk_ref, og_ref, xn_ref, *, tiles_per_seq, n_hy):
    tm = x_ref.shape[0]
    g = g_ref[...]
    x = x_ref[...]
    xn_ref[...] = (x * g).astype(BF16)
    r = lax.rsqrt(jnp.mean(x * x, axis=-1, keepdims=True) + EPS)
    og_ref[...] = (jnp.dot(xn_ref[...], wg_ref[...], preferred_element_type=F32) * r).T[0:og_ref.shape[0], :]
    for c in range(wp_ref.shape[1] // IN_CHUNK):
        cols = slice(c * IN_CHUNK, (c + 1) * IN_CHUNK)
        out_cols = slice(n_hy + c * IN_CHUNK, n_hy + (c + 1) * IN_CHUNK)
        proj_ref[:, out_cols] = (jnp.dot(xn_ref[...], wp_ref[:, cols], preferred_element_type=F32) * r
                                 ).astype(proj_ref.dtype)

    halo = _rms(jnp.concatenate([xb_ref[...], xa_ref[...]], axis=0), g).astype(BF16)
    pos = pl.program_id(0) % tiles_per_seq
    row = lax.broadcasted_iota(jnp.int32, (tm, IN_CHUNK), 0)
    for c in range(wc_ref.shape[1] // IN_CHUNK):
        cols = slice(c * IN_CHUNK, (c + 1) * IN_CHUNK)
        u = jnp.dot(xn_ref[...], wc_ref[:, cols], preferred_element_type=F32) * r
        uh = jnp.dot(halo, wc_ref[:, cols], preferred_element_type=F32)
        before = jnp.where(pos == 0, 0.0, uh[HALO - 1:HALO, :])
        after = jnp.where(pos == tiles_per_seq - 1, 0.0, uh[HALO:HALO + 1, :])
        prev = jnp.where(row == 0, before, pltpu.roll(u, 1, 0))
        nxt = jnp.where(row == tm - 1, after, pltpu.roll(u, tm - 1, 0))
        y = prev * cw_ref[0:1, cols] + u * cw_ref[1:2, cols] + nxt * cw_ref[2:3, cols] + cb_ref[:, cols]
        if c * IN_CHUNK < n_hy:
            proj_ref[:, cols] = y.astype(proj_ref.dtype)
        else:
            qk_cols = slice(c * IN_CHUNK - n_hy, (c + 1) * IN_CHUNK - n_hy)
            qk_ref[:, qk_cols] = (y * jax.nn.sigmoid(y)).astype(qk_ref.dtype)


def _in_proj(x, g, w_conv, w_plain, w_gate, n_gate, conv_w, conv_b, n_hy, seq_len):
    M, K = x.shape
    nc, npl = w_conv.shape[1], w_plain.shape[1]
    tm = min(IN_PROJ_ROWS, seq_len)
    hb = tm // HALO
    const = lambda a: pl.BlockSpec(a.shape, lambda i: (0,) * a.ndim, pipeline_mode=pl.Buffered(1))
    row = lambda n: pl.BlockSpec((tm, n), lambda i: (i, 0))
    g = g.reshape(1, K)
    conv_b = conv_b.reshape(1, nc)
    return pl.pallas_call(
        functools.partial(_in_proj_kernel, tiles_per_seq=seq_len // tm, n_hy=n_hy),
        grid=(M // tm,),
        in_specs=[row(K),
                  pl.BlockSpec((HALO, K), lambda i: (jnp.maximum(i * hb - 1, 0), 0)),
                  pl.BlockSpec((HALO, K), lambda i: (jnp.minimum((i + 1) * hb, M // HALO - 1), 0)),
                  const(g), const(w_conv), const(w_plain), const(w_gate), const(conv_w), const(conv_b)],
        out_specs=[row(n_hy + npl), row(nc - n_hy), pl.BlockSpec((n_gate, tm), lambda i: (0, i))],
        out_shape=[jax.ShapeDtypeStruct((M, n_hy + npl), BF16), jax.ShapeDtypeStruct((M, nc - n_hy), BF16),
                   jax.ShapeDtypeStruct((n_gate, M), F32)],
        scratch_shapes=[pltpu.VMEM((tm, K), BF16)],
        compiler_params=_params("parallel"),
        name="in_proj",
    )(x, x, x, g, w_conv, w_plain, w_gate, conv_w, conv_b)


def _filter_kernel(feat_ref, w1_ref, b1_ref, f1_ref, w2_ref, b2_ref, f2_ref, w3h_ref, w3l_ref, dec_ref, dir_ref,
                   o_ref, *, L):
    tl = o_ref.shape[0]
    hp = lax.Precision.HIGHEST
    pos = (lax.broadcasted_iota(jnp.int32, (tl, LANES), 0) + pl.program_id(0) * tl).astype(F32)
    lane = lax.broadcasted_iota(jnp.int32, (tl, LANES), 1)
    t = pos * (1.0 / (L - 1))
    arg = feat_ref[...] * (pos * (2.0 * math.pi / L))
    z = jnp.where(lane < FILTER_BANDS, jnp.cos(arg),
                  jnp.where(lane < 2 * FILTER_BANDS, -jnp.sin(arg),
                            jnp.where(lane == 2 * FILTER_BANDS, t, 0.0)))
    hid = jnp.sin(f1_ref[...] * (jnp.dot(z, w1_ref[...], precision=hp, preferred_element_type=F32) + b1_ref[...]))
    hid = jnp.sin(f2_ref[...] * (jnp.dot(hid, w2_ref[...], precision=hp, preferred_element_type=F32) + b2_ref[...]))
    hid_hi = hid.astype(BF16)
    hid_lo = (hid - hid_hi.astype(F32)).astype(BF16)
    filt = (jnp.dot(hid_hi, w3h_ref[...], preferred_element_type=F32)
            + jnp.dot(hid_hi, w3l_ref[...], preferred_element_type=F32)
            + jnp.dot(hid_lo, w3h_ref[...], preferred_element_type=F32))
    filt = filt * jnp.exp(-t[:, 0:1] * dec_ref[...])
    filt = jnp.where(pos[:, 0:1] == 0.0, filt * dir_ref[...], filt)
    o_ref[...] = filt


def _hyena_filters(L, w1, b1, fr1, w2, b2, fr2, w3, d_hyena):
    tl = min(FILTER_ROWS, L)
    n_emb, n_hid = w1.shape
    n_out = w3.shape[1]
    bands = jnp.linspace(1e-4, FILTER_BANDS - 1, FILTER_BANDS, dtype=F32)
    feat = jnp.zeros((1, LANES), F32).at[0, :FILTER_BANDS].set(bands).at[0, FILTER_BANDS:2 * FILTER_BANDS].set(bands)
    w1p = jnp.zeros((LANES, n_hid), F32).at[:n_emb - 1].set(w1[1:]).at[n_emb - 1].set(w1[0])
    max_decay = math.log(DECAY_TARGET) / SHORT_DECAY_PCT
    min_decay = math.log(DECAY_TARGET) / LONG_DECAY_PCT
    deltas = jnp.abs(jnp.linspace(min_decay, max_decay, d_hyena, dtype=F32))
    reps = n_out // d_hyena
    dec = jnp.tile(deltas, reps).reshape(1, n_out)
    dirmask = jnp.tile(jnp.concatenate([jnp.ones((d_hyena,), F32), jnp.zeros((d_hyena,), F32)]), reps // 2)
    w3_hi = w3.astype(BF16)
    w3_lo = (w3 - w3_hi.astype(F32)).astype(BF16)
    full = lambda shape: pl.BlockSpec(shape, lambda i: (0,) * len(shape))
    return pl.pallas_call(
        functools.partial(_filter_kernel, L=L),
        grid=(L // tl,),
        in_specs=[full((1, LANES)), full((LANES, n_hid)), full((1, n_hid)), full((1, n_hid)),
                  full((n_hid, n_hid)), full((1, n_hid)), full((1, n_hid)), full((n_hid, n_out)),
                  full((n_hid, n_out)), full((1, n_out)), full((1, n_out))],
        out_specs=pl.BlockSpec((tl, n_out), lambda i: (i, 0)),
        out_shape=jax.ShapeDtypeStruct((L, n_out), F32),
        compiler_params=_params("parallel"),
        name="hyena_filters",
    )(feat, w1p, b1.reshape(1, -1), fr1.reshape(1, -1), w2, b2.reshape(1, -1), fr2.reshape(1, -1), w3_hi, w3_lo,
      dec, dirmask.reshape(1, n_out))


MID_ROWS = 4096
SIDE_PAD = 16


def _fft_tables(L):
    N = 2 * L
    N2 = LANES
    N1 = N // N2
    H1 = N1 // 2
    k1 = np.arange(H1)[:, None]
    n1 = np.arange(H1)[None, :]
    a1 = 2.0 * np.pi * ((k1 * n1) % N1) / N1
    c1, s1 = np.cos(a1), np.sin(a1)
    sign = np.where(np.arange(H1) % 2 == 0, 1.0, -1.0)[None, :]
    pad = np.zeros((SIDE_PAD - 1, H1))
    m1 = np.concatenate([c1, -s1, sign, pad], axis=0)
    wgt = np.where(np.arange(H1) == 0, 1.0, 2.0)[:, None]
    minv = np.concatenate([(wgt * c1).T, (-wgt * s1).T, sign.T, pad.T], axis=1)
    n2 = np.arange(N2)[:, None]
    k2 = np.arange(N2)[None, :]
    a2 = 2.0 * np.pi * ((n2 * k2) % N2) / N2
    cg, sg = np.cos(a2), np.sin(a2)
    g2 = np.block([[cg, -sg], [sg, cg]])
    g2i = np.block([[cg, sg], [-sg, cg]])
    at = 2.0 * np.pi * (np.arange(H1 + 8)[:, None] * np.arange(N2)[None, :]) / N
    tw = np.concatenate([np.cos(at), -np.sin(at)], axis=1)
    as_bf = lambda a: jnp.asarray(a.astype(np.float32)).astype(BF16)
    return dict(m1=as_bf(m1), minv=as_bf(minv), g2=as_bf(g2), g2i=as_bf(g2i),
                tw=jnp.asarray(tw.astype(np.float32)), N1=N1, H1=H1)


S_PITCH = LANES + 8
GROUP = 8


def _slab_scratch_rows(i):
    return pl.ds(i * S_PITCH, LANES)


def _slab_rows(i):
    return pl.ds(i * LANES, LANES)


def _slab_transpose_in(get_slab, dst_refs, h1):
    for i in range(h1):
        t = get_slab(i).T
        for dst in dst_refs:
            dst[_slab_scratch_rows(i), :] = t


def _slab_transpose_out(s_ref, put_slab, h1):
    for i in range(h1):
        put_slab(i, s_ref[_slab_scratch_rows(i), :].T)


def _slab_map(fn, h1):
    for i in range(h1):
        fn(_slab_scratch_rows(i))


def _fft_stage_a(s_ref, nch, m1, tw_ref, ab_ref, side_ref, n1):
    h1 = n1 // 2
    twr = tw_ref[0:h1, 0:LANES]
    twi = tw_ref[0:h1, LANES:2 * LANES]
    tsr = tw_ref[h1:h1 + 1, 0:LANES]
    tsi = tw_ref[h1:h1 + 1, LANES:2 * LANES]
    for c0 in range(0, nch, GROUP):
        rhs = jnp.concatenate([s_ref[pl.ds(c0 + i, h1, stride=S_PITCH), :] for i in range(GROUP)], axis=1)
        res = jnp.dot(m1, rhs.astype(BF16), preferred_element_type=F32)
        for i in range(GROUP):
            ar = res[0:h1, LANES * i:LANES * (i + 1)]
            ai = res[h1:2 * h1, LANES * i:LANES * (i + 1)]
            rows = pl.ds((c0 + i) * h1, h1)
            ab_ref[rows, 0:LANES] = ar * twr - ai * twi
            ab_ref[rows, LANES:2 * LANES] = ar * twi + ai * twr
        a_side = jnp.concatenate([res[2 * h1:2 * h1 + 1, LANES * i:LANES * (i + 1)] for i in range(GROUP)],
                                 axis=0)
        side_ref[c0:c0 + GROUP, 0:LANES] = a_side * tsr
        side_ref[c0:c0 + GROUP, LANES:2 * LANES] = a_side * tsi


def _cmul(a, b):
    ar, ai = a[:, 0:LANES], a[:, LANES:2 * LANES]
    br, bi = b[:, 0:LANES], b[:, LANES:2 * LANES]
    return jnp.concatenate([ar * br - ai * bi, ar * bi + ai * br], axis=1)


def _fft_mid(ab_ref, side_ref, get_h, get_h_side, g2, g2i, nrows):
    chunk = min(MID_ROWS, nrows)
    for r0 in range(0, nrows, chunk):
        rows = pl.ds(r0, chunk)
        spec = _bdot(ab_ref[rows, :], g2)
        ab_ref[rows, :] = _bdot(_cmul(spec, get_h(rows)), g2i)
    side_ref[...] = _bdot(_cmul(_bdot(side_ref[...], g2), get_h_side()), g2i)


def _fft_stage_a_inv(ab_ref, side_ref, nch, minv, tw_ref, s_ref, n1):
    h1 = n1 // 2
    twr = tw_ref[0:h1, 0:LANES]
    twi = tw_ref[0:h1, LANES:2 * LANES]
    tsr = tw_ref[h1:h1 + 1, 0:LANES]
    tsi = tw_ref[h1:h1 + 1, LANES:2 * LANES]
    first_row = lax.broadcasted_iota(jnp.int32, (SIDE_PAD, GROUP * LANES), 0) == 0
    for c0 in range(0, nch, GROUP):
        blk = ab_ref[c0 * h1:(c0 + GROUP) * h1, :]
        re, im = [], []
        for i in range(GROUP):
            br = blk[i * h1:(i + 1) * h1, 0:LANES]
            bi = blk[i * h1:(i + 1) * h1, LANES:2 * LANES]
            re.append(br * twr + bi * twi)
            im.append(bi * twr - br * twi)
        side = side_ref[c0:c0 + GROUP, :]
        side_re = side[:, 0:LANES] * tsr + side[:, LANES:2 * LANES] * tsi
        side_row = jnp.concatenate([side_re[i:i + 1, :] for i in range(GROUP)], axis=1)
        side_blk = jnp.where(first_row, jnp.broadcast_to(side_row, first_row.shape), 0.0)
        rhs = jnp.concatenate([jnp.concatenate(re, axis=1), jnp.concatenate(im, axis=1), side_blk], axis=0)
        y = jnp.dot(minv, rhs.astype(BF16), preferred_element_type=F32)
        for i in range(GROUP):
            s_ref[pl.ds(c0 + i, h1, stride=S_PITCH), :] = y[:, LANES * i:LANES * (i + 1)]


def _filter_spectrum_kernel(hf_ref, hb_ref, m1_ref, g2_ref, tw_ref, o_ref, oside_ref,
                            s_ref, s2_ref, af_ref, ab_ref, afs_ref, abs_ref, *, n1, inv_n):
    nch = hf_ref.shape[1]
    h1 = n1 // 2
    m1, g2 = m1_ref[...], g2_ref[...]
    _slab_transpose_in(lambda i: hf_ref[_slab_rows(i), :], [s_ref], h1)
    _slab_transpose_in(lambda i: hb_ref[_slab_rows(i), :], [s2_ref], h1)
    _fft_stage_a(s_ref, nch, m1, tw_ref, af_ref, afs_ref, n1)
    _fft_stage_a(s2_ref, nch, m1, tw_ref, ab_ref, abs_ref, n1)

    def two_sided(fwd, bwd):
        sf = _bdot(fwd, g2)
        sb = _bdot(bwd, g2)
        return jnp.concatenate([sf[:, 0:LANES] + sb[:, 0:LANES], sf[:, LANES:] - sb[:, LANES:]], axis=1) * inv_n

    chunk = min(MID_ROWS, nch * h1)
    for r0 in range(0, nch * h1, chunk):
        rows = pl.ds(r0, chunk)
        o_ref[rows, :] = two_sided(af_ref[rows, :], ab_ref[rows, :])
    oside_ref[...] = two_sided(afs_ref[...], abs_ref[...])


def _filter_spectrum(filt, L, d_hyena, tabs):
    h1 = tabs["H1"]
    order = filt.shape[1] // (2 * d_hyena)
    cb = LANES
    nblk = d_hyena // cb
    full = lambda a: pl.BlockSpec(a.shape, lambda o, c: (0,) * a.ndim)
    main = pltpu.VMEM((cb * h1, 2 * LANES), F32)
    side = pltpu.VMEM((cb, 2 * LANES), F32)
    return pl.pallas_call(
        functools.partial(_filter_spectrum_kernel, n1=tabs["N1"], inv_n=1.0 / (2 * L)),
        grid=(order, nblk),
        in_specs=[pl.BlockSpec((L, cb), lambda o, c: (0, o * 2 * nblk + c)),
                  pl.BlockSpec((L, cb), lambda o, c: (0, o * 2 * nblk + nblk + c)),
                  full(tabs["m1"]), full(tabs["g2"]), full(tabs["tw"])],
        out_specs=[pl.BlockSpec((cb * h1, 2 * LANES), lambda o, c: (o * nblk + c, 0)),
                   pl.BlockSpec((cb, 2 * LANES), lambda o, c: (o * nblk + c, 0))],
        out_shape=[jax.ShapeDtypeStruct((order * d_hyena * h1, 2 * LANES), F32),
                   jax.ShapeDtypeStruct((order * d_hyena, 2 * LANES), F32)],
        scratch_shapes=[pltpu.VMEM((h1 * S_PITCH, LANES), F32), pltpu.VMEM((h1 * S_PITCH, LANES), F32),
                        main, main, side, side],
        compiler_params=_params("parallel", "parallel"),
        name="filter_spectrum",
    )(filt, filt, tabs["m1"], tabs["g2"], tabs["tw"])


def _hyena_kernel(v_ref, vnext_ref, x1_ref, x2_ref, h_ref, hside_ref, d_ref, m1_ref, minv_ref, g2_ref, g2i_ref,
                  tw_ref, o_ref, s_ref, sv_ref, sn_ref, sx_ref, sz_ref, ab_ref, side_ref, *, n1):
    nch = v_ref.shape[2]
    h1 = n1 // 2
    m1, minv, g2, g2i = m1_ref[...], minv_ref[...], g2_ref[...], g2i_ref[...]

    def slab_of(p_ref):
        return lambda i: p_ref[0, _slab_rows(i), :].astype(F32)

    def long_conv(order, src_ref, gate_ref, skip_ref, keep_ref=None, prefetch_ref=None):
        _fft_stage_a(src_ref, nch, m1, tw_ref, ab_ref, side_ref, n1)
        _slab_transpose_in(slab_of(gate_ref), [sx_ref], h1)
        _fft_mid(ab_ref, side_ref, lambda rows: h_ref[order, rows, :], lambda: hside_ref[order], g2, g2i, nch * h1)
        _fft_stage_a_inv(ab_ref, side_ref, nch, minv, tw_ref, s_ref, n1)
        if prefetch_ref is not None:
            _slab_transpose_in(slab_of(prefetch_ref), [sn_ref], h1)
        d_slab = jnp.broadcast_to(d_ref[:, order:order + 1], (nch, LANES))

        def gate(rows):
            val = sx_ref[rows, :] * (s_ref[rows, :] + d_slab * skip_ref[rows, :])
            s_ref[rows, :] = val
            if keep_ref is not None:
                keep_ref[rows, :] = val

        _slab_map(gate, h1)

    first = pl.program_id(1) == 0

    @pl.when(first)
    def _():
        _slab_transpose_in(slab_of(v_ref), [sv_ref], h1)

    @pl.when(jnp.logical_not(first))
    def _():
        def copy(rows):
            sv_ref[rows, :] = sn_ref[rows, :]

        _slab_map(copy, h1)

    long_conv(0, sv_ref, x1_ref, sv_ref, keep_ref=sz_ref, prefetch_ref=vnext_ref)
    long_conv(1, s_ref, x2_ref, sz_ref)

    def put(i, out):
        o_ref[0, _slab_rows(i), :] = out.astype(o_ref.dtype)

    _slab_transpose_out(s_ref, put, h1)


def _hyena(proj, spec, spec_side, skip, tabs):
    B, L, _ = proj.shape
    D = skip.shape[1]
    n1, h1 = tabs["N1"], tabs["H1"]
    cb = LANES
    nblk = D // cb
    spec = spec.reshape(2, D * h1, 2 * LANES)
    spec_side = spec_side.reshape(2, D, 2 * LANES)
    col = lambda part: pl.BlockSpec((1, L, cb), lambda c, b: (b, 0, part * nblk + c))
    full = lambda a: pl.BlockSpec(a.shape, lambda c, b: (0,) * a.ndim)
    slabs = pltpu.VMEM((h1 * S_PITCH, LANES), F32)
    return pl.pallas_call(
        functools.partial(_hyena_kernel, n1=n1),
        grid=(nblk, B),
        in_specs=[col(0),
                  pl.BlockSpec((1, L, cb), lambda c, b: (jnp.minimum(b + 1, B - 1), 0, c)),
                  col(1), col(2),
                  pl.BlockSpec((2, cb * h1, 2 * LANES), lambda c, b: (0, c, 0), pipeline_mode=pl.Buffered(1)),
                  pl.BlockSpec((2, cb, 2 * LANES), lambda c, b: (0, c, 0)),
                  pl.BlockSpec((cb, 2), lambda c, b: (c, 0)),
                  full(tabs["m1"]), full(tabs["minv"]), full(tabs["g2"]), full(tabs["g2i"]), full(tabs["tw"])],
        out_specs=pl.BlockSpec((1, L, cb), lambda c, b: (b, 0, c)),
        out_shape=jax.ShapeDtypeStruct((B, L, D), BF16),
        scratch_shapes=[slabs, slabs, slabs, slabs, slabs, pltpu.VMEM((cb * h1, 2 * LANES), F32),
                        pltpu.VMEM((cb, 2 * LANES), F32)],
        compiler_params=_params("parallel", "arbitrary"),
        name="hyena_fftconv",
    )(proj, proj, proj, proj, spec, spec_side, skip.astype(F32).T,
      tabs["m1"], tabs["minv"], tabs["g2"], tabs["g2i"], tabs["tw"])


ML_CHUNK = 512


def _log_sigmoid(x):
    return jnp.minimum(x, 0.0) - jnp.log1p(jnp.exp(-jnp.abs(x)))


def _mlstm_direction(q, k, v, i_row, cum_row, logf_row, c_ref, m_ref, mask, scale):
    c, dh = q.shape
    g_row = i_row - cum_row
    f_col_rep = jnp.broadcast_to(cum_row, (LANES, c)).T
    m_prev = m_ref[0:1, 0:1]
    log_scale = math.log(scale)
    e = jnp.where(mask, g_row + log_scale, -jnp.inf)
    a = jnp.maximum(jnp.max(e, axis=1, keepdims=True) - log_scale, m_prev)
    sc = lax.dot_general(q.astype(BF16), k.astype(BF16), (((1,), (1,)), ((), ())), preferred_element_type=F32)
    p = sc * jnp.exp(e - a)
    vaug = jnp.concatenate([v, jnp.ones_like(v)], axis=1)
    inter = jnp.exp(m_prev - a)
    tot = _bdot(p, vaug) + inter * _bdot(q, c_ref[...])
    num, den = tot[:, 0:dh], tot[:, dh:2 * dh]
    m_t = f_col_rep + a
    h = num / jnp.maximum(jnp.abs(den), jnp.exp(-m_t))
    a_end = jnp.maximum(jnp.max(g_row, axis=1, keepdims=True), m_prev)
    f_end = jnp.sum(logf_row, axis=1, keepdims=True)
    we = jnp.exp(g_row - a_end) * scale
    kw = k.astype(F32).T * we
    c_ref[...] = jnp.exp(m_prev - a_end) * c_ref[...] + _bdot(kw, vaug)
    m_ref[...] = jnp.broadcast_to(f_end + a_end, m_ref.shape)
    return h


def _mlstm_kernel(qf_ref, kf_ref, vf_ref, gf_ref, qb_ref, kb_ref, vb_ref, gb_ref, bias_ref,
                  hf_ref, hb_ref, c_ref, m_ref, *, heads, scale, chunk):
    @pl.when(pl.program_id(1) == 0)
    def _():
        c_ref[...] = jnp.zeros_like(c_ref)
        m_ref[...] = jnp.zeros_like(m_ref)

    c = chunk
    nsub = qf_ref.shape[1] // c
    dh = qf_ref.shape[2] // heads
    r = lax.broadcasted_iota(jnp.int32, (c, c), 0)
    s = lax.broadcasted_iota(jnp.int32, (c, c), 1)
    mask_f = s <= r
    mask_b = s >= r

    def gate_rows(g, mask_t):
        g = g + bias_ref[...]
        logf = _log_sigmoid(g)
        tri = mask_t.astype(BF16)
        cum = sum(jnp.dot(part, tri, preferred_element_type=F32) for part in _split3(logf))
        return g, logf, cum

    for u in range(nsub):
        rf = slice(u * c, (u + 1) * c)
        rb = slice((nsub - 1 - u) * c, (nsub - u) * c)
        g_f, logf_f, cum_f = gate_rows(gf_ref[:, rf], mask_b)
        g_b, logf_b, cum_b = gate_rows(gb_ref[:, rb], mask_f)
        for h in range(heads):
            sl = slice(h * dh, (h + 1) * dh)
            fi, ff, bi, bf = h, heads + h, 2 * heads + h, 3 * heads + h
            hf_ref[0, rf, sl] = _mlstm_direction(
                qf_ref[0, rf, sl], kf_ref[0, rf, sl], vf_ref[0, rf, sl], g_f[fi:fi + 1], cum_f[ff:ff + 1],
                logf_f[ff:ff + 1], c_ref.at[h], m_ref.at[h], mask_f, scale).astype(hf_ref.dtype)
            hb_ref[0, rb, sl] = _mlstm_direction(
                qb_ref[0, rb, sl], kb_ref[0, rb, sl], vb_ref[0, rb, sl], g_b[bi:bi + 1], cum_b[bf:bf + 1],
                logf_b[bf:bf + 1], c_ref.at[heads + h], m_ref.at[heads + h], mask_b, scale).astype(hb_ref.dtype)


def _mlstm(qk, proj, v_col0, gates, gate_b, d_ml):
    B, L, _ = qk.shape
    H = MLSTM_HEADS
    dh = d_ml // H
    c = min(ML_CHUNK, L)
    rows = c
    nb = L // rows
    bias = jnp.broadcast_to(gate_b.astype(F32)[:, None], (4 * H, c))
    vb0 = v_col0 // d_ml
    specs = []
    for pos in (lambda ci: ci, lambda ci: nb - 1 - ci):
        specs += [pl.BlockSpec((1, rows, d_ml), lambda bi, ci, pos=pos: (bi, pos(ci), 0)),
                  pl.BlockSpec((1, rows, d_ml), lambda bi, ci, pos=pos: (bi, pos(ci), 1)),
                  pl.BlockSpec((1, rows, d_ml), lambda bi, ci, pos=pos: (bi, pos(ci), vb0)),
                  pl.BlockSpec((4 * H, rows), lambda bi, ci, pos=pos: (0, bi * nb + pos(ci)))]
    specs.append(pl.BlockSpec((4 * H, c), lambda bi, ci: (0, 0)))
    out_shape = jax.ShapeDtypeStruct((B, L, d_ml), BF16)
    return pl.pallas_call(
        functools.partial(_mlstm_kernel, heads=H, scale=dh ** -0.5, chunk=c),
        grid=(B, nb),
        in_specs=specs,
        out_specs=[pl.BlockSpec((1, rows, d_ml), lambda bi, ci: (bi, ci, 0)),
                   pl.BlockSpec((1, rows, d_ml), lambda bi, ci: (bi, nb - 1 - ci, 0))],
        out_shape=[out_shape, out_shape],
        scratch_shapes=[pltpu.VMEM((2 * H, dh, 2 * dh), F32), pltpu.VMEM((2 * H, 1, LANES), F32)],
        compiler_params=_params("parallel", "arbitrary"),
        name="mlstm_scan",
    )(qk, qk, proj, gates, qk, qk, proj, gates, bias)


def _lane_tile_rms(y, gain):
    parts = [y[:, i:i + LANES] for i in range(0, y.shape[1], LANES)]
    return jnp.concatenate([p * lax.rsqrt(jnp.mean(p * p, axis=-1, keepdims=True) + EPS) for p in parts],
                           axis=1) * gain


def _group_rms(y, gain, bd, group):
    sq = (y * y).astype(BF16)
    w = bd.shape[0]
    ss = jnp.concatenate([jnp.dot(sq[:, i:i + w], bd, preferred_element_type=F32)
                          for i in range(0, y.shape[1], w)], axis=1)
    return y * lax.rsqrt(ss * (1.0 / group) + EPS) * gain


GN_STRIPE = 256


def _block_diag_ones(n, group):
    idx = np.arange(n) // group
    return jnp.asarray((idx[:, None] == idx[None, :]).astype(np.float32)).astype(BF16)


def _mix_xattn_kernel(z_ref, hf_ref, hb_ref, o_ref, x_ref, mem_ref, ghy_ref, gml_ref, why_ref, wml_ref, bdh_ref,
                      gx_ref, gm_ref, wq_ref, wkv_ref, wo_ref, out_ref, kv_ref, *, hy_group, heads, scale):
    @pl.when(pl.program_id(1) == 0)
    def _():
        kv_ref[...] = _bdot(_rms(mem_ref[0], gm_ref[...]), wkv_ref[...]).astype(kv_ref.dtype)

    y_hy = _group_rms(z_ref[0].astype(F32), ghy_ref[...], bdh_ref[...], hy_group)
    h_sum = hf_ref[0].astype(F32) + hb_ref[0].astype(F32)
    y_ml = _lane_tile_rms(jax.nn.sigmoid(o_ref[0].astype(F32)) * h_sum, gml_ref[...])
    h1 = x_ref[0] + _bdot(y_hy, why_ref[...]) + _bdot(y_ml, wml_ref[...])

    D = h1.shape[1]
    dh = D // heads
    q = _bdot(_rms(h1, gx_ref[...]), wq_ref[...])
    outs = []
    for hd in range(heads):
        qh = q[:, hd * dh:(hd + 1) * dh]
        kh = kv_ref[:, hd * dh:(hd + 1) * dh]
        vh = kv_ref[:, D + hd * dh:D + (hd + 1) * dh]
        s = lax.dot_general(qh.astype(BF16), kh, (((1,), (1,)), ((), ())), preferred_element_type=F32) * scale
        e = jnp.exp(s - jnp.max(s, axis=-1, keepdims=True))
        p = e / jnp.sum(e, axis=-1, keepdims=True)
        outs.append(_bdot(p, vh))
    out_ref[0] = h1 + _bdot(jnp.concatenate(outs, axis=1), wo_ref[...])


def _mix_xattn(z, hf, hb, proj, o_col0, x, mem, g_hy, g_ml, w_out, g_x, g_mem, wq, wk, wv, wo):
    B, L, d_hy = z.shape
    tm = min(MIX_XATTN_ROWS, L)
    d_ml = hf.shape[2]
    D = x.shape[2]
    nm = mem.shape[1]
    ob = o_col0 // d_ml
    row = lambda n: pl.BlockSpec((1, tm, n), lambda bi, i: (bi, i, 0))
    const = lambda a: pl.BlockSpec(a.shape, lambda bi, i: (0,) * a.ndim, pipeline_mode=pl.Buffered(1))
    hy_group = d_hy // HYENA_GROUPS
    assert d_ml // MLSTM_HEADS == LANES
    consts = [g_hy.reshape(1, -1), g_ml.reshape(1, -1), w_out[:d_hy].astype(BF16), w_out[d_hy:].astype(BF16),
              _block_diag_ones(GN_STRIPE, hy_group), g_x.reshape(1, D),
              g_mem.reshape(1, D), wq.astype(BF16), jnp.concatenate([wk, wv], axis=1).astype(BF16), wo.astype(BF16)]
    return pl.pallas_call(
        functools.partial(_mix_xattn_kernel, hy_group=hy_group, heads=XATTN_HEADS,
                          scale=(D // XATTN_HEADS) ** -0.5),
        grid=(B, L // tm),
        in_specs=[row(d_hy), row(d_ml), row(d_ml), pl.BlockSpec((1, tm, d_ml), lambda bi, i: (bi, i, ob)), row(D),
                  pl.BlockSpec((1, nm, D), lambda bi, i: (bi, 0, 0))] + [const(a) for a in consts],
        out_specs=row(D),
        out_shape=jax.ShapeDtypeStruct((B, L, D), F32),
        scratch_shapes=[pltpu.VMEM((nm, 2 * D), BF16)],
        compiler_params=_params("parallel", "arbitrary"),
        name="mix_xattn",
    )(z, hf, hb, proj, x, mem, *consts)


MLP_CHUNK = 1024


def _mlp_kernel(h_ref, g_ref, w1_ref, w2_ref, gf_ref, out_ref, xn_ref, *, final_norm):
    x = h_ref[...]
    xn_ref[...] = _rms(x, g_ref[...]).astype(BF16)
    out_ref[...] = x
    for c in range(w1_ref.shape[1] // MLP_CHUNK):
        cols = slice(c * MLP_CHUNK, (c + 1) * MLP_CHUNK)
        a = jnp.maximum(jnp.dot(xn_ref[...], w1_ref[:, cols], preferred_element_type=F32), 0.0)
        out_ref[...] += _bdot(a * a, w2_ref[cols, :])
    if final_norm:
        out_ref[...] = _rms(out_ref[...], gf_ref[...])


def _mlp(h, g, w1, w2, gf, final_norm):
    M, D = h.shape
    dff = w1.shape[1]
    tm = min(MLP_ROWS, M)
    const = lambda shape: pl.BlockSpec(shape, lambda i: (0, 0), pipeline_mode=pl.Buffered(1))
    return pl.pallas_call(
        functools.partial(_mlp_kernel, final_norm=final_norm),
        grid=(M // tm,),
        in_specs=[pl.BlockSpec((tm, D), lambda i: (i, 0)), const((1, D)), const((D, dff)), const((dff, D)),
                  const((1, D))],
        out_specs=pl.BlockSpec((tm, D), lambda i: (i, 0)),
        out_shape=jax.ShapeDtypeStruct((M, D), F32),
        scratch_shapes=[pltpu.VMEM((tm, D), BF16)],
        compiler_params=_params("parallel"),
        name="mlp",
    )(h, g.reshape(1, D), w1.astype(BF16), w2.astype(BF16), gf.reshape(1, D))


def _pad_cols(w, n):
    return jnp.pad(w, ((0, 0), (0, n - w.shape[1])))


def _layer(h, mem, p, l):
    B, L, D = h.shape
    M = B * L
    d_hy = p["hy_norm_g"].shape[1]
    d_ml = p["ml_norm_g"].shape[1]
    hy_cols = 3 * d_hy
    qk_cols = 2 * d_ml
    n_gate = 4 * MLSTM_HEADS
    w_in = p["w_in"][l]
    vo0 = hy_cols + qk_cols
    g0 = vo0 + 2 * d_ml
    v_col0 = hy_cols
    o_col0 = hy_cols + d_ml
    n_in = hy_cols + 2 * d_ml
    proj, qk, gates = _in_proj(h.reshape(M, D), p["norm_mix_g"][l], w_in[:, :vo0].astype(BF16),
                               w_in[:, vo0:g0].astype(BF16),
                               _pad_cols(w_in[:, g0:g0 + n_gate], LANES).astype(BF16), n_gate,
                               jnp.concatenate([p["hy_conv_w"][l], p["ml_conv_w"][l]], axis=1),
                               jnp.concatenate([p["hy_conv_b"][l], p["ml_conv_b"][l]]),
                               n_hy=hy_cols, seq_len=L)
    proj = proj.reshape(B, L, n_in)
    qk = qk.reshape(B, L, qk_cols)

    tabs = _fft_tables(L)
    filt = _hyena_filters(L, p["hy_filt_w1"][l], p["hy_filt_b1"][l], p["hy_filt_freq1"][l], p["hy_filt_w2"][l],
                          p["hy_filt_b2"][l], p["hy_filt_freq2"][l], p["hy_filt_w3"][l], d_hy)
    spec, spec_side = _filter_spectrum(filt, L, d_hy, tabs)
    z_hy = _hyena(proj, spec, spec_side, p["hy_skip"][l], tabs)

    h_f, h_b = _mlstm(qk, proj, v_col0, gates, p["ml_gate_b"][l], d_ml)

    return _mix_xattn(z_hy, h_f, h_b, proj, o_col0, h, mem, p["hy_norm_g"][l], p["ml_norm_g"][l], p["w_out"][l],
                      p["norm_x_g"][l], p["norm_mem_g"][l], p["xa_wq"][l], p["xa_wk"][l], p["xa_wv"][l],
                      p["xa_wo"][l])


def kernel(x, mem, norm_mix_g, w_in, hy_conv_w, hy_conv_b, hy_filt_w1, hy_filt_b1, hy_filt_freq1, hy_filt_w2,
           hy_filt_b2, hy_filt_freq2, hy_filt_w3, hy_skip, hy_norm_g, ml_conv_w, ml_conv_b, ml_gate_b, ml_norm_g,
           w_out, norm_x_g, norm_mem_g, xa_wq, xa_wk, xa_wv, xa_wo, norm_ff_g, ff_w1, ff_w2, final_norm_g):
    p = dict(norm_mix_g=norm_mix_g, w_in=w_in, hy_conv_w=hy_conv_w, hy_conv_b=hy_conv_b, hy_filt_w1=hy_filt_w1,
             hy_filt_b1=hy_filt_b1, hy_filt_freq1=hy_filt_freq1, hy_filt_w2=hy_filt_w2, hy_filt_b2=hy_filt_b2,
             hy_filt_freq2=hy_filt_freq2, hy_filt_w3=hy_filt_w3, hy_skip=hy_skip, hy_norm_g=hy_norm_g,
             ml_conv_w=ml_conv_w, ml_conv_b=ml_conv_b, ml_gate_b=ml_gate_b, ml_norm_g=ml_norm_g, w_out=w_out,
             norm_x_g=norm_x_g, norm_mem_g=norm_mem_g, xa_wq=xa_wq, xa_wk=xa_wk, xa_wv=xa_wv, xa_wo=xa_wo)
    B, L, D = x.shape
    depth = w_in.shape[0]
    h = x
    for l in range(depth):
        h = _layer(h, mem, p, l)
        h = _mlp(h.reshape(B * L, D), norm_ff_g[l], ff_w1[l], ff_w2[l], final_norm_g,
                 final_norm=l == depth - 1).reshape(B, L, D)
    return h
```
